```python
import jax, jax.numpy as jnp
from jax import lax
import numpy as np

D_MODEL = 1024
BATCH = 8
SEQ = 2048
DEPTH = 2

MIX_WIDTH = D_MODEL // 2
N_BRANCH = 3
A_HEAD_DIM = 64
A_HEADS = MIX_WIDTH // A_HEAD_DIM
A_WIDTH = A_HEADS * A_HEAD_DIM
A_RANK_W = 64
A_RANK_A = 64
A_RANK_G = 128
A_GN_EPS = 64e-5
B_KEY_DIM = 128
B_VAL_DIM = 128
B_HEADS = MIX_WIDTH // B_VAL_DIM
B_WIDTH = B_HEADS * B_VAL_DIM
B_CHUNK = 64
C_HEAD_DIM = 64
C_HEADS = MIX_WIDTH // C_HEAD_DIM
C_KV_HEADS = 2
C_WIDTH = C_HEADS * C_HEAD_DIM
IDX_HEADS = 4
IDX_DIM = C_HEAD_DIM
TOPK_MAX = 256
Q_BLOCK = 128
ROPE_THETA = 10000.0
D_FF = 2816
CONV_WIDTH = 3
NORM_EPS = 1e-6

A_COLS = 3 * A_WIDTH + A_RANK_W + A_RANK_A + A_RANK_G
B_COLS = 4 * B_WIDTH
C_COLS = C_WIDTH + 2 * C_KV_HEADS * C_HEAD_DIM + IDX_HEADS * IDX_DIM + IDX_DIM + IDX_HEADS
GATE_COLS = N_BRANCH * D_MODEL
IN_COLS = A_COLS + B_COLS + C_COLS + GATE_COLS

kernel_name = 'hybrid_rwkv7_hgrn2_dsa_block'

F32 = jnp.float32


def _split(t, sizes):
    out, start = [], 0
    for s in sizes:
        out.append(t[..., start:start + s])
        start += s
    return out


def rms_norm(x, g):
    xf = x.astype(F32)
    y = xf * lax.rsqrt(jnp.mean(xf * xf, axis=-1, keepdims=True) + NORM_EPS)
    return (y * g.astype(F32)).astype(x.dtype)


def token_shift(p):
    return jnp.pad(p, ((0, 0), (1, 0), (0, 0)))[:, :-1]


def rope_tables(seq, dim):
    half = dim // 2
    inv_freq = ROPE_THETA ** (-jnp.arange(half, dtype=F32) * (2.0 / dim))
    ang = jnp.arange(seq, dtype=F32)[:, None] * inv_freq[None, :]
    return jnp.cos(ang), jnp.sin(ang)


def apply_rope(t, cos, sin):
    half = t.shape[-1] // 2
    tf = t.astype(F32)
    t1, t2 = tf[..., :half], tf[..., half:]
    c, s = cos[None, :, None, :], sin[None, :, None, :]
    return jnp.concatenate([t1 * c - t2 * s, t2 * c + t1 * s], axis=-1).astype(t.dtype)


def rwkv7_scan(r, w, k, v, a, b):
    bsz, _, heads, n = r.shape

    def step(state, inp):
        r_t, w_t, k_t, v_t, a_t, b_t = inp
        sa = jnp.einsum('bhij,bhj->bhi', state, a_t)
        state = (state * w_t[:, :, None, :] + sa[..., None] * b_t[:, :, None, :]
                 + v_t[..., None] * k_t[:, :, None, :])
        return state, jnp.einsum('bhij,bhj->bhi', state, r_t)

    xs = tuple(jnp.moveaxis(t, 1, 0) for t in (r, w, k, v, a, b))
    _, ys = lax.scan(step, jnp.zeros((bsz, heads, n, n), F32), xs)
    return jnp.moveaxis(ys, 0, 1)


def rwkv7_branch(p, mu, w0, w_up, a0, a_up, g_up, k_k, k_a, r_k, gn_g, gn_b):
    bsz, seq, _ = p.shape
    p = p + (token_shift(p) - p) * mu
    r, k, v, wd, ad, gd = _split(p, (A_WIDTH, A_WIDTH, A_WIDTH, A_RANK_W, A_RANK_A, A_RANK_G))
    w_log = -jax.nn.softplus(-(w0 + jnp.tanh(wd) @ w_up).astype(F32)) - 0.5
    decay = jnp.exp(-jnp.exp(w_log))
    a = jax.nn.sigmoid((a0 + ad @ a_up).astype(F32))
    g = jax.nn.sigmoid(gd) @ g_up

    def hd(t):
        return t.astype(F32).reshape(bsz, seq, A_HEADS, A_HEAD_DIM)

    kk = hd(k * k_k)
    kk = kk * lax.rsqrt(jnp.maximum(jnp.sum(kk * kk, axis=-1, keepdims=True), 1e-24))
    k_mod = hd(k.astype(F32) * (1.0 + (a - 1.0) * k_a.astype(F32)))
    r_h, v_h, a_h = hd(r), hd(v), hd(a)
    y = rwkv7_scan(r_h, hd(decay), k_mod, v_h, -kk, kk * a_h)
    mean = jnp.mean(y, axis=-1, keepdims=True)
    var = jnp.mean(jnp.square(y - mean), axis=-1, keepdims=True)
    y = ((y - mean) * lax.rsqrt(var + A_GN_EPS)).reshape(bsz, seq, A_WIDTH)
    y = y * gn_g.astype(F32) + gn_b.astype(F32)
    bonus = jnp.sum(r_h * k_mod * r_k.astype(F32), axis=-1, keepdims=True) * v_h
    y = y + bonus.reshape(bsz, seq, A_WIDTH)
    return (y * g.astype(F32)).astype(p.dtype)


def hgrn2_chunk_step(state, inp):
    q, k, g, v = inp
    b = jnp.cumsum(g, axis=2)
    causal = jnp.tril(jnp.ones((B_CHUNK, B_CHUNK), dtype=bool))
    diff = b[:, :, :, None, :] - b[:, :, None, :, :]
    decay = jnp.exp(jnp.where(causal[:, :, None], diff, -jnp.inf))
    scores = jnp.einsum('bhtd,bhsd,bhtsd->bhts', q, k, decay)
    o = (jnp.einsum('bhts,bhsv->bhtv', scores, v)
         + jnp.einsum('bhtd,bhdv->bhtv', q * jnp.exp(b), state))
    b_end = b[:, :, -1:, :]
    state = (state * jnp.exp(b_end)[:, :, 0, :, None]
             + jnp.einsum('bhsd,bhsv->bhdv', k * jnp.exp(b_end - b), v))
    return state, o


def hgrn2_branch(p, lb, gn_g):
    bsz, seq, _ = p.shape
    q, f, i, g = _split(p, (B_WIDTH, B_WIDTH, B_WIDTH, B_WIDTH))
    f = f.astype(F32)
    lb = lb.astype(F32)
    log_f = jnp.logaddexp(jnp.log(lb), jnp.log1p(-lb) + jax.nn.log_sigmoid(f))
    k_in = (1.0 - lb) * jax.nn.sigmoid(-f)
    n_chunk = seq // B_CHUNK

    def chunks(t, d):
        return t.astype(F32).reshape(bsz, n_chunk, B_CHUNK, B_HEADS, d).transpose(1, 0, 3, 2, 4)

    s0 = jnp.zeros((bsz, B_HEADS, B_KEY_DIM, B_VAL_DIM), F32)
    _, o = lax.scan(hgrn2_chunk_step, s0,
                    (chunks(q, B_KEY_DIM), chunks(k_in, B_KEY_DIM),
                     chunks(log_f, B_KEY_DIM), chunks(i, B_VAL_DIM)))
    o = o.transpose(1, 0, 3, 2, 4).reshape(bsz, seq, B_HEADS, B_VAL_DIM)
    o = o * lax.rsqrt(jnp.mean(o * o, axis=-1, keepdims=True) + NORM_EPS)
    o = o.reshape(bsz, seq, B_WIDTH) * gn_g.astype(F32) * jax.nn.silu(g.astype(F32))
    return o.astype(p.dtype)


def dsa_branch(p, cos, sin):
    bsz, seq, _ = p.shape
    kv_w = C_KV_HEADS * C_HEAD_DIM
    q, k, v, qi, ki, wi = _split(p, (C_WIDTH, kv_w, kv_w, IDX_HEADS * IDX_DIM, IDX_DIM, IDX_HEADS))
    q = apply_rope(q.reshape(bsz, seq, C_HEADS, C_HEAD_DIM), cos, sin)
    k = apply_rope(k.reshape(bsz, seq, C_KV_HEADS, C_HEAD_DIM), cos, sin)
    v = v.reshape(bsz, seq, C_KV_HEADS, C_HEAD_DIM)
    qi = apply_rope(qi.reshape(bsz, seq, IDX_HEADS, IDX_DIM), cos, sin)
    ki = apply_rope(ki[:, :, None, :], cos, sin)[:, :, 0, :].astype(F32)
    wi = wi.astype(F32) * (IDX_HEADS ** -0.5) * (IDX_DIM ** -0.5)
    n_blk = seq // Q_BLOCK
    k_sel = min(TOPK_MAX, seq // 4)
    group = C_HEADS // C_KV_HEADS
    s_pos = jnp.arange(seq)

    def blockify(t):
        return jnp.moveaxis(t.reshape(bsz, n_blk, Q_BLOCK, *t.shape[2:]), 1, 0)

    def attend(args):
        qb, qib, wib, t0 = args
        t_pos = t0 + jnp.arange(Q_BLOCK)
        causal = s_pos[None, :] <= t_pos[:, None]
        dots = jnp.einsum('bqhd,bsd->bqhs', qib.astype(F32), ki)
        score = jnp.einsum('bqhs,bqh->bqs', jax.nn.relu(dots), wib)
        score = jnp.where(causal[None], score, -jnp.inf)
        _, idx = lax.top_k(score, k_sel)
        kg = jax.vmap(lambda kb, ib: kb[ib])(k, idx)
        vg = jax.vmap(lambda vb, ib: vb[ib])(v, idx)
        valid = idx <= t_pos[None, :, None]
        qg = qb.reshape(bsz, Q_BLOCK, C_KV_HEADS, group, C_HEAD_DIM)
        logits = jnp.einsum('bqcgd,bqkcd->bqcgk', qg, kg).astype(F32) * (C_HEAD_DIM ** -0.5)
        logits = jnp.where(valid[:, :, None, None, :], logits, -jnp.inf)
        prob = jax.nn.softmax(logits, axis=-1).astype(vg.dtype)
        out = jnp.einsum('bqcgk,bqkcd->bqcgd', prob, vg)
        return out.reshape(bsz, Q_BLOCK, C_WIDTH)

    outs = lax.map(attend, (blockify(q), blockify(qi), blockify(wi),
                            jnp.arange(n_blk, dtype=jnp.int32) * Q_BLOCK))
    return jnp.moveaxis(outs, 0, 1).reshape(bsz, seq, C_WIDTH).astype(p.dtype)


def conv_glu(h, w_up, conv_w, conv_b, w_down):
    seq = h.shape[1]
    up = h @ w_up
    up_pad = jnp.pad(up, ((0, 0), (CONV_WIDTH - 1, 0), (0, 0)))
    c = conv_b
    for j in range(CONV_WIDTH):
        c = c + conv_w[j] * up_pad[:, j:j + seq]
    gate, val = _split(c, (D_FF, D_FF))
    return (jax.nn.silu(gate) * val) @ w_down


def setup_inputs(seed: int = 0) -> dict:
    key = jax.random.key(seed)
    ks = iter(jax.random.split(key, 32))

    def nrm(shape, scale):
        return jax.random.normal(next(ks), shape, F32) * scale

    def uni(shape, lo, hi):
        return jax.random.uniform(next(ks), shape, F32, lo, hi)

    f2 = 2 * D_FF
    return {
        'x': nrm((BATCH, SEQ, D_MODEL), 1.0),
        'norm_mix_g': 1.0 + nrm((DEPTH, D_MODEL), 0.02),
        'w_in': nrm((DEPTH, D_MODEL, IN_COLS), D_MODEL ** -0.5),
        'a_mu': uni((DEPTH, A_COLS), 0.0, 1.0),
        'a_w0': uni((DEPTH, A_WIDTH), -5.0, 0.0),
        'a_w_up': nrm((DEPTH, A_RANK_W, A_WIDTH), 0.5 * A_RANK_W ** -0.5),
        'a_a0': nrm((DEPTH, A_WIDTH), 0.5),
        'a_a_up': nrm((DEPTH, A_RANK_A, A_WIDTH), 0.5 * A_RANK_A ** -0.5),
        'a_g_up': nrm((DEPTH, A_RANK_G, A_WIDTH), A_RANK_G ** -0.5),
        'a_k_k': 0.85 + nrm((DEPTH, A_WIDTH), 0.05),
        'a_k_a': 1.0 + nrm((DEPTH, A_WIDTH), 0.05),
        'a_r_k': nrm((DEPTH, A_HEADS, A_HEAD_DIM), 0.1),
        'a_gn_g': 1.0 + nrm((DEPTH, A_WIDTH), 0.02),
        'a_gn_b': nrm((DEPTH, A_WIDTH), 0.02),
        'b_lb_logits': nrm((DEPTH, B_HEADS * B_KEY_DIM), 1.0),
        'b_gn_g': 1.0 + nrm((DEPTH, B_WIDTH), 0.02),
        'w_branch': nrm((DEPTH, N_BRANCH, MIX_WIDTH, D_MODEL), MIX_WIDTH ** -0.5),
        'w_o': nrm((DEPTH, D_MODEL, D_MODEL), D_MODEL ** -0.5),
        'norm_ffn_g': 1.0 + nrm((DEPTH, D_MODEL), 0.02),
        'w_up': nrm((DEPTH, D_MODEL, f2), D_MODEL ** -0.5),
        'conv_w': nrm((DEPTH, CONV_WIDTH, f2), CONV_WIDTH ** -0.5),
        'conv_b': nrm((DEPTH, f2), 0.02),
        'w_down': nrm((DEPTH, D_FF, D_MODEL), D_FF ** -0.5),
        'norm_final_g': 1.0 + nrm((D_MODEL,), 0.02),
    }


def reference(x, norm_mix_g, w_in, a_mu, a_w0, a_w_up, a_a0, a_a_up, a_g_up, a_k_k, a_k_a,
              a_r_k, a_gn_g, a_gn_b, b_lb_logits, b_gn_g, w_branch, w_o, norm_ffn_g,
              w_up, conv_w, conv_b, w_down, norm_final_g):
    bsz, seq, _ = x.shape
    cos, sin = rope_tables(seq, C_HEAD_DIM)
    lb_cum = jnp.cumsum(jax.nn.softmax(b_lb_logits.astype(F32), axis=0), axis=0)
    lower_bounds = lb_cum - lb_cum[0]
    h = x
    for layer in range(DEPTH):
        u = rms_norm(h, norm_mix_g[layer])
        p = u @ w_in[layer]
        p_a, p_b, p_c, p_gate = _split(p, (A_COLS, B_COLS, C_COLS, GATE_COLS))
        y_a = rwkv7_branch(p_a, a_mu[layer], a_w0[layer], a_w_up[layer], a_a0[layer],
                           a_a_up[layer], a_g_up[layer], a_k_k[layer], a_k_a[layer],
                           a_r_k[layer], a_gn_g[layer], a_gn_b[layer])
        y_b = hgrn2_branch(p_b, lower_bounds[layer], b_gn_g[layer])
        y_c = dsa_branch(p_c, cos, sin)
        ys = jnp.stack([y_a, y_b, y_c], axis=2)
        branch = jnp.einsum('bsnc,ncd->bsnd', ys, w_branch[layer])
        gates = jax.nn.sigmoid(p_gate.reshape(bsz, seq, N_BRANCH, D_MODEL))
        merged = jnp.sum(gates * branch, axis=2)
        h = h + merged @ w_o[layer]
        u = rms_norm(h, norm_ffn_g[layer])
        h = h + conv_glu(u, w_up[layer], conv_w[layer], conv_b[layer], w_down[layer])
    return rms_norm(h, norm_final_g)
```

```python
import functools

import jax
import jax.numpy as jnp
import numpy as np
from jax import lax
from jax.experimental import pallas as pl
from jax.experimental.pallas import tpu as pltpu

F32 = jnp.float32
BF16 = jnp.bfloat16

D_MODEL = 1024
DEPTH = 2
MIX_WIDTH = 512
N_BRANCH = 3
A_HEAD_DIM = 64
A_HEADS = 8
A_WIDTH = 512
A_RANK_W = 64
A_RANK_A = 64
A_RANK_G = 128
A_GN_EPS = 64e-5
B_KEY_DIM = 128
B_VAL_DIM = 128
B_HEADS = 4
B_WIDTH = 512
C_HEAD_DIM = 64
C_HEADS = 8
C_KV_HEADS = 2
C_WIDTH = 512
IDX_HEADS = 4
IDX_DIM = 64
TOPK_MAX = 256
ROPE_THETA = 10000.0
D_FF = 2816
CONV_WIDTH = 3
NORM_EPS = 1e-6

A_COLS = 3 * A_WIDTH + A_RANK_W + A_RANK_A + A_RANK_G
B_COLS = 4 * B_WIDTH
C_COLS = C_WIDTH + 2 * C_KV_HEADS * C_HEAD_DIM + IDX_HEADS * IDX_DIM + IDX_DIM + IDX_HEADS
GATE_COLS = N_BRANCH * D_MODEL

LANE = 128
SUBLANE = 8
VMEM_LIMIT = 48 * 1024 * 1024


def _round_up(n, m):
    return (n + m - 1) // m * m


def _rms(x, g):
    return x * lax.rsqrt(jnp.mean(x * x, axis=-1, keepdims=True) + NORM_EPS) * g


def _norm_matmul_kernel(x_ref, g_ref, w_ref, o_ref, xn_ref):
    @pl.when(pl.program_id(1) == 0)
    def _():
        xn_ref[...] = _rms(x_ref[...], g_ref[...]).astype(BF16)

    o_ref[...] = jnp.dot(xn_ref[...], w_ref[...], preferred_element_type=F32).astype(o_ref.dtype)


def norm_matmul(x, g, w, *, tm, tn, out_dtype):
    t, d = x.shape
    n = w.shape[1]
    assert t % tm == 0 and n % tn == 0
    return pl.pallas_call(
        _norm_matmul_kernel,
        grid=(t // tm, n // tn),
        in_specs=[
            pl.BlockSpec((tm, d), lambda i, j: (i, 0)),
            pl.BlockSpec((1, d), lambda i, j: (0, 0)),
            pl.BlockSpec((d, tn), lambda i, j: (0, j)),
        ],
        out_specs=pl.BlockSpec((tm, tn), lambda i, j: (i, j)),
        out_shape=jax.ShapeDtypeStruct((t, n), out_dtype),
        scratch_shapes=[pltpu.VMEM((tm, d), BF16)],
        compiler_params=pltpu.CompilerParams(
            dimension_semantics=("arbitrary", "arbitrary"), vmem_limit_bytes=VMEM_LIMIT),
        name="norm_matmul",
    )(x, g.reshape(1, d), w)


def _merge_kernel(ya_ref, yb_ref, yc_ref, gate_ref, h_ref, wb_ref, wo_ref, o_ref):
    merged = None
    for b, y_ref in enumerate((ya_ref, yb_ref, yc_ref)):
        br = jnp.dot(y_ref[...].astype(BF16), wb_ref[b], preferred_element_type=F32)
        gt = jax.nn.sigmoid(gate_ref[:, b * D_MODEL:(b + 1) * D_MODEL].astype(F32))
        merged = gt * br if merged is None else merged + gt * br
    o_ref[...] = h_ref[...] + jnp.dot(merged.astype(BF16), wo_ref[...], preferred_element_type=F32)


def merge(ya, yb, yc, gate, h, w_branch, w_o, *, tm):
    t = h.shape[0]
    assert t % tm == 0
    row = lambda w: pl.BlockSpec((tm, w), lambda i: (i, 0))
    return pl.pallas_call(
        _merge_kernel,
        grid=(t // tm,),
        in_specs=[row(MIX_WIDTH), row(MIX_WIDTH), row(MIX_WIDTH), row(GATE_COLS), row(D_MODEL),
                  pl.BlockSpec((N_BRANCH, MIX_WIDTH, D_MODEL), lambda i: (0, 0, 0)),
                  pl.BlockSpec((D_MODEL, D_MODEL), lambda i: (0, 0))],
        out_specs=row(D_MODEL),
        out_shape=jax.ShapeDtypeStruct((t, D_MODEL), F32),
        compiler_params=pltpu.CompilerParams(
            dimension_semantics=("arbitrary",), vmem_limit_bytes=VMEM_LIMIT),
        name="merge",
    )(ya, yb, yc, gate, h, w_branch, w_o)


def _shift_rows(cur, prev_tail, shift):
    rolled = pltpu.roll(cur, shift, 0)
    tail = pltpu.roll(prev_tail, shift, 0)
    row = lax.broadcasted_iota(jnp.int32, tail.shape, 0)
    top = jnp.where(row < shift, tail, rolled[:SUBLANE])
    return jnp.concatenate([top, rolled[SUBLANE:]], axis=0)


def _conv_glu_kernel(h_ref, g_ref, wug_ref, wuv_ref, cwg_ref, cwv_ref, cbg_ref, cbv_ref, wd_ref, *rest,
                     final_norm):
    if final_norm:
        gf_ref, o_ref, xn_ref, acc_ref, carry_ref = rest
    else:
        o_ref, xn_ref, acc_ref, carry_ref = rest
    ti = pl.program_id(1)
    fi = pl.program_id(2)
    nf = pl.num_programs(2)

    @pl.when(fi == 0)
    def _():
        xn_ref[...] = _rms(h_ref[0], g_ref[...]).astype(BF16)
        acc_ref[...] = jnp.zeros_like(acc_ref)

    @pl.when(ti == 0)
    def _():
        carry_ref[fi] = jnp.zeros(carry_ref.shape[1:], F32)

    xn = xn_ref[...]

    def conv(w_ref, cw_ref, cb_ref, slot):
        up = jnp.dot(xn, w_ref[...], preferred_element_type=F32)
        tail = carry_ref[fi, slot]
        c = (cb_ref[...] + cw_ref[2:3, :] * up + cw_ref[1:2, :] * _shift_rows(up, tail, 1)
             + cw_ref[0:1, :] * _shift_rows(up, tail, 2))
        carry_ref[fi, slot] = up[up.shape[0] - SUBLANE:]
        return c

    cg = conv(wug_ref, cwg_ref, cbg_ref, 0)
    cv = conv(wuv_ref, cwv_ref, cbv_ref, 1)
    act = (cg * jax.nn.sigmoid(cg) * cv).astype(BF16)
    acc_ref[...] += jnp.dot(act, wd_ref[...], preferred_element_type=F32)

    @pl.when(fi == nf - 1)
    def _():
        out = h_ref[0] + acc_ref[...]
        if final_norm:
            out = _rms(out, gf_ref[...])
        o_ref[0] = out


def conv_glu(h, g, w_up, conv_w, conv_b, w_down, gf, *, tm, tf):
    bsz, seq, d = h.shape
    assert seq % tm == 0 and D_FF % tf == 0
    nf = D_FF // tf
    final_norm = gf is not None
    vec = lambda: pl.BlockSpec((1, d), lambda b, t, f: (0, 0))
    in_specs = [
        pl.BlockSpec((1, tm, d), lambda b, t, f: (b, t, 0)),
        vec(),
        pl.BlockSpec((d, tf), lambda b, t, f: (0, f)),
        pl.BlockSpec((d, tf), lambda b, t, f: (0, f + nf)),
        pl.BlockSpec((CONV_WIDTH, tf), lambda b, t, f: (0, f)),
        pl.BlockSpec((CONV_WIDTH, tf), lambda b, t, f: (0, f + nf)),
        pl.BlockSpec((1, tf), lambda b, t, f: (0, f)),
        pl.BlockSpec((1, tf), lambda b, t, f: (0, f + nf)),
        pl.BlockSpec((tf, d), lambda b, t, f: (f, 0)),
    ]
    args = [h, g.reshape(1, d), w_up, w_up, conv_w, conv_w, conv_b.reshape(1, -1), conv_b.reshape(1, -1), w_down]
    if final_norm:
        in_specs.append(vec())
        args.append(gf.reshape(1, d))
    return pl.pallas_call(
        functools.partial(_conv_glu_kernel, final_norm=final_norm),
        grid=(bsz, seq // tm, nf),
        in_specs=in_specs,
        out_specs=pl.BlockSpec((1, tm, d), lambda b, t, f: (b, t, 0)),
        out_shape=jax.ShapeDtypeStruct((bsz, seq, d), F32),
        scratch_shapes=[pltpu.VMEM((tm, d), BF16), pltpu.VMEM((tm, d), F32),
                        pltpu.VMEM((nf, 2, SUBLANE, tf), F32)],
        compiler_params=pltpu.CompilerParams(
            dimension_semantics=("arbitrary", "arbitrary", "arbitrary"), vmem_limit_bytes=VMEM_LIMIT),
        name="conv_glu",
    )(*args)


def _split(t, sizes):
    out, start = [], 0
    for s in sizes:
        out.append(t[..., start:start + s])
        start += s
    return out


def _mm(x, y):
    return jnp.einsum('...ij,...jk->...ik', x.astype(BF16), y.astype(BF16), preferred_element_type=F32)


def _mm_hi(x, y):
    return jnp.einsum('...ij,...jk->...ik', x, y, precision=lax.Precision.HIGHEST)


def _rwkv7_chunked(r, logw, k, v, a, b, C=64):
    B, S, H, N = r.shape
    nc = S // C

    def ch(t):
        return t.reshape(B, nc, C, H, N).transpose(1, 0, 3, 2, 4)
    r, logw, k, v, a, b = map(ch, (r, logw, k, v, a, b))
    cum = jnp.cumsum(logw, axis=-2)
    cum_prev = cum - logw
    cum_end = cum[..., -1:, :]
    At = a * jnp.exp(cum_prev)
    Rt = r * jnp.exp(cum)
    Bt = b * jnp.exp(-cum)
    Kt = k * jnp.exp(-cum)
    Bh = b * jnp.exp(cum_end - cum)
    Kh = k * jnp.exp(cum_end - cum)
    Pc = jnp.exp(cum_end)[..., 0, :]
    T_ = lambda x: jnp.swapaxes(x, -1, -2)
    ti = jnp.arange(C)
    strict = ti[:, None] > ti[None, :]
    incl = ti[:, None] >= ti[None, :]
    Aab = jnp.where(strict, _mm(At, T_(Bt)), 0.0)
    Aak = jnp.where(strict, _mm(At, T_(Kt)), 0.0)
    Arb = jnp.where(incl, _mm(Rt, T_(Bt)), 0.0)
    Ark = jnp.where(incl, _mm(Rt, T_(Kt)), 0.0)
    Tm = jnp.eye(C, dtype=F32) + Aab
    Pw = Aab
    n = 1
    while 2 * n < C:
        Pw = _mm_hi(Pw, Pw)
        Tm = Tm + _mm_hi(Tm, Pw)
        n *= 2
    Abar = _mm(Tm, At)
    U0 = _mm(Tm, _mm(Aak, v))
    Rbar = Rt + _mm(Arb, Abar)
    Y0 = _mm(Arb, U0) + _mm(Ark, v)
    M = Pc[..., :, None] * jnp.eye(N, dtype=F32) + _mm(T_(Bh), Abar)
    Hd = _mm(T_(Bh), U0) + _mm(T_(Kh), v)

    def step(Hs, inp):
        Rbar_c, Y0_c, M_c, Hd_c = inp
        y = _mm(Rbar_c, Hs) + Y0_c
        Hs = _mm_hi(M_c, Hs) + Hd_c
        return Hs, y
    _, ys = lax.scan(step, jnp.zeros((B, H, N, N), F32), (Rbar, Y0, M, Hd))
    return ys.transpose(1, 0, 3, 2, 4).reshape(B, S, H, N)


def _token_shift(p):
    return jnp.pad(p, ((0, 0), (1, 0), (0, 0)))[:, :-1]


def _rwkv7_branch(p, mu, w0, w_up, a0, a_up, g_up, k_k, k_a, r_k, gn_g, gn_b):
    bsz, seq, _ = p.shape
    p = p + (_token_shift(p) - p) * mu
    r, k, v, wd, ad, gd = _split(p, (A_WIDTH, A_WIDTH, A_WIDTH, A_RANK_W, A_RANK_A, A_RANK_G))
    w_log = -jax.nn.softplus(-(w0 + _mm(jnp.tanh(wd), w_up))) - 0.5
    logw = -jnp.exp(w_log)
    a = jax.nn.sigmoid(a0 + _mm(ad, a_up))
    g = _mm(jax.nn.sigmoid(gd), g_up)

    def hd(t):
        return t.reshape(bsz, seq, A_HEADS, A_HEAD_DIM)
    kk = hd(k * k_k)
    kk = kk * lax.rsqrt(jnp.maximum(jnp.sum(kk * kk, axis=-1, keepdims=True), 1e-24))
    k_mod = hd(k * (1.0 + (a - 1.0) * k_a))
    r_h, v_h, a_h = hd(r), hd(v), hd(a)
    y = _rwkv7_chunked(r_h, hd(logw), k_mod, v_h, -kk, kk * a_h)
    mean = jnp.mean(y, axis=-1, keepdims=True)
    var = jnp.mean(jnp.square(y - mean), axis=-1, keepdims=True)
    y = ((y - mean) * lax.rsqrt(var + A_GN_EPS)).reshape(bsz, seq, A_WIDTH)
    y = y * gn_g + gn_b
    bonus = jnp.sum(r_h * k_mod * r_k, axis=-1, keepdims=True) * v_h
    y = y + bonus.reshape(bsz, seq, A_WIDTH)
    return y * g


def _hgrn2_sub(q, k, g, v, C=64, SB=16):
    B, S, H, dk = q.shape
    dv = v.shape[-1]
    nc = S // C

    def ch(t):
        return t.reshape(B, nc, C, H, t.shape[-1]).transpose(1, 0, 3, 2, 4)
    q, k, g, v = map(ch, (q, k, g, v))
    b = jnp.cumsum(g, axis=-2)
    nsb = C // SB
    T_ = lambda x: jnp.swapaxes(x, -1, -2)

    def step(state, inp):
        qc, kc, bc, vc = inp
        rows = []
        for I in range(nsb):
            sl = slice(I * SB, (I + 1) * SB)
            qI, bI = qc[..., sl, :], bc[..., sl, :]
            b0 = bc[..., I * SB - 1:I * SB, :] if I > 0 else jnp.zeros_like(bc[..., :1, :])
            qt = qI * jnp.exp(bI - b0)
            blocks = []
            if I > 0:
                kt = kc[..., :I * SB, :] * jnp.exp(b0 - bc[..., :I * SB, :])
                blocks.append(_mm(qt, T_(kt)))
            diff = bI[..., :, None, :] - bI[..., None, :, :]
            tri = jnp.tril(jnp.ones((SB, SB), bool))
            dec = jnp.exp(jnp.where(tri[:, :, None], diff, -jnp.inf))
            diag = jnp.sum(qI[..., :, None, :] * kc[..., sl, :][..., None, :, :] * dec, axis=-1)
            blocks.append(diag)
            if I < nsb - 1:
                blocks.append(jnp.zeros(diag.shape[:-1] + (C - (I + 1) * SB,), F32))
            rows.append(jnp.concatenate(blocks, axis=-1))
        scores = jnp.concatenate(rows, axis=-2)
        o = _mm(scores, vc) + _mm(qc * jnp.exp(bc), state)
        b_end = bc[..., -1:, :]
        state = state * T_(jnp.exp(b_end)) + _mm(T_(kc * jnp.exp(b_end - bc)), vc)
        return state, o
    _, o = lax.scan(step, jnp.zeros((B, H, dk, dv), F32), (q, k, b, v))
    return o.transpose(1, 0, 3, 2, 4).reshape(B, S, H, dv)


def _hgrn2_branch(p, lb, gn_g):
    bsz, seq, _ = p.shape
    q, f, i, g = _split(p, (B_WIDTH,) * 4)
    log_f = jnp.logaddexp(jnp.log(lb), jnp.log1p(-lb) + jax.nn.log_sigmoid(f))
    k_in = (1.0 - lb) * jax.nn.sigmoid(-f)
    hd = lambda t: t.reshape(bsz, seq, B_HEADS, -1)
    o = _hgrn2_sub(hd(q), hd(k_in), hd(log_f), hd(i))
    o = o * lax.rsqrt(jnp.mean(o * o, axis=-1, keepdims=True) + NORM_EPS)
    return o.reshape(bsz, seq, B_WIDTH) * gn_g * jax.nn.silu(g)


def _rope_tables(seq, dim):
    half = dim // 2
    inv_freq = ROPE_THETA ** (-jnp.arange(half, dtype=F32) * (2.0 / dim))
    ang = jnp.arange(seq, dtype=F32)[:, None] * inv_freq[None, :]
    return jnp.cos(ang), jnp.sin(ang)


def _apply_rope(t, cos, sin):
    half = t.shape[-1] // 2
    t1, t2 = t[..., :half], t[..., half:]
    c, s = cos[None, :, None, :], sin[None, :, None, :]
    return jnp.concatenate([t1 * c - t2 * s, t2 * c + t1 * s], axis=-1)


def _dsa_branch(p, cos, sin):
    bsz, seq, _ = p.shape
    k_sel = min(TOPK_MAX, seq // 4)
    kv_w = C_KV_HEADS * C_HEAD_DIM
    q, k, v, qi, ki, wi = _split(p[..., :C_COLS], (C_WIDTH, kv_w, kv_w, IDX_HEADS * IDX_DIM, IDX_DIM, IDX_HEADS))
    q = _apply_rope(q.reshape(bsz, seq, C_HEADS, C_HEAD_DIM), cos, sin)
    k = _apply_rope(k.reshape(bsz, seq, C_KV_HEADS, C_HEAD_DIM), cos, sin)
    v = v.reshape(bsz, seq, C_KV_HEADS, C_HEAD_DIM)
    qi = _apply_rope(qi.reshape(bsz, seq, IDX_HEADS, IDX_DIM), cos, sin)
    ki = _apply_rope(ki[:, :, None, :], cos, sin)[:, :, 0, :]
    wi = wi * (IDX_HEADS ** -0.5) * (IDX_DIM ** -0.5)
    dots = jnp.einsum('bqhd,bsd->bqhs', qi, ki, precision=lax.Precision.HIGHEST)
    score = jnp.sum(jax.nn.relu(dots) * wi[..., None], axis=2)
    pos = jnp.arange(seq)
    causal = pos[None, :] <= pos[:, None]
    bits = lax.bitcast_convert_type(score + 0.0, jnp.int32)
    key = jnp.where(bits < 0, bits ^ jnp.int32(0x7fffffff), bits)
    key = jnp.where(causal[None], key, jnp.int32(-2 ** 31))
    ukey = lax.bitcast_convert_type(key ^ jnp.int32(-2 ** 31), jnp.uint32)
    cand = jnp.zeros((bsz, seq, 1), jnp.uint32)
    for bit in range(31, -1, -1):
        trial = cand | jnp.uint32(1 << bit)
        cnt = jnp.sum((ukey >= trial).astype(jnp.int32), axis=-1, keepdims=True)
        cand = jnp.where(cnt >= k_sel, trial, cand)
    gt = ukey > cand
    eq = ukey == cand
    need = k_sel - jnp.sum(gt.astype(jnp.int32), axis=-1, keepdims=True)
    J = jnp.zeros((bsz, seq, 1), jnp.int32)
    for bit in range(int(np.ceil(np.log2(seq))), -1, -1):
        trial = J | (1 << bit)
        cnt = jnp.sum((eq & (pos[None, None, :] < trial)).astype(jnp.int32), axis=-1, keepdims=True)
        J = jnp.where(cnt <= need, trial, J)
    sel = (gt | (eq & (pos[None, None, :] < J))) & causal[None]
    group = C_HEADS // C_KV_HEADS
    qg = q.reshape(bsz, seq, C_KV_HEADS, group, C_HEAD_DIM)
    logits = jnp.einsum('bqcgd,bscd->bcgqs', qg.astype(BF16), k.astype(BF16),
                        preferred_element_type=F32) * (C_HEAD_DIM ** -0.5)
    logits = jnp.where(sel[:, None, None], logits, -jnp.inf)
    prob = jax.nn.softmax(logits, axis=-1)
    out = jnp.einsum('bcgqs,bscd->bqcgd', prob.astype(BF16), v.astype(BF16), preferred_element_type=F32)
    return out.reshape(bsz, seq, C_WIDTH)


C_PAD = _round_up(C_COLS, LANE)


def kernel(x, norm_mix_g, w_in, a_mu, a_w0, a_w_up, a_a0, a_a_up, a_g_up, a_k_k, a_k_a, a_r_k, a_gn_g,
           a_gn_b, b_lb_logits, b_gn_g, w_branch, w_o, norm_ffn_g, w_up, conv_w, conv_b, w_down,
           norm_final_g):
    bsz, seq, d = x.shape
    t = bsz * seq
    cos, sin = _rope_tables(seq, C_HEAD_DIM)
    lb_cum = jnp.cumsum(jax.nn.softmax(b_lb_logits.astype(F32), axis=0), axis=0)
    lower_bounds = lb_cum - lb_cum[0]
    h = x
    for layer in range(DEPTH):
        w = w_in[layer].astype(BF16)
        w_a, w_b, w_c, w_g = _split(w, (A_COLS, B_COLS, C_COLS, GATE_COLS))
        w_c = jnp.pad(w_c, ((0, 0), (0, C_PAD - C_COLS)))
        h2 = h.reshape(t, d)
        g = norm_mix_g[layer]
        p_a = norm_matmul(h2, g, w_a, tm=1024, tn=896, out_dtype=F32).reshape(bsz, seq, -1)
        p_b = norm_matmul(h2, g, w_b, tm=1024, tn=1024, out_dtype=F32).reshape(bsz, seq, -1)
        p_c = norm_matmul(h2, g, w_c, tm=1024, tn=C_PAD, out_dtype=F32).reshape(bsz, seq, -1)
        p_g = norm_matmul(h2, g, w_g, tm=1024, tn=1024, out_dtype=BF16)
        y_a = _rwkv7_branch(p_a, a_mu[layer], a_w0[layer], a_w_up[layer], a_a0[layer], a_a_up[layer],
                            a_g_up[layer], a_k_k[layer], a_k_a[layer], a_r_k[layer], a_gn_g[layer],
                            a_gn_b[layer])
        y_b = _hgrn2_branch(p_b, lower_bounds[layer], b_gn_g[layer])
        y_c = _dsa_branch(p_c, cos, sin)
        h2 = merge(y_a.reshape(t, -1), y_b.reshape(t, -1), y_c.reshape(t, -1), p_g, h2,
                   w_branch[layer].astype(BF16), w_o[layer].astype(BF16), tm=512)
        h = conv_glu(h2.reshape(bsz, seq, d), norm_ffn_g[layer], w_up[layer].astype(BF16), conv_w[layer],
                     conv_b[layer], w_down[layer].astype(BF16),
                     norm_final_g if layer == DEPTH - 1 else None, tm=512, tf=1408)
    return h
```

```python
import functools

import jax
import jax.numpy as jnp
import numpy as np
from jax import lax
from jax.experimental import pallas as pl
from jax.experimental.pallas import tpu as pltpu

F32 = jnp.float32
BF16 = jnp.bfloat16

D_MODEL = 1024
DEPTH = 2
MIX_WIDTH = 512
N_BRANCH = 3
A_HEAD_DIM = 64
A_HEADS = 8
A_WIDTH = 512
A_RANK_W = 64
A_RANK_A = 64
A_RANK_G = 128
A_GN_EPS = 64e-5
B_KEY_DIM = 128
B_VAL_DIM = 128
B_HEADS = 4
B_WIDTH = 512
C_HEAD_DIM = 64
C_HEADS = 8
C_KV_HEADS = 2
C_WIDTH = 512
IDX_HEADS = 4
IDX_DIM = 64
TOPK_MAX = 256
ROPE_THETA = 10000.0
D_FF = 2816
CONV_WIDTH = 3
NORM_EPS = 1e-6

A_COLS = 3 * A_WIDTH + A_RANK_W + A_RANK_A + A_RANK_G
B_COLS = 4 * B_WIDTH
C_COLS = C_WIDTH + 2 * C_KV_HEADS * C_HEAD_DIM + IDX_HEADS * IDX_DIM + IDX_DIM + IDX_HEADS
GATE_COLS = N_BRANCH * D_MODEL

LANE = 128
SUBLANE = 8
VMEM_LIMIT = 48 * 1024 * 1024


def _round_up(n, m):
    return (n + m - 1) // m * m


def _rms(x, g):
    return x * lax.rsqrt(jnp.mean(x * x, axis=-1, keepdims=True) + NORM_EPS) * g


def _norm_matmul_kernel(x_ref, g_ref, w_ref, o_ref, xn_ref):
    @pl.when(pl.program_id(1) == 0)
    def _():
        xn_ref[...] = _rms(x_ref[...], g_ref[...]).astype(BF16)

    o_ref[...] = jnp.dot(xn_ref[...], w_ref[...], preferred_element_type=F32).astype(o_ref.dtype)


def norm_matmul(x, g, w, *, tm, tn, out_dtype):
    t, d = x.shape
    n = w.shape[1]
    assert t % tm == 0 and n % tn == 0
    return pl.pallas_call(
        _norm_matmul_kernel,
        grid=(t // tm, n // tn),
        in_specs=[
            pl.BlockSpec((tm, d), lambda i, j: (i, 0)),
            pl.BlockSpec((1, d), lambda i, j: (0, 0)),
            pl.BlockSpec((d, tn), lambda i, j: (0, j)),
        ],
        out_specs=pl.BlockSpec((tm, tn), lambda i, j: (i, j)),
        out_shape=jax.ShapeDtypeStruct((t, n), out_dtype),
        scratch_shapes=[pltpu.VMEM((tm, d), BF16)],
        compiler_params=pltpu.CompilerParams(
            dimension_semantics=("arbitrary", "arbitrary"), vmem_limit_bytes=VMEM_LIMIT),
        name="norm_matmul",
    )(x, g.reshape(1, d), w)


def _merge_kernel(ya_ref, yb_ref, yc_ref, gate_ref, h_ref, wa_ref, wb_ref, wc_ref, wo_ref, o_ref):
    merged = None
    for b, (y_ref, w_ref) in enumerate(((ya_ref, wa_ref), (yb_ref, wb_ref), (yc_ref, wc_ref))):
        br = jnp.dot(y_ref[...].astype(BF16), w_ref[...], preferred_element_type=F32)
        gt = jax.nn.sigmoid(gate_ref[:, b * D_MODEL:(b + 1) * D_MODEL].astype(F32))
        merged = gt * br if merged is None else merged + gt * br
    o_ref[...] = h_ref[...] + jnp.dot(merged.astype(BF16), wo_ref[...], preferred_element_type=F32)


def merge(ys, gate, h, ws, w_o, *, tm):
    t = h.shape[0]
    assert t % tm == 0
    row = lambda w: pl.BlockSpec((tm, w), lambda i: (i, 0))
    const = lambda a: pl.BlockSpec(a.shape, lambda i: (0, 0))
    return pl.pallas_call(
        _merge_kernel,
        grid=(t // tm,),
        in_specs=[row(y.shape[1]) for y in ys] + [row(GATE_COLS), row(D_MODEL)]
                 + [const(w) for w in ws] + [const(w_o)],
        out_specs=row(D_MODEL),
        out_shape=jax.ShapeDtypeStruct((t, D_MODEL), F32),
        compiler_params=pltpu.CompilerParams(
            dimension_semantics=("arbitrary",), vmem_limit_bytes=VMEM_LIMIT),
        name="merge",
    )(*ys, gate, h, *ws, w_o)


def _shift_rows(cur, prev_tail, shift):
    rolled = pltpu.roll(cur, shift, 0)
    tail = pltpu.roll(prev_tail, shift, 0)
    row = lax.broadcasted_iota(jnp.int32, tail.shape, 0)
    top = jnp.where(row < shift, tail, rolled[:SUBLANE])
    return jnp.concatenate([top, rolled[SUBLANE:]], axis=0)


def _conv_glu_kernel(h_ref, g_ref, wug_ref, wuv_ref, cwg_ref, cwv_ref, cbg_ref, cbv_ref, wd_ref, *rest,
                     final_norm):
    if final_norm:
        gf_ref, o_ref, xn_ref, acc_ref, carry_ref = rest
    else:
        o_ref, xn_ref, acc_ref, carry_ref = rest
    ti = pl.program_id(1)
    fi = pl.program_id(2)
    nf = pl.num_programs(2)

    @pl.when(fi == 0)
    def _():
        xn_ref[...] = _rms(h_ref[0], g_ref[...]).astype(BF16)
        acc_ref[...] = jnp.zeros_like(acc_ref)

    @pl.when(ti == 0)
    def _():
        carry_ref[fi] = jnp.zeros(carry_ref.shape[1:], F32)

    xn = xn_ref[...]

    def conv(w_ref, cw_ref, cb_ref, slot):
        up = jnp.dot(xn, w_ref[...], preferred_element_type=F32)
        tail = carry_ref[fi, slot]
        c = (cb_ref[...] + cw_ref[2:3, :] * up + cw_ref[1:2, :] * _shift_rows(up, tail, 1)
             + cw_ref[0:1, :] * _shift_rows(up, tail, 2))
        carry_ref[fi, slot] = up[up.shape[0] - SUBLANE:]
        return c

    cg = conv(wug_ref, cwg_ref, cbg_ref, 0)
    cv = conv(wuv_ref, cwv_ref, cbv_ref, 1)
    act = (cg * jax.nn.sigmoid(cg) * cv).astype(BF16)
    acc_ref[...] += jnp.dot(act, wd_ref[...], preferred_element_type=F32)

    @pl.when(fi == nf - 1)
    def _():
        out = h_ref[0] + acc_ref[...]
        if final_norm:
            out = _rms(out, gf_ref[...])
        o_ref[0] = out


def conv_glu(h, g, w_up, conv_w, conv_b, w_down, gf, *, tm, tf):
    bsz, seq, d = h.shape
    assert seq % tm == 0 and D_FF % tf == 0
    nf = D_FF // tf
    final_norm = gf is not None
    vec = lambda: pl.BlockSpec((1, d), lambda b, t, f: (0, 0))
    in_specs = [
        pl.BlockSpec((1, tm, d), lambda b, t, f: (b, t, 0)),
        vec(),
        pl.BlockSpec((d, tf), lambda b, t, f: (0, f)),
        pl.BlockSpec((d, tf), lambda b, t, f: (0, f + nf)),
        pl.BlockSpec((CONV_WIDTH, tf), lambda b, t, f: (0, f)),
        pl.BlockSpec((CONV_WIDTH, tf), lambda b, t, f: (0, f + nf)),
        pl.BlockSpec((1, tf), lambda b, t, f: (0, f)),
        pl.BlockSpec((1, tf), lambda b, t, f: (0, f + nf)),
        pl.BlockSpec((tf, d), lambda b, t, f: (f, 0)),
    ]
    args = [h, g.reshape(1, d), w_up, w_up, conv_w, conv_w, conv_b.reshape(1, -1), conv_b.reshape(1, -1), w_down]
    if final_norm:
        in_specs.append(vec())
        args.append(gf.reshape(1, d))
    return pl.pallas_call(
        functools.partial(_conv_glu_kernel, final_norm=final_norm),
        grid=(bsz, seq // tm, nf),
        in_specs=in_specs,
        out_specs=pl.BlockSpec((1, tm, d), lambda b, t, f: (b, t, 0)),
        out_shape=jax.ShapeDtypeStruct((bsz, seq, d), F32),
        scratch_shapes=[pltpu.VMEM((tm, d), BF16), pltpu.VMEM((tm, d), F32),
                        pltpu.VMEM((nf, 2, SUBLANE, tf), F32)],
        compiler_params=pltpu.CompilerParams(
            dimension_semantics=("arbitrary", "arbitrary", "arbitrary"), vmem_limit_bytes=VMEM_LIMIT),
        name="conv_glu",
    )(*args)


HP = LANE
A_PADW = A_HEADS * HP
A_CHUNK = 64
A_LOW = A_RANK_W + A_RANK_A + A_RANK_G
A_PCOLS = 3 * A_PADW + A_LOW


def _dot(x, y):
    return jnp.dot(x.astype(BF16), y.astype(BF16), preferred_element_type=F32)


def _dot_t(x, y):
    return lax.dot_general(x.astype(BF16), y.astype(BF16), (((1,), (1,)), ((), ())), preferred_element_type=F32)


def _rwkv_kernel(p_ref, mu_ref, w0_ref, a0_ref, wa_ref, gup_ref, kk_ref, ka_ref, rk_ref, gng_ref, gnb_ref,
                 o_ref, carry_ref, h_ref):
    C = A_CHUNK
    W = A_PADW

    @pl.when(pl.program_id(1) == 0)
    def _():
        carry_ref[...] = jnp.zeros_like(carry_ref)
        h_ref[...] = jnp.zeros_like(h_ref)

    p = p_ref[0]
    prev = _shift_rows(p, carry_ref[...], 1)
    carry_ref[...] = p[C - SUBLANE:]
    p = p + (prev - p) * mu_ref[...]
    r, k, v = p[:, :W], p[:, W:2 * W], p[:, 2 * W:3 * W]
    low = p[:, 3 * W:3 * W + LANE]
    gd = p[:, 3 * W + LANE:]
    lane = lax.broadcasted_iota(jnp.int32, (C, LANE), 1)
    low = jnp.where(lane < A_RANK_W, jnp.tanh(low), low)
    wa = _dot(low, wa_ref[...])
    w_log = -jax.nn.softplus(-(w0_ref[...] + wa[:, :W])) - 0.5
    logw = -jnp.exp(w_log)
    a = jax.nn.sigmoid(a0_ref[...] + wa[:, W:])
    g = _dot(jax.nn.sigmoid(gd), gup_ref[...])
    k_mod = k * (1.0 + (a - 1.0) * ka_ref[...])
    kk = k * kk_ref[...]

    row = lax.broadcasted_iota(jnp.int32, (C, C), 0)
    col = lax.broadcasted_iota(jnp.int32, (C, C), 1)
    incl = row >= col
    strict = row > col
    cum = jnp.dot(incl.astype(F32), logw, precision=lax.Precision.HIGHEST, preferred_element_type=F32)
    cum_end = cum[C - 1:C, :]
    e_neg = jnp.exp(-cum)
    e_end = jnp.exp(cum_end - cum)
    r_t = r * jnp.exp(cum)
    a_scale = jnp.exp(cum - logw)
    p_end = jnp.exp(cum_end)
    eye = (lax.broadcasted_iota(jnp.int32, (HP, HP), 0) == lax.broadcasted_iota(jnp.int32, (HP, HP), 1)).astype(F32)
    eye_c = (row == col).astype(F32)
    valid = lane < A_HEAD_DIM

    heads = range(A_HEADS)
    sls = [slice(h * HP, (h + 1) * HP) for h in heads]
    each = lambda f: [f(h) for h in heads]
    kk_n = each(lambda h: kk[:, sls[h]] * lax.rsqrt(jnp.maximum(
        jnp.sum(kk[:, sls[h]] * kk[:, sls[h]], axis=-1, keepdims=True), 1e-24)))
    b_h = each(lambda h: kk_n[h] * a[:, sls[h]])
    at = each(lambda h: -kk_n[h] * a_scale[:, sls[h]])
    ar = each(lambda h: jnp.concatenate([at[h], r_t[:, sls[h]]], axis=0))
    ab = each(lambda h: _dot_t(ar[h], b_h[h] * e_neg[:, sls[h]]))
    ak = each(lambda h: _dot_t(ar[h], k_mod[:, sls[h]] * e_neg[:, sls[h]]))
    a_ab = each(lambda h: jnp.where(strict, ab[h][:C], 0.0))
    a_rb = each(lambda h: jnp.where(incl, ab[h][C:], 0.0))
    a_ak = each(lambda h: jnp.where(strict, ak[h][:C], 0.0))
    a_rk = each(lambda h: jnp.where(incl, ak[h][C:], 0.0))
    tm = each(lambda h: eye_c + a_ab[h])
    pw = each(lambda h: _dot(a_ab[h], a_ab[h]))
    akv = each(lambda h: _dot(a_ak[h], v[:, sls[h]]))
    n = 2
    while 2 * n < C:
        x = each(lambda h: _dot(jnp.concatenate([tm[h], pw[h]], axis=0), pw[h]))
        tm = each(lambda h: tm[h] + x[h][:C])
        pw = each(lambda h: x[h][C:])
        n *= 2
    tm = each(lambda h: tm[h] + _dot(tm[h], pw[h]))
    x2 = each(lambda h: _dot(tm[h], jnp.concatenate([at[h], akv[h]], axis=1)))
    x3 = each(lambda h: _dot(a_rb[h], x2[h]))
    x4 = each(lambda h: _dot((b_h[h] * e_end[:, sls[h]]).T, x2[h]))
    rkv = each(lambda h: _dot(a_rk[h], v[:, sls[h]]))
    kv = each(lambda h: _dot((k_mod[:, sls[h]] * e_end[:, sls[h]]).T, v[:, sls[h]]))
    rm = each(lambda h: jnp.concatenate([r_t[:, sls[h]] + x3[h][:, :HP],
                                         eye * p_end[:, sls[h]] + x4[h][:, :HP]], axis=0))
    x5 = each(lambda h: _dot(rm[h], h_ref[h]))
    outs = []
    for h in heads:
        sl = sls[h]
        h_ref[h] = x5[h][C:] + x4[h][:, HP:] + kv[h]
        y = x5[h][:C] + x3[h][:, HP:] + rkv[h]
        mean = jnp.sum(y, axis=-1, keepdims=True) * (1.0 / A_HEAD_DIM)
        dlt = jnp.where(valid, y - mean, 0.0)
        var = jnp.sum(dlt * dlt, axis=-1, keepdims=True) * (1.0 / A_HEAD_DIM)
        yn = dlt * lax.rsqrt(var + A_GN_EPS) * gng_ref[:, sl] + gnb_ref[:, sl]
        bonus = jnp.sum(r[:, sl] * k_mod[:, sl] * rk_ref[:, sl], axis=-1, keepdims=True) * v[:, sl]
        outs.append((yn + bonus) * g[:, sl])
    o_ref[0] = jnp.concatenate(outs, axis=1).astype(o_ref.dtype)


def _pad_heads(t, heads):
    hd = t.shape[-1] // heads
    t = t.reshape(t.shape[:-1] + (heads, hd))
    t = jnp.pad(t, [(0, 0)] * (t.ndim - 1) + [(0, HP - hd)])
    return t.reshape(t.shape[:-2] + (heads * HP,))


def rwkv7_pad_in_weight(w_a):
    r, k, v, low = w_a[:, :A_WIDTH], w_a[:, A_WIDTH:2 * A_WIDTH], w_a[:, 2 * A_WIDTH:3 * A_WIDTH], w_a[:, 3 * A_WIDTH:]
    return jnp.concatenate([_pad_heads(r, A_HEADS), _pad_heads(k, A_HEADS), _pad_heads(v, A_HEADS), low], axis=1)


def rwkv7(p, mu, w0, w_up, a0, a_up, g_up, k_k, k_a, r_k, gn_g, gn_b):
    bsz, seq, _ = p.shape
    C = A_CHUNK
    assert seq % C == 0
    mu3 = jnp.concatenate([_pad_heads(mu[i * A_WIDTH:(i + 1) * A_WIDTH], A_HEADS) for i in range(3)]
                          + [mu[3 * A_WIDTH:]]).reshape(1, -1)
    z = jnp.zeros((A_RANK_W, A_PADW), F32)
    wa = jnp.concatenate([jnp.concatenate([_pad_heads(w_up, A_HEADS), z], axis=1),
                          jnp.concatenate([z, _pad_heads(a_up, A_HEADS)], axis=1)], axis=0).astype(BF16)
    vecs = [_pad_heads(t.reshape(-1), A_HEADS).reshape(1, -1) for t in (k_k, k_a, r_k, gn_g, gn_b)]
    full = lambda shape: pl.BlockSpec(shape, lambda b, c: (0,) * len(shape))
    return pl.pallas_call(
        _rwkv_kernel,
        grid=(bsz, seq // C),
        in_specs=[pl.BlockSpec((1, C, A_PCOLS), lambda b, c: (b, c, 0)),
                  full((1, A_PCOLS)), full((1, A_PADW)), full((1, A_PADW)),
                  full((LANE, 2 * A_PADW)), full((A_RANK_G, A_PADW))] + [full((1, A_PADW))] * 5,
        out_specs=pl.BlockSpec((1, C, A_PADW), lambda b, c: (b, c, 0)),
        out_shape=jax.ShapeDtypeStruct((bsz, seq, A_PADW), BF16),
        scratch_shapes=[pltpu.VMEM((SUBLANE, A_PCOLS), F32), pltpu.VMEM((A_HEADS, HP, HP), F32)],
        compiler_params=pltpu.CompilerParams(
            dimension_semantics=("arbitrary", "arbitrary"), vmem_limit_bytes=VMEM_LIMIT),
        name="rwkv7",
    )(p, mu3, _pad_heads(w0, A_HEADS).reshape(1, -1), _pad_heads(a0, A_HEADS).reshape(1, -1), wa,
      _pad_heads(g_up, A_HEADS).astype(BF16), *vecs)


B_CHUNK = 64
B_SUB = 16


def _dot_tl(x, y):
    return lax.dot_general(x.astype(BF16), y.astype(BF16), (((0,), (0,)), ((), ())), preferred_element_type=F32)


def _hgrn_kernel(p_ref, lb_ref, gn_ref, o_ref, st_ref):
    C, SB, W = B_CHUNK, B_SUB, B_WIDTH
    nsb = C // SB
    D = B_KEY_DIM

    @pl.when(pl.program_id(1) == 0)
    def _():
        st_ref[...] = jnp.zeros_like(st_ref)

    p = p_ref[0]
    q, f, val, gate = p[:, :W], p[:, W:2 * W], p[:, 2 * W:3 * W], p[:, 3 * W:]
    lb = lb_ref[...]
    log_f = jnp.logaddexp(jnp.log(lb), jnp.log1p(-lb) + jax.nn.log_sigmoid(f))
    k = (1.0 - lb) * jax.nn.sigmoid(-f)
    row = lax.broadcasted_iota(jnp.int32, (C, C), 0)
    col = lax.broadcasted_iota(jnp.int32, (C, C), 1)
    b = jnp.dot((row >= col).astype(F32), log_f, precision=lax.Precision.HIGHEST,
                preferred_element_type=F32)
    b_end = b[C - 1:C, :]
    b0 = jnp.concatenate(
        [jnp.zeros((SB, W), F32)] + [jnp.broadcast_to(b[i * SB - 1:i * SB, :], (SB, W)) for i in range(1, nsb)],
        axis=0)
    qt = q * jnp.exp(b - b0)
    qe = q * jnp.exp(b)
    ke = k * jnp.exp(b_end - b)
    e_end = jnp.exp(b_end)
    rowd = lax.broadcasted_iota(jnp.int32, (C, D), 0)
    lane_c = lax.broadcasted_iota(jnp.int32, (SB, C), 1)
    row_sb = lax.broadcasted_iota(jnp.int32, (SB, C), 0)

    heads = range(B_HEADS)
    sls = [slice(h * D, (h + 1) * D) for h in heads]
    each = lambda fn: [fn(h) for h in heads]

    def off_operands(h):
        kts, qts = [], []
        for i in range(1, nsb):
            b0_i = b[i * SB - 1:i * SB, sls[h]]
            kts.append(jnp.where(rowd < i * SB, k[:, sls[h]] * jnp.exp(jnp.minimum(b0_i - b[:, sls[h]], 0.0)), 0.0))
            qts.append(jnp.where((rowd >= i * SB) & (rowd < (i + 1) * SB), qt[:, sls[h]], 0.0))
        return jnp.concatenate(kts, axis=1), jnp.concatenate(qts, axis=1)
    offs = each(off_operands)
    st_off = each(lambda h: _dot_t(offs[h][0], offs[h][1]))

    def diag_t(h):
        blocks = []
        for i in range(nsb):
            rs = slice(i * SB, (i + 1) * SB)
            k_i, b_i = k[rs, sls[h]], b[rs, sls[h]]
            acc = jnp.zeros((SB, C), F32)
            for j in range(SB):
                t = i * SB + j
                z = k_i * q[t:t + 1, sls[h]] * jnp.exp(jnp.minimum(b[t:t + 1, sls[h]] - b_i, 0.0))
                acc = jnp.where(lane_c == t, jnp.sum(z, axis=-1, keepdims=True), acc)
            blocks.append(jnp.where(row_sb + i * SB <= lane_c, acc, 0.0))
        return jnp.concatenate(blocks, axis=0)
    st_diag = each(diag_t)
    o_intra = each(lambda h: _dot_tl(st_off[h] + st_diag[h], val[:, sls[h]]))
    o_state = each(lambda h: _dot_t(qe[:, sls[h]], st_ref[h]))
    upd = each(lambda h: _dot_tl(val[:, sls[h]], ke[:, sls[h]]))
    outs = []
    for h in heads:
        st_ref[h] = st_ref[h] * e_end[:, sls[h]] + upd[h]
        o = o_intra[h] + o_state[h]
        o = o * lax.rsqrt(jnp.mean(o * o, axis=-1, keepdims=True) + NORM_EPS)
        gt = gate[:, sls[h]]
        outs.append(o * gn_ref[:, sls[h]] * (gt * jax.nn.sigmoid(gt)))
    o_ref[0] = jnp.concatenate(outs, axis=1).astype(o_ref.dtype)


def hgrn2(p, lb, gn_g):
    bsz, seq, _ = p.shape
    C = B_CHUNK
    assert seq % C == 0
    vec = pl.BlockSpec((1, B_WIDTH), lambda b, c: (0, 0))
    return pl.pallas_call(
        _hgrn_kernel,
        grid=(bsz, seq // C),
        in_specs=[pl.BlockSpec((1, C, B_COLS), lambda b, c: (b, c, 0)), vec, vec],
        out_specs=pl.BlockSpec((1, C, B_WIDTH), lambda b, c: (b, c, 0)),
        out_shape=jax.ShapeDtypeStruct((bsz, seq, B_WIDTH), BF16),
        scratch_shapes=[pltpu.VMEM((B_HEADS, B_VAL_DIM, B_KEY_DIM), F32)],
        compiler_params=pltpu.CompilerParams(
            dimension_semantics=("arbitrary", "arbitrary"), vmem_limit_bytes=VMEM_LIMIT),
        name="hgrn2",
    )(p, lb.reshape(1, -1), gn_g.reshape(1, -1))


def _split(t, sizes):
    out, start = [], 0
    for s in sizes:
        out.append(t[..., start:start + s])
        start += s
    return out


def _mm(x, y):
    return jnp.einsum('...ij,...jk->...ik', x.astype(BF16), y.astype(BF16), preferred_element_type=F32)


def _mm_hi(x, y):
    return jnp.einsum('...ij,...jk->...ik', x, y, precision=lax.Precision.HIGHEST)


def _rwkv7_chunked(r, logw, k, v, a, b, C=64):
    B, S, H, N = r.shape
    nc = S // C

    def ch(t):
        return t.reshape(B, nc, C, H, N).transpose(1, 0, 3, 2, 4)
    r, logw, k, v, a, b = map(ch, (r, logw, k, v, a, b))
    cum = jnp.cumsum(logw, axis=-2)
    cum_prev = cum - logw
    cum_end = cum[..., -1:, :]
    At = a * jnp.exp(cum_prev)
    Rt = r * jnp.exp(cum)
    Bt = b * jnp.exp(-cum)
    Kt = k * jnp.exp(-cum)
    Bh = b * jnp.exp(cum_end - cum)
    Kh = k * jnp.exp(cum_end - cum)
    Pc = jnp.exp(cum_end)[..., 0, :]
    T_ = lambda x: jnp.swapaxes(x, -1, -2)
    ti = jnp.arange(C)
    strict = ti[:, None] > ti[None, :]
    incl = ti[:, None] >= ti[None, :]
    Aab = jnp.where(strict, _mm(At, T_(Bt)), 0.0)
    Aak = jnp.where(strict, _mm(At, T_(Kt)), 0.0)
    Arb = jnp.where(incl, _mm(Rt, T_(Bt)), 0.0)
    Ark = jnp.where(incl, _mm(Rt, T_(Kt)), 0.0)
    Tm = jnp.eye(C, dtype=F32) + Aab
    Pw = Aab
    n = 1
    while 2 * n < C:
        Pw = _mm_hi(Pw, Pw)
        Tm = Tm + _mm_hi(Tm, Pw)
        n *= 2
    Abar = _mm(Tm, At)
    U0 = _mm(Tm, _mm(Aak, v))
    Rbar = Rt + _mm(Arb, Abar)
    Y0 = _mm(Arb, U0) + _mm(Ark, v)
    M = Pc[..., :, None] * jnp.eye(N, dtype=F32) + _mm(T_(Bh), Abar)
    Hd = _mm(T_(Bh), U0) + _mm(T_(Kh), v)

    def step(Hs, inp):
        Rbar_c, Y0_c, M_c, Hd_c = inp
        y = _mm(Rbar_c, Hs) + Y0_c
        Hs = _mm_hi(M_c, Hs) + Hd_c
        return Hs, y
    _, ys = lax.scan(step, jnp.zeros((B, H, N, N), F32), (Rbar, Y0, M, Hd))
    return ys.transpose(1, 0, 3, 2, 4).reshape(B, S, H, N)


def _token_shift(p):
    return jnp.pad(p, ((0, 0), (1, 0), (0, 0)))[:, :-1]


def _rwkv7_branch(p, mu, w0, w_up, a0, a_up, g_up, k_k, k_a, r_k, gn_g, gn_b):
    bsz, seq, _ = p.shape
    p = p + (_token_shift(p) - p) * mu
    r, k, v, wd, ad, gd = _split(p, (A_WIDTH, A_WIDTH, A_WIDTH, A_RANK_W, A_RANK_A, A_RANK_G))
    w_log = -jax.nn.softplus(-(w0 + _mm(jnp.tanh(wd), w_up))) - 0.5
    logw = -jnp.exp(w_log)
    a = jax.nn.sigmoid(a0 + _mm(ad, a_up))
    g = _mm(jax.nn.sigmoid(gd), g_up)

    def hd(t):
        return t.reshape(bsz, seq, A_HEADS, A_HEAD_DIM)
    kk = hd(k * k_k)
    kk = kk * lax.rsqrt(jnp.maximum(jnp.sum(kk * kk, axis=-1, keepdims=True), 1e-24))
    k_mod = hd(k * (1.0 + (a - 1.0) * k_a))
    r_h, v_h, a_h = hd(r), hd(v), hd(a)
    y = _rwkv7_chunked(r_h, hd(logw), k_mod, v_h, -kk, kk * a_h)
    mean = jnp.mean(y, axis=-1, keepdims=True)
    var = jnp.mean(jnp.square(y - mean), axis=-1, keepdims=True)
    y = ((y - mean) * lax.rsqrt(var + A_GN_EPS)).reshape(bsz, seq, A_WIDTH)
    y = y * gn_g + gn_b
    bonus = jnp.sum(r_h * k_mod * r_k, axis=-1, keepdims=True) * v_h
    y = y + bonus.reshape(bsz, seq, A_WIDTH)
    return y * g


def _hgrn2_sub(q, k, g, v, C=64, SB=16):
    B, S, H, dk = q.shape
    dv = v.shape[-1]
    nc = S // C

    def ch(t):
        return t.reshape(B, nc, C, H, t.shape[-1]).transpose(1, 0, 3, 2, 4)
    q, k, g, v = map(ch, (q, k, g, v))
    b = jnp.cumsum(g, axis=-2)
    nsb = C // SB
    T_ = lambda x: jnp.swapaxes(x, -1, -2)

    def step(state, inp):
        qc, kc, bc, vc = inp
        rows = []
        for I in range(nsb):
            sl = slice(I * SB, (I + 1) * SB)
            qI, bI = qc[..., sl, :], bc[..., sl, :]
            b0 = bc[..., I * SB - 1:I * SB, :] if I > 0 else jnp.zeros_like(bc[..., :1, :])
            qt = qI * jnp.exp(bI - b0)
            blocks = []
            if I > 0:
                kt = kc[..., :I * SB, :] * jnp.exp(b0 - bc[..., :I * SB, :])
                blocks.append(_mm(qt, T_(kt)))
            diff = bI[..., :, None, :] - bI[..., None, :, :]
            tri = jnp.tril(jnp.ones((SB, SB), bool))
            dec = jnp.exp(jnp.where(tri[:, :, None], diff, -jnp.inf))
            diag = jnp.sum(qI[..., :, None, :] * kc[..., sl, :][..., None, :, :] * dec, axis=-1)
            blocks.append(diag)
            if I < nsb - 1:
                blocks.append(jnp.zeros(diag.shape[:-1] + (C - (I + 1) * SB,), F32))
            rows.append(jnp.concatenate(blocks, axis=-1))
        scores = jnp.concatenate(rows, axis=-2)
        o = _mm(scores, vc) + _mm(qc * jnp.exp(bc), state)
        b_end = bc[..., -1:, :]
        state = state * T_(jnp.exp(b_end)) + _mm(T_(kc * jnp.exp(b_end - bc)), vc)
        return state, o
    _, o = lax.scan(step, jnp.zeros((B, H, dk, dv), F32), (q, k, b, v))
    return o.transpose(1, 0, 3, 2, 4).reshape(B, S, H, dv)


def _hgrn2_branch(p, lb, gn_g):
    bsz, seq, _ = p.shape
    q, f, i, g = _split(p, (B_WIDTH,) * 4)
    log_f = jnp.logaddexp(jnp.log(lb), jnp.log1p(-lb) + jax.nn.log_sigmoid(f))
    k_in = (1.0 - lb) * jax.nn.sigmoid(-f)
    hd = lambda t: t.reshape(bsz, seq, B_HEADS, -1)
    o = _hgrn2_sub(hd(q), hd(k_in), hd(log_f), hd(i))
    o = o * lax.rsqrt(jnp.mean(o * o, axis=-1, keepdims=True) + NORM_EPS)
    return o.reshape(bsz, seq, B_WIDTH) * gn_g * jax.nn.silu(g)


def _rope_tables(seq, dim):
    half = dim // 2
    inv_freq = ROPE_THETA ** (-jnp.arange(half, dtype=F32) * (2.0 / dim))
    ang = jnp.arange(seq, dtype=F32)[:, None] * inv_freq[None, :]
    return jnp.cos(ang), jnp.sin(ang)


def _apply_rope(t, cos, sin):
    half = t.shape[-1] // 2
    t1, t2 = t[..., :half], t[..., half:]
    c, s = cos[None, :, None, :], sin[None, :, None, :]
    return jnp.concatenate([t1 * c - t2 * s, t2 * c + t1 * s], axis=-1)


def _dsa_branch(p, cos, sin):
    bsz, seq, _ = p.shape
    k_sel = min(TOPK_MAX, seq // 4)
    kv_w = C_KV_HEADS * C_HEAD_DIM
    q, k, v, qi, ki, wi = _split(p[..., :C_COLS], (C_WIDTH, kv_w, kv_w, IDX_HEADS * IDX_DIM, IDX_DIM, IDX_HEADS))
    q = _apply_rope(q.reshape(bsz, seq, C_HEADS, C_HEAD_DIM), cos, sin)
    k = _apply_rope(k.reshape(bsz, seq, C_KV_HEADS, C_HEAD_DIM), cos, sin)
    v = v.reshape(bsz, seq, C_KV_HEADS, C_HEAD_DIM)
    qi = _apply_rope(qi.reshape(bsz, seq, IDX_HEADS, IDX_DIM), cos, sin)
    ki = _apply_rope(ki[:, :, None, :], cos, sin)[:, :, 0, :]
    wi = wi * (IDX_HEADS ** -0.5) * (IDX_DIM ** -0.5)
    dots = jnp.einsum('bqhd,bsd->bqhs', qi, ki, precision=lax.Precision.HIGHEST)
    score = jnp.sum(jax.nn.relu(dots) * wi[..., None], axis=2)
    pos = jnp.arange(seq)
    causal = pos[None, :] <= pos[:, None]
    bits = lax.bitcast_convert_type(score + 0.0, jnp.int32)
    key = jnp.where(bits < 0, bits ^ jnp.int32(0x7fffffff), bits)
    key = jnp.where(causal[None], key, jnp.int32(-2 ** 31))
    ukey = lax.bitcast_convert_type(key ^ jnp.int32(-2 ** 31), jnp.uint32)
    cand = jnp.zeros((bsz, seq, 1), jnp.uint32)
    for bit in range(31, -1, -1):
        trial = cand | jnp.uint32(1 << bit)
        cnt = jnp.sum((ukey >= trial).astype(jnp.int32), axis=-1, keepdims=True)
        cand = jnp.where(cnt >= k_sel, trial, cand)
    gt = ukey > cand
    eq = ukey == cand
    need = k_sel - jnp.sum(gt.astype(jnp.int32), axis=-1, keepdims=True)
    J = jnp.zeros((bsz, seq, 1), jnp.int32)
    for bit in range(int(np.ceil(np.log2(seq))), -1, -1):
        trial = J | (1 << bit)
        cnt = jnp.sum((eq & (pos[None, None, :] < trial)).astype(jnp.int32), axis=-1, keepdims=True)
        J = jnp.where(cnt <= need, trial, J)
    sel = (gt | (eq & (pos[None, None, :] < J))) & causal[None]
    group = C_HEADS // C_KV_HEADS
    qg = q.reshape(bsz, seq, C_KV_HEADS, group, C_HEAD_DIM)
    logits = jnp.einsum('bqcgd,bscd->bcgqs', qg.astype(BF16), k.astype(BF16),
                        preferred_element_type=F32) * (C_HEAD_DIM ** -0.5)
    logits = jnp.where(sel[:, None, None], logits, -jnp.inf)
    prob = jax.nn.softmax(logits, axis=-1)
    out = jnp.einsum('bcgqs,bscd->bqcgd', prob.astype(BF16), v.astype(BF16), preferred_element_type=F32)
    return out.reshape(bsz, seq, C_WIDTH)


C_PAD = _round_up(C_COLS, LANE)


def kernel(x, norm_mix_g, w_in, a_mu, a_w0, a_w_up, a_a0, a_a_up, a_g_up, a_k_k, a_k_a, a_r_k, a_gn_g,
           a_gn_b, b_lb_logits, b_gn_g, w_branch, w_o, norm_ffn_g, w_up, conv_w, conv_b, w_down,
           norm_final_g):
    bsz, seq, d = x.shape
    t = bsz * seq
    cos, sin = _rope_tables(seq, C_HEAD_DIM)
    lb_cum = jnp.cumsum(jax.nn.softmax(b_lb_logits.astype(F32), axis=0), axis=0)
    lower_bounds = lb_cum - lb_cum[0]
    h = x
    for layer in range(DEPTH):
        w = w_in[layer].astype(BF16)
        w_a, w_b, w_c, w_g = _split(w, (A_COLS, B_COLS, C_COLS, GATE_COLS))
        w_a = rwkv7_pad_in_weight(w_a)
        w_c = jnp.pad(w_c, ((0, 0), (0, C_PAD - C_COLS)))
        h2 = h.reshape(t, d)
        g = norm_mix_g[layer]
        p_a = norm_matmul(h2, g, w_a, tm=1024, tn=A_PCOLS // 2, out_dtype=F32).reshape(bsz, seq, -1)
        p_b = norm_matmul(h2, g, w_b, tm=1024, tn=1024, out_dtype=F32).reshape(bsz, seq, -1)
        p_c = norm_matmul(h2, g, w_c, tm=1024, tn=C_PAD, out_dtype=F32).reshape(bsz, seq, -1)
        p_g = norm_matmul(h2, g, w_g, tm=1024, tn=1024, out_dtype=BF16)
        y_a = rwkv7(p_a, a_mu[layer], a_w0[layer], a_w_up[layer], a_a0[layer], a_a_up[layer],
                    a_g_up[layer], a_k_k[layer], a_k_a[layer], a_r_k[layer], a_gn_g[layer], a_gn_b[layer])
        y_b = hgrn2(p_b, lower_bounds[layer], b_gn_g[layer])
        y_c = _dsa_branch(p_c, cos, sin)
        wbr = w_branch[layer].astype(BF16)
        wbr_a = jnp.pad(wbr[0].reshape(A_HEADS, A_HEAD_DIM, d), ((0, 0), (0, HP - A_HEAD_DIM), (0, 0)))
        h2 = merge((y_a.reshape(t, -1), y_b.reshape(t, -1), y_c.reshape(t, -1)), p_g, h2,
                   (wbr_a.reshape(A_PADW, d), wbr[1], wbr[2]), w_o[layer].astype(BF16), tm=512)
        h = conv_glu(h2.reshape(bsz, seq, d), norm_ffn_g[layer], w_up[layer].astype(BF16), conv_w[layer],
                     conv_b[layer], w_down[layer].astype(BF16),
                     norm_final_g if layer == DEPTH - 1 else None, tm=512, tf=1408)
    return h
```

```python
import functools

import jax
import jax.numpy as jnp
import numpy as np
from jax import lax
from jax.experimental import pallas as pl
from jax.experimental.pallas import tpu as pltpu

F32 = jnp.float32
BF16 = jnp.bfloat16

D_MODEL = 1024
DEPTH = 2
MIX_WIDTH = 512
N_BRANCH = 3
A_HEAD_DIM = 64
A_HEADS = 8
A_WIDTH = 512
A_RANK_W = 64
A_RANK_A = 64
A_RANK_G = 128
A_GN_EPS = 64e-5
B_KEY_DIM = 128
B_VAL_DIM = 128
B_HEADS = 4
B_WIDTH = 512
C_HEAD_DIM = 64
C_HEADS = 8
C_KV_HEADS = 2
C_WIDTH = 512
IDX_HEADS = 4
IDX_DIM = 64
TOPK_MAX = 256
ROPE_THETA = 10000.0
D_FF = 2816
CONV_WIDTH = 3
NORM_EPS = 1e-6

A_COLS = 3 * A_WIDTH + A_RANK_W + A_RANK_A + A_RANK_G
B_COLS = 4 * B_WIDTH
C_COLS = C_WIDTH + 2 * C_KV_HEADS * C_HEAD_DIM + IDX_HEADS * IDX_DIM + IDX_DIM + IDX_HEADS
GATE_COLS = N_BRANCH * D_MODEL

LANE = 128
SUBLANE = 8
VMEM_LIMIT = 48 * 1024 * 1024


def _round_up(n, m):
    return (n + m - 1) // m * m


def _rms(x, g):
    return x * lax.rsqrt(jnp.mean(x * x, axis=-1, keepdims=True) + NORM_EPS) * g


def _norm_matmul_kernel(x_ref, g_ref, w_ref, o_ref, xn_ref):
    @pl.when(pl.program_id(1) == 0)
    def _():
        xn_ref[...] = _rms(x_ref[...], g_ref[...]).astype(BF16)

    o_ref[...] = jnp.dot(xn_ref[...], w_ref[...], preferred_element_type=F32).astype(o_ref.dtype)


def norm_matmul(x, g, w, *, tm, tn, out_dtype):
    t, d = x.shape
    n = w.shape[1]
    assert t % tm == 0 and n % tn == 0
    return pl.pallas_call(
        _norm_matmul_kernel,
        grid=(t // tm, n // tn),
        in_specs=[
            pl.BlockSpec((tm, d), lambda i, j: (i, 0)),
            pl.BlockSpec((1, d), lambda i, j: (0, 0)),
            pl.BlockSpec((d, tn), lambda i, j: (0, j)),
        ],
        out_specs=pl.BlockSpec((tm, tn), lambda i, j: (i, j)),
        out_shape=jax.ShapeDtypeStruct((t, n), out_dtype),
        scratch_shapes=[pltpu.VMEM((tm, d), BF16)],
        compiler_params=pltpu.CompilerParams(
            dimension_semantics=("arbitrary", "arbitrary"), vmem_limit_bytes=VMEM_LIMIT),
        name="norm_matmul",
    )(x, g.reshape(1, d), w)


def _merge_kernel(ya_ref, yb_ref, yc_ref, gate_ref, h_ref, wa_ref, wb_ref, wc_ref, wo_ref, o_ref):
    merged = None
    for b, (y_ref, w_ref) in enumerate(((ya_ref, wa_ref), (yb_ref, wb_ref), (yc_ref, wc_ref))):
        br = jnp.dot(y_ref[...].astype(BF16), w_ref[...], preferred_element_type=F32)
        gt = jax.nn.sigmoid(gate_ref[:, b * D_MODEL:(b + 1) * D_MODEL].astype(F32))
        merged = gt * br if merged is None else merged + gt * br
    o_ref[...] = h_ref[...] + jnp.dot(merged.astype(BF16), wo_ref[...], preferred_element_type=F32)


def merge(ys, gate, h, ws, w_o, *, tm):
    t = h.shape[0]
    assert t % tm == 0
    row = lambda w: pl.BlockSpec((tm, w), lambda i: (i, 0))
    const = lambda a: pl.BlockSpec(a.shape, lambda i: (0, 0))
    return pl.pallas_call(
        _merge_kernel,
        grid=(t // tm,),
        in_specs=[row(y.shape[1]) for y in ys] + [row(GATE_COLS), row(D_MODEL)]
                 + [const(w) for w in ws] + [const(w_o)],
        out_specs=row(D_MODEL),
        out_shape=jax.ShapeDtypeStruct((t, D_MODEL), F32),
        compiler_params=pltpu.CompilerParams(
            dimension_semantics=("arbitrary",), vmem_limit_bytes=VMEM_LIMIT),
        name="merge",
    )(*ys, gate, h, *ws, w_o)


def _shift_rows(cur, prev_tail, shift):
    rolled = pltpu.roll(cur, shift, 0)
    tail = pltpu.roll(prev_tail, shift, 0)
    row = lax.broadcasted_iota(jnp.int32, tail.shape, 0)
    top = jnp.where(row < shift, tail, rolled[:SUBLANE])
    return jnp.concatenate([top, rolled[SUBLANE:]], axis=0)


def _conv_glu_kernel(h_ref, g_ref, wug_ref, wuv_ref, cwg_ref, cwv_ref, cbg_ref, cbv_ref, wd_ref, *rest,
                     final_norm):
    if final_norm:
        gf_ref, o_ref, xn_ref, acc_ref, carry_ref = rest
    else:
        o_ref, xn_ref, acc_ref, carry_ref = rest
    ti = pl.program_id(1)
    fi = pl.program_id(2)
    nf = pl.num_programs(2)

    @pl.when(fi == 0)
    def _():
        xn_ref[...] = _rms(h_ref[0], g_ref[...]).astype(BF16)
        acc_ref[...] = jnp.zeros_like(acc_ref)

    @pl.when(ti == 0)
    def _():
        carry_ref[fi] = jnp.zeros(carry_ref.shape[1:], F32)

    xn = xn_ref[...]

    def conv(w_ref, cw_ref, cb_ref, slot):
        up = jnp.dot(xn, w_ref[...], preferred_element_type=F32)
        tail = carry_ref[fi, slot]
        c = (cb_ref[...] + cw_ref[2:3, :] * up + cw_ref[1:2, :] * _shift_rows(up, tail, 1)
             + cw_ref[0:1, :] * _shift_rows(up, tail, 2))
        carry_ref[fi, slot] = up[up.shape[0] - SUBLANE:]
        return c

    cg = conv(wug_ref, cwg_ref, cbg_ref, 0)
    cv = conv(wuv_ref, cwv_ref, cbv_ref, 1)
    act = (cg * jax.nn.sigmoid(cg) * cv).astype(BF16)
    acc_ref[...] += jnp.dot(act, wd_ref[...], preferred_element_type=F32)

    @pl.when(fi == nf - 1)
    def _():
        out = h_ref[0] + acc_ref[...]
        if final_norm:
            out = _rms(out, gf_ref[...])
        o_ref[0] = out


def conv_glu(h, g, w_up, conv_w, conv_b, w_down, gf, *, tm, tf):
    bsz, seq, d = h.shape
    assert seq % tm == 0 and D_FF % tf == 0
    nf = D_FF // tf
    final_norm = gf is not None
    vec = lambda: pl.BlockSpec((1, d), lambda b, t, f: (0, 0))
    in_specs = [
        pl.BlockSpec((1, tm, d), lambda b, t, f: (b, t, 0)),
        vec(),
        pl.BlockSpec((d, tf), lambda b, t, f: (0, f)),
        pl.BlockSpec((d, tf), lambda b, t, f: (0, f + nf)),
        pl.BlockSpec((CONV_WIDTH, tf), lambda b, t, f: (0, f)),
        pl.BlockSpec((CONV_WIDTH, tf), lambda b, t, f: (0, f + nf)),
        pl.BlockSpec((1, tf), lambda b, t, f: (0, f)),
        pl.BlockSpec((1, tf), lambda b, t, f: (0, f + nf)),
        pl.BlockSpec((tf, d), lambda b, t, f: (f, 0)),
    ]
    args = [h, g.reshape(1, d), w_up, w_up, conv_w, conv_w, conv_b.reshape(1, -1), conv_b.reshape(1, -1), w_down]
    if final_norm:
        in_specs.append(vec())
        args.append(gf.reshape(1, d))
    return pl.pallas_call(
        functools.partial(_conv_glu_kernel, final_norm=final_norm),
        grid=(bsz, seq // tm, nf),
        in_specs=in_specs,
        out_specs=pl.BlockSpec((1, tm, d), lambda b, t, f: (b, t, 0)),
        out_shape=jax.ShapeDtypeStruct((bsz, seq, d), F32),
        scratch_shapes=[pltpu.VMEM((tm, d), BF16), pltpu.VMEM((tm, d), F32),
                        pltpu.VMEM((nf, 2, SUBLANE, tf), F32)],
        compiler_params=pltpu.CompilerParams(
            dimension_semantics=("arbitrary", "arbitrary", "arbitrary"), vmem_limit_bytes=VMEM_LIMIT),
        name="conv_glu",
    )(*args)


HP = LANE
A_PADW = A_HEADS * HP
A_CHUNK = 64
A_LOW = A_RANK_W + A_RANK_A + A_RANK_G
A_PCOLS = 3 * A_PADW + A_LOW


def _dot(x, y):
    return jnp.dot(x.astype(BF16), y.astype(BF16), preferred_element_type=F32)


def _dot_t(x, y):
    return lax.dot_general(x.astype(BF16), y.astype(BF16), (((1,), (1,)), ((), ())), preferred_element_type=F32)


def _rwkv_kernel(p_ref, mu_ref, w0_ref, a0_ref, wa_ref, gup_ref, kk_ref, ka_ref, rk_ref, gng_ref, gnb_ref,
                 o_ref, carry_ref, h_ref):
    C = A_CHUNK
    W = A_PADW

    @pl.when(pl.program_id(1) == 0)
    def _():
        carry_ref[...] = jnp.zeros_like(carry_ref)
        h_ref[...] = jnp.zeros_like(h_ref)

    p = p_ref[0]
    prev = _shift_rows(p, carry_ref[...], 1)
    carry_ref[...] = p[C - SUBLANE:]
    p = p + (prev - p) * mu_ref[...]
    r, k, v = p[:, :W], p[:, W:2 * W], p[:, 2 * W:3 * W]
    low = p[:, 3 * W:3 * W + LANE]
    gd = p[:, 3 * W + LANE:]
    lane = lax.broadcasted_iota(jnp.int32, (C, LANE), 1)
    low = jnp.where(lane < A_RANK_W, jnp.tanh(low), low)
    wa = _dot(low, wa_ref[...])
    w_log = -jax.nn.softplus(-(w0_ref[...] + wa[:, :W])) - 0.5
    logw = -jnp.exp(w_log)
    a = jax.nn.sigmoid(a0_ref[...] + wa[:, W:])
    g = _dot(jax.nn.sigmoid(gd), gup_ref[...])
    k_mod = k * (1.0 + (a - 1.0) * ka_ref[...])
    kk = k * kk_ref[...]

    row = lax.broadcasted_iota(jnp.int32, (C, C), 0)
    col = lax.broadcasted_iota(jnp.int32, (C, C), 1)
    incl = row >= col
    strict = row > col
    cum = jnp.dot(incl.astype(F32), logw, precision=lax.Precision.HIGHEST, preferred_element_type=F32)
    cum_end = cum[C - 1:C, :]
    e_neg = jnp.exp(-cum)
    e_end = jnp.exp(cum_end - cum)
    r_t = r * jnp.exp(cum)
    a_scale = jnp.exp(cum - logw)
    p_end = jnp.exp(cum_end)
    eye = (lax.broadcasted_iota(jnp.int32, (HP, HP), 0) == lax.broadcasted_iota(jnp.int32, (HP, HP), 1)).astype(F32)
    eye_c = (row == col).astype(F32)
    valid = lane < A_HEAD_DIM

    heads = range(A_HEADS)
    sls = [slice(h * HP, (h + 1) * HP) for h in heads]
    each = lambda f: [f(h) for h in heads]
    kk_n = each(lambda h: kk[:, sls[h]] * lax.rsqrt(jnp.maximum(
        jnp.sum(kk[:, sls[h]] * kk[:, sls[h]], axis=-1, keepdims=True), 1e-24)))
    b_h = each(lambda h: kk_n[h] * a[:, sls[h]])
    at = each(lambda h: -kk_n[h] * a_scale[:, sls[h]])
    ar = each(lambda h: jnp.concatenate([at[h], r_t[:, sls[h]]], axis=0))
    ab = each(lambda h: _dot_t(ar[h], b_h[h] * e_neg[:, sls[h]]))
    ak = each(lambda h: _dot_t(ar[h], k_mod[:, sls[h]] * e_neg[:, sls[h]]))
    a_ab = each(lambda h: jnp.where(strict, ab[h][:C], 0.0))
    a_rb = each(lambda h: jnp.where(incl, ab[h][C:], 0.0))
    a_ak = each(lambda h: jnp.where(strict, ak[h][:C], 0.0))
    a_rk = each(lambda h: jnp.where(incl, ak[h][C:], 0.0))
    tm = each(lambda h: eye_c + a_ab[h])
    pw = each(lambda h: _dot(a_ab[h], a_ab[h]))
    akv = each(lambda h: _dot(a_ak[h], v[:, sls[h]]))
    n = 2
    while 2 * n < C:
        x = each(lambda h: _dot(jnp.concatenate([tm[h], pw[h]], axis=0), pw[h]))
        tm = each(lambda h: tm[h] + x[h][:C])
        pw = each(lambda h: x[h][C:])
        n *= 2
    tm = each(lambda h: tm[h] + _dot(tm[h], pw[h]))
    x2 = each(lambda h: _dot(tm[h], jnp.concatenate([at[h], akv[h]], axis=1)))
    x3 = each(lambda h: _dot(a_rb[h], x2[h]))
    x4 = each(lambda h: _dot((b_h[h] * e_end[:, sls[h]]).T, x2[h]))
    rkv = each(lambda h: _dot(a_rk[h], v[:, sls[h]]))
    kv = each(lambda h: _dot((k_mod[:, sls[h]] * e_end[:, sls[h]]).T, v[:, sls[h]]))
    rm = each(lambda h: jnp.concatenate([r_t[:, sls[h]] + x3[h][:, :HP],
                                         eye * p_end[:, sls[h]] + x4[h][:, :HP]], axis=0))
    x5 = each(lambda h: _dot(rm[h], h_ref[h]))
    outs = []
    for h in heads:
        sl = sls[h]
        h_ref[h] = x5[h][C:] + x4[h][:, HP:] + kv[h]
        y = x5[h][:C] + x3[h][:, HP:] + rkv[h]
        mean = jnp.sum(y, axis=-1, keepdims=True) * (1.0 / A_HEAD_DIM)
        dlt = jnp.where(valid, y - mean, 0.0)
        var = jnp.sum(dlt * dlt, axis=-1, keepdims=True) * (1.0 / A_HEAD_DIM)
        yn = dlt * lax.rsqrt(var + A_GN_EPS) * gng_ref[:, sl] + gnb_ref[:, sl]
        bonus = jnp.sum(r[:, sl] * k_mod[:, sl] * rk_ref[:, sl], axis=-1, keepdims=True) * v[:, sl]
        outs.append((yn + bonus) * g[:, sl])
    o_ref[0] = jnp.concatenate(outs, axis=1).astype(o_ref.dtype)


def _pad_heads(t, heads):
    hd = t.shape[-1] // heads
    t = t.reshape(t.shape[:-1] + (heads, hd))
    t = jnp.pad(t, [(0, 0)] * (t.ndim - 1) + [(0, HP - hd)])
    return t.reshape(t.shape[:-2] + (heads * HP,))


def rwkv7_pad_in_weight(w_a):
    r, k, v, low = w_a[:, :A_WIDTH], w_a[:, A_WIDTH:2 * A_WIDTH], w_a[:, 2 * A_WIDTH:3 * A_WIDTH], w_a[:, 3 * A_WIDTH:]
    return jnp.concatenate([_pad_heads(r, A_HEADS), _pad_heads(k, A_HEADS), _pad_heads(v, A_HEADS), low], axis=1)


def rwkv7(p, mu, w0, w_up, a0, a_up, g_up, k_k, k_a, r_k, gn_g, gn_b):
    bsz, seq, _ = p.shape
    C = A_CHUNK
    assert seq % C == 0
    mu3 = jnp.concatenate([_pad_heads(mu[i * A_WIDTH:(i + 1) * A_WIDTH], A_HEADS) for i in range(3)]
                          + [mu[3 * A_WIDTH:]]).reshape(1, -1)
    z = jnp.zeros((A_RANK_W, A_PADW), F32)
    wa = jnp.concatenate([jnp.concatenate([_pad_heads(w_up, A_HEADS), z], axis=1),
                          jnp.concatenate([z, _pad_heads(a_up, A_HEADS)], axis=1)], axis=0).astype(BF16)
    vecs = [_pad_heads(t.reshape(-1), A_HEADS).reshape(1, -1) for t in (k_k, k_a, r_k, gn_g, gn_b)]
    full = lambda shape: pl.BlockSpec(shape, lambda b, c: (0,) * len(shape))
    return pl.pallas_call(
        _rwkv_kernel,
        grid=(bsz, seq // C),
        in_specs=[pl.BlockSpec((1, C, A_PCOLS), lambda b, c: (b, c, 0)),
                  full((1, A_PCOLS)), full((1, A_PADW)), full((1, A_PADW)),
                  full((LANE, 2 * A_PADW)), full((A_RANK_G, A_PADW))] + [full((1, A_PADW))] * 5,
        out_specs=pl.BlockSpec((1, C, A_PADW), lambda b, c: (b, c, 0)),
        out_shape=jax.ShapeDtypeStruct((bsz, seq, A_PADW), BF16),
        scratch_shapes=[pltpu.VMEM((SUBLANE, A_PCOLS), F32), pltpu.VMEM((A_HEADS, HP, HP), F32)],
        compiler_params=pltpu.CompilerParams(
            dimension_semantics=("arbitrary", "arbitrary"), vmem_limit_bytes=VMEM_LIMIT),
        name="rwkv7",
    )(p, mu3, _pad_heads(w0, A_HEADS).reshape(1, -1), _pad_heads(a0, A_HEADS).reshape(1, -1), wa,
      _pad_heads(g_up, A_HEADS).astype(BF16), *vecs)


B_CHUNK = 64
B_SUB = 16


def _dot_tl(x, y):
    return lax.dot_general(x.astype(BF16), y.astype(BF16), (((0,), (0,)), ((), ())), preferred_element_type=F32)


def _hgrn_kernel(p_ref, lb_ref, gn_ref, o_ref, st_ref):
    C, SB, W = B_CHUNK, B_SUB, B_WIDTH
    nsb = C // SB
    D = B_KEY_DIM

    @pl.when(pl.program_id(1) == 0)
    def _():
        st_ref[...] = jnp.zeros_like(st_ref)

    p = p_ref[0]
    q, f, val, gate = p[:, :W], p[:, W:2 * W], p[:, 2 * W:3 * W], p[:, 3 * W:]
    lb = lb_ref[...]
    log_f = jnp.logaddexp(jnp.log(lb), jnp.log1p(-lb) + jax.nn.log_sigmoid(f))
    k = (1.0 - lb) * jax.nn.sigmoid(-f)
    row = lax.broadcasted_iota(jnp.int32, (C, C), 0)
    col = lax.broadcasted_iota(jnp.int32, (C, C), 1)
    b = jnp.dot((row >= col).astype(F32), log_f, precision=lax.Precision.HIGHEST,
                preferred_element_type=F32)
    b_end = b[C - 1:C, :]
    b0 = jnp.concatenate(
        [jnp.zeros((SB, W), F32)] + [jnp.broadcast_to(b[i * SB - 1:i * SB, :], (SB, W)) for i in range(1, nsb)],
        axis=0)
    qt = q * jnp.exp(b - b0)
    qe = q * jnp.exp(b)
    ke = k * jnp.exp(b_end - b)
    e_end = jnp.exp(b_end)
    rowd = lax.broadcasted_iota(jnp.int32, (C, D), 0)
    lane_c = lax.broadcasted_iota(jnp.int32, (SB, C), 1)
    row_sb = lax.broadcasted_iota(jnp.int32, (SB, C), 0)

    heads = range(B_HEADS)
    sls = [slice(h * D, (h + 1) * D) for h in heads]
    each = lambda fn: [fn(h) for h in heads]

    def off_operands(h):
        kts, qts = [], []
        for i in range(1, nsb):
            b0_i = b[i * SB - 1:i * SB, sls[h]]
            kts.append(jnp.where(rowd < i * SB, k[:, sls[h]] * jnp.exp(jnp.minimum(b0_i - b[:, sls[h]], 0.0)), 0.0))
            qts.append(jnp.where((rowd >= i * SB) & (rowd < (i + 1) * SB), qt[:, sls[h]], 0.0))
        return jnp.concatenate(kts, axis=1), jnp.concatenate(qts, axis=1)
    offs = each(off_operands)
    st_off = each(lambda h: _dot_t(offs[h][0], offs[h][1]))

    def diag_t(h):
        blocks = []
        for i in range(nsb):
            rs = slice(i * SB, (i + 1) * SB)
            k_i, b_i = k[rs, sls[h]], b[rs, sls[h]]
            acc = jnp.zeros((SB, C), F32)
            for j in range(SB):
                t = i * SB + j
                z = k_i * q[t:t + 1, sls[h]] * jnp.exp(jnp.minimum(b[t:t + 1, sls[h]] - b_i, 0.0))
                acc = jnp.where(lane_c == t, jnp.sum(z, axis=-1, keepdims=True), acc)
            blocks.append(jnp.where(row_sb + i * SB <= lane_c, acc, 0.0))
        return jnp.concatenate(blocks, axis=0)
    st_diag = each(diag_t)
    o_intra = each(lambda h: _dot_tl(st_off[h] + st_diag[h], val[:, sls[h]]))
    o_state = each(lambda h: _dot_t(qe[:, sls[h]], st_ref[h]))
    upd = each(lambda h: _dot_tl(val[:, sls[h]], ke[:, sls[h]]))
    outs = []
    for h in heads:
        st_ref[h] = st_ref[h] * e_end[:, sls[h]] + upd[h]
        o = o_intra[h] + o_state[h]
        o = o * lax.rsqrt(jnp.mean(o * o, axis=-1, keepdims=True) + NORM_EPS)
        gt = gate[:, sls[h]]
        outs.append(o * gn_ref[:, sls[h]] * (gt * jax.nn.sigmoid(gt)))
    o_ref[0] = jnp.concatenate(outs, axis=1).astype(o_ref.dtype)


def hgrn2(p, lb, gn_g):
    bsz, seq, _ = p.shape
    C = B_CHUNK
    assert seq % C == 0
    vec = pl.BlockSpec((1, B_WIDTH), lambda b, c: (0, 0))
    return pl.pallas_call(
        _hgrn_kernel,
        grid=(bsz, seq // C),
        in_specs=[pl.BlockSpec((1, C, B_COLS), lambda b, c: (b, c, 0)), vec, vec],
        out_specs=pl.BlockSpec((1, C, B_WIDTH), lambda b, c: (b, c, 0)),
        out_shape=jax.ShapeDtypeStruct((bsz, seq, B_WIDTH), BF16),
        scratch_shapes=[pltpu.VMEM((B_HEADS, B_VAL_DIM, B_KEY_DIM), F32)],
        compiler_params=pltpu.CompilerParams(
            dimension_semantics=("arbitrary", "arbitrary"), vmem_limit_bytes=VMEM_LIMIT),
        name="hgrn2",
    )(p, lb.reshape(1, -1), gn_g.reshape(1, -1))


C_PAD = _round_up(C_COLS, LANE)
QB = 128
C_KV_W = C_KV_HEADS * C_HEAD_DIM
C_K_OFF = C_WIDTH
C_QI_OFF = C_WIDTH + 2 * C_KV_W
C_KI_OFF = C_QI_OFF + IDX_HEADS * IDX_DIM
INT_MIN = -2 ** 31
NEG = -1e30
HALF = C_HEAD_DIM // 2


def _rope(t, cos, sin_signed):
    outs = []
    lane = lax.broadcasted_iota(jnp.int32, (t.shape[0], LANE), 1)
    first_half = (lane % C_HEAD_DIM) < HALF
    for c in range(t.shape[1] // LANE):
        x = t[:, c * LANE:(c + 1) * LANE]
        swapped = jnp.where(first_half, pltpu.roll(x, LANE - HALF, 1), pltpu.roll(x, HALF, 1))
        outs.append(x * cos + swapped * sin_signed)
    return outs


GROUP = C_HEADS // C_KV_HEADS


def _split_hi_lo(x):
    hi = x.astype(BF16)
    return hi, (x - hi.astype(F32)).astype(BF16)


def _dsa_kernel(pq_ref, kv_ref, ki_ref, cosq_ref, sinq_ref, cosf_ref, sinf_ref, o_ref,
                k_s, vt_s, ki_s, key_s, *, k_sel, seq):
    i = pl.program_id(1)
    nk = i + 1
    lane = lax.broadcasted_iota(jnp.int32, (QB, LANE), 1)
    rowi = lax.broadcasted_iota(jnp.int32, (QB, LANE), 0)
    low_half = lane < C_HEAD_DIM

    @pl.when(i == 0)
    def _():
        kv = kv_ref[0]
        k_s[...] = _rope(kv[:, :LANE], cosf_ref[...], sinf_ref[...])[0].astype(BF16)
        vt = kv[:, LANE:].T
        row_f = lax.broadcasted_iota(jnp.int32, (LANE, seq), 0)
        for c in range(C_KV_HEADS):
            in_c = (row_f >= c * C_HEAD_DIM) & (row_f < (c + 1) * C_HEAD_DIM)
            vt_s[c] = jnp.where(in_c, vt, jnp.where(row_f == (1 - c) * C_HEAD_DIM, 1.0, 0.0)).astype(BF16)
        lane_f = lax.broadcasted_iota(jnp.int32, (seq, LANE), 1)
        ki = jnp.where(lane_f < IDX_DIM, ki_ref[0], 0.0)
        ki = _rope(ki, cosf_ref[...], sinf_ref[...])[0]
        ki = ki + pltpu.roll(ki, IDX_DIM, 1)
        hi, lo = _split_hi_lo(ki)
        ki_s[...] = jnp.concatenate([hi, lo, hi], axis=1)

    pq = pq_ref[0]
    cos, sin = cosq_ref[...], sinq_ref[...]
    q = _rope(pq[:, :C_WIDTH], cos, sin)
    qi = _rope(pq[:, C_QI_OFF:C_KI_OFF], cos, sin)
    qi_rows = jnp.concatenate(
        [jnp.where(low_half if h % 2 == 0 else ~low_half, qi[h // 2], 0.0) for h in range(IDX_HEADS)], axis=0)
    hi, lo = _split_hi_lo(qi_rows)
    qi3 = jnp.concatenate([hi, hi, lo], axis=1)
    wt = pq[:, C_KI_OFF:C_KI_OFF + LANE].T
    wscale = (IDX_HEADS ** -0.5) * (IDX_DIM ** -0.5)
    w_rows = [wt[IDX_DIM + h:IDX_DIM + h + 1, :] * wscale for h in range(IDX_HEADS)]
    causal_off = i * QB + lane

    npair = (nk + 1) // 2

    def score_tile(jp, _):
        ki_t = ki_s[pl.ds(pl.multiple_of(jp * (2 * QB), 2 * QB), 2 * QB), :]
        d = lax.dot_general(ki_t, qi3, (((1,), (1,)), ((), ())), preferred_element_type=F32)
        for u in range(2):
            j = 2 * jp + u
            sc = jnp.zeros((QB, QB), F32)
            for h in range(IDX_HEADS):
                sc = sc + jnp.maximum(d[u * QB:(u + 1) * QB, h * QB:(h + 1) * QB], 0.0) * w_rows[h]
            bits = pltpu.bitcast(sc + 0.0, jnp.int32)
            key = jnp.where(bits < 0, bits ^ jnp.int32(0x7fffffff), bits)
            key_s[j] = jnp.where(j * QB + rowi <= causal_off, key, jnp.int32(INT_MIN))
        return 0
    lax.fori_loop(0, npair, score_tile, 0)

    def count(pred):
        def body(jp, acc):
            for u in range(2):
                acc = acc + jnp.where(pred(key_s[2 * jp + u], 2 * jp + u), 1, 0)
            return acc
        return jnp.sum(lax.fori_loop(0, npair, body, jnp.zeros((QB, QB), jnp.int32)), axis=0, keepdims=True)

    def bit_step(b, cand):
        trial = cand | jnp.left_shift(jnp.int32(1), 31 - b)
        cnt = count(lambda key, j: key >= (trial ^ jnp.int32(INT_MIN)))
        return jnp.where(cnt >= k_sel, trial, cand)
    cand = lax.fori_loop(0, 32, bit_step, jnp.zeros((1, QB), jnp.int32))
    thr = cand ^ jnp.int32(INT_MIN)
    need = k_sel - count(lambda key, j: key > thr)

    nbits = int(np.ceil(np.log2(seq))) + 1

    def idx_step(b, jj):
        trial = jj | jnp.left_shift(jnp.int32(1), nbits - 1 - b)
        cnt = count(lambda key, j: (key == thr) & (j * QB + rowi < trial))
        return jnp.where(cnt <= need, trial, jj)
    jlim = lax.fori_loop(0, nbits, idx_step, jnp.zeros((1, QB), jnp.int32))

    scale = C_HEAD_DIM ** -0.5
    kvs = range(C_KV_HEADS)
    in_cs = [(lane >= c * C_HEAD_DIM) & (lane < (c + 1) * C_HEAD_DIM) for c in kvs]

    def q_rows(c):
        rows = []
        for g in range(GROUP):
            h = c * GROUP + g
            x = q[h // 2]
            if h % 2 != c:
                x = pltpu.roll(x, C_HEAD_DIM, 1)
            rows.append(jnp.where(in_cs[c], x * scale, 0.0))
        return jnp.concatenate(rows, axis=0).astype(BF16)
    qcs = [q_rows(c) for c in kvs]

    def att_tile(jp, carry):
        off = pl.multiple_of(jp * (2 * QB), 2 * QB)
        biases = []
        for u in range(2):
            key = key_s[2 * jp + u]
            pos = (2 * jp + u) * QB + rowi
            sel = ((key > thr) | ((key == thr) & (pos < jlim))) & (pos <= causal_off)
            biases.append(jnp.where(sel, 0.0, NEG))
        bias = jnp.concatenate([jnp.concatenate([b_] * GROUP, axis=1) for b_ in biases], axis=0)
        k_t = k_s[pl.ds(off, 2 * QB), :]
        s = [lax.dot_general(k_t, qcs[c], (((1,), (1,)), ((), ())), preferred_element_type=F32) for c in kvs]
        s = [jnp.where(bias < 0.0, NEG, s[c]) for c in kvs]
        m_new = [jnp.maximum(carry[2 * c], jnp.max(s[c], axis=0, keepdims=True)) for c in kvs]
        p = [jnp.exp(s[c] - m_new[c]).astype(BF16) for c in kvs]
        pv = [jnp.dot(vt_s[c, :, pl.ds(off, 2 * QB)], p[c], preferred_element_type=F32) for c in kvs]
        out = []
        for c in kvs:
            out += [m_new[c], carry[2 * c + 1] * jnp.exp(carry[2 * c] - m_new[c]) + pv[c]]
        return tuple(out)
    init = (jnp.full((1, GROUP * QB), NEG, F32), jnp.zeros((LANE, GROUP * QB), F32)) * C_KV_HEADS
    res = lax.fori_loop(0, npair, att_tile, init)

    heads_out = [None] * C_HEADS
    for c in kvs:
        in_c = in_cs[c]
        acc = res[2 * c + 1]
        ones_row = (1 - c) * C_HEAD_DIM
        out_t = acc / acc[ones_row:ones_row + 1, :]
        for g in range(GROUP):
            h = c * GROUP + g
            x = jnp.where(in_c, out_t[:, g * QB:(g + 1) * QB].T, 0.0)
            if h % 2 != c:
                x = pltpu.roll(x, C_HEAD_DIM, 1)
            heads_out[h] = x
    o_ref[0] = jnp.concatenate([heads_out[2 * m] + heads_out[2 * m + 1] for m in range(C_HEADS // 2)],
                               axis=1).astype(o_ref.dtype)


def _dsa_kernel_rows(pq_ref, kv_ref, ki_ref, cosq_ref, sinq_ref, cosf_ref, sinf_ref, o_ref,
                     k_s, v_s, ki_s, key_s, *, k_sel, seq):
    i = pl.program_id(1)
    nk = i + 1
    lane = lax.broadcasted_iota(jnp.int32, (QB, LANE), 1)
    rowi = lax.broadcasted_iota(jnp.int32, (QB, LANE), 0)
    low_half = lane < C_HEAD_DIM

    @pl.when(i == 0)
    def _():
        kv = kv_ref[0]
        k_s[...] = _rope(kv[:, :LANE], cosf_ref[...], sinf_ref[...])[0].astype(BF16)
        v_s[...] = kv[:, LANE:].astype(BF16)
        lane_f = lax.broadcasted_iota(jnp.int32, (seq, LANE), 1)
        ki = jnp.where(lane_f < IDX_DIM, ki_ref[0], 0.0)
        ki = _rope(ki, cosf_ref[...], sinf_ref[...])[0]
        ki_s[...] = ki + pltpu.roll(ki, IDX_DIM, 1)

    pq = pq_ref[0]
    cos, sin = cosq_ref[...], sinq_ref[...]
    q = _rope(pq[:, :C_WIDTH], cos, sin)
    qi = _rope(pq[:, C_QI_OFF:C_KI_OFF], cos, sin)
    wslab = pq[:, C_KI_OFF:C_KI_OFF + LANE]
    wscale = (IDX_HEADS ** -0.5) * (IDX_DIM ** -0.5)
    wi = [jnp.sum(jnp.where(lane == IDX_DIM + h, wslab, 0.0), axis=-1, keepdims=True) * wscale
          for h in range(IDX_HEADS)]
    qi_h = [jnp.where(low_half if h % 2 == 0 else ~low_half, qi[h // 2], 0.0) for h in range(IDX_HEADS)]

    def score_tile(j, _):
        ki_t = ki_s[pl.ds(pl.multiple_of(j * QB, QB), QB), :]
        sc = jnp.zeros((QB, QB), F32)
        for h in range(IDX_HEADS):
            d = lax.dot_general(qi_h[h], ki_t, (((1,), (1,)), ((), ())), precision=lax.Precision.HIGHEST,
                                preferred_element_type=F32)
            sc = sc + jnp.maximum(d, 0.0) * wi[h]
        bits = pltpu.bitcast(sc + 0.0, jnp.int32)
        key = jnp.where(bits < 0, bits ^ jnp.int32(0x7fffffff), bits)
        key_s[j] = jnp.where(j * QB + lane <= i * QB + rowi, key, jnp.int32(INT_MIN))
        return 0
    lax.fori_loop(0, nk, score_tile, 0)

    def count(pred):
        def body(j, acc):
            return acc + jnp.where(pred(key_s[j], j), 1, 0)
        return jnp.sum(lax.fori_loop(0, nk, body, jnp.zeros((QB, QB), jnp.int32)), axis=-1, keepdims=True)

    def bit_step(b, cand):
        trial = cand | jnp.left_shift(jnp.int32(1), 31 - b)
        cnt = count(lambda key, j: key >= (trial ^ jnp.int32(INT_MIN)))
        return jnp.where(cnt >= k_sel, trial, cand)
    cand = lax.fori_loop(0, 32, bit_step, jnp.zeros((QB, 1), jnp.int32))
    thr = cand ^ jnp.int32(INT_MIN)
    need = k_sel - count(lambda key, j: key > thr)

    nbits = int(np.ceil(np.log2(seq))) + 1

    def idx_step(b, jj):
        trial = jj | jnp.left_shift(jnp.int32(1), nbits - 1 - b)
        cnt = count(lambda key, j: (key == thr) & (j * QB + lane < trial))
        return jnp.where(cnt <= need, trial, jj)
    jlim = lax.fori_loop(0, nbits, idx_step, jnp.zeros((QB, 1), jnp.int32))

    group = C_HEADS // C_KV_HEADS
    scale = C_HEAD_DIM ** -0.5
    outs = []
    for c in range(C_KV_HEADS):
        in_c = (lane >= c * C_HEAD_DIM) & (lane < (c + 1) * C_HEAD_DIM)
        rows = []
        for g in range(group):
            h = c * group + g
            x = q[h // 2]
            if h % 2 != c:
                x = pltpu.roll(x, C_HEAD_DIM, 1)
            rows.append(jnp.where(in_c, x * scale, 0.0))
        qc = jnp.concatenate(rows, axis=0).astype(BF16)
        ones_lane = (1 - c) * C_HEAD_DIM

        def att_tile(j, carry):
            m, acc = carry
            off = pl.multiple_of(j * QB, QB)
            key = key_s[j]
            sel = ((key > thr) | ((key == thr) & (j * QB + lane < jlim))) & (j * QB + lane <= i * QB + rowi)
            bias = jnp.where(sel, 0.0, NEG)
            s = lax.dot_general(qc, k_s[pl.ds(off, QB), :], (((1,), (1,)), ((), ())),
                                preferred_element_type=F32)
            s = jnp.where(jnp.concatenate([bias] * group, axis=0) < 0.0, NEG, s)
            m_new = jnp.maximum(m, jnp.max(s, axis=-1, keepdims=True))
            alpha = jnp.exp(m - m_new)
            p = jnp.exp(s - m_new)
            v_t = v_s[pl.ds(off, QB), :]
            v_t = jnp.where(in_c, v_t, jnp.where(lane == ones_lane, 1.0, 0.0).astype(BF16))
            acc = acc * alpha + jnp.dot(p.astype(BF16), v_t, preferred_element_type=F32)
            return m_new, acc
        m0 = jnp.full((group * QB, 1), NEG, F32)
        acc0 = jnp.zeros((group * QB, LANE), F32)
        _, acc = lax.fori_loop(0, nk, att_tile, (m0, acc0))
        l = jnp.sum(jnp.where(jnp.concatenate([lane] * group, axis=0) == ones_lane, acc, 0.0),
                    axis=-1, keepdims=True)
        out = acc / l
        for g in range(group):
            h = c * group + g
            x = jnp.where(in_c, out[g * QB:(g + 1) * QB], 0.0)
            if h % 2 != c:
                x = pltpu.roll(x, C_HEAD_DIM, 1)
            outs.append(x)
    o_ref[0] = jnp.concatenate([outs[2 * m] + outs[2 * m + 1] for m in range(C_HEADS // 2)],
                               axis=1).astype(o_ref.dtype)


def _rope_lanes(seq):
    inv_freq = ROPE_THETA ** (-jnp.arange(HALF, dtype=F32) * (2.0 / C_HEAD_DIM))
    ang = jnp.arange(seq, dtype=F32)[:, None] * inv_freq[None, :]
    cos, sin = jnp.cos(ang), jnp.sin(ang)
    reps = LANE // C_HEAD_DIM
    return (jnp.tile(jnp.concatenate([cos, cos], axis=1), (1, reps)),
            jnp.tile(jnp.concatenate([-sin, sin], axis=1), (1, reps)))


def dsa(p):
    bsz, seq, width = p.shape
    assert seq % (2 * QB) == 0 and width == C_PAD
    k_sel = min(TOPK_MAX, seq // 4)
    cos, sin = _rope_lanes(seq)
    return pl.pallas_call(
        functools.partial(_dsa_kernel, k_sel=k_sel, seq=seq),
        grid=(bsz, seq // QB),
        in_specs=[pl.BlockSpec((1, QB, C_PAD), lambda b, i: (b, i, 0)),
                  pl.BlockSpec((1, seq, 2 * C_KV_W), lambda b, i: (b, 0, C_K_OFF // (2 * C_KV_W))),
                  pl.BlockSpec((1, seq, LANE), lambda b, i: (b, 0, C_KI_OFF // LANE)),
                  pl.BlockSpec((QB, LANE), lambda b, i: (i, 0)),
                  pl.BlockSpec((QB, LANE), lambda b, i: (i, 0)),
                  pl.BlockSpec((seq, LANE), lambda b, i: (0, 0)),
                  pl.BlockSpec((seq, LANE), lambda b, i: (0, 0))],
        out_specs=pl.BlockSpec((1, QB, C_WIDTH), lambda b, i: (b, i, 0)),
        out_shape=jax.ShapeDtypeStruct((bsz, seq, C_WIDTH), BF16),
        scratch_shapes=[pltpu.VMEM((seq, LANE), BF16), pltpu.VMEM((C_KV_HEADS, LANE, seq), BF16),
                        pltpu.VMEM((seq, 3 * LANE), BF16), pltpu.VMEM((seq // QB, QB, QB), jnp.int32)],
        compiler_params=pltpu.CompilerParams(
            dimension_semantics=("arbitrary", "arbitrary"), vmem_limit_bytes=VMEM_LIMIT),
        name="dsa",
    )(p, p, p, cos, sin, cos, sin)


def _split(t, sizes):
    out, start = [], 0
    for s in sizes:
        out.append(t[..., start:start + s])
        start += s
    return out


def _mm(x, y):
    return jnp.einsum('...ij,...jk->...ik', x.astype(BF16), y.astype(BF16), preferred_element_type=F32)


def _mm_hi(x, y):
    return jnp.einsum('...ij,...jk->...ik', x, y, precision=lax.Precision.HIGHEST)


def _rwkv7_chunked(r, logw, k, v, a, b, C=64):
    B, S, H, N = r.shape
    nc = S // C

    def ch(t):
        return t.reshape(B, nc, C, H, N).transpose(1, 0, 3, 2, 4)
    r, logw, k, v, a, b = map(ch, (r, logw, k, v, a, b))
    cum = jnp.cumsum(logw, axis=-2)
    cum_prev = cum - logw
    cum_end = cum[..., -1:, :]
    At = a * jnp.exp(cum_prev)
    Rt = r * jnp.exp(cum)
    Bt = b * jnp.exp(-cum)
    Kt = k * jnp.exp(-cum)
    Bh = b * jnp.exp(cum_end - cum)
    Kh = k * jnp.exp(cum_end - cum)
    Pc = jnp.exp(cum_end)[..., 0, :]
    T_ = lambda x: jnp.swapaxes(x, -1, -2)
    ti = jnp.arange(C)
    strict = ti[:, None] > ti[None, :]
    incl = ti[:, None] >= ti[None, :]
    Aab = jnp.where(strict, _mm(At, T_(Bt)), 0.0)
    Aak = jnp.where(strict, _mm(At, T_(Kt)), 0.0)
    Arb = jnp.where(incl, _mm(Rt, T_(Bt)), 0.0)
    Ark = jnp.where(incl, _mm(Rt, T_(Kt)), 0.0)
    Tm = jnp.eye(C, dtype=F32) + Aab
    Pw = Aab
    n = 1
    while 2 * n < C:
        Pw = _mm_hi(Pw, Pw)
        Tm = Tm + _mm_hi(Tm, Pw)
        n *= 2
    Abar = _mm(Tm, At)
    U0 = _mm(Tm, _mm(Aak, v))
    Rbar = Rt + _mm(Arb, Abar)
    Y0 = _mm(Arb, U0) + _mm(Ark, v)
    M = Pc[..., :, None] * jnp.eye(N, dtype=F32) + _mm(T_(Bh), Abar)
    Hd = _mm(T_(Bh), U0) + _mm(T_(Kh), v)

    def step(Hs, inp):
        Rbar_c, Y0_c, M_c, Hd_c = inp
        y = _mm(Rbar_c, Hs) + Y0_c
        Hs = _mm_hi(M_c, Hs) + Hd_c
        return Hs, y
    _, ys = lax.scan(step, jnp.zeros((B, H, N, N), F32), (Rbar, Y0, M, Hd))
    return ys.transpose(1, 0, 3, 2, 4).reshape(B, S, H, N)


def _token_shift(p):
    return jnp.pad(p, ((0, 0), (1, 0), (0, 0)))[:, :-1]


def _rwkv7_branch(p, mu, w0, w_up, a0, a_up, g_up, k_k, k_a, r_k, gn_g, gn_b):
    bsz, seq, _ = p.shape
    p = p + (_token_shift(p) - p) * mu
    r, k, v, wd, ad, gd = _split(p, (A_WIDTH, A_WIDTH, A_WIDTH, A_RANK_W, A_RANK_A, A_RANK_G))
    w_log = -jax.nn.softplus(-(w0 + _mm(jnp.tanh(wd), w_up))) - 0.5
    logw = -jnp.exp(w_log)
    a = jax.nn.sigmoid(a0 + _mm(ad, a_up))
    g = _mm(jax.nn.sigmoid(gd), g_up)

    def hd(t):
        return t.reshape(bsz, seq, A_HEADS, A_HEAD_DIM)
    kk = hd(k * k_k)
    kk = kk * lax.rsqrt(jnp.maximum(jnp.sum(kk * kk, axis=-1, keepdims=True), 1e-24))
    k_mod = hd(k * (1.0 + (a - 1.0) * k_a))
    r_h, v_h, a_h = hd(r), hd(v), hd(a)
    y = _rwkv7_chunked(r_h, hd(logw), k_mod, v_h, -kk, kk * a_h)
    mean = jnp.mean(y, axis=-1, keepdims=True)
    var = jnp.mean(jnp.square(y - mean), axis=-1, keepdims=True)
    y = ((y - mean) * lax.rsqrt(var + A_GN_EPS)).reshape(bsz, seq, A_WIDTH)
    y = y * gn_g + gn_b
    bonus = jnp.sum(r_h * k_mod * r_k, axis=-1, keepdims=True) * v_h
    y = y + bonus.reshape(bsz, seq, A_WIDTH)
    return y * g


def _hgrn2_sub(q, k, g, v, C=64, SB=16):
    B, S, H, dk = q.shape
    dv = v.shape[-1]
    nc = S // C

    def ch(t):
        return t.reshape(B, nc, C, H, t.shape[-1]).transpose(1, 0, 3, 2, 4)
    q, k, g, v = map(ch, (q, k, g, v))
    b = jnp.cumsum(g, axis=-2)
    nsb = C // SB
    T_ = lambda x: jnp.swapaxes(x, -1, -2)

    def step(state, inp):
        qc, kc, bc, vc = inp
        rows = []
        for I in range(nsb):
            sl = slice(I * SB, (I + 1) * SB)
            qI, bI = qc[..., sl, :], bc[..., sl, :]
            b0 = bc[..., I * SB - 1:I * SB, :] if I > 0 else jnp.zeros_like(bc[..., :1, :])
            qt = qI * jnp.exp(bI - b0)
            blocks = []
            if I > 0:
                kt = kc[..., :I * SB, :] * jnp.exp(b0 - bc[..., :I * SB, :])
                blocks.append(_mm(qt, T_(kt)))
            diff = bI[..., :, None, :] - bI[..., None, :, :]
            tri = jnp.tril(jnp.ones((SB, SB), bool))
            dec = jnp.exp(jnp.where(tri[:, :, None], diff, -jnp.inf))
            diag = jnp.sum(qI[..., :, None, :] * kc[..., sl, :][..., None, :, :] * dec, axis=-1)
            blocks.append(diag)
            if I < nsb - 1:
                blocks.append(jnp.zeros(diag.shape[:-1] + (C - (I + 1) * SB,), F32))
            rows.append(jnp.concatenate(blocks, axis=-1))
        scores = jnp.concatenate(rows, axis=-2)
        o = _mm(scores, vc) + _mm(qc * jnp.exp(bc), state)
        b_end = bc[..., -1:, :]
        state = state * T_(jnp.exp(b_end)) + _mm(T_(kc * jnp.exp(b_end - bc)), vc)
        return state, o
    _, o = lax.scan(step, jnp.zeros((B, H, dk, dv), F32), (q, k, b, v))
    return o.transpose(1, 0, 3, 2, 4).reshape(B, S, H, dv)


def _hgrn2_branch(p, lb, gn_g):
    bsz, seq, _ = p.shape
    q, f, i, g = _split(p, (B_WIDTH,) * 4)
    log_f = jnp.logaddexp(jnp.log(lb), jnp.log1p(-lb) + jax.nn.log_sigmoid(f))
    k_in = (1.0 - lb) * jax.nn.sigmoid(-f)
    hd = lambda t: t.reshape(bsz, seq, B_HEADS, -1)
    o = _hgrn2_sub(hd(q), hd(k_in), hd(log_f), hd(i))
    o = o * lax.rsqrt(jnp.mean(o * o, axis=-1, keepdims=True) + NORM_EPS)
    return o.reshape(bsz, seq, B_WIDTH) * gn_g * jax.nn.silu(g)


def _rope_tables(seq, dim):
    half = dim // 2
    inv_freq = ROPE_THETA ** (-jnp.arange(half, dtype=F32) * (2.0 / dim))
    ang = jnp.arange(seq, dtype=F32)[:, None] * inv_freq[None, :]
    return jnp.cos(ang), jnp.sin(ang)


def _apply_rope(t, cos, sin):
    half = t.shape[-1] // 2
    t1, t2 = t[..., :half], t[..., half:]
    c, s = cos[None, :, None, :], sin[None, :, None, :]
    return jnp.concatenate([t1 * c - t2 * s, t2 * c + t1 * s], axis=-1)


def _dsa_branch(p, cos, sin):
    bsz, seq, _ = p.shape
    k_sel = min(TOPK_MAX, seq // 4)
    kv_w = C_KV_HEADS * C_HEAD_DIM
    q, k, v, qi, ki, wi = _split(p[..., :C_COLS], (C_WIDTH, kv_w, kv_w, IDX_HEADS * IDX_DIM, IDX_DIM, IDX_HEADS))
    q = _apply_rope(q.reshape(bsz, seq, C_HEADS, C_HEAD_DIM), cos, sin)
    k = _apply_rope(k.reshape(bsz, seq, C_KV_HEADS, C_HEAD_DIM), cos, sin)
    v = v.reshape(bsz, seq, C_KV_HEADS, C_HEAD_DIM)
    qi = _apply_rope(qi.reshape(bsz, seq, IDX_HEADS, IDX_DIM), cos, sin)
    ki = _apply_rope(ki[:, :, None, :], cos, sin)[:, :, 0, :]
    wi = wi * (IDX_HEADS ** -0.5) * (IDX_DIM ** -0.5)
    dots = jnp.einsum('bqhd,bsd->bqhs', qi, ki, precision=lax.Precision.HIGHEST)
    score = jnp.sum(jax.nn.relu(dots) * wi[..., None], axis=2)
    pos = jnp.arange(seq)
    causal = pos[None, :] <= pos[:, None]
    bits = lax.bitcast_convert_type(score + 0.0, jnp.int32)
    key = jnp.where(bits < 0, bits ^ jnp.int32(0x7fffffff), bits)
    key = jnp.where(causal[None], key, jnp.int32(-2 ** 31))
    ukey = lax.bitcast_convert_type(key ^ jnp.int32(-2 ** 31), jnp.uint32)
    cand = jnp.zeros((bsz, seq, 1), jnp.uint32)
    for bit in range(31, -1, -1):
        trial = cand | jnp.uint32(1 << bit)
        cnt = jnp.sum((ukey >= trial).astype(jnp.int32), axis=-1, keepdims=True)
        cand = jnp.where(cnt >= k_sel, trial, cand)
    gt = ukey > cand
    eq = ukey == cand
    need = k_sel - jnp.sum(gt.astype(jnp.int32), axis=-1, keepdims=True)
    J = jnp.zeros((bsz, seq, 1), jnp.int32)
    for bit in range(int(np.ceil(np.log2(seq))), -1, -1):
        trial = J | (1 << bit)
        cnt = jnp.sum((eq & (pos[None, None, :] < trial)).astype(jnp.int32), axis=-1, keepdims=True)
        J = jnp.where(cnt <= need, trial, J)
    sel = (gt | (eq & (pos[None, None, :] < J))) & causal[None]
    group = C_HEADS // C_KV_HEADS
    qg = q.reshape(bsz, seq, C_KV_HEADS, group, C_HEAD_DIM)
    logits = jnp.einsum('bqcgd,bscd->bcgqs', qg.astype(BF16), k.astype(BF16),
                        preferred_element_type=F32) * (C_HEAD_DIM ** -0.5)
    logits = jnp.where(sel[:, None, None], logits, -jnp.inf)
    prob = jax.nn.softmax(logits, axis=-1)
    out = jnp.einsum('bcgqs,bscd->bqcgd', prob.astype(BF16), v.astype(BF16), preferred_element_type=F32)
    return out.reshape(bsz, seq, C_WIDTH)


def kernel(x, norm_mix_g, w_in, a_mu, a_w0, a_w_up, a_a0, a_a_up, a_g_up, a_k_k, a_k_a, a_r_k, a_gn_g,
           a_gn_b, b_lb_logits, b_gn_g, w_branch, w_o, norm_ffn_g, w_up, conv_w, conv_b, w_down,
           norm_final_g):
    bsz, seq, d = x.shape
    t = bsz * seq
    lb_cum = jnp.cumsum(jax.nn.softmax(b_lb_logits.astype(F32), axis=0), axis=0)
    lower_bounds = lb_cum - lb_cum[0]
    h = x
    for layer in range(DEPTH):
        w = w_in[layer].astype(BF16)
        w_a, w_b, w_c, w_g = _split(w, (A_COLS, B_COLS, C_COLS, GATE_COLS))
        w_a = rwkv7_pad_in_weight(w_a)
        w_c = jnp.pad(w_c, ((0, 0), (0, C_PAD - C_COLS)))
        h2 = h.reshape(t, d)
        g = norm_mix_g[layer]
        p_a = norm_matmul(h2, g, w_a, tm=1024, tn=A_PCOLS // 2, out_dtype=F32).reshape(bsz, seq, -1)
        p_b = norm_matmul(h2, g, w_b, tm=1024, tn=1024, out_dtype=F32).reshape(bsz, seq, -1)
        p_c = norm_matmul(h2, g, w_c, tm=1024, tn=C_PAD, out_dtype=F32).reshape(bsz, seq, -1)
        p_g = norm_matmul(h2, g, w_g, tm=1024, tn=1024, out_dtype=BF16)
        y_a = rwkv7(p_a, a_mu[layer], a_w0[layer], a_w_up[layer], a_a0[layer], a_a_up[layer],
                    a_g_up[layer], a_k_k[layer], a_k_a[layer], a_r_k[layer], a_gn_g[layer], a_gn_b[layer])
        y_b = hgrn2(p_b, lower_bounds[layer], b_gn_g[layer])
        y_c = dsa(p_c)
        wbr = w_branch[layer].astype(BF16)
        wbr_a = jnp.pad(wbr[0].reshape(A_HEADS, A_HEAD_DIM, d), ((0, 0), (0, HP - A_HEAD_DIM), (0, 0)))
        h2 = merge((y_a.reshape(t, -1), y_b.reshape(t, -1), y_c.reshape(t, -1)), p_g, h2,
                   (wbr_a.reshape(A_PADW, d), wbr[1], wbr[2]), w_o[layer].astype(BF16), tm=512)
        h = conv_glu(h2.reshape(bsz, seq, d), norm_ffn_g[layer], w_up[layer].astype(BF16), conv_w[layer],
                     conv_b[layer], w_down[layer].astype(BF16),
                     norm_final_g if layer == DEPTH - 1 else None, tm=512, tf=1408)
    return h
```

```python
import functools

import jax
import jax.numpy as jnp
import numpy as np
from jax import lax
from jax.experimental import pallas as pl
from jax.experimental.pallas import tpu as pltpu

F32 = jnp.float32
BF16 = jnp.bfloat16

D_MODEL = 1024
DEPTH = 2
MIX_WIDTH = 512
N_BRANCH = 3
A_HEAD_DIM = 64
A_HEADS = 8
A_WIDTH = 512
A_RANK_W = 64
A_RANK_A = 64
A_RANK_G = 128
A_GN_EPS = 64e-5
B_KEY_DIM = 128
B_VAL_DIM = 128
B_HEADS = 4
B_WIDTH = 512
C_HEAD_DIM = 64
C_HEADS = 8
C_KV_HEADS = 2
C_WIDTH = 512
IDX_HEADS = 4
IDX_DIM = 64
TOPK_MAX = 256
ROPE_THETA = 10000.0
D_FF = 2816
CONV_WIDTH = 3
NORM_EPS = 1e-6

A_COLS = 3 * A_WIDTH + A_RANK_W + A_RANK_A + A_RANK_G
B_COLS = 4 * B_WIDTH
C_COLS = C_WIDTH + 2 * C_KV_HEADS * C_HEAD_DIM + IDX_HEADS * IDX_DIM + IDX_DIM + IDX_HEADS
GATE_COLS = N_BRANCH * D_MODEL

LANE = 128
SUBLANE = 8
VMEM_LIMIT = 48 * 1024 * 1024


def _round_up(n, m):
    return (n + m - 1) // m * m


def _rms(x, g):
    return x * lax.rsqrt(jnp.mean(x * x, axis=-1, keepdims=True) + NORM_EPS) * g


def _norm_matmul_kernel(x_ref, g_ref, w_ref, o_ref, xn_ref):
    @pl.when(pl.program_id(1) == 0)
    def _():
        xn_ref[...] = _rms(x_ref[...], g_ref[...]).astype(BF16)

    o_ref[...] = jnp.dot(xn_ref[...], w_ref[...], preferred_element_type=F32).astype(o_ref.dtype)


def norm_matmul(x, g, w, *, tm, tn, out_dtype):
    t, d = x.shape
    n = w.shape[1]
    assert t % tm == 0 and n % tn == 0
    return pl.pallas_call(
        _norm_matmul_kernel,
        grid=(t // tm, n // tn),
        in_specs=[
            pl.BlockSpec((tm, d), lambda i, j: (i, 0)),
            pl.BlockSpec((1, d), lambda i, j: (0, 0)),
            pl.BlockSpec((d, tn), lambda i, j: (0, j)),
        ],
        out_specs=pl.BlockSpec((tm, tn), lambda i, j: (i, j)),
        out_shape=jax.ShapeDtypeStruct((t, n), out_dtype),
        scratch_shapes=[pltpu.VMEM((tm, d), BF16)],
        compiler_params=pltpu.CompilerParams(
            dimension_semantics=("arbitrary", "arbitrary"), vmem_limit_bytes=VMEM_LIMIT),
        name="norm_matmul",
    )(x, g.reshape(1, d), w)


def _merge_kernel(ya_ref, yb_ref, yc_ref, gate_ref, h_ref, wa_ref, wb_ref, wc_ref, wo_ref, o_ref):
    merged = None
    for b, (y_ref, w_ref) in enumerate(((ya_ref, wa_ref), (yb_ref, wb_ref), (yc_ref, wc_ref))):
        br = jnp.dot(y_ref[...].astype(BF16), w_ref[...], preferred_element_type=F32)
        gt = jax.nn.sigmoid(gate_ref[:, b * D_MODEL:(b + 1) * D_MODEL].astype(F32))
        merged = gt * br if merged is None else merged + gt * br
    o_ref[...] = h_ref[...] + jnp.dot(merged.astype(BF16), wo_ref[...], preferred_element_type=F32)


def merge(ys, gate, h, ws, w_o, *, tm):
    t = h.shape[0]
    assert t % tm == 0
    row = lambda w: pl.BlockSpec((tm, w), lambda i: (i, 0))
    const = lambda a: pl.BlockSpec(a.shape, lambda i: (0, 0))
    return pl.pallas_call(
        _merge_kernel,
        grid=(t // tm,),
        in_specs=[row(y.shape[1]) for y in ys] + [row(GATE_COLS), row(D_MODEL)]
                 + [const(w) for w in ws] + [const(w_o)],
        out_specs=row(D_MODEL),
        out_shape=jax.ShapeDtypeStruct((t, D_MODEL), F32),
        compiler_params=pltpu.CompilerParams(
            dimension_semantics=("arbitrary",), vmem_limit_bytes=VMEM_LIMIT),
        name="merge",
    )(*ys, gate, h, *ws, w_o)


def _shift_rows(cur, prev_tail, shift):
    rolled = pltpu.roll(cur, shift, 0)
    tail = pltpu.roll(prev_tail, shift, 0)
    row = lax.broadcasted_iota(jnp.int32, tail.shape, 0)
    top = jnp.where(row < shift, tail, rolled[:SUBLANE])
    return jnp.concatenate([top, rolled[SUBLANE:]], axis=0)


def _conv_glu_kernel(h_ref, g_ref, wug_ref, wuv_ref, cwg_ref, cwv_ref, cbg_ref, cbv_ref, wd_ref, *rest,
                     final_norm):
    if final_norm:
        gf_ref, o_ref, xn_ref, acc_ref, carry_ref = rest
    else:
        o_ref, xn_ref, acc_ref, carry_ref = rest
    ti = pl.program_id(1)
    fi = pl.program_id(2)
    nf = pl.num_programs(2)

    @pl.when(fi == 0)
    def _():
        xn_ref[...] = _rms(h_ref[0], g_ref[...]).astype(BF16)
        acc_ref[...] = jnp.zeros_like(acc_ref)

    @pl.when(ti == 0)
    def _():
        carry_ref[fi] = jnp.zeros(carry_ref.shape[1:], F32)

    xn = xn_ref[...]

    def conv(w_ref, cw_ref, cb_ref, slot):
        up = jnp.dot(xn, w_ref[...], preferred_element_type=F32)
        tail = carry_ref[fi, slot]
        c = (cb_ref[...] + cw_ref[2:3, :] * up + cw_ref[1:2, :] * _shift_rows(up, tail, 1)
             + cw_ref[0:1, :] * _shift_rows(up, tail, 2))
        carry_ref[fi, slot] = up[up.shape[0] - SUBLANE:]
        return c

    cg = conv(wug_ref, cwg_ref, cbg_ref, 0)
    cv = conv(wuv_ref, cwv_ref, cbv_ref, 1)
    act = (cg * jax.nn.sigmoid(cg) * cv).astype(BF16)
    acc_ref[...] += jnp.dot(act, wd_ref[...], preferred_element_type=F32)

    @pl.when(fi == nf - 1)
    def _():
        out = h_ref[0] + acc_ref[...]
        if final_norm:
            out = _rms(out, gf_ref[...])
        o_ref[0] = out


def conv_glu(h, g, w_up, conv_w, conv_b, w_down, gf, *, tm, tf):
    bsz, seq, d = h.shape
    assert seq % tm == 0 and D_FF % tf == 0
    nf = D_FF // tf
    final_norm = gf is not None
    vec = lambda: pl.BlockSpec((1, d), lambda b, t, f: (0, 0))
    in_specs = [
        pl.BlockSpec((1, tm, d), lambda b, t, f: (b, t, 0)),
        vec(),
        pl.BlockSpec((d, tf), lambda b, t, f: (0, f)),
        pl.BlockSpec((d, tf), lambda b, t, f: (0, f + nf)),
        pl.BlockSpec((CONV_WIDTH, tf), lambda b, t, f: (0, f)),
        pl.BlockSpec((CONV_WIDTH, tf), lambda b, t, f: (0, f + nf)),
        pl.BlockSpec((1, tf), lambda b, t, f: (0, f)),
        pl.BlockSpec((1, tf), lambda b, t, f: (0, f + nf)),
        pl.BlockSpec((tf, d), lambda b, t, f: (f, 0)),
    ]
    args = [h, g.reshape(1, d), w_up, w_up, conv_w, conv_w, conv_b.reshape(1, -1), conv_b.reshape(1, -1), w_down]
    if final_norm:
        in_specs.append(vec())
        args.append(gf.reshape(1, d))
    return pl.pallas_call(
        functools.partial(_conv_glu_kernel, final_norm=final_norm),
        grid=(bsz, seq // tm, nf),
        in_specs=in_specs,
        out_specs=pl.BlockSpec((1, tm, d), lambda b, t, f: (b, t, 0)),
        out_shape=jax.ShapeDtypeStruct((bsz, seq, d), F32),
        scratch_shapes=[pltpu.VMEM((tm, d), BF16), pltpu.VMEM((tm, d), F32),
                        pltpu.VMEM((nf, 2, SUBLANE, tf), F32)],
        compiler_params=pltpu.CompilerParams(
            dimension_semantics=("arbitrary", "arbitrary", "arbitrary"), vmem_limit_bytes=VMEM_LIMIT),
        name="conv_glu",
    )(*args)


HP = LANE
A_PADW = A_HEADS * HP
A_CHUNK = 64
A_LOW = A_RANK_W + A_RANK_A + A_RANK_G
A_PCOLS = 3 * A_PADW + A_LOW


def _dot(x, y):
    return jnp.dot(x.astype(BF16), y.astype(BF16), preferred_element_type=F32)


def _dot_t(x, y):
    return lax.dot_general(x.astype(BF16), y.astype(BF16), (((1,), (1,)), ((), ())), preferred_element_type=F32)


def _rwkv_kernel(p_ref, mu_ref, w0_ref, a0_ref, wa_ref, gup_ref, kk_ref, ka_ref, rk_ref, gng_ref, gnb_ref,
                 o_ref, carry_ref, h_ref):
    C = A_CHUNK
    W = A_PADW

    @pl.when(pl.program_id(1) == 0)
    def _():
        carry_ref[...] = jnp.zeros_like(carry_ref)
        h_ref[...] = jnp.zeros_like(h_ref)

    p = p_ref[0].astype(F32)
    prev = _shift_rows(p, carry_ref[...], 1)
    carry_ref[...] = p[C - SUBLANE:]
    p = p + (prev - p) * mu_ref[...]
    r, k, v = p[:, :W], p[:, W:2 * W], p[:, 2 * W:3 * W]
    low = p[:, 3 * W:3 * W + LANE]
    gd = p[:, 3 * W + LANE:]
    lane = lax.broadcasted_iota(jnp.int32, (C, LANE), 1)
    low = jnp.where(lane < A_RANK_W, jnp.tanh(low), low)
    wa = _dot(low, wa_ref[...])
    w_log = -jax.nn.softplus(-(w0_ref[...] + wa[:, :W])) - 0.5
    logw = -jnp.exp(w_log)
    a = jax.nn.sigmoid(a0_ref[...] + wa[:, W:])
    g = _dot(jax.nn.sigmoid(gd), gup_ref[...])
    k_mod = k * (1.0 + (a - 1.0) * ka_ref[...])
    kk = k * kk_ref[...]

    row = lax.broadcasted_iota(jnp.int32, (C, C), 0)
    col = lax.broadcasted_iota(jnp.int32, (C, C), 1)
    incl = row >= col
    strict = row > col
    cum = jnp.dot(incl.astype(F32), logw, precision=lax.Precision.HIGHEST, preferred_element_type=F32)
    cum_end = cum[C - 1:C, :]
    e_neg = jnp.exp(-cum)
    e_end = jnp.exp(cum_end - cum)
    r_t = r * jnp.exp(cum)
    a_scale = jnp.exp(cum - logw)
    p_end = jnp.exp(cum_end)
    eye = (lax.broadcasted_iota(jnp.int32, (HP, HP), 0) == lax.broadcasted_iota(jnp.int32, (HP, HP), 1)).astype(F32)
    eye_c = (row == col).astype(F32)
    valid = lane < A_HEAD_DIM

    heads = range(A_HEADS)
    sls = [slice(h * HP, (h + 1) * HP) for h in heads]
    each = lambda f: [f(h) for h in heads]
    kk_n = each(lambda h: kk[:, sls[h]] * lax.rsqrt(jnp.maximum(
        jnp.sum(kk[:, sls[h]] * kk[:, sls[h]], axis=-1, keepdims=True), 1e-24)))
    b_h = each(lambda h: kk_n[h] * a[:, sls[h]])
    at = each(lambda h: -kk_n[h] * a_scale[:, sls[h]])
    ar = each(lambda h: jnp.concatenate([at[h], r_t[:, sls[h]]], axis=0))
    ab = each(lambda h: _dot_t(ar[h], b_h[h] * e_neg[:, sls[h]]))
    ak = each(lambda h: _dot_t(ar[h], k_mod[:, sls[h]] * e_neg[:, sls[h]]))
    a_ab = each(lambda h: jnp.where(strict, ab[h][:C], 0.0))
    a_rb = each(lambda h: jnp.where(incl, ab[h][C:], 0.0))
    a_ak = each(lambda h: jnp.where(strict, ak[h][:C], 0.0))
    a_rk = each(lambda h: jnp.where(incl, ak[h][C:], 0.0))
    tm = each(lambda h: eye_c + a_ab[h])
    pw = each(lambda h: _dot(a_ab[h], a_ab[h]))
    akv = each(lambda h: _dot(a_ak[h], v[:, sls[h]]))
    n = 2
    while 2 * n < C:
        x = each(lambda h: _dot(jnp.concatenate([tm[h], pw[h]], axis=0), pw[h]))
        tm = each(lambda h: tm[h] + x[h][:C])
        pw = each(lambda h: x[h][C:])
        n *= 2
    tm = each(lambda h: tm[h] + _dot(tm[h], pw[h]))
    x2 = each(lambda h: _dot(tm[h], jnp.concatenate([at[h], akv[h]], axis=1)))
    x3 = each(lambda h: _dot(a_rb[h], x2[h]))
    x4 = each(lambda h: _dot((b_h[h] * e_end[:, sls[h]]).T, x2[h]))
    rkv = each(lambda h: _dot(a_rk[h], v[:, sls[h]]))
    kv = each(lambda h: _dot((k_mod[:, sls[h]] * e_end[:, sls[h]]).T, v[:, sls[h]]))
    rm = each(lambda h: jnp.concatenate([r_t[:, sls[h]] + x3[h][:, :HP],
                                         eye * p_end[:, sls[h]] + x4[h][:, :HP]], axis=0))
    x5 = each(lambda h: _dot(rm[h], h_ref[h]))
    outs = []
    for h in heads:
        sl = sls[h]
        h_ref[h] = x5[h][C:] + x4[h][:, HP:] + kv[h]
        y = x5[h][:C] + x3[h][:, HP:] + rkv[h]
        mean = jnp.sum(y, axis=-1, keepdims=True) * (1.0 / A_HEAD_DIM)
        dlt = jnp.where(valid, y - mean, 0.0)
        var = jnp.sum(dlt * dlt, axis=-1, keepdims=True) * (1.0 / A_HEAD_DIM)
        yn = dlt * lax.rsqrt(var + A_GN_EPS) * gng_ref[:, sl] + gnb_ref[:, sl]
        bonus = jnp.sum(r[:, sl] * k_mod[:, sl] * rk_ref[:, sl], axis=-1, keepdims=True) * v[:, sl]
        outs.append((yn + bonus) * g[:, sl])
    o_ref[0] = jnp.concatenate(outs, axis=1).astype(o_ref.dtype)


def _pad_heads(t, heads):
    hd = t.shape[-1] // heads
    t = t.reshape(t.shape[:-1] + (heads, hd))
    t = jnp.pad(t, [(0, 0)] * (t.ndim - 1) + [(0, HP - hd)])
    return t.reshape(t.shape[:-2] + (heads * HP,))


def rwkv7_pad_in_weight(w_a):
    r, k, v, low = w_a[:, :A_WIDTH], w_a[:, A_WIDTH:2 * A_WIDTH], w_a[:, 2 * A_WIDTH:3 * A_WIDTH], w_a[:, 3 * A_WIDTH:]
    return jnp.concatenate([_pad_heads(r, A_HEADS), _pad_heads(k, A_HEADS), _pad_heads(v, A_HEADS), low], axis=1)


def rwkv7(p, mu, w0, w_up, a0, a_up, g_up, k_k, k_a, r_k, gn_g, gn_b):
    bsz, seq, _ = p.shape
    C = A_CHUNK
    assert seq % C == 0
    mu3 = jnp.concatenate([_pad_heads(mu[i * A_WIDTH:(i + 1) * A_WIDTH], A_HEADS) for i in range(3)]
                          + [mu[3 * A_WIDTH:]]).reshape(1, -1)
    z = jnp.zeros((A_RANK_W, A_PADW), F32)
    wa = jnp.concatenate([jnp.concatenate([_pad_heads(w_up, A_HEADS), z], axis=1),
                          jnp.concatenate([z, _pad_heads(a_up, A_HEADS)], axis=1)], axis=0).astype(BF16)
    vecs = [_pad_heads(t.reshape(-1), A_HEADS).reshape(1, -1) for t in (k_k, k_a, r_k, gn_g, gn_b)]
    full = lambda shape: pl.BlockSpec(shape, lambda b, c: (0,) * len(shape))
    return pl.pallas_call(
        _rwkv_kernel,
        grid=(bsz, seq // C),
        in_specs=[pl.BlockSpec((1, C, A_PCOLS), lambda b, c: (b, c, 0)),
                  full((1, A_PCOLS)), full((1, A_PADW)), full((1, A_PADW)),
                  full((LANE, 2 * A_PADW)), full((A_RANK_G, A_PADW))] + [full((1, A_PADW))] * 5,
        out_specs=pl.BlockSpec((1, C, A_PADW), lambda b, c: (b, c, 0)),
        out_shape=jax.ShapeDtypeStruct((bsz, seq, A_PADW), BF16),
        scratch_shapes=[pltpu.VMEM((SUBLANE, A_PCOLS), F32), pltpu.VMEM((A_HEADS, HP, HP), F32)],
        compiler_params=pltpu.CompilerParams(
            dimension_semantics=("arbitrary", "arbitrary"), vmem_limit_bytes=VMEM_LIMIT),
        name="rwkv7",
    )(p, mu3, _pad_heads(w0, A_HEADS).reshape(1, -1), _pad_heads(a0, A_HEADS).reshape(1, -1), wa,
      _pad_heads(g_up, A_HEADS).astype(BF16), *vecs)


B_CHUNK = 64
B_SUB = 16


def _dot_tl(x, y):
    return lax.dot_general(x.astype(BF16), y.astype(BF16), (((0,), (0,)), ((), ())), preferred_element_type=F32)


def _hgrn_kernel(p_ref, lb_ref, gn_ref, o_ref, st_ref):
    C, SB, W = B_CHUNK, B_SUB, B_WIDTH
    nsb = C // SB
    D = B_KEY_DIM

    @pl.when(pl.program_id(1) == 0)
    def _():
        st_ref[...] = jnp.zeros_like(st_ref)

    p = p_ref[0].astype(F32)
    q, f, val, gate = p[:, :W], p[:, W:2 * W], p[:, 2 * W:3 * W], p[:, 3 * W:]
    lb = lb_ref[...]
    log_f = jnp.logaddexp(jnp.log(lb), jnp.log1p(-lb) + jax.nn.log_sigmoid(f))
    k = (1.0 - lb) * jax.nn.sigmoid(-f)
    row = lax.broadcasted_iota(jnp.int32, (C, C), 0)
    col = lax.broadcasted_iota(jnp.int32, (C, C), 1)
    b = jnp.dot((row >= col).astype(F32), log_f, precision=lax.Precision.HIGHEST,
                preferred_element_type=F32)
    b_end = b[C - 1:C, :]
    b0 = jnp.concatenate(
        [jnp.zeros((SB, W), F32)] + [jnp.broadcast_to(b[i * SB - 1:i * SB, :], (SB, W)) for i in range(1, nsb)],
        axis=0)
    qt = q * jnp.exp(b - b0)
    qe = q * jnp.exp(b)
    ke = k * jnp.exp(b_end - b)
    e_end = jnp.exp(b_end)
    rowd = lax.broadcasted_iota(jnp.int32, (C, D), 0)
    lane_c = lax.broadcasted_iota(jnp.int32, (SB, C), 1)
    row_sb = lax.broadcasted_iota(jnp.int32, (SB, C), 0)

    heads = range(B_HEADS)
    sls = [slice(h * D, (h + 1) * D) for h in heads]
    each = lambda fn: [fn(h) for h in heads]

    def off_operands(h):
        kts, qts = [], []
        for i in range(1, nsb):
            b0_i = b[i * SB - 1:i * SB, sls[h]]
            kts.append(jnp.where(rowd < i * SB, k[:, sls[h]] * jnp.exp(jnp.minimum(b0_i - b[:, sls[h]], 0.0)), 0.0))
            qts.append(jnp.where((rowd >= i * SB) & (rowd < (i + 1) * SB), qt[:, sls[h]], 0.0))
        return jnp.concatenate(kts, axis=1), jnp.concatenate(qts, axis=1)
    offs = each(off_operands)
    st_off = each(lambda h: _dot_t(offs[h][0], offs[h][1]))

    def diag_t(h):
        blocks = []
        for i in range(nsb):
            rs = slice(i * SB, (i + 1) * SB)
            k_i, b_i = k[rs, sls[h]], b[rs, sls[h]]
            acc = jnp.zeros((SB, C), F32)
            for j in range(SB):
                t = i * SB + j
                z = k_i * q[t:t + 1, sls[h]] * jnp.exp(jnp.minimum(b[t:t + 1, sls[h]] - b_i, 0.0))
                acc = jnp.where(lane_c == t, jnp.sum(z, axis=-1, keepdims=True), acc)
            blocks.append(jnp.where(row_sb + i * SB <= lane_c, acc, 0.0))
        return jnp.concatenate(blocks, axis=0)
    st_diag = each(diag_t)
    o_intra = each(lambda h: _dot_tl(st_off[h] + st_diag[h], val[:, sls[h]]))
    o_state = each(lambda h: _dot_t(qe[:, sls[h]], st_ref[h]))
    upd = each(lambda h: _dot_tl(val[:, sls[h]], ke[:, sls[h]]))
    outs = []
    for h in heads:
        st_ref[h] = st_ref[h] * e_end[:, sls[h]] + upd[h]
        o = o_intra[h] + o_state[h]
        o = o * lax.rsqrt(jnp.mean(o * o, axis=-1, keepdims=True) + NORM_EPS)
        gt = gate[:, sls[h]]
        outs.append(o * gn_ref[:, sls[h]] * (gt * jax.nn.sigmoid(gt)))
    o_ref[0] = jnp.concatenate(outs, axis=1).astype(o_ref.dtype)


def hgrn2(p, lb, gn_g):
    bsz, seq, _ = p.shape
    C = B_CHUNK
    assert seq % C == 0
    vec = pl.BlockSpec((1, B_WIDTH), lambda b, c: (0, 0))
    return pl.pallas_call(
        _hgrn_kernel,
        grid=(bsz, seq // C),
        in_specs=[pl.BlockSpec((1, C, B_COLS), lambda b, c: (b, c, 0)), vec, vec],
        out_specs=pl.BlockSpec((1, C, B_WIDTH), lambda b, c: (b, c, 0)),
        out_shape=jax.ShapeDtypeStruct((bsz, seq, B_WIDTH), BF16),
        scratch_shapes=[pltpu.VMEM((B_HEADS, B_VAL_DIM, B_KEY_DIM), F32)],
        compiler_params=pltpu.CompilerParams(
            dimension_semantics=("arbitrary", "arbitrary"), vmem_limit_bytes=VMEM_LIMIT),
        name="hgrn2",
    )(p, lb.reshape(1, -1), gn_g.reshape(1, -1))


C_PAD = _round_up(C_COLS, LANE)
QB = 128
C_KV_W = C_KV_HEADS * C_HEAD_DIM
C_K_OFF = C_WIDTH
C_QI_OFF = C_WIDTH + 2 * C_KV_W
C_KI_OFF = C_QI_OFF + IDX_HEADS * IDX_DIM
INT_MIN = -2 ** 31
NEG = -1e30
HALF = C_HEAD_DIM // 2


def _rope(t, cos, sin_signed):
    outs = []
    lane = lax.broadcasted_iota(jnp.int32, (t.shape[0], LANE), 1)
    first_half = (lane % C_HEAD_DIM) < HALF
    for c in range(t.shape[1] // LANE):
        x = t[:, c * LANE:(c + 1) * LANE]
        swapped = jnp.where(first_half, pltpu.roll(x, LANE - HALF, 1), pltpu.roll(x, HALF, 1))
        outs.append(x * cos + swapped * sin_signed)
    return outs


GROUP = C_HEADS // C_KV_HEADS


def _split_hi_lo(x):
    hi = x.astype(BF16)
    return hi, (x - hi.astype(F32)).astype(BF16)


def _dsa_kernel(pq_ref, kv_ref, ki_ref, cosq_ref, sinq_ref, cosf_ref, sinf_ref, o_ref,
                k_s, vt_s, ki_s, key_s, *, k_sel, seq):
    i = pl.program_id(1)
    nk = i + 1
    lane = lax.broadcasted_iota(jnp.int32, (QB, LANE), 1)
    rowi = lax.broadcasted_iota(jnp.int32, (QB, LANE), 0)
    low_half = lane < C_HEAD_DIM

    @pl.when(i == 0)
    def _():
        kv = kv_ref[0].astype(F32)
        k_s[...] = _rope(kv[:, :LANE], cosf_ref[...], sinf_ref[...])[0].astype(BF16)
        vt = kv[:, LANE:].T
        row_f = lax.broadcasted_iota(jnp.int32, (LANE, seq), 0)
        for c in range(C_KV_HEADS):
            in_c = (row_f >= c * C_HEAD_DIM) & (row_f < (c + 1) * C_HEAD_DIM)
            vt_s[c] = jnp.where(in_c, vt, jnp.where(row_f == (1 - c) * C_HEAD_DIM, 1.0, 0.0)).astype(BF16)
        lane_f = lax.broadcasted_iota(jnp.int32, (seq, LANE), 1)
        ki = jnp.where(lane_f < IDX_DIM, ki_ref[0].astype(F32), 0.0)
        ki = _rope(ki, cosf_ref[...], sinf_ref[...])[0]
        ki = ki + pltpu.roll(ki, IDX_DIM, 1)
        hi, lo = _split_hi_lo(ki)
        ki_s[...] = jnp.concatenate([hi, lo, hi], axis=1)

    pq = pq_ref[0].astype(F32)
    cos, sin = cosq_ref[...], sinq_ref[...]
    q = _rope(pq[:, :C_WIDTH], cos, sin)
    qi = _rope(pq[:, C_QI_OFF:C_KI_OFF], cos, sin)
    qi_rows = jnp.concatenate(
        [jnp.where(low_half if h % 2 == 0 else ~low_half, qi[h // 2], 0.0) for h in range(IDX_HEADS)], axis=0)
    hi, lo = _split_hi_lo(qi_rows)
    qi3 = jnp.concatenate([hi, hi, lo], axis=1)
    wt = pq[:, C_KI_OFF:C_KI_OFF + LANE].T
    wscale = (IDX_HEADS ** -0.5) * (IDX_DIM ** -0.5)
    w_rows = [wt[IDX_DIM + h:IDX_DIM + h + 1, :] * wscale for h in range(IDX_HEADS)]
    causal_off = i * QB + lane

    npair = (nk + 1) // 2

    def score_tile(jp, _):
        ki_t = ki_s[pl.ds(pl.multiple_of(jp * (2 * QB), 2 * QB), 2 * QB), :]
        d = lax.dot_general(ki_t, qi3, (((1,), (1,)), ((), ())), preferred_element_type=F32)
        for u in range(2):
            j = 2 * jp + u
            sc = jnp.zeros((QB, QB), F32)
            for h in range(IDX_HEADS):
                sc = sc + jnp.maximum(d[u * QB:(u + 1) * QB, h * QB:(h + 1) * QB], 0.0) * w_rows[h]
            bits = pltpu.bitcast(sc + 0.0, jnp.int32)
            key = jnp.where(bits < 0, bits ^ jnp.int32(0x7fffffff), bits)
            key_s[j] = jnp.where(j * QB + rowi <= causal_off, key, jnp.int32(INT_MIN))
        return 0
    lax.fori_loop(0, npair, score_tile, 0)

    def count(pred):
        def body(jp, acc):
            for u in range(2):
                acc = acc + jnp.where(pred(key_s[2 * jp + u], 2 * jp + u), 1, 0)
            return acc
        return jnp.sum(lax.fori_loop(0, npair, body, jnp.zeros((QB, QB), jnp.int32)), axis=0, keepdims=True)

    def bit_step(b, cand):
        trial = cand | jnp.left_shift(jnp.int32(1), 31 - b)
        cnt = count(lambda key, j: key >= (trial ^ jnp.int32(INT_MIN)))
        return jnp.where(cnt >= k_sel, trial, cand)
    cand = lax.fori_loop(0, 32, bit_step, jnp.zeros((1, QB), jnp.int32))
    thr = cand ^ jnp.int32(INT_MIN)
    need = k_sel - count(lambda key, j: key > thr)

    need_f = need.astype(F32)
    tri = (lax.broadcasted_iota(jnp.int32, (2 * QB, 2 * QB), 0)
           >= lax.broadcasted_iota(jnp.int32, (2 * QB, 2 * QB), 1)).astype(BF16)

    scale = C_HEAD_DIM ** -0.5
    kvs = range(C_KV_HEADS)
    in_cs = [(lane >= c * C_HEAD_DIM) & (lane < (c + 1) * C_HEAD_DIM) for c in kvs]

    def q_rows(c):
        rows = []
        for g in range(GROUP):
            h = c * GROUP + g
            x = q[h // 2]
            if h % 2 != c:
                x = pltpu.roll(x, C_HEAD_DIM, 1)
            rows.append(jnp.where(in_cs[c], x * scale, 0.0))
        return jnp.concatenate(rows, axis=0).astype(BF16)
    qcs = [q_rows(c) for c in kvs]

    def att_tile(jp, carry):
        off = pl.multiple_of(jp * (2 * QB), 2 * QB)
        keys = jnp.concatenate([key_s[2 * jp], key_s[2 * jp + 1]], axis=0)
        pos = off + lax.broadcasted_iota(jnp.int32, (2 * QB, LANE), 0)
        tie = keys == thr
        rank = carry[-1] + jnp.dot(tri, jnp.where(tie, 1.0, 0.0).astype(BF16), preferred_element_type=F32)
        sel = ((keys > thr) | (tie & (rank <= need_f))) & (pos <= causal_off[:1, :])
        bias1 = jnp.where(sel, 0.0, NEG)
        bias = jnp.concatenate([bias1] * GROUP, axis=1)
        k_t = k_s[pl.ds(off, 2 * QB), :]
        s = [lax.dot_general(k_t, qcs[c], (((1,), (1,)), ((), ())), preferred_element_type=F32) for c in kvs]
        s = [jnp.where(bias < 0.0, NEG, s[c]) for c in kvs]
        m_new = [jnp.maximum(carry[2 * c], jnp.max(s[c], axis=0, keepdims=True)) for c in kvs]
        p = [jnp.exp(s[c] - m_new[c]).astype(BF16) for c in kvs]
        pv = [jnp.dot(vt_s[c, :, pl.ds(off, 2 * QB)], p[c], preferred_element_type=F32) for c in kvs]
        out = []
        for c in kvs:
            out += [m_new[c], carry[2 * c + 1] * jnp.exp(carry[2 * c] - m_new[c]) + pv[c]]
        return tuple(out) + (rank[2 * QB - 1:, :],)
    init = ((jnp.full((1, GROUP * QB), NEG, F32), jnp.zeros((LANE, GROUP * QB), F32)) * C_KV_HEADS
            + (jnp.zeros((1, QB), F32),))
    res = lax.fori_loop(0, npair, att_tile, init)

    heads_out = [None] * C_HEADS
    for c in kvs:
        in_c = in_cs[c]
        acc = res[2 * c + 1]
        ones_row = (1 - c) * C_HEAD_DIM
        out_t = acc / acc[ones_row:ones_row + 1, :]
        for g in range(GROUP):
            h = c * GROUP + g
            x = jnp.where(in_c, out_t[:, g * QB:(g + 1) * QB].T, 0.0)
            if h % 2 != c:
                x = pltpu.roll(x, C_HEAD_DIM, 1)
            heads_out[h] = x
    o_ref[0] = jnp.concatenate([heads_out[2 * m] + heads_out[2 * m + 1] for m in range(C_HEADS // 2)],
                               axis=1).astype(o_ref.dtype)


def _dsa_kernel_rows(pq_ref, kv_ref, ki_ref, cosq_ref, sinq_ref, cosf_ref, sinf_ref, o_ref,
                     k_s, v_s, ki_s, key_s, *, k_sel, seq):
    i = pl.program_id(1)
    nk = i + 1
    lane = lax.broadcasted_iota(jnp.int32, (QB, LANE), 1)
    rowi = lax.broadcasted_iota(jnp.int32, (QB, LANE), 0)
    low_half = lane < C_HEAD_DIM

    @pl.when(i == 0)
    def _():
        kv = kv_ref[0]
        k_s[...] = _rope(kv[:, :LANE], cosf_ref[...], sinf_ref[...])[0].astype(BF16)
        v_s[...] = kv[:, LANE:].astype(BF16)
        lane_f = lax.broadcasted_iota(jnp.int32, (seq, LANE), 1)
        ki = jnp.where(lane_f < IDX_DIM, ki_ref[0], 0.0)
        ki = _rope(ki, cosf_ref[...], sinf_ref[...])[0]
        ki_s[...] = ki + pltpu.roll(ki, IDX_DIM, 1)

    pq = pq_ref[0]
    cos, sin = cosq_ref[...], sinq_ref[...]
    q = _rope(pq[:, :C_WIDTH], cos, sin)
    qi = _rope(pq[:, C_QI_OFF:C_KI_OFF], cos, sin)
    wslab = pq[:, C_KI_OFF:C_KI_OFF + LANE]
    wscale = (IDX_HEADS ** -0.5) * (IDX_DIM ** -0.5)
    wi = [jnp.sum(jnp.where(lane == IDX_DIM + h, wslab, 0.0), axis=-1, keepdims=True) * wscale
          for h in range(IDX_HEADS)]
    qi_h = [jnp.where(low_half if h % 2 == 0 else ~low_half, qi[h // 2], 0.0) for h in range(IDX_HEADS)]

    def score_tile(j, _):
        ki_t = ki_s[pl.ds(pl.multiple_of(j * QB, QB), QB), :]
        sc = jnp.zeros((QB, QB), F32)
        for h in range(IDX_HEADS):
            d = lax.dot_general(qi_h[h], ki_t, (((1,), (1,)), ((), ())), precision=lax.Precision.HIGHEST,
                                preferred_element_type=F32)
            sc = sc + jnp.maximum(d, 0.0) * wi[h]
        bits = pltpu.bitcast(sc + 0.0, jnp.int32)
        key = jnp.where(bits < 0, bits ^ jnp.int32(0x7fffffff), bits)
        key_s[j] = jnp.where(j * QB + lane <= i * QB + rowi, key, jnp.int32(INT_MIN))
        return 0
    lax.fori_loop(0, nk, score_tile, 0)

    def count(pred):
        def body(j, acc):
            return acc + jnp.where(pred(key_s[j], j), 1, 0)
        return jnp.sum(lax.fori_loop(0, nk, body, jnp.zeros((QB, QB), jnp.int32)), axis=-1, keepdims=True)

    def bit_step(b, cand):
        trial = cand | jnp.left_shift(jnp.int32(1), 31 - b)
        cnt = count(lambda key, j: key >= (trial ^ jnp.int32(INT_MIN)))
        return jnp.where(cnt >= k_sel, trial, cand)
    cand = lax.fori_loop(0, 32, bit_step, jnp.zeros((QB, 1), jnp.int32))
    thr = cand ^ jnp.int32(INT_MIN)
    need = k_sel - count(lambda key, j: key > thr)

    nbits = int(np.ceil(np.log2(seq))) + 1

    def idx_step(b, jj):
        trial = jj | jnp.left_shift(jnp.int32(1), nbits - 1 - b)
        cnt = count(lambda key, j: (key == thr) & (j * QB + lane < trial))
        return jnp.where(cnt <= need, trial, jj)
    jlim = lax.fori_loop(0, nbits, idx_step, jnp.zeros((QB, 1), jnp.int32))

    group = C_HEADS // C_KV_HEADS
    scale = C_HEAD_DIM ** -0.5
    outs = []
    for c in range(C_KV_HEADS):
        in_c = (lane >= c * C_HEAD_DIM) & (lane < (c + 1) * C_HEAD_DIM)
        rows = []
        for g in range(group):
            h = c * group + g
            x = q[h // 2]
            if h % 2 != c:
                x = pltpu.roll(x, C_HEAD_DIM, 1)
            rows.append(jnp.where(in_c, x * scale, 0.0))
        qc = jnp.concatenate(rows, axis=0).astype(BF16)
        ones_lane = (1 - c) * C_HEAD_DIM

        def att_tile(j, carry):
            m, acc = carry
            off = pl.multiple_of(j * QB, QB)
            key = key_s[j]
            sel = ((key > thr) | ((key == thr) & (j * QB + lane < jlim))) & (j * QB + lane <= i * QB + rowi)
            bias = jnp.where(sel, 0.0, NEG)
            s = lax.dot_general(qc, k_s[pl.ds(off, QB), :], (((1,), (1,)), ((), ())),
                                preferred_element_type=F32)
            s = jnp.where(jnp.concatenate([bias] * group, axis=0) < 0.0, NEG, s)
            m_new = jnp.maximum(m, jnp.max(s, axis=-1, keepdims=True))
            alpha = jnp.exp(m - m_new)
            p = jnp.exp(s - m_new)
            v_t = v_s[pl.ds(off, QB), :]
            v_t = jnp.where(in_c, v_t, jnp.where(lane == ones_lane, 1.0, 0.0).astype(BF16))
            acc = acc * alpha + jnp.dot(p.astype(BF16), v_t, preferred_element_type=F32)
            return m_new, acc
        m0 = jnp.full((group * QB, 1), NEG, F32)
        acc0 = jnp.zeros((group * QB, LANE), F32)
        _, acc = lax.fori_loop(0, nk, att_tile, (m0, acc0))
        l = jnp.sum(jnp.where(jnp.concatenate([lane] * group, axis=0) == ones_lane, acc, 0.0),
                    axis=-1, keepdims=True)
        out = acc / l
        for g in range(group):
            h = c * group + g
            x = jnp.where(in_c, out[g * QB:(g + 1) * QB], 0.0)
            if h % 2 != c:
                x = pltpu.roll(x, C_HEAD_DIM, 1)
            outs.append(x)
    o_ref[0] = jnp.concatenate([outs[2 * m] + outs[2 * m + 1] for m in range(C_HEADS // 2)],
                               axis=1).astype(o_ref.dtype)


def _rope_lanes(seq):
    inv_freq = ROPE_THETA ** (-jnp.arange(HALF, dtype=F32) * (2.0 / C_HEAD_DIM))
    ang = jnp.arange(seq, dtype=F32)[:, None] * inv_freq[None, :]
    cos, sin = jnp.cos(ang), jnp.sin(ang)
    reps = LANE // C_HEAD_DIM
    return (jnp.tile(jnp.concatenate([cos, cos], axis=1), (1, reps)),
            jnp.tile(jnp.concatenate([-sin, sin], axis=1), (1, reps)))


def dsa(p):
    bsz, seq, width = p.shape
    assert seq % (2 * QB) == 0 and width == C_PAD
    k_sel = min(TOPK_MAX, seq // 4)
    cos, sin = _rope_lanes(seq)
    return pl.pallas_call(
        functools.partial(_dsa_kernel, k_sel=k_sel, seq=seq),
        grid=(bsz, seq // QB),
        in_specs=[pl.BlockSpec((1, QB, C_PAD), lambda b, i: (b, i, 0)),
                  pl.BlockSpec((1, seq, 2 * C_KV_W), lambda b, i: (b, 0, C_K_OFF // (2 * C_KV_W))),
                  pl.BlockSpec((1, seq, LANE), lambda b, i: (b, 0, C_KI_OFF // LANE)),
                  pl.BlockSpec((QB, LANE), lambda b, i: (i, 0)),
                  pl.BlockSpec((QB, LANE), lambda b, i: (i, 0)),
                  pl.BlockSpec((seq, LANE), lambda b, i: (0, 0)),
                  pl.BlockSpec((seq, LANE), lambda b, i: (0, 0))],
        out_specs=pl.BlockSpec((1, QB, C_WIDTH), lambda b, i: (b, i, 0)),
        out_shape=jax.ShapeDtypeStruct((bsz, seq, C_WIDTH), BF16),
        scratch_shapes=[pltpu.VMEM((seq, LANE), BF16), pltpu.VMEM((C_KV_HEADS, LANE, seq), BF16),
                        pltpu.VMEM((seq, 3 * LANE), BF16), pltpu.VMEM((seq // QB, QB, QB), jnp.int32)],
        compiler_params=pltpu.CompilerParams(
            dimension_semantics=("arbitrary", "arbitrary"), vmem_limit_bytes=VMEM_LIMIT),
        name="dsa",
    )(p, p, p, cos, sin, cos, sin)


def _split(t, sizes):
    out, start = [], 0
    for s in sizes:
        out.append(t[..., start:start + s])
        start += s
    return out


def _mm(x, y):
    return jnp.einsum('...ij,...jk->...ik', x.astype(BF16), y.astype(BF16), preferred_element_type=F32)


def _mm_hi(x, y):
    return jnp.einsum('...ij,...jk->...ik', x, y, precision=lax.Precision.HIGHEST)


def _rwkv7_chunked(r, logw, k, v, a, b, C=64):
    B, S, H, N = r.shape
    nc = S // C

    def ch(t):
        return t.reshape(B, nc, C, H, N).transpose(1, 0, 3, 2, 4)
    r, logw, k, v, a, b = map(ch, (r, logw, k, v, a, b))
    cum = jnp.cumsum(logw, axis=-2)
    cum_prev = cum - logw
    cum_end = cum[..., -1:, :]
    At = a * jnp.exp(cum_prev)
    Rt = r * jnp.exp(cum)
    Bt = b * jnp.exp(-cum)
    Kt = k * jnp.exp(-cum)
    Bh = b * jnp.exp(cum_end - cum)
    Kh = k * jnp.exp(cum_end - cum)
    Pc = jnp.exp(cum_end)[..., 0, :]
    T_ = lambda x: jnp.swapaxes(x, -1, -2)
    ti = jnp.arange(C)
    strict = ti[:, None] > ti[None, :]
    incl = ti[:, None] >= ti[None, :]
    Aab = jnp.where(strict, _mm(At, T_(Bt)), 0.0)
    Aak = jnp.where(strict, _mm(At, T_(Kt)), 0.0)
    Arb = jnp.where(incl, _mm(Rt, T_(Bt)), 0.0)
    Ark = jnp.where(incl, _mm(Rt, T_(Kt)), 0.0)
    Tm = jnp.eye(C, dtype=F32) + Aab
    Pw = Aab
    n = 1
    while 2 * n < C:
        Pw = _mm_hi(Pw, Pw)
        Tm = Tm + _mm_hi(Tm, Pw)
        n *= 2
    Abar = _mm(Tm, At)
    U0 = _mm(Tm, _mm(Aak, v))
    Rbar = Rt + _mm(Arb, Abar)
    Y0 = _mm(Arb, U0) + _mm(Ark, v)
    M = Pc[..., :, None] * jnp.eye(N, dtype=F32) + _mm(T_(Bh), Abar)
    Hd = _mm(T_(Bh), U0) + _mm(T_(Kh), v)

    def step(Hs, inp):
        Rbar_c, Y0_c, M_c, Hd_c = inp
        y = _mm(Rbar_c, Hs) + Y0_c
        Hs = _mm_hi(M_c, Hs) + Hd_c
        return Hs, y
    _, ys = lax.scan(step, jnp.zeros((B, H, N, N), F32), (Rbar, Y0, M, Hd))
    return ys.transpose(1, 0, 3, 2, 4).reshape(B, S, H, N)


def _token_shift(p):
    return jnp.pad(p, ((0, 0), (1, 0), (0, 0)))[:, :-1]


def _rwkv7_branch(p, mu, w0, w_up, a0, a_up, g_up, k_k, k_a, r_k, gn_g, gn_b):
    bsz, seq, _ = p.shape
    p = p + (_token_shift(p) - p) * mu
    r, k, v, wd, ad, gd = _split(p, (A_WIDTH, A_WIDTH, A_WIDTH, A_RANK_W, A_RANK_A, A_RANK_G))
    w_log = -jax.nn.softplus(-(w0 + _mm(jnp.tanh(wd), w_up))) - 0.5
    logw = -jnp.exp(w_log)
    a = jax.nn.sigmoid(a0 + _mm(ad, a_up))
    g = _mm(jax.nn.sigmoid(gd), g_up)

    def hd(t):
        return t.reshape(bsz, seq, A_HEADS, A_HEAD_DIM)
    kk = hd(k * k_k)
    kk = kk * lax.rsqrt(jnp.maximum(jnp.sum(kk * kk, axis=-1, keepdims=True), 1e-24))
    k_mod = hd(k * (1.0 + (a - 1.0) * k_a))
    r_h, v_h, a_h = hd(r), hd(v), hd(a)
    y = _rwkv7_chunked(r_h, hd(logw), k_mod, v_h, -kk, kk * a_h)
    mean = jnp.mean(y, axis=-1, keepdims=True)
    var = jnp.mean(jnp.square(y - mean), axis=-1, keepdims=True)
    y = ((y - mean) * lax.rsqrt(var + A_GN_EPS)).reshape(bsz, seq, A_WIDTH)
    y = y * gn_g + gn_b
    bonus = jnp.sum(r_h * k_mod * r_k, axis=-1, keepdims=True) * v_h
    y = y + bonus.reshape(bsz, seq, A_WIDTH)
    return y * g


def _hgrn2_sub(q, k, g, v, C=64, SB=16):
    B, S, H, dk = q.shape
    dv = v.shape[-1]
    nc = S // C

    def ch(t):
        return t.reshape(B, nc, C, H, t.shape[-1]).transpose(1, 0, 3, 2, 4)
    q, k, g, v = map(ch, (q, k, g, v))
    b = jnp.cumsum(g, axis=-2)
    nsb = C // SB
    T_ = lambda x: jnp.swapaxes(x, -1, -2)

    def step(state, inp):
        qc, kc, bc, vc = inp
        rows = []
        for I in range(nsb):
            sl = slice(I * SB, (I + 1) * SB)
            qI, bI = qc[..., sl, :], bc[..., sl, :]
            b0 = bc[..., I * SB - 1:I * SB, :] if I > 0 else jnp.zeros_like(bc[..., :1, :])
            qt = qI * jnp.exp(bI - b0)
            blocks = []
            if I > 0:
                kt = kc[..., :I * SB, :] * jnp.exp(b0 - bc[..., :I * SB, :])
                blocks.append(_mm(qt, T_(kt)))
            diff = bI[..., :, None, :] - bI[..., None, :, :]
            tri = jnp.tril(jnp.ones((SB, SB), bool))
            dec = jnp.exp(jnp.where(tri[:, :, None], diff, -jnp.inf))
            diag = jnp.sum(qI[..., :, None, :] * kc[..., sl, :][..., None, :, :] * dec, axis=-1)
            blocks.append(diag)
            if I < nsb - 1:
                blocks.append(jnp.zeros(diag.shape[:-1] + (C - (I + 1) * SB,), F32))
            rows.append(jnp.concatenate(blocks, axis=-1))
        scores = jnp.concatenate(rows, axis=-2)
        o = _mm(scores, vc) + _mm(qc * jnp.exp(bc), state)
        b_end = bc[..., -1:, :]
        state = state * T_(jnp.exp(b_end)) + _mm(T_(kc * jnp.exp(b_end - bc)), vc)
        return state, o
    _, o = lax.scan(step, jnp.zeros((B, H, dk, dv), F32), (q, k, b, v))
    return o.transpose(1, 0, 3, 2, 4).reshape(B, S, H, dv)


def _hgrn2_branch(p, lb, gn_g):
    bsz, seq, _ = p.shape
    q, f, i, g = _split(p, (B_WIDTH,) * 4)
    log_f = jnp.logaddexp(jnp.log(lb), jnp.log1p(-lb) + jax.nn.log_sigmoid(f))
    k_in = (1.0 - lb) * jax.nn.sigmoid(-f)
    hd = lambda t: t.reshape(bsz, seq, B_HEADS, -1)
    o = _hgrn2_sub(hd(q), hd(k_in), hd(log_f), hd(i))
    o = o * lax.rsqrt(jnp.mean(o * o, axis=-1, keepdims=True) + NORM_EPS)
    return o.reshape(bsz, seq, B_WIDTH) * gn_g * jax.nn.silu(g)


def _rope_tables(seq, dim):
    half = dim // 2
    inv_freq = ROPE_THETA ** (-jnp.arange(half, dtype=F32) * (2.0 / dim))
    ang = jnp.arange(seq, dtype=F32)[:, None] * inv_freq[None, :]
    return jnp.cos(ang), jnp.sin(ang)


def _apply_rope(t, cos, sin):
    half = t.shape[-1] // 2
    t1, t2 = t[..., :half], t[..., half:]
    c, s = cos[None, :, None, :], sin[None, :, None, :]
    return jnp.concatenate([t1 * c - t2 * s, t2 * c + t1 * s], axis=-1)


def _dsa_branch(p, cos, sin):
    bsz, seq, _ = p.shape
    k_sel = min(TOPK_MAX, seq // 4)
    kv_w = C_KV_HEADS * C_HEAD_DIM
    q, k, v, qi, ki, wi = _split(p[..., :C_COLS], (C_WIDTH, kv_w, kv_w, IDX_HEADS * IDX_DIM, IDX_DIM, IDX_HEADS))
    q = _apply_rope(q.reshape(bsz, seq, C_HEADS, C_HEAD_DIM), cos, sin)
    k = _apply_rope(k.reshape(bsz, seq, C_KV_HEADS, C_HEAD_DIM), cos, sin)
    v = v.reshape(bsz, seq, C_KV_HEADS, C_HEAD_DIM)
    qi = _apply_rope(qi.reshape(bsz, seq, IDX_HEADS, IDX_DIM), cos, sin)
    ki = _apply_rope(ki[:, :, None, :], cos, sin)[:, :, 0, :]
    wi = wi * (IDX_HEADS ** -0.5) * (IDX_DIM ** -0.5)
    dots = jnp.einsum('bqhd,bsd->bqhs', qi, ki, precision=lax.Precision.HIGHEST)
    score = jnp.sum(jax.nn.relu(dots) * wi[..., None], axis=2)
    pos = jnp.arange(seq)
    causal = pos[None, :] <= pos[:, None]
    bits = lax.bitcast_convert_type(score + 0.0, jnp.int32)
    key = jnp.where(bits < 0, bits ^ jnp.int32(0x7fffffff), bits)
    key = jnp.where(causal[None], key, jnp.int32(-2 ** 31))
    ukey = lax.bitcast_convert_type(key ^ jnp.int32(-2 ** 31), jnp.uint32)
    cand = jnp.zeros((bsz, seq, 1), jnp.uint32)
    for bit in range(31, -1, -1):
        trial = cand | jnp.uint32(1 << bit)
        cnt = jnp.sum((ukey >= trial).astype(jnp.int32), axis=-1, keepdims=True)
        cand = jnp.where(cnt >= k_sel, trial, cand)
    gt = ukey > cand
    eq = ukey == cand
    need = k_sel - jnp.sum(gt.astype(jnp.int32), axis=-1, keepdims=True)
    J = jnp.zeros((bsz, seq, 1), jnp.int32)
    for bit in range(int(np.ceil(np.log2(seq))), -1, -1):
        trial = J | (1 << bit)
        cnt = jnp.sum((eq & (pos[None, None, :] < trial)).astype(jnp.int32), axis=-1, keepdims=True)
        J = jnp.where(cnt <= need, trial, J)
    sel = (gt | (eq & (pos[None, None, :] < J))) & causal[None]
    group = C_HEADS // C_KV_HEADS
    qg = q.reshape(bsz, seq, C_KV_HEADS, group, C_HEAD_DIM)
    logits = jnp.einsum('bqcgd,bscd->bcgqs', qg.astype(BF16), k.astype(BF16),
                        preferred_element_type=F32) * (C_HEAD_DIM ** -0.5)
    logits = jnp.where(sel[:, None, None], logits, -jnp.inf)
    prob = jax.nn.softmax(logits, axis=-1)
    out = jnp.einsum('bcgqs,bscd->bqcgd', prob.astype(BF16), v.astype(BF16), preferred_element_type=F32)
    return out.reshape(bsz, seq, C_WIDTH)


def kernel(x, norm_mix_g, w_in, a_mu, a_w0, a_w_up, a_a0, a_a_up, a_g_up, a_k_k, a_k_a, a_r_k, a_gn_g,
           a_gn_b, b_lb_logits, b_gn_g, w_branch, w_o, norm_ffn_g, w_up, conv_w, conv_b, w_down,
           norm_final_g):
    bsz, seq, d = x.shape
    t = bsz * seq
    lb_cum = jnp.cumsum(jax.nn.softmax(b_lb_logits.astype(F32), axis=0), axis=0)
    lower_bounds = lb_cum - lb_cum[0]
    h = x
    for layer in range(DEPTH):
        w = w_in[layer].astype(BF16)
        w_a, w_b, w_c, w_g = _split(w, (A_COLS, B_COLS, C_COLS, GATE_COLS))
        w_a = rwkv7_pad_in_weight(w_a)
        w_c = jnp.pad(w_c, ((0, 0), (0, C_PAD - C_COLS)))
        h2 = h.reshape(t, d)
        g = norm_mix_g[layer]
        p_a = norm_matmul(h2, g, w_a, tm=2048, tn=A_PCOLS // 2, out_dtype=BF16).reshape(bsz, seq, -1)
        p_b = norm_matmul(h2, g, w_b, tm=2048, tn=1024, out_dtype=BF16).reshape(bsz, seq, -1)
        p_c = norm_matmul(h2, g, w_c, tm=2048, tn=C_PAD, out_dtype=F32).reshape(bsz, seq, -1)
        p_g = norm_matmul(h2, g, w_g, tm=2048, tn=1024, out_dtype=BF16)
        y_a = rwkv7(p_a, a_mu[layer], a_w0[layer], a_w_up[layer], a_a0[layer], a_a_up[layer],
                    a_g_up[layer], a_k_k[layer], a_k_a[layer], a_r_k[layer], a_gn_g[layer], a_gn_b[layer])
        y_b = hgrn2(p_b, lower_bounds[layer], b_gn_g[layer])
        y_c = dsa(p_c)
        wbr = w_branch[layer].astype(BF16)
        wbr_a = jnp.pad(wbr[0].reshape(A_HEADS, A_HEAD_DIM, d), ((0, 0), (0, HP - A_HEAD_DIM), (0, 0)))
        h2 = merge((y_a.reshape(t, -1), y_b.reshape(t, -1), y_c.reshape(t, -1)), p_g, h2,
                   (wbr_a.reshape(A_PADW, d), wbr[1], wbr[2]), w_o[layer].astype(BF16), tm=512)
        h = conv_glu(h2.reshape(bsz, seq, d), norm_ffn_g[layer], w_up[layer].astype(BF16), conv_w[layer],
                     conv_b[layer], w_down[layer].astype(BF16),
                     norm_final_g if layer == DEPTH - 1 else None, tm=512, tf=1408)
    return h
```

```python
import functools

import jax
import jax.numpy as jnp
import numpy as np
from jax import lax
from jax.experimental import pallas as pl
from jax.experimental.pallas import tpu as pltpu

F32 = jnp.float32
BF16 = jnp.bfloat16

D_MODEL = 1024
DEPTH = 2
MIX_WIDTH = 512
N_BRANCH = 3
A_HEAD_DIM = 64
A_HEADS = 8
A_WIDTH = 512
A_RANK_W = 64
A_RANK_A = 64
A_RANK_G = 128
A_GN_EPS = 64e-5
B_KEY_DIM = 128
B_VAL_DIM = 128
B_HEADS = 4
B_WIDTH = 512
C_HEAD_DIM = 64
C_HEADS = 8
C_KV_HEADS = 2
C_WIDTH = 512
IDX_HEADS = 4
IDX_DIM = 64
TOPK_MAX = 256
ROPE_THETA = 10000.0
D_FF = 2816
CONV_WIDTH = 3
NORM_EPS = 1e-6

A_COLS = 3 * A_WIDTH + A_RANK_W + A_RANK_A + A_RANK_G
B_COLS = 4 * B_WIDTH
C_COLS = C_WIDTH + 2 * C_KV_HEADS * C_HEAD_DIM + IDX_HEADS * IDX_DIM + IDX_DIM + IDX_HEADS
GATE_COLS = N_BRANCH * D_MODEL

LANE = 128
SUBLANE = 8
VMEM_LIMIT = 48 * 1024 * 1024


def _round_up(n, m):
    return (n + m - 1) // m * m


def _rms(x, g):
    return x * lax.rsqrt(jnp.mean(x * x, axis=-1, keepdims=True) + NORM_EPS) * g


def _norm_matmul_kernel(x_ref, g_ref, w_ref, o_ref, xn_ref):
    @pl.when(pl.program_id(1) == 0)
    def _():
        xn_ref[...] = _rms(x_ref[...], g_ref[...]).astype(BF16)

    o_ref[...] = jnp.dot(xn_ref[...], w_ref[...], preferred_element_type=F32).astype(o_ref.dtype)


def norm_matmul(x, g, w, *, tm, tn, out_dtype):
    t, d = x.shape
    n = w.shape[1]
    assert t % tm == 0 and n % tn == 0
    return pl.pallas_call(
        _norm_matmul_kernel,
        grid=(t // tm, n // tn),
        in_specs=[
            pl.BlockSpec((tm, d), lambda i, j: (i, 0)),
            pl.BlockSpec((1, d), lambda i, j: (0, 0)),
            pl.BlockSpec((d, tn), lambda i, j: (0, j)),
        ],
        out_specs=pl.BlockSpec((tm, tn), lambda i, j: (i, j)),
        out_shape=jax.ShapeDtypeStruct((t, n), out_dtype),
        scratch_shapes=[pltpu.VMEM((tm, d), BF16)],
        compiler_params=pltpu.CompilerParams(
            dimension_semantics=("arbitrary", "arbitrary"), vmem_limit_bytes=VMEM_LIMIT),
        name="norm_matmul",
    )(x, g.reshape(1, d), w)


def _merge_kernel(ya_ref, yb_ref, yc_ref, gate_ref, h_ref, wa_ref, wb_ref, wc_ref, wo_ref, o_ref):
    merged = None
    for b, (y_ref, w_ref) in enumerate(((ya_ref, wa_ref), (yb_ref, wb_ref), (yc_ref, wc_ref))):
        br = jnp.dot(y_ref[...].astype(BF16), w_ref[...], preferred_element_type=F32)
        gt = jax.nn.sigmoid(gate_ref[:, b * D_MODEL:(b + 1) * D_MODEL].astype(F32))
        merged = gt * br if merged is None else merged + gt * br
    o_ref[...] = h_ref[...] + jnp.dot(merged.astype(BF16), wo_ref[...], preferred_element_type=F32)


def merge(ys, gate, h, ws, w_o, *, tm):
    t = h.shape[0]
    assert t % tm == 0
    row = lambda w: pl.BlockSpec((tm, w), lambda i: (i, 0))
    const = lambda a: pl.BlockSpec(a.shape, lambda i: (0, 0))
    return pl.pallas_call(
        _merge_kernel,
        grid=(t // tm,),
        in_specs=[row(y.shape[1]) for y in ys] + [row(GATE_COLS), row(D_MODEL)]
                 + [const(w) for w in ws] + [const(w_o)],
        out_specs=row(D_MODEL),
        out_shape=jax.ShapeDtypeStruct((t, D_MODEL), F32),
        compiler_params=pltpu.CompilerParams(
            dimension_semantics=("arbitrary",), vmem_limit_bytes=VMEM_LIMIT),
        name="merge",
    )(*ys, gate, h, *ws, w_o)


def _shift_rows(cur, prev_tail, shift):
    rolled = pltpu.roll(cur, shift, 0)
    tail = pltpu.roll(prev_tail, shift, 0)
    row = lax.broadcasted_iota(jnp.int32, tail.shape, 0)
    top = jnp.where(row < shift, tail, rolled[:SUBLANE])
    return jnp.concatenate([top, rolled[SUBLANE:]], axis=0)


def _conv_glu_kernel(h_ref, g_ref, wug_ref, wuv_ref, cwg_ref, cwv_ref, cbg_ref, cbv_ref, wd_ref, *rest,
                     final_norm):
    if final_norm:
        gf_ref, o_ref, xn_ref, acc_ref, carry_ref = rest
    else:
        o_ref, xn_ref, acc_ref, carry_ref = rest
    ti = pl.program_id(1)
    fi = pl.program_id(2)
    nf = pl.num_programs(2)

    @pl.when(fi == 0)
    def _():
        xn_ref[...] = _rms(h_ref[0], g_ref[...]).astype(BF16)
        acc_ref[...] = jnp.zeros_like(acc_ref)

    @pl.when(ti == 0)
    def _():
        carry_ref[fi] = jnp.zeros(carry_ref.shape[1:], F32)

    xn = xn_ref[...]

    def conv(w_ref, cw_ref, cb_ref, slot):
        up = jnp.dot(xn, w_ref[...], preferred_element_type=F32)
        tail = carry_ref[fi, slot]
        c = (cb_ref[...] + cw_ref[2:3, :] * up + cw_ref[1:2, :] * _shift_rows(up, tail, 1)
             + cw_ref[0:1, :] * _shift_rows(up, tail, 2))
        carry_ref[fi, slot] = up[up.shape[0] - SUBLANE:]
        return c

    cg = conv(wug_ref, cwg_ref, cbg_ref, 0)
    cv = conv(wuv_ref, cwv_ref, cbv_ref, 1)
    act = (cg * jax.nn.sigmoid(cg) * cv).astype(BF16)
    acc_ref[...] += jnp.dot(act, wd_ref[...], preferred_element_type=F32)

    @pl.when(fi == nf - 1)
    def _():
        out = h_ref[0] + acc_ref[...]
        if final_norm:
            out = _rms(out, gf_ref[...])
        o_ref[0] = out


def conv_glu(h, g, w_up, conv_w, conv_b, w_down, gf, *, tm, tf):
    bsz, seq, d = h.shape
    assert seq % tm == 0 and D_FF % tf == 0
    nf = D_FF // tf
    final_norm = gf is not None
    vec = lambda: pl.BlockSpec((1, d), lambda b, t, f: (0, 0))
    in_specs = [
        pl.BlockSpec((1, tm, d), lambda b, t, f: (b, t, 0)),
        vec(),
        pl.BlockSpec((d, tf), lambda b, t, f: (0, f)),
        pl.BlockSpec((d, tf), lambda b, t, f: (0, f + nf)),
        pl.BlockSpec((CONV_WIDTH, tf), lambda b, t, f: (0, f)),
        pl.BlockSpec((CONV_WIDTH, tf), lambda b, t, f: (0, f + nf)),
        pl.BlockSpec((1, tf), lambda b, t, f: (0, f)),
        pl.BlockSpec((1, tf), lambda b, t, f: (0, f + nf)),
        pl.BlockSpec((tf, d), lambda b, t, f: (f, 0)),
    ]
    args = [h, g.reshape(1, d), w_up, w_up, conv_w, conv_w, conv_b.reshape(1, -1), conv_b.reshape(1, -1), w_down]
    if final_norm:
        in_specs.append(vec())
        args.append(gf.reshape(1, d))
    return pl.pallas_call(
        functools.partial(_conv_glu_kernel, final_norm=final_norm),
        grid=(bsz, seq // tm, nf),
        in_specs=in_specs,
        out_specs=pl.BlockSpec((1, tm, d), lambda b, t, f: (b, t, 0)),
        out_shape=jax.ShapeDtypeStruct((bsz, seq, d), F32),
        scratch_shapes=[pltpu.VMEM((tm, d), BF16), pltpu.VMEM((tm, d), F32),
                        pltpu.VMEM((nf, 2, SUBLANE, tf), F32)],
        compiler_params=pltpu.CompilerParams(
            dimension_semantics=("arbitrary", "arbitrary", "arbitrary"), vmem_limit_bytes=VMEM_LIMIT),
        name="conv_glu",
    )(*args)


HP = LANE
A_PADW = A_HEADS * HP
A_CHUNK = 64
A_LOW = A_RANK_W + A_RANK_A + A_RANK_G
A_PCOLS = 3 * A_PADW + A_LOW


def _dot(x, y):
    return jnp.dot(x.astype(BF16), y.astype(BF16), preferred_element_type=F32)


def _dot_t(x, y):
    return lax.dot_general(x.astype(BF16), y.astype(BF16), (((1,), (1,)), ((), ())), preferred_element_type=F32)


def _rwkv_kernel(p_ref, mu_ref, w0_ref, a0_ref, wa_ref, gup_ref, kk_ref, ka_ref, rk_ref, gng_ref, gnb_ref,
                 o_ref, carry_ref, h_ref):
    C = A_CHUNK
    W = A_PADW

    @pl.when(pl.program_id(1) == 0)
    def _():
        carry_ref[...] = jnp.zeros_like(carry_ref)
        h_ref[...] = jnp.zeros_like(h_ref)

    p = p_ref[0].astype(F32)
    prev = _shift_rows(p, carry_ref[...], 1)
    carry_ref[...] = p[C - SUBLANE:]
    p = p + (prev - p) * mu_ref[...]
    r, k, v = p[:, :W], p[:, W:2 * W], p[:, 2 * W:3 * W]
    low = p[:, 3 * W:3 * W + LANE]
    gd = p[:, 3 * W + LANE:]
    lane = lax.broadcasted_iota(jnp.int32, (C, LANE), 1)
    low = jnp.where(lane < A_RANK_W, jnp.tanh(low), low)
    wa = _dot(low, wa_ref[...])
    w_log = -jax.nn.softplus(-(w0_ref[...] + wa[:, :W])) - 0.5
    logw = -jnp.exp(w_log)
    a = jax.nn.sigmoid(a0_ref[...] + wa[:, W:])
    g = _dot(jax.nn.sigmoid(gd), gup_ref[...])
    k_mod = k * (1.0 + (a - 1.0) * ka_ref[...])
    kk = k * kk_ref[...]

    row = lax.broadcasted_iota(jnp.int32, (C, C), 0)
    col = lax.broadcasted_iota(jnp.int32, (C, C), 1)
    incl = row >= col
    strict = row > col
    cum = jnp.dot(incl.astype(F32), logw, precision=lax.Precision.HIGHEST, preferred_element_type=F32)
    cum_end = cum[C - 1:C, :]
    e_neg = jnp.exp(-cum)
    e_end = jnp.exp(cum_end - cum)
    r_t = r * jnp.exp(cum)
    a_scale = jnp.exp(cum - logw)
    p_end = jnp.exp(cum_end)
    eye = (lax.broadcasted_iota(jnp.int32, (HP, HP), 0) == lax.broadcasted_iota(jnp.int32, (HP, HP), 1)).astype(F32)
    eye_c = (row == col).astype(F32)
    valid = lane < A_HEAD_DIM

    heads = range(A_HEADS)
    sls = [slice(h * HP, (h + 1) * HP) for h in heads]
    each = lambda f: [f(h) for h in heads]
    kk_n = each(lambda h: kk[:, sls[h]] * lax.rsqrt(jnp.maximum(
        jnp.sum(kk[:, sls[h]] * kk[:, sls[h]], axis=-1, keepdims=True), 1e-24)))
    b_h = each(lambda h: kk_n[h] * a[:, sls[h]])
    at = each(lambda h: -kk_n[h] * a_scale[:, sls[h]])
    ar = each(lambda h: jnp.concatenate([at[h], r_t[:, sls[h]]], axis=0))
    ab = each(lambda h: _dot_t(ar[h], b_h[h] * e_neg[:, sls[h]]))
    ak = each(lambda h: _dot_t(ar[h], k_mod[:, sls[h]] * e_neg[:, sls[h]]))
    a_ab = each(lambda h: jnp.where(strict, ab[h][:C], 0.0))
    a_rb = each(lambda h: jnp.where(incl, ab[h][C:], 0.0))
    a_ak = each(lambda h: jnp.where(strict, ak[h][:C], 0.0))
    a_rk = each(lambda h: jnp.where(incl, ak[h][C:], 0.0))
    tm = each(lambda h: eye_c + a_ab[h])
    pw = each(lambda h: _dot(a_ab[h], a_ab[h]))
    akv = each(lambda h: _dot(a_ak[h], v[:, sls[h]]))
    n = 2
    while 2 * n < C:
        x = each(lambda h: _dot(jnp.concatenate([tm[h], pw[h]], axis=0), pw[h]))
        tm = each(lambda h: tm[h] + x[h][:C])
        pw = each(lambda h: x[h][C:])
        n *= 2
    tm = each(lambda h: tm[h] + _dot(tm[h], pw[h]))
    x2 = each(lambda h: _dot(tm[h], jnp.concatenate([at[h], akv[h]], axis=1)))
    x3 = each(lambda h: _dot(a_rb[h], x2[h]))
    x4 = each(lambda h: _dot((b_h[h] * e_end[:, sls[h]]).T, x2[h]))
    rkv = each(lambda h: _dot(a_rk[h], v[:, sls[h]]))
    kv = each(lambda h: _dot((k_mod[:, sls[h]] * e_end[:, sls[h]]).T, v[:, sls[h]]))
    rm = each(lambda h: jnp.concatenate([r_t[:, sls[h]] + x3[h][:, :HP],
                                         eye * p_end[:, sls[h]] + x4[h][:, :HP]], axis=0))
    x5 = each(lambda h: _dot(rm[h], h_ref[h]))
    outs = []
    for h in heads:
        sl = sls[h]
        h_ref[h] = x5[h][C:] + x4[h][:, HP:] + kv[h]
        y = x5[h][:C] + x3[h][:, HP:] + rkv[h]
        mean = jnp.sum(y, axis=-1, keepdims=True) * (1.0 / A_HEAD_DIM)
        dlt = jnp.where(valid, y - mean, 0.0)
        var = jnp.sum(dlt * dlt, axis=-1, keepdims=True) * (1.0 / A_HEAD_DIM)
        yn = dlt * lax.rsqrt(var + A_GN_EPS) * gng_ref[:, sl] + gnb_ref[:, sl]
        bonus = jnp.sum(r[:, sl] * k_mod[:, sl] * rk_ref[:, sl], axis=-1, keepdims=True) * v[:, sl]
        outs.append((yn + bonus) * g[:, sl])
    o_ref[0] = jnp.concatenate(outs, axis=1).astype(o_ref.dtype)


def _pad_heads(t, heads):
    hd = t.shape[-1] // heads
    t = t.reshape(t.shape[:-1] + (heads, hd))
    t = jnp.pad(t, [(0, 0)] * (t.ndim - 1) + [(0, HP - hd)])
    return t.reshape(t.shape[:-2] + (heads * HP,))


def rwkv7_pad_in_weight(w_a):
    r, k, v, low = w_a[:, :A_WIDTH], w_a[:, A_WIDTH:2 * A_WIDTH], w_a[:, 2 * A_WIDTH:3 * A_WIDTH], w_a[:, 3 * A_WIDTH:]
    return jnp.concatenate([_pad_heads(r, A_HEADS), _pad_heads(k, A_HEADS), _pad_heads(v, A_HEADS), low], axis=1)


def rwkv7(p, mu, w0, w_up, a0, a_up, g_up, k_k, k_a, r_k, gn_g, gn_b):
    bsz, seq, _ = p.shape
    C = A_CHUNK
    assert seq % C == 0
    mu3 = jnp.concatenate([_pad_heads(mu[i * A_WIDTH:(i + 1) * A_WIDTH], A_HEADS) for i in range(3)]
                          + [mu[3 * A_WIDTH:]]).reshape(1, -1)
    z = jnp.zeros((A_RANK_W, A_PADW), F32)
    wa = jnp.concatenate([jnp.concatenate([_pad_heads(w_up, A_HEADS), z], axis=1),
                          jnp.concatenate([z, _pad_heads(a_up, A_HEADS)], axis=1)], axis=0).astype(BF16)
    vecs = [_pad_heads(t.reshape(-1), A_HEADS).reshape(1, -1) for t in (k_k, k_a, r_k, gn_g, gn_b)]
    full = lambda shape: pl.BlockSpec(shape, lambda b, c: (0,) * len(shape))
    return pl.pallas_call(
        _rwkv_kernel,
        grid=(bsz, seq // C),
        in_specs=[pl.BlockSpec((1, C, A_PCOLS), lambda b, c: (b, c, 0)),
                  full((1, A_PCOLS)), full((1, A_PADW)), full((1, A_PADW)),
                  full((LANE, 2 * A_PADW)), full((A_RANK_G, A_PADW))] + [full((1, A_PADW))] * 5,
        out_specs=pl.BlockSpec((1, C, A_PADW), lambda b, c: (b, c, 0)),
        out_shape=jax.ShapeDtypeStruct((bsz, seq, A_PADW), BF16),
        scratch_shapes=[pltpu.VMEM((SUBLANE, A_PCOLS), F32), pltpu.VMEM((A_HEADS, HP, HP), F32)],
        compiler_params=pltpu.CompilerParams(
            dimension_semantics=("arbitrary", "arbitrary"), vmem_limit_bytes=VMEM_LIMIT),
        name="rwkv7",
    )(p, mu3, _pad_heads(w0, A_HEADS).reshape(1, -1), _pad_heads(a0, A_HEADS).reshape(1, -1), wa,
      _pad_heads(g_up, A_HEADS).astype(BF16), *vecs)


B_CHUNK = 64
B_SUB = 16


def _dot_tl(x, y):
    return lax.dot_general(x.astype(BF16), y.astype(BF16), (((0,), (0,)), ((), ())), preferred_element_type=F32)


def _hgrn_kernel(p_ref, lb_ref, gn_ref, o_ref, st_ref):
    C, SB, W = B_CHUNK, B_SUB, B_WIDTH
    nsb = C // SB
    D = B_KEY_DIM

    @pl.when(pl.program_id(1) == 0)
    def _():
        st_ref[...] = jnp.zeros_like(st_ref)

    p = p_ref[0].astype(F32)
    q, f, val, gate = p[:, :W], p[:, W:2 * W], p[:, 2 * W:3 * W], p[:, 3 * W:]
    lb = lb_ref[...]
    log_f = jnp.logaddexp(jnp.log(lb), jnp.log1p(-lb) + jax.nn.log_sigmoid(f))
    k = (1.0 - lb) * jax.nn.sigmoid(-f)
    row = lax.broadcasted_iota(jnp.int32, (C, C), 0)
    col = lax.broadcasted_iota(jnp.int32, (C, C), 1)
    b = jnp.dot((row >= col).astype(F32), log_f, precision=lax.Precision.HIGHEST,
                preferred_element_type=F32)
    b_end = b[C - 1:C, :]
    b0 = jnp.concatenate(
        [jnp.zeros((SB, W), F32)] + [jnp.broadcast_to(b[i * SB - 1:i * SB, :], (SB, W)) for i in range(1, nsb)],
        axis=0)
    qt = q * jnp.exp(b - b0)
    qe = q * jnp.exp(b)
    ke = k * jnp.exp(b_end - b)
    e_end = jnp.exp(b_end)
    rowd = lax.broadcasted_iota(jnp.int32, (C, D), 0)
    lane_c = lax.broadcasted_iota(jnp.int32, (SB, C), 1)
    row_sb = lax.broadcasted_iota(jnp.int32, (SB, C), 0)

    heads = range(B_HEADS)
    sls = [slice(h * D, (h + 1) * D) for h in heads]
    each = lambda fn: [fn(h) for h in heads]

    def off_operands(h):
        kts, qts = [], []
        for i in range(1, nsb):
            b0_i = b[i * SB - 1:i * SB, sls[h]]
            kts.append(jnp.where(rowd < i * SB, k[:, sls[h]] * jnp.exp(jnp.minimum(b0_i - b[:, sls[h]], 0.0)), 0.0))
            qts.append(jnp.where((rowd >= i * SB) & (rowd < (i + 1) * SB), qt[:, sls[h]], 0.0))
        return jnp.concatenate(kts, axis=1), jnp.concatenate(qts, axis=1)
    offs = each(off_operands)
    st_off = each(lambda h: _dot_t(offs[h][0], offs[h][1]))

    def diag_t(h):
        blocks = []
        for i in range(nsb):
            rs = slice(i * SB, (i + 1) * SB)
            k_i, b_i = k[rs, sls[h]], b[rs, sls[h]]
            acc = jnp.zeros((SB, C), F32)
            for j in range(SB):
                t = i * SB + j
                z = k_i * q[t:t + 1, sls[h]] * jnp.exp(jnp.minimum(b[t:t + 1, sls[h]] - b_i, 0.0))
                acc = jnp.where(lane_c == t, jnp.sum(z, axis=-1, keepdims=True), acc)
            blocks.append(jnp.where(row_sb + i * SB <= lane_c, acc, 0.0))
        return jnp.concatenate(blocks, axis=0)
    st_diag = each(diag_t)
    o_intra = each(lambda h: _dot_tl(st_off[h] + st_diag[h], val[:, sls[h]]))
    o_state = each(lambda h: _dot_t(qe[:, sls[h]], st_ref[h]))
    upd = each(lambda h: _dot_tl(val[:, sls[h]], ke[:, sls[h]]))
    outs = []
    for h in heads:
        st_ref[h] = st_ref[h] * e_end[:, sls[h]] + upd[h]
        o = o_intra[h] + o_state[h]
        o = o * lax.rsqrt(jnp.mean(o * o, axis=-1, keepdims=True) + NORM_EPS)
        gt = gate[:, sls[h]]
        outs.append(o * gn_ref[:, sls[h]] * (gt * jax.nn.sigmoid(gt)))
    o_ref[0] = jnp.concatenate(outs, axis=1).astype(o_ref.dtype)


def hgrn2(p, lb, gn_g):
    bsz, seq, _ = p.shape
    C = B_CHUNK
    assert seq % C == 0
    vec = pl.BlockSpec((1, B_WIDTH), lambda b, c: (0, 0))
    return pl.pallas_call(
        _hgrn_kernel,
        grid=(bsz, seq // C),
        in_specs=[pl.BlockSpec((1, C, B_COLS), lambda b, c: (b, c, 0)), vec, vec],
        out_specs=pl.BlockSpec((1, C, B_WIDTH), lambda b, c: (b, c, 0)),
        out_shape=jax.ShapeDtypeStruct((bsz, seq, B_WIDTH), BF16),
        scratch_shapes=[pltpu.VMEM((B_HEADS, B_VAL_DIM, B_KEY_DIM), F32)],
        compiler_params=pltpu.CompilerParams(
            dimension_semantics=("arbitrary", "arbitrary"), vmem_limit_bytes=VMEM_LIMIT),
        name="hgrn2",
    )(p, lb.reshape(1, -1), gn_g.reshape(1, -1))


C_PAD = _round_up(C_COLS, LANE)
QB = 128
C_KV_W = C_KV_HEADS * C_HEAD_DIM
C_K_OFF = C_WIDTH
C_QI_OFF = C_WIDTH + 2 * C_KV_W
C_KI_OFF = C_QI_OFF + IDX_HEADS * IDX_DIM
INT_MIN = -2 ** 31
NEG = -1e30
HALF = C_HEAD_DIM // 2
HALF_RANGE = 1 << 15
VT_ROWS = C_HEAD_DIM + 16


def _rope(t, cos, sin_signed):
    outs = []
    lane = lax.broadcasted_iota(jnp.int32, (t.shape[0], LANE), 1)
    first_half = (lane % C_HEAD_DIM) < HALF
    for c in range(t.shape[1] // LANE):
        x = t[:, c * LANE:(c + 1) * LANE]
        swapped = jnp.where(first_half, pltpu.roll(x, LANE - HALF, 1), pltpu.roll(x, HALF, 1))
        outs.append(x * cos + swapped * sin_signed)
    return outs


GROUP = C_HEADS // C_KV_HEADS


def _split_hi_lo(x):
    hi = x.astype(BF16)
    return hi, (x - hi.astype(F32)).astype(BF16)


def _dsa_kernel(pq_ref, kv_ref, ki_ref, cosq_ref, sinq_ref, cosf_ref, sinf_ref, o_ref,
                k_s, vt_s, ki_s, key_s, khi_s, klo_s, *, k_sel, seq):
    i = pl.program_id(1)
    nk = i + 1
    lane = lax.broadcasted_iota(jnp.int32, (QB, LANE), 1)
    rowi = lax.broadcasted_iota(jnp.int32, (QB, LANE), 0)
    low_half = lane < C_HEAD_DIM

    @pl.when(i == 0)
    def _():
        kv = kv_ref[0].astype(F32)
        k_s[...] = _rope(kv[:, :LANE], cosf_ref[...], sinf_ref[...])[0].astype(BF16)
        vt = kv[:, LANE:].T
        extra = (lax.broadcasted_iota(jnp.int32, (VT_ROWS - C_HEAD_DIM, seq), 0) == 0).astype(F32)
        for c in range(C_KV_HEADS):
            vt_s[c] = jnp.concatenate([vt[c * C_HEAD_DIM:(c + 1) * C_HEAD_DIM], extra], axis=0).astype(BF16)
        lane_f = lax.broadcasted_iota(jnp.int32, (seq, LANE), 1)
        ki = jnp.where(lane_f < IDX_DIM, ki_ref[0].astype(F32), 0.0)
        ki = _rope(ki, cosf_ref[...], sinf_ref[...])[0]
        ki = ki + pltpu.roll(ki, IDX_DIM, 1)
        hi, lo = _split_hi_lo(ki)
        ki_s[...] = jnp.concatenate([hi, lo, hi], axis=1)

    pq = pq_ref[0].astype(F32)
    cos, sin = cosq_ref[...], sinq_ref[...]
    q = _rope(pq[:, :C_WIDTH], cos, sin)
    qi = _rope(pq[:, C_QI_OFF:C_KI_OFF], cos, sin)
    qi_rows = jnp.concatenate(
        [jnp.where(low_half if h % 2 == 0 else ~low_half, qi[h // 2], 0.0) for h in range(IDX_HEADS)], axis=0)
    hi, lo = _split_hi_lo(qi_rows)
    qi3 = jnp.concatenate([hi, hi, lo], axis=1)
    wt = pq[:, C_KI_OFF:C_KI_OFF + LANE].T
    wscale = (IDX_HEADS ** -0.5) * (IDX_DIM ** -0.5)
    w_rows = [wt[IDX_DIM + h:IDX_DIM + h + 1, :] * wscale for h in range(IDX_HEADS)]
    causal_off = i * QB + lane

    npair = (nk + 1) // 2

    def score_tile(jp, _):
        ki_t = ki_s[pl.ds(pl.multiple_of(jp * (2 * QB), 2 * QB), 2 * QB), :]
        d = lax.dot_general(ki_t, qi3, (((1,), (1,)), ((), ())), preferred_element_type=F32)
        for u in range(2):
            j = 2 * jp + u
            sc = jnp.zeros((QB, QB), F32)
            for h in range(IDX_HEADS):
                sc = sc + jnp.maximum(d[u * QB:(u + 1) * QB, h * QB:(h + 1) * QB], 0.0) * w_rows[h]
            bits = pltpu.bitcast(sc + 0.0, jnp.int32)
            key = jnp.where(bits < 0, bits ^ jnp.int32(0x7fffffff), bits)
            key = jnp.where(j * QB + rowi <= causal_off, key, jnp.int32(INT_MIN))
            key_s[j] = key
            khi_s[j] = jnp.right_shift(key, 16).astype(jnp.int16)
            klo_s[j] = ((key & 0xffff) - HALF_RANGE).astype(jnp.int16)
        return 0
    lax.fori_loop(0, npair, score_tile, 0)

    def count16(ref, pred):
        def body(jp, acc):
            for u in range(2):
                acc = acc + jnp.where(pred(ref[2 * jp + u]), jnp.int16(1), jnp.int16(0))
            return acc
        acc = lax.fori_loop(0, npair, body, jnp.zeros((QB, QB), jnp.int16))
        return jnp.sum(acc.astype(jnp.int32), axis=0, keepdims=True)

    def half_search(ref, base):
        def bit_step(b, cand):
            trial = cand | jnp.left_shift(jnp.int32(1), 15 - b)
            t16 = (trial - HALF_RANGE).astype(jnp.int16)
            cnt = base + count16(ref, lambda x: x >= t16)
            return jnp.where(cnt >= k_sel, trial, cand)
        return lax.fori_loop(0, 16, bit_step, jnp.zeros((1, QB), jnp.int32))

    cand_hi = half_search(khi_s, 0)
    hi16 = (cand_hi - HALF_RANGE).astype(jnp.int16)
    above = count16(khi_s, lambda x: x > hi16)

    def mask_low(jp, _):
        for u in range(2):
            j = 2 * jp + u
            klo_s[j] = jnp.where(khi_s[j] == hi16, klo_s[j], jnp.int16(-HALF_RANGE))
        return 0
    lax.fori_loop(0, npair, mask_low, 0)
    cand_lo = half_search(klo_s, above)
    thr = (jnp.left_shift(cand_hi, 16) | cand_lo) ^ jnp.int32(INT_MIN)

    def count(pred):
        def body(jp, acc):
            for u in range(2):
                acc = acc + jnp.where(pred(key_s[2 * jp + u]), 1, 0)
            return acc
        return jnp.sum(lax.fori_loop(0, npair, body, jnp.zeros((QB, QB), jnp.int32)), axis=0, keepdims=True)
    need = k_sel - count(lambda key: key > thr)

    need_f = need.astype(F32)
    tri = (lax.broadcasted_iota(jnp.int32, (2 * QB, 2 * QB), 0)
           >= lax.broadcasted_iota(jnp.int32, (2 * QB, 2 * QB), 1)).astype(BF16)

    scale = C_HEAD_DIM ** -0.5
    kvs = range(C_KV_HEADS)
    in_cs = [(lane >= c * C_HEAD_DIM) & (lane < (c + 1) * C_HEAD_DIM) for c in kvs]

    def q_rows(c):
        rows = []
        for g in range(GROUP):
            h = c * GROUP + g
            x = q[h // 2]
            if h % 2 != c:
                x = pltpu.roll(x, C_HEAD_DIM, 1)
            rows.append(jnp.where(in_cs[c], x * scale, 0.0))
        return jnp.concatenate(rows, axis=0).astype(BF16)
    qcs = [q_rows(c) for c in kvs]

    def att_tile(jp, carry):
        off = pl.multiple_of(jp * (2 * QB), 2 * QB)
        keys = jnp.concatenate([key_s[2 * jp], key_s[2 * jp + 1]], axis=0)
        pos = off + lax.broadcasted_iota(jnp.int32, (2 * QB, LANE), 0)
        tie = keys == thr
        rank = carry[-1] + jnp.dot(tri, jnp.where(tie, 1.0, 0.0).astype(BF16), preferred_element_type=F32)
        sel = ((keys > thr) | (tie & (rank <= need_f))) & (pos <= causal_off[:1, :])
        bias1 = jnp.where(sel, 0.0, NEG)
        bias = jnp.concatenate([bias1] * GROUP, axis=1)
        k_t = k_s[pl.ds(off, 2 * QB), :]
        s = [lax.dot_general(k_t, qcs[c], (((1,), (1,)), ((), ())), preferred_element_type=F32) for c in kvs]
        s = [s[c] + bias for c in kvs]
        m_new = [jnp.maximum(carry[2 * c], jnp.max(s[c], axis=0, keepdims=True)) for c in kvs]
        p = [jnp.exp(s[c] - m_new[c]).astype(BF16) for c in kvs]
        pv = [jnp.dot(vt_s[c, :, pl.ds(off, 2 * QB)], p[c], preferred_element_type=F32) for c in kvs]
        out = []
        for c in kvs:
            out += [m_new[c], carry[2 * c + 1] * jnp.exp(carry[2 * c] - m_new[c]) + pv[c]]
        return tuple(out) + (rank[2 * QB - 1:, :],)
    init = ((jnp.full((1, GROUP * QB), NEG, F32), jnp.zeros((VT_ROWS, GROUP * QB), F32)) * C_KV_HEADS
            + (jnp.zeros((1, QB), F32),))
    res = lax.fori_loop(0, npair, att_tile, init)

    tiles = []
    for c in kvs:
        acc = res[2 * c + 1]
        out_t = acc[:C_HEAD_DIM] / acc[C_HEAD_DIM:C_HEAD_DIM + 1]
        for g in range(0, GROUP, 2):
            pair = jnp.concatenate([out_t[:, g * QB:(g + 1) * QB], out_t[:, (g + 1) * QB:(g + 2) * QB]], axis=0)
            tiles.append(pair.T)
    o_ref[0] = jnp.concatenate(tiles, axis=1).astype(o_ref.dtype)


def _dsa_kernel_rows(pq_ref, kv_ref, ki_ref, cosq_ref, sinq_ref, cosf_ref, sinf_ref, o_ref,
                     k_s, v_s, ki_s, key_s, *, k_sel, seq):
    i = pl.program_id(1)
    nk = i + 1
    lane = lax.broadcasted_iota(jnp.int32, (QB, LANE), 1)
    rowi = lax.broadcasted_iota(jnp.int32, (QB, LANE), 0)
    low_half = lane < C_HEAD_DIM

    @pl.when(i == 0)
    def _():
        kv = kv_ref[0]
        k_s[...] = _rope(kv[:, :LANE], cosf_ref[...], sinf_ref[...])[0].astype(BF16)
        v_s[...] = kv[:, LANE:].astype(BF16)
        lane_f = lax.broadcasted_iota(jnp.int32, (seq, LANE), 1)
        ki = jnp.where(lane_f < IDX_DIM, ki_ref[0], 0.0)
        ki = _rope(ki, cosf_ref[...], sinf_ref[...])[0]
        ki_s[...] = ki + pltpu.roll(ki, IDX_DIM, 1)

    pq = pq_ref[0]
    cos, sin = cosq_ref[...], sinq_ref[...]
    q = _rope(pq[:, :C_WIDTH], cos, sin)
    qi = _rope(pq[:, C_QI_OFF:C_KI_OFF], cos, sin)
    wslab = pq[:, C_KI_OFF:C_KI_OFF + LANE]
    wscale = (IDX_HEADS ** -0.5) * (IDX_DIM ** -0.5)
    wi = [jnp.sum(jnp.where(lane == IDX_DIM + h, wslab, 0.0), axis=-1, keepdims=True) * wscale
          for h in range(IDX_HEADS)]
    qi_h = [jnp.where(low_half if h % 2 == 0 else ~low_half, qi[h // 2], 0.0) for h in range(IDX_HEADS)]

    def score_tile(j, _):
        ki_t = ki_s[pl.ds(pl.multiple_of(j * QB, QB), QB), :]
        sc = jnp.zeros((QB, QB), F32)
        for h in range(IDX_HEADS):
            d = lax.dot_general(qi_h[h], ki_t, (((1,), (1,)), ((), ())), precision=lax.Precision.HIGHEST,
                                preferred_element_type=F32)
            sc = sc + jnp.maximum(d, 0.0) * wi[h]
        bits = pltpu.bitcast(sc + 0.0, jnp.int32)
        key = jnp.where(bits < 0, bits ^ jnp.int32(0x7fffffff), bits)
        key_s[j] = jnp.where(j * QB + lane <= i * QB + rowi, key, jnp.int32(INT_MIN))
        return 0
    lax.fori_loop(0, nk, score_tile, 0)

    def count(pred):
        def body(j, acc):
            return acc + jnp.where(pred(key_s[j], j), 1, 0)
        return jnp.sum(lax.fori_loop(0, nk, body, jnp.zeros((QB, QB), jnp.int32)), axis=-1, keepdims=True)

    def bit_step(b, cand):
        trial = cand | jnp.left_shift(jnp.int32(1), 31 - b)
        cnt = count(lambda key, j: key >= (trial ^ jnp.int32(INT_MIN)))
        return jnp.where(cnt >= k_sel, trial, cand)
    cand = lax.fori_loop(0, 32, bit_step, jnp.zeros((QB, 1), jnp.int32))
    thr = cand ^ jnp.int32(INT_MIN)
    need = k_sel - count(lambda key, j: key > thr)

    nbits = int(np.ceil(np.log2(seq))) + 1

    def idx_step(b, jj):
        trial = jj | jnp.left_shift(jnp.int32(1), nbits - 1 - b)
        cnt = count(lambda key, j: (key == thr) & (j * QB + lane < trial))
        return jnp.where(cnt <= need, trial, jj)
    jlim = lax.fori_loop(0, nbits, idx_step, jnp.zeros((QB, 1), jnp.int32))

    group = C_HEADS // C_KV_HEADS
    scale = C_HEAD_DIM ** -0.5
    outs = []
    for c in range(C_KV_HEADS):
        in_c = (lane >= c * C_HEAD_DIM) & (lane < (c + 1) * C_HEAD_DIM)
        rows = []
        for g in range(group):
            h = c * group + g
            x = q[h // 2]
            if h % 2 != c:
                x = pltpu.roll(x, C_HEAD_DIM, 1)
            rows.append(jnp.where(in_c, x * scale, 0.0))
        qc = jnp.concatenate(rows, axis=0).astype(BF16)
        ones_lane = (1 - c) * C_HEAD_DIM

        def att_tile(j, carry):
            m, acc = carry
            off = pl.multiple_of(j * QB, QB)
            key = key_s[j]
            sel = ((key > thr) | ((key == thr) & (j * QB + lane < jlim))) & (j * QB + lane <= i * QB + rowi)
            bias = jnp.where(sel, 0.0, NEG)
            s = lax.dot_general(qc, k_s[pl.ds(off, QB), :], (((1,), (1,)), ((), ())),
                                preferred_element_type=F32)
            s = jnp.where(jnp.concatenate([bias] * group, axis=0) < 0.0, NEG, s)
            m_new = jnp.maximum(m, jnp.max(s, axis=-1, keepdims=True))
            alpha = jnp.exp(m - m_new)
            p = jnp.exp(s - m_new)
            v_t = v_s[pl.ds(off, QB), :]
            v_t = jnp.where(in_c, v_t, jnp.where(lane == ones_lane, 1.0, 0.0).astype(BF16))
            acc = acc * alpha + jnp.dot(p.astype(BF16), v_t, preferred_element_type=F32)
            return m_new, acc
        m0 = jnp.full((group * QB, 1), NEG, F32)
        acc0 = jnp.zeros((group * QB, LANE), F32)
        _, acc = lax.fori_loop(0, nk, att_tile, (m0, acc0))
        l = jnp.sum(jnp.where(jnp.concatenate([lane] * group, axis=0) == ones_lane, acc, 0.0),
                    axis=-1, keepdims=True)
        out = acc / l
        for g in range(group):
            h = c * group + g
            x = jnp.where(in_c, out[g * QB:(g + 1) * QB], 0.0)
            if h % 2 != c:
                x = pltpu.roll(x, C_HEAD_DIM, 1)
            outs.append(x)
    o_ref[0] = jnp.concatenate([outs[2 * m] + outs[2 * m + 1] for m in range(C_HEADS // 2)],
                               axis=1).astype(o_ref.dtype)


def _rope_lanes(seq):
    inv_freq = ROPE_THETA ** (-jnp.arange(HALF, dtype=F32) * (2.0 / C_HEAD_DIM))
    ang = jnp.arange(seq, dtype=F32)[:, None] * inv_freq[None, :]
    cos, sin = jnp.cos(ang), jnp.sin(ang)
    reps = LANE // C_HEAD_DIM
    return (jnp.tile(jnp.concatenate([cos, cos], axis=1), (1, reps)),
            jnp.tile(jnp.concatenate([-sin, sin], axis=1), (1, reps)))


def dsa(p):
    bsz, seq, width = p.shape
    assert seq % (2 * QB) == 0 and width == C_PAD
    k_sel = min(TOPK_MAX, seq // 4)
    cos, sin = _rope_lanes(seq)
    return pl.pallas_call(
        functools.partial(_dsa_kernel, k_sel=k_sel, seq=seq),
        grid=(bsz, seq // QB),
        in_specs=[pl.BlockSpec((1, QB, C_PAD), lambda b, i: (b, i, 0)),
                  pl.BlockSpec((1, seq, 2 * C_KV_W), lambda b, i: (b, 0, C_K_OFF // (2 * C_KV_W))),
                  pl.BlockSpec((1, seq, LANE), lambda b, i: (b, 0, C_KI_OFF // LANE)),
                  pl.BlockSpec((QB, LANE), lambda b, i: (i, 0)),
                  pl.BlockSpec((QB, LANE), lambda b, i: (i, 0)),
                  pl.BlockSpec((seq, LANE), lambda b, i: (0, 0)),
                  pl.BlockSpec((seq, LANE), lambda b, i: (0, 0))],
        out_specs=pl.BlockSpec((1, QB, C_WIDTH), lambda b, i: (b, i, 0)),
        out_shape=jax.ShapeDtypeStruct((bsz, seq, C_WIDTH), BF16),
        scratch_shapes=[pltpu.VMEM((seq, LANE), BF16), pltpu.VMEM((C_KV_HEADS, VT_ROWS, seq), BF16),
                        pltpu.VMEM((seq, 3 * LANE), BF16), pltpu.VMEM((seq // QB, QB, QB), jnp.int32),
                        pltpu.VMEM((seq // QB, QB, QB), jnp.int16), pltpu.VMEM((seq // QB, QB, QB), jnp.int16)],
        compiler_params=pltpu.CompilerParams(
            dimension_semantics=("arbitrary", "arbitrary"), vmem_limit_bytes=VMEM_LIMIT),
        name="dsa",
    )(p, p, p, cos, sin, cos, sin)


def _split(t, sizes):
    out, start = [], 0
    for s in sizes:
        out.append(t[..., start:start + s])
        start += s
    return out


def _mm(x, y):
    return jnp.einsum('...ij,...jk->...ik', x.astype(BF16), y.astype(BF16), preferred_element_type=F32)


def _mm_hi(x, y):
    return jnp.einsum('...ij,...jk->...ik', x, y, precision=lax.Precision.HIGHEST)


def _rwkv7_chunked(r, logw, k, v, a, b, C=64):
    B, S, H, N = r.shape
    nc = S // C

    def ch(t):
        return t.reshape(B, nc, C, H, N).transpose(1, 0, 3, 2, 4)
    r, logw, k, v, a, b = map(ch, (r, logw, k, v, a, b))
    cum = jnp.cumsum(logw, axis=-2)
    cum_prev = cum - logw
    cum_end = cum[..., -1:, :]
    At = a * jnp.exp(cum_prev)
    Rt = r * jnp.exp(cum)
    Bt = b * jnp.exp(-cum)
    Kt = k * jnp.exp(-cum)
    Bh = b * jnp.exp(cum_end - cum)
    Kh = k * jnp.exp(cum_end - cum)
    Pc = jnp.exp(cum_end)[..., 0, :]
    T_ = lambda x: jnp.swapaxes(x, -1, -2)
    ti = jnp.arange(C)
    strict = ti[:, None] > ti[None, :]
    incl = ti[:, None] >= ti[None, :]
    Aab = jnp.where(strict, _mm(At, T_(Bt)), 0.0)
    Aak = jnp.where(strict, _mm(At, T_(Kt)), 0.0)
    Arb = jnp.where(incl, _mm(Rt, T_(Bt)), 0.0)
    Ark = jnp.where(incl, _mm(Rt, T_(Kt)), 0.0)
    Tm = jnp.eye(C, dtype=F32) + Aab
    Pw = Aab
    n = 1
    while 2 * n < C:
        Pw = _mm_hi(Pw, Pw)
        Tm = Tm + _mm_hi(Tm, Pw)
        n *= 2
    Abar = _mm(Tm, At)
    U0 = _mm(Tm, _mm(Aak, v))
    Rbar = Rt + _mm(Arb, Abar)
    Y0 = _mm(Arb, U0) + _mm(Ark, v)
    M = Pc[..., :, None] * jnp.eye(N, dtype=F32) + _mm(T_(Bh), Abar)
    Hd = _mm(T_(Bh), U0) + _mm(T_(Kh), v)

    def step(Hs, inp):
        Rbar_c, Y0_c, M_c, Hd_c = inp
        y = _mm(Rbar_c, Hs) + Y0_c
        Hs = _mm_hi(M_c, Hs) + Hd_c
        return Hs, y
    _, ys = lax.scan(step, jnp.zeros((B, H, N, N), F32), (Rbar, Y0, M, Hd))
    return ys.transpose(1, 0, 3, 2, 4).reshape(B, S, H, N)


def _token_shift(p):
    return jnp.pad(p, ((0, 0), (1, 0), (0, 0)))[:, :-1]


def _rwkv7_branch(p, mu, w0, w_up, a0, a_up, g_up, k_k, k_a, r_k, gn_g, gn_b):
    bsz, seq, _ = p.shape
    p = p + (_token_shift(p) - p) * mu
    r, k, v, wd, ad, gd = _split(p, (A_WIDTH, A_WIDTH, A_WIDTH, A_RANK_W, A_RANK_A, A_RANK_G))
    w_log = -jax.nn.softplus(-(w0 + _mm(jnp.tanh(wd), w_up))) - 0.5
    logw = -jnp.exp(w_log)
    a = jax.nn.sigmoid(a0 + _mm(ad, a_up))
    g = _mm(jax.nn.sigmoid(gd), g_up)

    def hd(t):
        return t.reshape(bsz, seq, A_HEADS, A_HEAD_DIM)
    kk = hd(k * k_k)
    kk = kk * lax.rsqrt(jnp.maximum(jnp.sum(kk * kk, axis=-1, keepdims=True), 1e-24))
    k_mod = hd(k * (1.0 + (a - 1.0) * k_a))
    r_h, v_h, a_h = hd(r), hd(v), hd(a)
    y = _rwkv7_chunked(r_h, hd(logw), k_mod, v_h, -kk, kk * a_h)
    mean = jnp.mean(y, axis=-1, keepdims=True)
    var = jnp.mean(jnp.square(y - mean), axis=-1, keepdims=True)
    y = ((y - mean) * lax.rsqrt(var + A_GN_EPS)).reshape(bsz, seq, A_WIDTH)
    y = y * gn_g + gn_b
    bonus = jnp.sum(r_h * k_mod * r_k, axis=-1, keepdims=True) * v_h
    y = y + bonus.reshape(bsz, seq, A_WIDTH)
    return y * g


def _hgrn2_sub(q, k, g, v, C=64, SB=16):
    B, S, H, dk = q.shape
    dv = v.shape[-1]
    nc = S // C

    def ch(t):
        return t.reshape(B, nc, C, H, t.shape[-1]).transpose(1, 0, 3, 2, 4)
    q, k, g, v = map(ch, (q, k, g, v))
    b = jnp.cumsum(g, axis=-2)
    nsb = C // SB
    T_ = lambda x: jnp.swapaxes(x, -1, -2)

    def step(state, inp):
        qc, kc, bc, vc = inp
        rows = []
        for I in range(nsb):
            sl = slice(I * SB, (I + 1) * SB)
            qI, bI = qc[..., sl, :], bc[..., sl, :]
            b0 = bc[..., I * SB - 1:I * SB, :] if I > 0 else jnp.zeros_like(bc[..., :1, :])
            qt = qI * jnp.exp(bI - b0)
            blocks = []
            if I > 0:
                kt = kc[..., :I * SB, :] * jnp.exp(b0 - bc[..., :I * SB, :])
                blocks.append(_mm(qt, T_(kt)))
            diff = bI[..., :, None, :] - bI[..., None, :, :]
            tri = jnp.tril(jnp.ones((SB, SB), bool))
            dec = jnp.exp(jnp.where(tri[:, :, None], diff, -jnp.inf))
            diag = jnp.sum(qI[..., :, None, :] * kc[..., sl, :][..., None, :, :] * dec, axis=-1)
            blocks.append(diag)
            if I < nsb - 1:
                blocks.append(jnp.zeros(diag.shape[:-1] + (C - (I + 1) * SB,), F32))
            rows.append(jnp.concatenate(blocks, axis=-1))
        scores = jnp.concatenate(rows, axis=-2)
        o = _mm(scores, vc) + _mm(qc * jnp.exp(bc), state)
        b_end = bc[..., -1:, :]
        state = state * T_(jnp.exp(b_end)) + _mm(T_(kc * jnp.exp(b_end - bc)), vc)
        return state, o
    _, o = lax.scan(step, jnp.zeros((B, H, dk, dv), F32), (q, k, b, v))
    return o.transpose(1, 0, 3, 2, 4).reshape(B, S, H, dv)


def _hgrn2_branch(p, lb, gn_g):
    bsz, seq, _ = p.shape
    q, f, i, g = _split(p, (B_WIDTH,) * 4)
    log_f = jnp.logaddexp(jnp.log(lb), jnp.log1p(-lb) + jax.nn.log_sigmoid(f))
    k_in = (1.0 - lb) * jax.nn.sigmoid(-f)
    hd = lambda t: t.reshape(bsz, seq, B_HEADS, -1)
    o = _hgrn2_sub(hd(q), hd(k_in), hd(log_f), hd(i))
    o = o * lax.rsqrt(jnp.mean(o * o, axis=-1, keepdims=True) + NORM_EPS)
    return o.reshape(bsz, seq, B_WIDTH) * gn_g * jax.nn.silu(g)


def _rope_tables(seq, dim):
    half = dim // 2
    inv_freq = ROPE_THETA ** (-jnp.arange(half, dtype=F32) * (2.0 / dim))
    ang = jnp.arange(seq, dtype=F32)[:, None] * inv_freq[None, :]
    return jnp.cos(ang), jnp.sin(ang)


def _apply_rope(t, cos, sin):
    half = t.shape[-1] // 2
    t1, t2 = t[..., :half], t[..., half:]
    c, s = cos[None, :, None, :], sin[None, :, None, :]
    return jnp.concatenate([t1 * c - t2 * s, t2 * c + t1 * s], axis=-1)


def _dsa_branch(p, cos, sin):
    bsz, seq, _ = p.shape
    k_sel = min(TOPK_MAX, seq // 4)
    kv_w = C_KV_HEADS * C_HEAD_DIM
    q, k, v, qi, ki, wi = _split(p[..., :C_COLS], (C_WIDTH, kv_w, kv_w, IDX_HEADS * IDX_DIM, IDX_DIM, IDX_HEADS))
    q = _apply_rope(q.reshape(bsz, seq, C_HEADS, C_HEAD_DIM), cos, sin)
    k = _apply_rope(k.reshape(bsz, seq, C_KV_HEADS, C_HEAD_DIM), cos, sin)
    v = v.reshape(bsz, seq, C_KV_HEADS, C_HEAD_DIM)
    qi = _apply_rope(qi.reshape(bsz, seq, IDX_HEADS, IDX_DIM), cos, sin)
    ki = _apply_rope(ki[:, :, None, :], cos, sin)[:, :, 0, :]
    wi = wi * (IDX_HEADS ** -0.5) * (IDX_DIM ** -0.5)
    dots = jnp.einsum('bqhd,bsd->bqhs', qi, ki, precision=lax.Precision.HIGHEST)
    score = jnp.sum(jax.nn.relu(dots) * wi[..., None], axis=2)
    pos = jnp.arange(seq)
    causal = pos[None, :] <= pos[:, None]
    bits = lax.bitcast_convert_type(score + 0.0, jnp.int32)
    key = jnp.where(bits < 0, bits ^ jnp.int32(0x7fffffff), bits)
    key = jnp.where(causal[None], key, jnp.int32(-2 ** 31))
    ukey = lax.bitcast_convert_type(key ^ jnp.int32(-2 ** 31), jnp.uint32)
    cand = jnp.zeros((bsz, seq, 1), jnp.uint32)
    for bit in range(31, -1, -1):
        trial = cand | jnp.uint32(1 << bit)
        cnt = jnp.sum((ukey >= trial).astype(jnp.int32), axis=-1, keepdims=True)
        cand = jnp.where(cnt >= k_sel, trial, cand)
    gt = ukey > cand
    eq = ukey == cand
    need = k_sel - jnp.sum(gt.astype(jnp.int32), axis=-1, keepdims=True)
    J = jnp.zeros((bsz, seq, 1), jnp.int32)
    for bit in range(int(np.ceil(np.log2(seq))), -1, -1):
        trial = J | (1 << bit)
        cnt = jnp.sum((eq & (pos[None, None, :] < trial)).astype(jnp.int32), axis=-1, keepdims=True)
        J = jnp.where(cnt <= need, trial, J)
    sel = (gt | (eq & (pos[None, None, :] < J))) & causal[None]
    group = C_HEADS // C_KV_HEADS
    qg = q.reshape(bsz, seq, C_KV_HEADS, group, C_HEAD_DIM)
    logits = jnp.einsum('bqcgd,bscd->bcgqs', qg.astype(BF16), k.astype(BF16),
                        preferred_element_type=F32) * (C_HEAD_DIM ** -0.5)
    logits = jnp.where(sel[:, None, None], logits, -jnp.inf)
    prob = jax.nn.softmax(logits, axis=-1)
    out = jnp.einsum('bcgqs,bscd->bqcgd', prob.astype(BF16), v.astype(BF16), preferred_element_type=F32)
    return out.reshape(bsz, seq, C_WIDTH)


def kernel(x, norm_mix_g, w_in, a_mu, a_w0, a_w_up, a_a0, a_a_up, a_g_up, a_k_k, a_k_a, a_r_k, a_gn_g,
           a_gn_b, b_lb_logits, b_gn_g, w_branch, w_o, norm_ffn_g, w_up, conv_w, conv_b, w_down,
           norm_final_g):
    bsz, seq, d = x.shape
    t = bsz * seq
    lb_cum = jnp.cumsum(jax.nn.softmax(b_lb_logits.astype(F32), axis=0), axis=0)
    lower_bounds = lb_cum - lb_cum[0]
    h = x
    for layer in range(DEPTH):
        w = w_in[layer].astype(BF16)
        w_a, w_b, w_c, w_g = _split(w, (A_COLS, B_COLS, C_COLS, GATE_COLS))
        w_a = rwkv7_pad_in_weight(w_a)
        w_c = jnp.pad(w_c, ((0, 0), (0, C_PAD - C_COLS)))
        h2 = h.reshape(t, d)
        g = norm_mix_g[layer]
        p_a = norm_matmul(h2, g, w_a, tm=2048, tn=A_PCOLS // 2, out_dtype=BF16).reshape(bsz, seq, -1)
        p_b = norm_matmul(h2, g, w_b, tm=2048, tn=1024, out_dtype=BF16).reshape(bsz, seq, -1)
        p_c = norm_matmul(h2, g, w_c, tm=2048, tn=C_PAD, out_dtype=F32).reshape(bsz, seq, -1)
        p_g = norm_matmul(h2, g, w_g, tm=2048, tn=1024, out_dtype=BF16)
        y_a = rwkv7(p_a, a_mu[layer], a_w0[layer], a_w_up[layer], a_a0[layer], a_a_up[layer],
                    a_g_up[layer], a_k_k[layer], a_k_a[layer], a_r_k[layer], a_gn_g[layer], a_gn_b[layer])
        y_b = hgrn2(p_b, lower_bounds[layer], b_gn_g[layer])
        y_c = dsa(p_c)
        wbr = w_branch[layer].astype(BF16)
        wbr_a = jnp.pad(wbr[0].reshape(A_HEADS, A_HEAD_DIM, d), ((0, 0), (0, HP - A_HEAD_DIM), (0, 0)))
        h2 = merge((y_a.reshape(t, -1), y_b.reshape(t, -1), y_c.reshape(t, -1)), p_g, h2,
                   (wbr_a.reshape(A_PADW, d), wbr[1], wbr[2]), w_o[layer].astype(BF16), tm=512)
        h = conv_glu(h2.reshape(bsz, seq, d), norm_ffn_g[layer], w_up[layer].astype(BF16), conv_w[layer],
                     conv_b[layer], w_down[layer].astype(BF16),
                     norm_final_g if layer == DEPTH - 1 else None, tm=512, tf=1408)
    return h
```

```python
import functools

import jax
import jax.numpy as jnp
import numpy as np
from jax import lax
from jax.experimental import pallas as pl
from jax.experimental.pallas import tpu as pltpu

F32 = jnp.float32
BF16 = jnp.bfloat16

D_MODEL = 1024
DEPTH = 2
MIX_WIDTH = 512
N_BRANCH = 3
A_HEAD_DIM = 64
A_HEADS = 8
A_WIDTH = 512
A_RANK_W = 64
A_RANK_A = 64
A_RANK_G = 128
A_GN_EPS = 64e-5
B_KEY_DIM = 128
B_VAL_DIM = 128
B_HEADS = 4
B_WIDTH = 512
C_HEAD_DIM = 64
C_HEADS = 8
C_KV_HEADS = 2
C_WIDTH = 512
IDX_HEADS = 4
IDX_DIM = 64
TOPK_MAX = 256
ROPE_THETA = 10000.0
D_FF = 2816
CONV_WIDTH = 3
NORM_EPS = 1e-6

A_COLS = 3 * A_WIDTH + A_RANK_W + A_RANK_A + A_RANK_G
B_COLS = 4 * B_WIDTH
C_COLS = C_WIDTH + 2 * C_KV_HEADS * C_HEAD_DIM + IDX_HEADS * IDX_DIM + IDX_DIM + IDX_HEADS
GATE_COLS = N_BRANCH * D_MODEL

LANE = 128
SUBLANE = 8
VMEM_LIMIT = 48 * 1024 * 1024


def _round_up(n, m):
    return (n + m - 1) // m * m


def _rms(x, g):
    return x * lax.rsqrt(jnp.mean(x * x, axis=-1, keepdims=True) + NORM_EPS) * g


def _norm_matmul_kernel(x_ref, g_ref, w_ref, o_ref, xn_ref):
    @pl.when(pl.program_id(1) == 0)
    def _():
        xn_ref[...] = _rms(x_ref[...], g_ref[...]).astype(BF16)

    o_ref[...] = jnp.dot(xn_ref[...], w_ref[...], preferred_element_type=F32).astype(o_ref.dtype)


def norm_matmul(x, g, w, *, tm, tn, out_dtype):
    t, d = x.shape
    n = w.shape[1]
    assert t % tm == 0 and n % tn == 0
    return pl.pallas_call(
        _norm_matmul_kernel,
        grid=(t // tm, n // tn),
        in_specs=[
            pl.BlockSpec((tm, d), lambda i, j: (i, 0)),
            pl.BlockSpec((1, d), lambda i, j: (0, 0)),
            pl.BlockSpec((d, tn), lambda i, j: (0, j)),
        ],
        out_specs=pl.BlockSpec((tm, tn), lambda i, j: (i, j)),
        out_shape=jax.ShapeDtypeStruct((t, n), out_dtype),
        scratch_shapes=[pltpu.VMEM((tm, d), BF16)],
        compiler_params=pltpu.CompilerParams(
            dimension_semantics=("arbitrary", "arbitrary"), vmem_limit_bytes=VMEM_LIMIT),
        name="norm_matmul",
    )(x, g.reshape(1, d), w)


def _merge_kernel(ya_ref, yb_ref, yc_ref, gate_ref, h_ref, wa_ref, wb_ref, wc_ref, wo_ref, o_ref):
    merged = None
    for b, (y_ref, w_ref) in enumerate(((ya_ref, wa_ref), (yb_ref, wb_ref), (yc_ref, wc_ref))):
        br = jnp.dot(y_ref[...].astype(BF16), w_ref[...], preferred_element_type=F32)
        gt = jax.nn.sigmoid(gate_ref[:, b * D_MODEL:(b + 1) * D_MODEL].astype(F32))
        merged = gt * br if merged is None else merged + gt * br
    o_ref[...] = h_ref[...] + jnp.dot(merged.astype(BF16), wo_ref[...], preferred_element_type=F32)


def merge(ys, gate, h, ws, w_o, *, tm):
    t = h.shape[0]
    assert t % tm == 0
    row = lambda w: pl.BlockSpec((tm, w), lambda i: (i, 0))
    const = lambda a: pl.BlockSpec(a.shape, lambda i: (0, 0))
    return pl.pallas_call(
        _merge_kernel,
        grid=(t // tm,),
        in_specs=[row(y.shape[1]) for y in ys] + [row(GATE_COLS), row(D_MODEL)]
                 + [const(w) for w in ws] + [const(w_o)],
        out_specs=row(D_MODEL),
        out_shape=jax.ShapeDtypeStruct((t, D_MODEL), F32),
        compiler_params=pltpu.CompilerParams(
            dimension_semantics=("arbitrary",), vmem_limit_bytes=VMEM_LIMIT),
        name="merge",
    )(*ys, gate, h, *ws, w_o)


def _shift_rows(cur, prev_tail, shift):
    rolled = pltpu.roll(cur, shift, 0)
    tail = pltpu.roll(prev_tail, shift, 0)
    row = lax.broadcasted_iota(jnp.int32, tail.shape, 0)
    top = jnp.where(row < shift, tail, rolled[:SUBLANE])
    return jnp.concatenate([top, rolled[SUBLANE:]], axis=0)


def _conv_glu_kernel(h_ref, g_ref, wug_ref, wuv_ref, cwg_ref, cwv_ref, cbg_ref, cbv_ref, wd_ref, *rest,
                     final_norm):
    if final_norm:
        gf_ref, o_ref, xn_ref, acc_ref, carry_ref = rest
    else:
        o_ref, xn_ref, acc_ref, carry_ref = rest
    ti = pl.program_id(1)
    fi = pl.program_id(2)
    nf = pl.num_programs(2)

    @pl.when(fi == 0)
    def _():
        xn_ref[...] = _rms(h_ref[0], g_ref[...]).astype(BF16)
        acc_ref[...] = jnp.zeros_like(acc_ref)

    @pl.when(ti == 0)
    def _():
        carry_ref[fi] = jnp.zeros(carry_ref.shape[1:], F32)

    xn = xn_ref[...]

    def conv(w_ref, cw_ref, cb_ref, slot):
        up = jnp.dot(xn, w_ref[...], preferred_element_type=F32)
        tail = carry_ref[fi, slot]
        c = (cb_ref[...] + cw_ref[2:3, :] * up + cw_ref[1:2, :] * _shift_rows(up, tail, 1)
             + cw_ref[0:1, :] * _shift_rows(up, tail, 2))
        carry_ref[fi, slot] = up[up.shape[0] - SUBLANE:]
        return c

    cg = conv(wug_ref, cwg_ref, cbg_ref, 0)
    cv = conv(wuv_ref, cwv_ref, cbv_ref, 1)
    act = (cg * jax.nn.sigmoid(cg) * cv).astype(BF16)
    acc_ref[...] += jnp.dot(act, wd_ref[...], preferred_element_type=F32)

    @pl.when(fi == nf - 1)
    def _():
        out = h_ref[0] + acc_ref[...]
        if final_norm:
            out = _rms(out, gf_ref[...])
        o_ref[0] = out


def conv_glu(h, g, w_up, conv_w, conv_b, w_down, gf, *, tm, tf):
    bsz, seq, d = h.shape
    assert seq % tm == 0 and D_FF % tf == 0
    nf = D_FF // tf
    final_norm = gf is not None
    vec = lambda: pl.BlockSpec((1, d), lambda b, t, f: (0, 0))
    in_specs = [
        pl.BlockSpec((1, tm, d), lambda b, t, f: (b, t, 0)),
        vec(),
        pl.BlockSpec((d, tf), lambda b, t, f: (0, f)),
        pl.BlockSpec((d, tf), lambda b, t, f: (0, f + nf)),
        pl.BlockSpec((CONV_WIDTH, tf), lambda b, t, f: (0, f)),
        pl.BlockSpec((CONV_WIDTH, tf), lambda b, t, f: (0, f + nf)),
        pl.BlockSpec((1, tf), lambda b, t, f: (0, f)),
        pl.BlockSpec((1, tf), lambda b, t, f: (0, f + nf)),
        pl.BlockSpec((tf, d), lambda b, t, f: (f, 0)),
    ]
    args = [h, g.reshape(1, d), w_up, w_up, conv_w, conv_w, conv_b.reshape(1, -1), conv_b.reshape(1, -1), w_down]
    if final_norm:
        in_specs.append(vec())
        args.append(gf.reshape(1, d))
    return pl.pallas_call(
        functools.partial(_conv_glu_kernel, final_norm=final_norm),
        grid=(bsz, seq // tm, nf),
        in_specs=in_specs,
        out_specs=pl.BlockSpec((1, tm, d), lambda b, t, f: (b, t, 0)),
        out_shape=jax.ShapeDtypeStruct((bsz, seq, d), F32),
        scratch_shapes=[pltpu.VMEM((tm, d), BF16), pltpu.VMEM((tm, d), F32),
                        pltpu.VMEM((nf, 2, SUBLANE, tf), F32)],
        compiler_params=pltpu.CompilerParams(
            dimension_semantics=("arbitrary", "arbitrary", "arbitrary"), vmem_limit_bytes=VMEM_LIMIT),
        name="conv_glu",
    )(*args)


HP = LANE
A_PADW = A_HEADS * HP
A_CHUNK = 64
A_LOW = A_RANK_W + A_RANK_A + A_RANK_G
A_PCOLS = 3 * A_PADW + A_LOW


def _dot(x, y):
    return jnp.dot(x.astype(BF16), y.astype(BF16), preferred_element_type=F32)


def _dot_t(x, y):
    return lax.dot_general(x.astype(BF16), y.astype(BF16), (((1,), (1,)), ((), ())), preferred_element_type=F32)


def _rwkv_kernel(p_ref, mu_ref, w0_ref, a0_ref, wa_ref, gup_ref, kk_ref, ka_ref, rk_ref, gng_ref, gnb_ref,
                 o_ref, carry_ref, h_ref):
    C = A_CHUNK
    W = A_PADW

    @pl.when(pl.program_id(1) == 0)
    def _():
        carry_ref[...] = jnp.zeros_like(carry_ref)
        h_ref[...] = jnp.zeros_like(h_ref)

    p = p_ref[0].astype(F32)
    prev = _shift_rows(p, carry_ref[...], 1)
    carry_ref[...] = p[C - SUBLANE:]
    p = p + (prev - p) * mu_ref[...]
    lane = lax.broadcasted_iota(jnp.int32, (C, LANE), 1)
    valid = lane < A_HEAD_DIM

    def pad_heads(x):
        slabs = []
        for h in range(A_HEADS):
            s = x[:, (h // 2) * LANE:(h // 2 + 1) * LANE]
            if h % 2:
                s = pltpu.roll(s, A_HEAD_DIM, 1)
            slabs.append(jnp.where(valid, s, 0.0))
        return jnp.concatenate(slabs, axis=1)
    r, k, v = (pad_heads(p[:, i * A_WIDTH:(i + 1) * A_WIDTH]) for i in range(3))
    low = p[:, 3 * A_WIDTH:3 * A_WIDTH + LANE]
    gd = p[:, 3 * A_WIDTH + LANE:]
    low = jnp.where(lane < A_RANK_W, jnp.tanh(low), low)
    wa = _dot(low, wa_ref[...])
    w_log = -jax.nn.softplus(-(w0_ref[...] + wa[:, :W])) - 0.5
    logw = -jnp.exp(w_log)
    a = jax.nn.sigmoid(a0_ref[...] + wa[:, W:])
    g = _dot(jax.nn.sigmoid(gd), gup_ref[...])
    k_mod = k * (1.0 + (a - 1.0) * ka_ref[...])
    kk = k * kk_ref[...]

    row = lax.broadcasted_iota(jnp.int32, (C, C), 0)
    col = lax.broadcasted_iota(jnp.int32, (C, C), 1)
    incl = row >= col
    strict = row > col
    cum = jnp.dot(incl.astype(F32), logw, precision=lax.Precision.HIGHEST, preferred_element_type=F32)
    cum_end = cum[C - 1:C, :]
    e_neg = jnp.exp(-cum)
    e_end = jnp.exp(cum_end - cum)
    r_t = r * jnp.exp(cum)
    a_scale = jnp.exp(cum - logw)
    p_end = jnp.exp(cum_end)
    eye = (lax.broadcasted_iota(jnp.int32, (HP, HP), 0) == lax.broadcasted_iota(jnp.int32, (HP, HP), 1)).astype(F32)
    eye_c = (row == col).astype(F32)

    heads = range(A_HEADS)
    sls = [slice(h * HP, (h + 1) * HP) for h in heads]
    each = lambda f: [f(h) for h in heads]
    kk_n = each(lambda h: kk[:, sls[h]] * lax.rsqrt(jnp.maximum(
        jnp.sum(kk[:, sls[h]] * kk[:, sls[h]], axis=-1, keepdims=True), 1e-24)))
    b_h = each(lambda h: kk_n[h] * a[:, sls[h]])
    at = each(lambda h: -kk_n[h] * a_scale[:, sls[h]])
    ar = each(lambda h: jnp.concatenate([at[h], r_t[:, sls[h]]], axis=0))
    ab = each(lambda h: _dot_t(ar[h], b_h[h] * e_neg[:, sls[h]]))
    ak = each(lambda h: _dot_t(ar[h], k_mod[:, sls[h]] * e_neg[:, sls[h]]))
    a_ab = each(lambda h: jnp.where(strict, ab[h][:C], 0.0))
    a_rb = each(lambda h: jnp.where(incl, ab[h][C:], 0.0))
    a_ak = each(lambda h: jnp.where(strict, ak[h][:C], 0.0))
    a_rk = each(lambda h: jnp.where(incl, ak[h][C:], 0.0))
    tm = each(lambda h: eye_c + a_ab[h])
    pw = each(lambda h: _dot(a_ab[h], a_ab[h]))
    akv = each(lambda h: _dot(a_ak[h], v[:, sls[h]]))
    n = 2
    while 2 * n < C:
        x = each(lambda h: _dot(jnp.concatenate([tm[h], pw[h]], axis=0), pw[h]))
        tm = each(lambda h: tm[h] + x[h][:C])
        pw = each(lambda h: x[h][C:])
        n *= 2
    tm = each(lambda h: tm[h] + _dot(tm[h], pw[h]))
    x2 = each(lambda h: _dot(tm[h], jnp.concatenate([at[h], akv[h]], axis=1)))
    x3 = each(lambda h: _dot(a_rb[h], x2[h]))
    x4 = each(lambda h: _dot((b_h[h] * e_end[:, sls[h]]).T, x2[h]))
    rkv = each(lambda h: _dot(a_rk[h], v[:, sls[h]]))
    kv = each(lambda h: _dot((k_mod[:, sls[h]] * e_end[:, sls[h]]).T, v[:, sls[h]]))
    rm = each(lambda h: jnp.concatenate([r_t[:, sls[h]] + x3[h][:, :HP],
                                         eye * p_end[:, sls[h]] + x4[h][:, :HP]], axis=0))
    x5 = each(lambda h: _dot(rm[h], h_ref[h]))
    outs = []
    for h in heads:
        sl = sls[h]
        h_ref[h] = x5[h][C:] + x4[h][:, HP:] + kv[h]
        y = x5[h][:C] + x3[h][:, HP:] + rkv[h]
        mean = jnp.sum(y, axis=-1, keepdims=True) * (1.0 / A_HEAD_DIM)
        dlt = jnp.where(valid, y - mean, 0.0)
        var = jnp.sum(dlt * dlt, axis=-1, keepdims=True) * (1.0 / A_HEAD_DIM)
        yn = dlt * lax.rsqrt(var + A_GN_EPS) * gng_ref[:, sl] + gnb_ref[:, sl]
        bonus = jnp.sum(r[:, sl] * k_mod[:, sl] * rk_ref[:, sl], axis=-1, keepdims=True) * v[:, sl]
        outs.append((yn + bonus) * g[:, sl])
    o_ref[0] = jnp.concatenate(
        [outs[2 * m] + pltpu.roll(outs[2 * m + 1], A_HEAD_DIM, 1) for m in range(A_HEADS // 2)],
        axis=1).astype(o_ref.dtype)


def _pad_heads(t, heads):
    hd = t.shape[-1] // heads
    t = t.reshape(t.shape[:-1] + (heads, hd))
    t = jnp.pad(t, [(0, 0)] * (t.ndim - 1) + [(0, HP - hd)])
    return t.reshape(t.shape[:-2] + (heads * HP,))


def rwkv7(p, mu, w0, w_up, a0, a_up, g_up, k_k, k_a, r_k, gn_g, gn_b):
    bsz, seq, _ = p.shape
    C = A_CHUNK
    assert seq % C == 0
    z = jnp.zeros((A_RANK_W, A_PADW), F32)
    wa = jnp.concatenate([jnp.concatenate([_pad_heads(w_up, A_HEADS), z], axis=1),
                          jnp.concatenate([z, _pad_heads(a_up, A_HEADS)], axis=1)], axis=0).astype(BF16)
    vecs = [_pad_heads(t.reshape(-1), A_HEADS).reshape(1, -1) for t in (k_k, k_a, r_k, gn_g, gn_b)]
    full = lambda shape: pl.BlockSpec(shape, lambda b, c: (0,) * len(shape))
    return pl.pallas_call(
        _rwkv_kernel,
        grid=(bsz, seq // C),
        in_specs=[pl.BlockSpec((1, C, A_COLS), lambda b, c: (b, c, 0)),
                  full((1, A_COLS)), full((1, A_PADW)), full((1, A_PADW)),
                  full((LANE, 2 * A_PADW)), full((A_RANK_G, A_PADW))] + [full((1, A_PADW))] * 5,
        out_specs=pl.BlockSpec((1, C, A_WIDTH), lambda b, c: (b, c, 0)),
        out_shape=jax.ShapeDtypeStruct((bsz, seq, A_WIDTH), BF16),
        scratch_shapes=[pltpu.VMEM((SUBLANE, A_COLS), F32), pltpu.VMEM((A_HEADS, HP, HP), F32)],
        compiler_params=pltpu.CompilerParams(
            dimension_semantics=("arbitrary", "arbitrary"), vmem_limit_bytes=VMEM_LIMIT),
        name="rwkv7",
    )(p, mu.reshape(1, -1), _pad_heads(w0, A_HEADS).reshape(1, -1), _pad_heads(a0, A_HEADS).reshape(1, -1), wa,
      _pad_heads(g_up, A_HEADS).astype(BF16), *vecs)


B_CHUNK = 64
B_SUB = 16


def _dot_tl(x, y):
    return lax.dot_general(x.astype(BF16), y.astype(BF16), (((0,), (0,)), ((), ())), preferred_element_type=F32)


def _hgrn_kernel(p_ref, lb_ref, gn_ref, o_ref, st_ref):
    C, SB, W = B_CHUNK, B_SUB, B_WIDTH
    nsb = C // SB
    D = B_KEY_DIM

    @pl.when(pl.program_id(1) == 0)
    def _():
        st_ref[...] = jnp.zeros_like(st_ref)

    p = p_ref[0].astype(F32)
    q, f, val, gate = p[:, :W], p[:, W:2 * W], p[:, 2 * W:3 * W], p[:, 3 * W:]
    lb = lb_ref[...]
    log_f = jnp.logaddexp(jnp.log(lb), jnp.log1p(-lb) + jax.nn.log_sigmoid(f))
    k = (1.0 - lb) * jax.nn.sigmoid(-f)
    row = lax.broadcasted_iota(jnp.int32, (C, C), 0)
    col = lax.broadcasted_iota(jnp.int32, (C, C), 1)
    b = jnp.dot((row >= col).astype(F32), log_f, precision=lax.Precision.HIGHEST,
                preferred_element_type=F32)
    b_end = b[C - 1:C, :]
    b0 = jnp.concatenate(
        [jnp.zeros((SB, W), F32)] + [jnp.broadcast_to(b[i * SB - 1:i * SB, :], (SB, W)) for i in range(1, nsb)],
        axis=0)
    qt = q * jnp.exp(b - b0)
    qe = q * jnp.exp(b)
    ke = k * jnp.exp(b_end - b)
    e_end = jnp.exp(b_end)
    rowd = lax.broadcasted_iota(jnp.int32, (C, D), 0)
    lane_c = lax.broadcasted_iota(jnp.int32, (SB, C), 1)
    row_sb = lax.broadcasted_iota(jnp.int32, (SB, C), 0)

    heads = range(B_HEADS)
    sls = [slice(h * D, (h + 1) * D) for h in heads]
    each = lambda fn: [fn(h) for h in heads]

    def off_operands(h):
        kts, qts = [], []
        for i in range(1, nsb):
            b0_i = b[i * SB - 1:i * SB, sls[h]]
            kts.append(jnp.where(rowd < i * SB, k[:, sls[h]] * jnp.exp(jnp.minimum(b0_i - b[:, sls[h]], 0.0)), 0.0))
            qts.append(jnp.where((rowd >= i * SB) & (rowd < (i + 1) * SB), qt[:, sls[h]], 0.0))
        return jnp.concatenate(kts, axis=1), jnp.concatenate(qts, axis=1)
    offs = each(off_operands)
    st_off = each(lambda h: _dot_t(offs[h][0], offs[h][1]))

    def diag_t(h):
        blocks = []
        for i in range(nsb):
            rs = slice(i * SB, (i + 1) * SB)
            k_i, b_i = k[rs, sls[h]], b[rs, sls[h]]
            acc = jnp.zeros((SB, C), F32)
            for j in range(SB):
                t = i * SB + j
                z = k_i * q[t:t + 1, sls[h]] * jnp.exp(jnp.minimum(b[t:t + 1, sls[h]] - b_i, 0.0))
                acc = jnp.where(lane_c == t, jnp.sum(z, axis=-1, keepdims=True), acc)
            blocks.append(jnp.where(row_sb + i * SB <= lane_c, acc, 0.0))
        return jnp.concatenate(blocks, axis=0)
    st_diag = each(diag_t)
    o_intra = each(lambda h: _dot_tl(st_off[h] + st_diag[h], val[:, sls[h]]))
    o_state = each(lambda h: _dot_t(qe[:, sls[h]], st_ref[h]))
    upd = each(lambda h: _dot_tl(val[:, sls[h]], ke[:, sls[h]]))
    outs = []
    for h in heads:
        st_ref[h] = st_ref[h] * e_end[:, sls[h]] + upd[h]
        o = o_intra[h] + o_state[h]
        o = o * lax.rsqrt(jnp.mean(o * o, axis=-1, keepdims=True) + NORM_EPS)
        gt = gate[:, sls[h]]
        outs.append(o * gn_ref[:, sls[h]] * (gt * jax.nn.sigmoid(gt)))
    o_ref[0] = jnp.concatenate(outs, axis=1).astype(o_ref.dtype)


def hgrn2(p, lb, gn_g):
    bsz, seq, _ = p.shape
    C = B_CHUNK
    assert seq % C == 0
    vec = pl.BlockSpec((1, B_WIDTH), lambda b, c: (0, 0))
    return pl.pallas_call(
        _hgrn_kernel,
        grid=(bsz, seq // C),
        in_specs=[pl.BlockSpec((1, C, B_COLS), lambda b, c: (b, c, 0)), vec, vec],
        out_specs=pl.BlockSpec((1, C, B_WIDTH), lambda b, c: (b, c, 0)),
        out_shape=jax.ShapeDtypeStruct((bsz, seq, B_WIDTH), BF16),
        scratch_shapes=[pltpu.VMEM((B_HEADS, B_VAL_DIM, B_KEY_DIM), F32)],
        compiler_params=pltpu.CompilerParams(
            dimension_semantics=("arbitrary", "arbitrary"), vmem_limit_bytes=VMEM_LIMIT),
        name="hgrn2",
    )(p, lb.reshape(1, -1), gn_g.reshape(1, -1))


C_PAD = _round_up(C_COLS, LANE)
QB = 128
C_KV_W = C_KV_HEADS * C_HEAD_DIM
C_K_OFF = C_WIDTH
C_QI_OFF = C_WIDTH + 2 * C_KV_W
C_KI_OFF = C_QI_OFF + IDX_HEADS * IDX_DIM
INT_MIN = -2 ** 31
NEG = -1e30
HALF = C_HEAD_DIM // 2
HALF_RANGE = 1 << 15
VT_ROWS = C_HEAD_DIM + 16


def _rope(t, cos, sin_signed):
    outs = []
    lane = lax.broadcasted_iota(jnp.int32, (t.shape[0], LANE), 1)
    first_half = (lane % C_HEAD_DIM) < HALF
    for c in range(t.shape[1] // LANE):
        x = t[:, c * LANE:(c + 1) * LANE]
        swapped = jnp.where(first_half, pltpu.roll(x, LANE - HALF, 1), pltpu.roll(x, HALF, 1))
        outs.append(x * cos + swapped * sin_signed)
    return outs


GROUP = C_HEADS // C_KV_HEADS


def _split_hi_lo(x):
    hi = x.astype(BF16)
    return hi, (x - hi.astype(F32)).astype(BF16)


def _dsa_kernel(pq_ref, kv_ref, ki_ref, cosq_ref, sinq_ref, cosf_ref, sinf_ref, o_ref,
                k_s, vt_s, ki_s, key_s, khi_s, klo_s, *, k_sel, seq):
    i = pl.program_id(1)
    nk = i + 1
    lane = lax.broadcasted_iota(jnp.int32, (QB, LANE), 1)
    rowi = lax.broadcasted_iota(jnp.int32, (QB, LANE), 0)
    low_half = lane < C_HEAD_DIM

    @pl.when(i == 0)
    def _():
        kv = kv_ref[0].astype(F32)
        k_s[...] = _rope(kv[:, :LANE], cosf_ref[...], sinf_ref[...])[0].astype(BF16)
        vt = kv[:, LANE:].T
        extra = (lax.broadcasted_iota(jnp.int32, (VT_ROWS - C_HEAD_DIM, seq), 0) == 0).astype(F32)
        for c in range(C_KV_HEADS):
            vt_s[c] = jnp.concatenate([vt[c * C_HEAD_DIM:(c + 1) * C_HEAD_DIM], extra], axis=0).astype(BF16)
        lane_f = lax.broadcasted_iota(jnp.int32, (seq, LANE), 1)
        ki = jnp.where(lane_f < IDX_DIM, ki_ref[0].astype(F32), 0.0)
        ki = _rope(ki, cosf_ref[...], sinf_ref[...])[0]
        ki_s[...] = (ki + pltpu.roll(ki, IDX_DIM, 1)).astype(BF16)

    pq = pq_ref[0].astype(F32)
    cos, sin = cosq_ref[...], sinq_ref[...]
    q = _rope(pq[:, :C_WIDTH], cos, sin)
    qi = _rope(pq[:, C_QI_OFF:C_KI_OFF], cos, sin)
    qi3 = jnp.concatenate(
        [jnp.where(low_half if h % 2 == 0 else ~low_half, qi[h // 2], 0.0) for h in range(IDX_HEADS)],
        axis=0).astype(BF16)
    wt = pq[:, C_KI_OFF:C_KI_OFF + LANE].T
    wscale = (IDX_HEADS ** -0.5) * (IDX_DIM ** -0.5)
    w_rows = [wt[IDX_DIM + h:IDX_DIM + h + 1, :] * wscale for h in range(IDX_HEADS)]
    causal_off = i * QB + lane

    npair = (nk + 1) // 2

    def score_tile(jp, _):
        ki_t = ki_s[pl.ds(pl.multiple_of(jp * (2 * QB), 2 * QB), 2 * QB), :]
        d = lax.dot_general(ki_t, qi3, (((1,), (1,)), ((), ())), preferred_element_type=F32)
        for u in range(2):
            j = 2 * jp + u
            sc = jnp.zeros((QB, QB), F32)
            for h in range(IDX_HEADS):
                sc = sc + jnp.maximum(d[u * QB:(u + 1) * QB, h * QB:(h + 1) * QB], 0.0) * w_rows[h]
            bits = pltpu.bitcast(sc + 0.0, jnp.int32)
            key = jnp.where(bits < 0, bits ^ jnp.int32(0x7fffffff), bits)
            key = jnp.where(j * QB + rowi <= causal_off, key, jnp.int32(INT_MIN))
            key_s[j] = key
            khi_s[j] = jnp.right_shift(key, 16).astype(jnp.int16)
            klo_s[j] = ((key & 0xffff) - HALF_RANGE).astype(jnp.int16)
        return 0
    lax.fori_loop(0, npair, score_tile, 0)

    def count16(ref, pred):
        def body(jp, acc):
            for u in range(2):
                acc = acc + jnp.where(pred(ref[2 * jp + u]), jnp.int16(1), jnp.int16(0))
            return acc
        acc = lax.fori_loop(0, npair, body, jnp.zeros((QB, QB), jnp.int16))
        return jnp.sum(acc.astype(jnp.int32), axis=0, keepdims=True)

    def half_search(ref, base):
        def bit_step(b, cand):
            trial = cand | jnp.left_shift(jnp.int32(1), 15 - b)
            t16 = (trial - HALF_RANGE).astype(jnp.int16)
            cnt = base + count16(ref, lambda x: x >= t16)
            return jnp.where(cnt >= k_sel, trial, cand)
        return lax.fori_loop(0, 16, bit_step, jnp.zeros((1, QB), jnp.int32))

    cand_hi = half_search(khi_s, 0)
    hi16 = (cand_hi - HALF_RANGE).astype(jnp.int16)
    above = count16(khi_s, lambda x: x > hi16)

    def mask_low(jp, _):
        for u in range(2):
            j = 2 * jp + u
            klo_s[j] = jnp.where(khi_s[j] == hi16, klo_s[j], jnp.int16(-HALF_RANGE))
        return 0
    lax.fori_loop(0, npair, mask_low, 0)
    cand_lo = half_search(klo_s, above)
    thr = (jnp.left_shift(cand_hi, 16) | cand_lo) ^ jnp.int32(INT_MIN)

    def count(pred):
        def body(jp, acc):
            for u in range(2):
                acc = acc + jnp.where(pred(key_s[2 * jp + u]), 1, 0)
            return acc
        return jnp.sum(lax.fori_loop(0, npair, body, jnp.zeros((QB, QB), jnp.int32)), axis=0, keepdims=True)
    need = k_sel - count(lambda key: key > thr)

    need_f = need.astype(F32)
    tri = (lax.broadcasted_iota(jnp.int32, (2 * QB, 2 * QB), 0)
           >= lax.broadcasted_iota(jnp.int32, (2 * QB, 2 * QB), 1)).astype(BF16)

    scale = C_HEAD_DIM ** -0.5
    kvs = range(C_KV_HEADS)
    in_cs = [(lane >= c * C_HEAD_DIM) & (lane < (c + 1) * C_HEAD_DIM) for c in kvs]

    chains = range(C_HEADS // 2)
    kv_of = [n * 2 // GROUP for n in chains]

    def q_rows(n):
        c = kv_of[n]
        rows = []
        for h in (2 * n, 2 * n + 1):
            x = q[n]
            if h % 2 != c:
                x = pltpu.roll(x, C_HEAD_DIM, 1)
            rows.append(jnp.where(in_cs[c], x * scale, 0.0))
        return jnp.concatenate(rows, axis=0).astype(BF16)
    qcs = [q_rows(n) for n in chains]

    def att_tile(jp, carry):
        off = pl.multiple_of(jp * (2 * QB), 2 * QB)
        keys = jnp.concatenate([key_s[2 * jp], key_s[2 * jp + 1]], axis=0)
        pos = off + lax.broadcasted_iota(jnp.int32, (2 * QB, LANE), 0)
        tie = keys == thr
        rank = carry[-1] + jnp.dot(tri, jnp.where(tie, 1.0, 0.0).astype(BF16), preferred_element_type=F32)
        sel = ((keys > thr) | (tie & (rank <= need_f))) & (pos <= causal_off[:1, :])
        bias1 = jnp.where(sel, 0.0, NEG)
        bias = jnp.concatenate([bias1, bias1], axis=1)
        k_t = k_s[pl.ds(off, 2 * QB), :]
        s = [lax.dot_general(k_t, qcs[n], (((1,), (1,)), ((), ())), preferred_element_type=F32) for n in chains]
        s = [s[n] + bias for n in chains]
        m_new = [jnp.maximum(carry[2 * n], jnp.max(s[n], axis=0, keepdims=True)) for n in chains]
        p = [jnp.exp(s[n] - m_new[n]).astype(BF16) for n in chains]
        v_t = [vt_s[c, :, pl.ds(off, 2 * QB)] for c in kvs]
        pv = [jnp.dot(v_t[kv_of[n]], p[n], preferred_element_type=F32) for n in chains]
        out = []
        for n in chains:
            out += [m_new[n], carry[2 * n + 1] * jnp.exp(carry[2 * n] - m_new[n]) + pv[n]]
        return tuple(out) + (rank[2 * QB - 1:, :],)
    init = ((jnp.full((1, 2 * QB), NEG, F32), jnp.zeros((VT_ROWS, 2 * QB), F32)) * len(chains)
            + (jnp.zeros((1, QB), F32),))
    res = lax.fori_loop(0, npair, att_tile, init)

    tiles = []
    for n in chains:
        acc = res[2 * n + 1]
        out_t = acc[:C_HEAD_DIM] / acc[C_HEAD_DIM:C_HEAD_DIM + 1]
        tiles.append(jnp.concatenate([out_t[:, :QB], out_t[:, QB:]], axis=0).T)
    o_ref[0] = jnp.concatenate(tiles, axis=1).astype(o_ref.dtype)


def _dsa_kernel_rows(pq_ref, kv_ref, ki_ref, cosq_ref, sinq_ref, cosf_ref, sinf_ref, o_ref,
                     k_s, v_s, ki_s, key_s, *, k_sel, seq):
    i = pl.program_id(1)
    nk = i + 1
    lane = lax.broadcasted_iota(jnp.int32, (QB, LANE), 1)
    rowi = lax.broadcasted_iota(jnp.int32, (QB, LANE), 0)
    low_half = lane < C_HEAD_DIM

    @pl.when(i == 0)
    def _():
        kv = kv_ref[0]
        k_s[...] = _rope(kv[:, :LANE], cosf_ref[...], sinf_ref[...])[0].astype(BF16)
        v_s[...] = kv[:, LANE:].astype(BF16)
        lane_f = lax.broadcasted_iota(jnp.int32, (seq, LANE), 1)
        ki = jnp.where(lane_f < IDX_DIM, ki_ref[0], 0.0)
        ki = _rope(ki, cosf_ref[...], sinf_ref[...])[0]
        ki_s[...] = ki + pltpu.roll(ki, IDX_DIM, 1)

    pq = pq_ref[0]
    cos, sin = cosq_ref[...], sinq_ref[...]
    q = _rope(pq[:, :C_WIDTH], cos, sin)
    qi = _rope(pq[:, C_QI_OFF:C_KI_OFF], cos, sin)
    wslab = pq[:, C_KI_OFF:C_KI_OFF + LANE]
    wscale = (IDX_HEADS ** -0.5) * (IDX_DIM ** -0.5)
    wi = [jnp.sum(jnp.where(lane == IDX_DIM + h, wslab, 0.0), axis=-1, keepdims=True) * wscale
          for h in range(IDX_HEADS)]
    qi_h = [jnp.where(low_half if h % 2 == 0 else ~low_half, qi[h // 2], 0.0) for h in range(IDX_HEADS)]

    def score_tile(j, _):
        ki_t = ki_s[pl.ds(pl.multiple_of(j * QB, QB), QB), :]
        sc = jnp.zeros((QB, QB), F32)
        for h in range(IDX_HEADS):
            d = lax.dot_general(qi_h[h], ki_t, (((1,), (1,)), ((), ())), precision=lax.Precision.HIGHEST,
                                preferred_element_type=F32)
            sc = sc + jnp.maximum(d, 0.0) * wi[h]
        bits = pltpu.bitcast(sc + 0.0, jnp.int32)
        key = jnp.where(bits < 0, bits ^ jnp.int32(0x7fffffff), bits)
        key_s[j] = jnp.where(j * QB + lane <= i * QB + rowi, key, jnp.int32(INT_MIN))
        return 0
    lax.fori_loop(0, nk, score_tile, 0)

    def count(pred):
        def body(j, acc):
            return acc + jnp.where(pred(key_s[j], j), 1, 0)
        return jnp.sum(lax.fori_loop(0, nk, body, jnp.zeros((QB, QB), jnp.int32)), axis=-1, keepdims=True)

    def bit_step(b, cand):
        trial = cand | jnp.left_shift(jnp.int32(1), 31 - b)
        cnt = count(lambda key, j: key >= (trial ^ jnp.int32(INT_MIN)))
        return jnp.where(cnt >= k_sel, trial, cand)
    cand = lax.fori_loop(0, 32, bit_step, jnp.zeros((QB, 1), jnp.int32))
    thr = cand ^ jnp.int32(INT_MIN)
    need = k_sel - count(lambda key, j: key > thr)

    nbits = int(np.ceil(np.log2(seq))) + 1

    def idx_step(b, jj):
        trial = jj | jnp.left_shift(jnp.int32(1), nbits - 1 - b)
        cnt = count(lambda key, j: (key == thr) & (j * QB + lane < trial))
        return jnp.where(cnt <= need, trial, jj)
    jlim = lax.fori_loop(0, nbits, idx_step, jnp.zeros((QB, 1), jnp.int32))

    group = C_HEADS // C_KV_HEADS
    scale = C_HEAD_DIM ** -0.5
    outs = []
    for c in range(C_KV_HEADS):
        in_c = (lane >= c * C_HEAD_DIM) & (lane < (c + 1) * C_HEAD_DIM)
        rows = []
        for g in range(group):
            h = c * group + g
            x = q[h // 2]
            if h % 2 != c:
                x = pltpu.roll(x, C_HEAD_DIM, 1)
            rows.append(jnp.where(in_c, x * scale, 0.0))
        qc = jnp.concatenate(rows, axis=0).astype(BF16)
        ones_lane = (1 - c) * C_HEAD_DIM

        def att_tile(j, carry):
            m, acc = carry
            off = pl.multiple_of(j * QB, QB)
            key = key_s[j]
            sel = ((key > thr) | ((key == thr) & (j * QB + lane < jlim))) & (j * QB + lane <= i * QB + rowi)
            bias = jnp.where(sel, 0.0, NEG)
            s = lax.dot_general(qc, k_s[pl.ds(off, QB), :], (((1,), (1,)), ((), ())),
                                preferred_element_type=F32)
            s = jnp.where(jnp.concatenate([bias] * group, axis=0) < 0.0, NEG, s)
            m_new = jnp.maximum(m, jnp.max(s, axis=-1, keepdims=True))
            alpha = jnp.exp(m - m_new)
            p = jnp.exp(s - m_new)
            v_t = v_s[pl.ds(off, QB), :]
            v_t = jnp.where(in_c, v_t, jnp.where(lane == ones_lane, 1.0, 0.0).astype(BF16))
            acc = acc * alpha + jnp.dot(p.astype(BF16), v_t, preferred_element_type=F32)
            return m_new, acc
        m0 = jnp.full((group * QB, 1), NEG, F32)
        acc0 = jnp.zeros((group * QB, LANE), F32)
        _, acc = lax.fori_loop(0, nk, att_tile, (m0, acc0))
        l = jnp.sum(jnp.where(jnp.concatenate([lane] * group, axis=0) == ones_lane, acc, 0.0),
                    axis=-1, keepdims=True)
        out = acc / l
        for g in range(group):
            h = c * group + g
            x = jnp.where(in_c, out[g * QB:(g + 1) * QB], 0.0)
            if h % 2 != c:
                x = pltpu.roll(x, C_HEAD_DIM, 1)
            outs.append(x)
    o_ref[0] = jnp.concatenate([outs[2 * m] + outs[2 * m + 1] for m in range(C_HEADS // 2)],
                               axis=1).astype(o_ref.dtype)


def _rope_lanes(seq):
    inv_freq = ROPE_THETA ** (-jnp.arange(HALF, dtype=F32) * (2.0 / C_HEAD_DIM))
    ang = jnp.arange(seq, dtype=F32)[:, None] * inv_freq[None, :]
    cos, sin = jnp.cos(ang), jnp.sin(ang)
    reps = LANE // C_HEAD_DIM
    return (jnp.tile(jnp.concatenate([cos, cos], axis=1), (1, reps)),
            jnp.tile(jnp.concatenate([-sin, sin], axis=1), (1, reps)))


def dsa(p):
    bsz, seq, width = p.shape
    assert seq % (2 * QB) == 0 and width == C_PAD
    k_sel = min(TOPK_MAX, seq // 4)
    cos, sin = _rope_lanes(seq)
    return pl.pallas_call(
        functools.partial(_dsa_kernel, k_sel=k_sel, seq=seq),
        grid=(bsz, seq // QB),
        in_specs=[pl.BlockSpec((1, QB, C_PAD), lambda b, i: (b, i, 0)),
                  pl.BlockSpec((1, seq, 2 * C_KV_W), lambda b, i: (b, 0, C_K_OFF // (2 * C_KV_W))),
                  pl.BlockSpec((1, seq, LANE), lambda b, i: (b, 0, C_KI_OFF // LANE)),
                  pl.BlockSpec((QB, LANE), lambda b, i: (i, 0)),
                  pl.BlockSpec((QB, LANE), lambda b, i: (i, 0)),
                  pl.BlockSpec((seq, LANE), lambda b, i: (0, 0)),
                  pl.BlockSpec((seq, LANE), lambda b, i: (0, 0))],
        out_specs=pl.BlockSpec((1, QB, C_WIDTH), lambda b, i: (b, i, 0)),
        out_shape=jax.ShapeDtypeStruct((bsz, seq, C_WIDTH), BF16),
        scratch_shapes=[pltpu.VMEM((seq, LANE), BF16), pltpu.VMEM((C_KV_HEADS, VT_ROWS, seq), BF16),
                        pltpu.VMEM((seq, LANE), BF16), pltpu.VMEM((seq // QB, QB, QB), jnp.int32),
                        pltpu.VMEM((seq // QB, QB, QB), jnp.int16), pltpu.VMEM((seq // QB, QB, QB), jnp.int16)],
        compiler_params=pltpu.CompilerParams(
            dimension_semantics=("arbitrary", "arbitrary"), vmem_limit_bytes=VMEM_LIMIT),
        name="dsa",
    )(p, p, p, cos, sin, cos, sin)


def _split(t, sizes):
    out, start = [], 0
    for s in sizes:
        out.append(t[..., start:start + s])
        start += s
    return out


def _mm(x, y):
    return jnp.einsum('...ij,...jk->...ik', x.astype(BF16), y.astype(BF16), preferred_element_type=F32)


def _mm_hi(x, y):
    return jnp.einsum('...ij,...jk->...ik', x, y, precision=lax.Precision.HIGHEST)


def _rwkv7_chunked(r, logw, k, v, a, b, C=64):
    B, S, H, N = r.shape
    nc = S // C

    def ch(t):
        return t.reshape(B, nc, C, H, N).transpose(1, 0, 3, 2, 4)
    r, logw, k, v, a, b = map(ch, (r, logw, k, v, a, b))
    cum = jnp.cumsum(logw, axis=-2)
    cum_prev = cum - logw
    cum_end = cum[..., -1:, :]
    At = a * jnp.exp(cum_prev)
    Rt = r * jnp.exp(cum)
    Bt = b * jnp.exp(-cum)
    Kt = k * jnp.exp(-cum)
    Bh = b * jnp.exp(cum_end - cum)
    Kh = k * jnp.exp(cum_end - cum)
    Pc = jnp.exp(cum_end)[..., 0, :]
    T_ = lambda x: jnp.swapaxes(x, -1, -2)
    ti = jnp.arange(C)
    strict = ti[:, None] > ti[None, :]
    incl = ti[:, None] >= ti[None, :]
    Aab = jnp.where(strict, _mm(At, T_(Bt)), 0.0)
    Aak = jnp.where(strict, _mm(At, T_(Kt)), 0.0)
    Arb = jnp.where(incl, _mm(Rt, T_(Bt)), 0.0)
    Ark = jnp.where(incl, _mm(Rt, T_(Kt)), 0.0)
    Tm = jnp.eye(C, dtype=F32) + Aab
    Pw = Aab
    n = 1
    while 2 * n < C:
        Pw = _mm_hi(Pw, Pw)
        Tm = Tm + _mm_hi(Tm, Pw)
        n *= 2
    Abar = _mm(Tm, At)
    U0 = _mm(Tm, _mm(Aak, v))
    Rbar = Rt + _mm(Arb, Abar)
    Y0 = _mm(Arb, U0) + _mm(Ark, v)
    M = Pc[..., :, None] * jnp.eye(N, dtype=F32) + _mm(T_(Bh), Abar)
    Hd = _mm(T_(Bh), U0) + _mm(T_(Kh), v)

    def step(Hs, inp):
        Rbar_c, Y0_c, M_c, Hd_c = inp
        y = _mm(Rbar_c, Hs) + Y0_c
        Hs = _mm_hi(M_c, Hs) + Hd_c
        return Hs, y
    _, ys = lax.scan(step, jnp.zeros((B, H, N, N), F32), (Rbar, Y0, M, Hd))
    return ys.transpose(1, 0, 3, 2, 4).reshape(B, S, H, N)


def _token_shift(p):
    return jnp.pad(p, ((0, 0), (1, 0), (0, 0)))[:, :-1]


def _rwkv7_branch(p, mu, w0, w_up, a0, a_up, g_up, k_k, k_a, r_k, gn_g, gn_b):
    bsz, seq, _ = p.shape
    p = p + (_token_shift(p) - p) * mu
    r, k, v, wd, ad, gd = _split(p, (A_WIDTH, A_WIDTH, A_WIDTH, A_RANK_W, A_RANK_A, A_RANK_G))
    w_log = -jax.nn.softplus(-(w0 + _mm(jnp.tanh(wd), w_up))) - 0.5
    logw = -jnp.exp(w_log)
    a = jax.nn.sigmoid(a0 + _mm(ad, a_up))
    g = _mm(jax.nn.sigmoid(gd), g_up)

    def hd(t):
        return t.reshape(bsz, seq, A_HEADS, A_HEAD_DIM)
    kk = hd(k * k_k)
    kk = kk * lax.rsqrt(jnp.maximum(jnp.sum(kk * kk, axis=-1, keepdims=True), 1e-24))
    k_mod = hd(k * (1.0 + (a - 1.0) * k_a))
    r_h, v_h, a_h = hd(r), hd(v), hd(a)
    y = _rwkv7_chunked(r_h, hd(logw), k_mod, v_h, -kk, kk * a_h)
    mean = jnp.mean(y, axis=-1, keepdims=True)
    var = jnp.mean(jnp.square(y - mean), axis=-1, keepdims=True)
    y = ((y - mean) * lax.rsqrt(var + A_GN_EPS)).reshape(bsz, seq, A_WIDTH)
    y = y * gn_g + gn_b
    bonus = jnp.sum(r_h * k_mod * r_k, axis=-1, keepdims=True) * v_h
    y = y + bonus.reshape(bsz, seq, A_WIDTH)
    return y * g


def _hgrn2_sub(q, k, g, v, C=64, SB=16):
    B, S, H, dk = q.shape
    dv = v.shape[-1]
    nc = S // C

    def ch(t):
        return t.reshape(B, nc, C, H, t.shape[-1]).transpose(1, 0, 3, 2, 4)
    q, k, g, v = map(ch, (q, k, g, v))
    b = jnp.cumsum(g, axis=-2)
    nsb = C // SB
    T_ = lambda x: jnp.swapaxes(x, -1, -2)

    def step(state, inp):
        qc, kc, bc, vc = inp
        rows = []
        for I in range(nsb):
            sl = slice(I * SB, (I + 1) * SB)
            qI, bI = qc[..., sl, :], bc[..., sl, :]
            b0 = bc[..., I * SB - 1:I * SB, :] if I > 0 else jnp.zeros_like(bc[..., :1, :])
            qt = qI * jnp.exp(bI - b0)
            blocks = []
            if I > 0:
                kt = kc[..., :I * SB, :] * jnp.exp(b0 - bc[..., :I * SB, :])
                blocks.append(_mm(qt, T_(kt)))
            diff = bI[..., :, None, :] - bI[..., None, :, :]
            tri = jnp.tril(jnp.ones((SB, SB), bool))
            dec = jnp.exp(jnp.where(tri[:, :, None], diff, -jnp.inf))
            diag = jnp.sum(qI[..., :, None, :] * kc[..., sl, :][..., None, :, :] * dec, axis=-1)
            blocks.append(diag)
            if I < nsb - 1:
                blocks.append(jnp.zeros(diag.shape[:-1] + (C - (I + 1) * SB,), F32))
            rows.append(jnp.concatenate(blocks, axis=-1))
        scores = jnp.concatenate(rows, axis=-2)
        o = _mm(scores, vc) + _mm(qc * jnp.exp(bc), state)
        b_end = bc[..., -1:, :]
        state = state * T_(jnp.exp(b_end)) + _mm(T_(kc * jnp.exp(b_end - bc)), vc)
        return state, o
    _, o = lax.scan(step, jnp.zeros((B, H, dk, dv), F32), (q, k, b, v))
    return o.transpose(1, 0, 3, 2, 4).reshape(B, S, H, dv)


def _hgrn2_branch(p, lb, gn_g):
    bsz, seq, _ = p.shape
    q, f, i, g = _split(p, (B_WIDTH,) * 4)
    log_f = jnp.logaddexp(jnp.log(lb), jnp.log1p(-lb) + jax.nn.log_sigmoid(f))
    k_in = (1.0 - lb) * jax.nn.sigmoid(-f)
    hd = lambda t: t.reshape(bsz, seq, B_HEADS, -1)
    o = _hgrn2_sub(hd(q), hd(k_in), hd(log_f), hd(i))
    o = o * lax.rsqrt(jnp.mean(o * o, axis=-1, keepdims=True) + NORM_EPS)
    return o.reshape(bsz, seq, B_WIDTH) * gn_g * jax.nn.silu(g)


def _rope_tables(seq, dim):
    half = dim // 2
    inv_freq = ROPE_THETA ** (-jnp.arange(half, dtype=F32) * (2.0 / dim))
    ang = jnp.arange(seq, dtype=F32)[:, None] * inv_freq[None, :]
    return jnp.cos(ang), jnp.sin(ang)


def _apply_rope(t, cos, sin):
    half = t.shape[-1] // 2
    t1, t2 = t[..., :half], t[..., half:]
    c, s = cos[None, :, None, :], sin[None, :, None, :]
    return jnp.concatenate([t1 * c - t2 * s, t2 * c + t1 * s], axis=-1)


def _dsa_branch(p, cos, sin):
    bsz, seq, _ = p.shape
    k_sel = min(TOPK_MAX, seq // 4)
    kv_w = C_KV_HEADS * C_HEAD_DIM
    q, k, v, qi, ki, wi = _split(p[..., :C_COLS], (C_WIDTH, kv_w, kv_w, IDX_HEADS * IDX_DIM, IDX_DIM, IDX_HEADS))
    q = _apply_rope(q.reshape(bsz, seq, C_HEADS, C_HEAD_DIM), cos, sin)
    k = _apply_rope(k.reshape(bsz, seq, C_KV_HEADS, C_HEAD_DIM), cos, sin)
    v = v.reshape(bsz, seq, C_KV_HEADS, C_HEAD_DIM)
    qi = _apply_rope(qi.reshape(bsz, seq, IDX_HEADS, IDX_DIM), cos, sin)
    ki = _apply_rope(ki[:, :, None, :], cos, sin)[:, :, 0, :]
    wi = wi * (IDX_HEADS ** -0.5) * (IDX_DIM ** -0.5)
    dots = jnp.einsum('bqhd,bsd->bqhs', qi, ki, precision=lax.Precision.HIGHEST)
    score = jnp.sum(jax.nn.relu(dots) * wi[..., None], axis=2)
    pos = jnp.arange(seq)
    causal = pos[None, :] <= pos[:, None]
    bits = lax.bitcast_convert_type(score + 0.0, jnp.int32)
    key = jnp.where(bits < 0, bits ^ jnp.int32(0x7fffffff), bits)
    key = jnp.where(causal[None], key, jnp.int32(-2 ** 31))
    ukey = lax.bitcast_convert_type(key ^ jnp.int32(-2 ** 31), jnp.uint32)
    cand = jnp.zeros((bsz, seq, 1), jnp.uint32)
    for bit in range(31, -1, -1):
        trial = cand | jnp.uint32(1 << bit)
        cnt = jnp.sum((ukey >= trial).astype(jnp.int32), axis=-1, keepdims=True)
        cand = jnp.where(cnt >= k_sel, trial, cand)
    gt = ukey > cand
    eq = ukey == cand
    need = k_sel - jnp.sum(gt.astype(jnp.int32), axis=-1, keepdims=True)
    J = jnp.zeros((bsz, seq, 1), jnp.int32)
    for bit in range(int(np.ceil(np.log2(seq))), -1, -1):
        trial = J | (1 << bit)
        cnt = jnp.sum((eq & (pos[None, None, :] < trial)).astype(jnp.int32), axis=-1, keepdims=True)
        J = jnp.where(cnt <= need, trial, J)
    sel = (gt | (eq & (pos[None, None, :] < J))) & causal[None]
    group = C_HEADS // C_KV_HEADS
    qg = q.reshape(bsz, seq, C_KV_HEADS, group, C_HEAD_DIM)
    logits = jnp.einsum('bqcgd,bscd->bcgqs', qg.astype(BF16), k.astype(BF16),
                        preferred_element_type=F32) * (C_HEAD_DIM ** -0.5)
    logits = jnp.where(sel[:, None, None], logits, -jnp.inf)
    prob = jax.nn.softmax(logits, axis=-1)
    out = jnp.einsum('bcgqs,bscd->bqcgd', prob.astype(BF16), v.astype(BF16), preferred_element_type=F32)
    return out.reshape(bsz, seq, C_WIDTH)


def kernel(x, norm_mix_g, w_in, a_mu, a_w0, a_w_up, a_a0, a_a_up, a_g_up, a_k_k, a_k_a, a_r_k, a_gn_g,
           a_gn_b, b_lb_logits, b_gn_g, w_branch, w_o, norm_ffn_g, w_up, conv_w, conv_b, w_down,
           norm_final_g):
    bsz, seq, d = x.shape
    t = bsz * seq
    lb_cum = jnp.cumsum(jax.nn.softmax(b_lb_logits.astype(F32), axis=0), axis=0)
    lower_bounds = lb_cum - lb_cum[0]
    h = x
    for layer in range(DEPTH):
        w = w_in[layer].astype(BF16)
        w_a, w_b, w_c, w_g = _split(w, (A_COLS, B_COLS, C_COLS, GATE_COLS))
        w_c = jnp.pad(w_c, ((0, 0), (0, C_PAD - C_COLS)))
        h2 = h.reshape(t, d)
        g = norm_mix_g[layer]
        p_a = norm_matmul(h2, g, w_a, tm=2048, tn=A_COLS // 2, out_dtype=BF16).reshape(bsz, seq, -1)
        p_b = norm_matmul(h2, g, w_b, tm=2048, tn=1024, out_dtype=BF16).reshape(bsz, seq, -1)
        p_c = norm_matmul(h2, g, w_c, tm=2048, tn=C_PAD, out_dtype=F32).reshape(bsz, seq, -1)
        p_g = norm_matmul(h2, g, w_g, tm=2048, tn=1024, out_dtype=BF16)
        y_a = rwkv7(p_a, a_mu[layer], a_w0[layer], a_w_up[layer], a_a0[layer], a_a_up[layer],
                    a_g_up[layer], a_k_k[layer], a_k_a[layer], a_r_k[layer], a_gn_g[layer], a_gn_b[layer])
        y_b = hgrn2(p_b, lower_bounds[layer], b_gn_g[layer])
        y_c = dsa(p_c)
        wbr = w_branch[layer].astype(BF16)
        h2 = merge((y_a.reshape(t, -1), y_b.reshape(t, -1), y_c.reshape(t, -1)), p_g, h2,
                   (wbr[0], wbr[1], wbr[2]), w_o[layer].astype(BF16), tm=512)
        h = conv_glu(h2.reshape(bsz, seq, d), norm_ffn_g[layer], w_up[layer].astype(BF16), conv_w[layer],
                     conv_b[layer], w_down[layer].astype(BF16),
                     norm_final_g if layer == DEPTH - 1 else None, tm=512, tf=1408)
    return h
```

```python
import functools

import jax
import jax.numpy as jnp
import numpy as np
from jax import lax
from jax.experimental import pallas as pl
from jax.experimental.pallas import tpu as pltpu

F32 = jnp.float32
BF16 = jnp.bfloat16

D_MODEL = 1024
DEPTH = 2
MIX_WIDTH = 512
N_BRANCH = 3
A_HEAD_DIM = 64
A_HEADS = 8
A_WIDTH = 512
A_RANK_W = 64
A_RANK_A = 64
A_RANK_G = 128
A_GN_EPS = 64e-5
B_KEY_DIM = 128
B_VAL_DIM = 128
B_HEADS = 4
B_WIDTH = 512
C_HEAD_DIM = 64
C_HEADS = 8
C_KV_HEADS = 2
C_WIDTH = 512
IDX_HEADS = 4
IDX_DIM = 64
TOPK_MAX = 256
ROPE_THETA = 10000.0
D_FF = 2816
CONV_WIDTH = 3
NORM_EPS = 1e-6

A_COLS = 3 * A_WIDTH + A_RANK_W + A_RANK_A + A_RANK_G
B_COLS = 4 * B_WIDTH
C_COLS = C_WIDTH + 2 * C_KV_HEADS * C_HEAD_DIM + IDX_HEADS * IDX_DIM + IDX_DIM + IDX_HEADS
GATE_COLS = N_BRANCH * D_MODEL

LANE = 128
SUBLANE = 8
VMEM_LIMIT = 48 * 1024 * 1024


def _round_up(n, m):
    return (n + m - 1) // m * m


def _rms(x, g):
    return x * lax.rsqrt(jnp.mean(x * x, axis=-1, keepdims=True) + NORM_EPS) * g


def _norm_matmul_kernel(x_ref, g_ref, w_ref, o_ref, xn_ref):
    @pl.when(pl.program_id(1) == 0)
    def _():
        xn_ref[...] = _rms(x_ref[...], g_ref[...]).astype(BF16)

    o_ref[...] = jnp.dot(xn_ref[...], w_ref[...], preferred_element_type=F32).astype(o_ref.dtype)


def norm_matmul(x, g, w, *, tm, tn, out_dtype):
    t, d = x.shape
    n = w.shape[1]
    assert t % tm == 0 and n % tn == 0
    return pl.pallas_call(
        _norm_matmul_kernel,
        grid=(t // tm, n // tn),
        in_specs=[
            pl.BlockSpec((tm, d), lambda i, j: (i, 0)),
            pl.BlockSpec((1, d), lambda i, j: (0, 0)),
            pl.BlockSpec((d, tn), lambda i, j: (0, j)),
        ],
        out_specs=pl.BlockSpec((tm, tn), lambda i, j: (i, j)),
        out_shape=jax.ShapeDtypeStruct((t, n), out_dtype),
        scratch_shapes=[pltpu.VMEM((tm, d), BF16)],
        compiler_params=pltpu.CompilerParams(
            dimension_semantics=("arbitrary", "arbitrary"), vmem_limit_bytes=VMEM_LIMIT),
        name="norm_matmul",
    )(x, g.reshape(1, d), w)


def _merge_kernel(ya_ref, yb_ref, yc_ref, gate_ref, h_ref, wa_ref, wb_ref, wc_ref, wo_ref, o_ref):
    merged = None
    for b, (y_ref, w_ref) in enumerate(((ya_ref, wa_ref), (yb_ref, wb_ref), (yc_ref, wc_ref))):
        br = jnp.dot(y_ref[...].astype(BF16), w_ref[...], preferred_element_type=F32)
        gt = jax.nn.sigmoid(gate_ref[:, b * D_MODEL:(b + 1) * D_MODEL].astype(F32))
        merged = gt * br if merged is None else merged + gt * br
    o_ref[...] = h_ref[...] + jnp.dot(merged.astype(BF16), wo_ref[...], preferred_element_type=F32)


def merge(ys, gate, h, ws, w_o, *, tm):
    t = h.shape[0]
    assert t % tm == 0
    row = lambda w: pl.BlockSpec((tm, w), lambda i: (i, 0))
    const = lambda a: pl.BlockSpec(a.shape, lambda i: (0, 0))
    return pl.pallas_call(
        _merge_kernel,
        grid=(t // tm,),
        in_specs=[row(y.shape[1]) for y in ys] + [row(GATE_COLS), row(D_MODEL)]
                 + [const(w) for w in ws] + [const(w_o)],
        out_specs=row(D_MODEL),
        out_shape=jax.ShapeDtypeStruct((t, D_MODEL), F32),
        compiler_params=pltpu.CompilerParams(
            dimension_semantics=("arbitrary",), vmem_limit_bytes=VMEM_LIMIT),
        name="merge",
    )(*ys, gate, h, *ws, w_o)


def _shift_rows(cur, prev_tail, shift):
    rolled = pltpu.roll(cur, shift, 0)
    tail = pltpu.roll(prev_tail, shift, 0)
    row = lax.broadcasted_iota(jnp.int32, tail.shape, 0)
    top = jnp.where(row < shift, tail, rolled[:SUBLANE])
    return jnp.concatenate([top, rolled[SUBLANE:]], axis=0)


def _conv_glu_kernel(h_ref, g_ref, wug_ref, wuv_ref, cwg_ref, cwv_ref, cbg_ref, cbv_ref, wd_ref, *rest,
                     final_norm):
    if final_norm:
        gf_ref, o_ref, xn_ref, acc_ref, carry_ref = rest
    else:
        o_ref, xn_ref, acc_ref, carry_ref = rest
    ti = pl.program_id(1)
    fi = pl.program_id(2)
    nf = pl.num_programs(2)

    @pl.when(fi == 0)
    def _():
        xn_ref[...] = _rms(h_ref[0], g_ref[...]).astype(BF16)
        acc_ref[...] = jnp.zeros_like(acc_ref)

    @pl.when(ti == 0)
    def _():
        carry_ref[fi] = jnp.zeros(carry_ref.shape[1:], F32)

    xn = xn_ref[...]

    def conv(w_ref, cw_ref, cb_ref, slot):
        up = jnp.dot(xn, w_ref[...], preferred_element_type=F32)
        tail = carry_ref[fi, slot]
        c = (cb_ref[...] + cw_ref[2:3, :] * up + cw_ref[1:2, :] * _shift_rows(up, tail, 1)
             + cw_ref[0:1, :] * _shift_rows(up, tail, 2))
        carry_ref[fi, slot] = up[up.shape[0] - SUBLANE:]
        return c

    cg = conv(wug_ref, cwg_ref, cbg_ref, 0)
    cv = conv(wuv_ref, cwv_ref, cbv_ref, 1)
    act = (cg * jax.nn.sigmoid(cg) * cv).astype(BF16)
    acc_ref[...] += jnp.dot(act, wd_ref[...], preferred_element_type=F32)

    @pl.when(fi == nf - 1)
    def _():
        out = h_ref[0] + acc_ref[...]
        if final_norm:
            out = _rms(out, gf_ref[...])
        o_ref[0] = out


def conv_glu(h, g, w_up, conv_w, conv_b, w_down, gf, *, tm, tf):
    bsz, seq, d = h.shape
    assert seq % tm == 0 and D_FF % tf == 0
    nf = D_FF // tf
    final_norm = gf is not None
    vec = lambda: pl.BlockSpec((1, d), lambda b, t, f: (0, 0))
    in_specs = [
        pl.BlockSpec((1, tm, d), lambda b, t, f: (b, t, 0)),
        vec(),
        pl.BlockSpec((d, tf), lambda b, t, f: (0, f)),
        pl.BlockSpec((d, tf), lambda b, t, f: (0, f + nf)),
        pl.BlockSpec((CONV_WIDTH, tf), lambda b, t, f: (0, f)),
        pl.BlockSpec((CONV_WIDTH, tf), lambda b, t, f: (0, f + nf)),
        pl.BlockSpec((1, tf), lambda b, t, f: (0, f)),
        pl.BlockSpec((1, tf), lambda b, t, f: (0, f + nf)),
        pl.BlockSpec((tf, d), lambda b, t, f: (f, 0)),
    ]
    args = [h, g.reshape(1, d), w_up, w_up, conv_w, conv_w, conv_b.reshape(1, -1), conv_b.reshape(1, -1), w_down]
    if final_norm:
        in_specs.append(vec())
        args.append(gf.reshape(1, d))
    return pl.pallas_call(
        functools.partial(_conv_glu_kernel, final_norm=final_norm),
        grid=(bsz, seq // tm, nf),
        in_specs=in_specs,
        out_specs=pl.BlockSpec((1, tm, d), lambda b, t, f: (b, t, 0)),
        out_shape=jax.ShapeDtypeStruct((bsz, seq, d), F32),
        scratch_shapes=[pltpu.VMEM((tm, d), BF16), pltpu.VMEM((tm, d), F32),
                        pltpu.VMEM((nf, 2, SUBLANE, tf), F32)],
        compiler_params=pltpu.CompilerParams(
            dimension_semantics=("arbitrary", "arbitrary", "arbitrary"), vmem_limit_bytes=VMEM_LIMIT),
        name="conv_glu",
    )(*args)


HP = LANE
A_PADW = A_HEADS * HP
A_CHUNK = 64
A_LOW = A_RANK_W + A_RANK_A + A_RANK_G
A_PCOLS = 3 * A_PADW + A_LOW


def _dot(x, y):
    return jnp.dot(x.astype(BF16), y.astype(BF16), preferred_element_type=F32)


def _dot_t(x, y):
    return lax.dot_general(x.astype(BF16), y.astype(BF16), (((1,), (1,)), ((), ())), preferred_element_type=F32)


A_NB = 2


def _rwkv_kernel(p_ref, mu_ref, w0_ref, a0_ref, wa_ref, gup_ref, kk_ref, ka_ref, rk_ref, gng_ref, gnb_ref,
                 o_ref, carry_ref, h_ref):
    C = A_CHUNK
    W = A_PADW
    NB = A_NB
    R = NB * C

    @pl.when(pl.program_id(1) == 0)
    def _():
        carry_ref[...] = jnp.zeros_like(carry_ref)
        h_ref[...] = jnp.zeros_like(h_ref)

    blocks = []
    for s in range(NB):
        ps = p_ref[s].astype(F32)
        prev = _shift_rows(ps, carry_ref[s], 1)
        carry_ref[s] = ps[C - SUBLANE:]
        blocks.append(ps + (prev - ps) * mu_ref[...])
    p = jnp.concatenate(blocks, axis=0)
    lane = lax.broadcasted_iota(jnp.int32, (R, LANE), 1)
    valid = lane < A_HEAD_DIM
    valid_c = lax.broadcasted_iota(jnp.int32, (C, LANE), 1) < A_HEAD_DIM

    def pad_heads(x):
        slabs = []
        for h in range(A_HEADS):
            t = x[:, (h // 2) * LANE:(h // 2 + 1) * LANE]
            if h % 2:
                t = pltpu.roll(t, A_HEAD_DIM, 1)
            slabs.append(jnp.where(valid, t, 0.0))
        return jnp.concatenate(slabs, axis=1)
    r, k, v = (pad_heads(p[:, i * A_WIDTH:(i + 1) * A_WIDTH]) for i in range(3))
    low = p[:, 3 * A_WIDTH:3 * A_WIDTH + LANE]
    gd = p[:, 3 * A_WIDTH + LANE:]
    low = jnp.where(lane < A_RANK_W, jnp.tanh(low), low)
    wa = _dot(low, wa_ref[...])
    w_log = -jax.nn.softplus(-(w0_ref[...] + wa[:, :W])) - 0.5
    logw = -jnp.exp(w_log)
    a = jax.nn.sigmoid(a0_ref[...] + wa[:, W:])
    g = _dot(jax.nn.sigmoid(gd), gup_ref[...])
    k_mod = k * (1.0 + (a - 1.0) * ka_ref[...])
    kk = k * kk_ref[...]

    row = lax.broadcasted_iota(jnp.int32, (C, C), 0)
    col = lax.broadcasted_iota(jnp.int32, (C, C), 1)
    incl = row >= col
    strict = row > col
    rowr = lax.broadcasted_iota(jnp.int32, (R, R), 0)
    colr = lax.broadcasted_iota(jnp.int32, (R, R), 1)
    same_seq = (rowr // C) == (colr // C)
    cum = jnp.dot((same_seq & (rowr >= colr)).astype(F32), logw, precision=lax.Precision.HIGHEST,
                  preferred_element_type=F32)
    cum_end = [cum[(s + 1) * C - 1:(s + 1) * C, :] for s in range(NB)]
    e_neg = jnp.exp(-cum)
    e_end = jnp.concatenate([jnp.exp(cum_end[s] - cum[s * C:(s + 1) * C]) for s in range(NB)], axis=0)
    r_t = r * jnp.exp(cum)
    a_scale = jnp.exp(cum - logw)
    p_end = [jnp.exp(cum_end[s]) for s in range(NB)]
    eye = (lax.broadcasted_iota(jnp.int32, (HP, HP), 0) == lax.broadcasted_iota(jnp.int32, (HP, HP), 1)).astype(F32)
    eye_c = (row == col).astype(F32)

    chains = [(s, h) for s in range(NB) for h in range(A_HEADS)]
    nch = range(len(chains))

    def blk(x, n):
        s, h = chains[n]
        return x[s * C:(s + 1) * C, h * HP:(h + 1) * HP]
    each = lambda f: [f(n) for n in nch]
    kk_n = each(lambda n: blk(kk, n) * lax.rsqrt(jnp.maximum(
        jnp.sum(blk(kk, n) * blk(kk, n), axis=-1, keepdims=True), 1e-24)))
    b_h = each(lambda n: kk_n[n] * blk(a, n))
    at = each(lambda n: -kk_n[n] * blk(a_scale, n))
    ar = each(lambda n: jnp.concatenate([at[n], blk(r_t, n)], axis=0))
    ab = each(lambda n: _dot_t(ar[n], b_h[n] * blk(e_neg, n)))
    ak = each(lambda n: _dot_t(ar[n], blk(k_mod, n) * blk(e_neg, n)))
    a_ab = each(lambda n: jnp.where(strict, ab[n][:C], 0.0))
    a_rb = each(lambda n: jnp.where(incl, ab[n][C:], 0.0))
    a_ak = each(lambda n: jnp.where(strict, ak[n][:C], 0.0))
    a_rk = each(lambda n: jnp.where(incl, ak[n][C:], 0.0))
    tm = each(lambda n: eye_c + a_ab[n])
    pw = each(lambda n: _dot(a_ab[n], a_ab[n]))
    akv = each(lambda n: _dot(a_ak[n], blk(v, n)))
    m = 2
    while 2 * m < C:
        x = each(lambda n: _dot(jnp.concatenate([tm[n], pw[n]], axis=0), pw[n]))
        tm = each(lambda n: tm[n] + x[n][:C])
        pw = each(lambda n: x[n][C:])
        m *= 2
    tm = each(lambda n: tm[n] + _dot(tm[n], pw[n]))
    x2 = each(lambda n: _dot(tm[n], jnp.concatenate([at[n], akv[n]], axis=1)))
    x3 = each(lambda n: _dot(a_rb[n], x2[n]))
    x4 = each(lambda n: _dot((b_h[n] * blk(e_end, n)).T, x2[n]))
    rkv = each(lambda n: _dot(a_rk[n], blk(v, n)))
    kv = each(lambda n: _dot((blk(k_mod, n) * blk(e_end, n)).T, blk(v, n)))
    rm = each(lambda n: jnp.concatenate([blk(r_t, n) + x3[n][:, :HP],
                                         eye * p_end[chains[n][0]][:, chains[n][1] * HP:(chains[n][1] + 1) * HP]
                                         + x4[n][:, :HP]], axis=0))
    x5 = each(lambda n: _dot(rm[n], h_ref[n]))
    for s in range(NB):
        outs = []
        for h in range(A_HEADS):
            n = s * A_HEADS + h
            sl = slice(h * HP, (h + 1) * HP)
            h_ref[n] = x5[n][C:] + x4[n][:, HP:] + kv[n]
            y = x5[n][:C] + x3[n][:, HP:] + rkv[n]
            mean = jnp.sum(y, axis=-1, keepdims=True) * (1.0 / A_HEAD_DIM)
            dlt = jnp.where(valid_c, y - mean, 0.0)
            var = jnp.sum(dlt * dlt, axis=-1, keepdims=True) * (1.0 / A_HEAD_DIM)
            yn = dlt * lax.rsqrt(var + A_GN_EPS) * gng_ref[:, sl] + gnb_ref[:, sl]
            bonus = jnp.sum(blk(r, n) * blk(k_mod, n) * rk_ref[:, sl], axis=-1, keepdims=True) * blk(v, n)
            outs.append((yn + bonus) * blk(g, n))
        o_ref[s] = jnp.concatenate(
            [outs[2 * i] + pltpu.roll(outs[2 * i + 1], A_HEAD_DIM, 1) for i in range(A_HEADS // 2)],
            axis=1).astype(o_ref.dtype)


def _rwkv_kernel_one(p_ref, mu_ref, w0_ref, a0_ref, wa_ref, gup_ref, kk_ref, ka_ref, rk_ref, gng_ref, gnb_ref,
                     o_ref, carry_ref, h_ref):
    C = A_CHUNK
    W = A_PADW

    @pl.when(pl.program_id(1) == 0)
    def _():
        carry_ref[...] = jnp.zeros_like(carry_ref)
        h_ref[...] = jnp.zeros_like(h_ref)

    p = p_ref[0].astype(F32)
    prev = _shift_rows(p, carry_ref[...], 1)
    carry_ref[...] = p[C - SUBLANE:]
    p = p + (prev - p) * mu_ref[...]
    lane = lax.broadcasted_iota(jnp.int32, (C, LANE), 1)
    valid = lane < A_HEAD_DIM

    def pad_heads(x):
        slabs = []
        for h in range(A_HEADS):
            s = x[:, (h // 2) * LANE:(h // 2 + 1) * LANE]
            if h % 2:
                s = pltpu.roll(s, A_HEAD_DIM, 1)
            slabs.append(jnp.where(valid, s, 0.0))
        return jnp.concatenate(slabs, axis=1)
    r, k, v = (pad_heads(p[:, i * A_WIDTH:(i + 1) * A_WIDTH]) for i in range(3))
    low = p[:, 3 * A_WIDTH:3 * A_WIDTH + LANE]
    gd = p[:, 3 * A_WIDTH + LANE:]
    low = jnp.where(lane < A_RANK_W, jnp.tanh(low), low)
    wa = _dot(low, wa_ref[...])
    w_log = -jax.nn.softplus(-(w0_ref[...] + wa[:, :W])) - 0.5
    logw = -jnp.exp(w_log)
    a = jax.nn.sigmoid(a0_ref[...] + wa[:, W:])
    g = _dot(jax.nn.sigmoid(gd), gup_ref[...])
    k_mod = k * (1.0 + (a - 1.0) * ka_ref[...])
    kk = k * kk_ref[...]

    row = lax.broadcasted_iota(jnp.int32, (C, C), 0)
    col = lax.broadcasted_iota(jnp.int32, (C, C), 1)
    incl = row >= col
    strict = row > col
    cum = jnp.dot(incl.astype(F32), logw, precision=lax.Precision.HIGHEST, preferred_element_type=F32)
    cum_end = cum[C - 1:C, :]
    e_neg = jnp.exp(-cum)
    e_end = jnp.exp(cum_end - cum)
    r_t = r * jnp.exp(cum)
    a_scale = jnp.exp(cum - logw)
    p_end = jnp.exp(cum_end)
    eye = (lax.broadcasted_iota(jnp.int32, (HP, HP), 0) == lax.broadcasted_iota(jnp.int32, (HP, HP), 1)).astype(F32)
    eye_c = (row == col).astype(F32)

    heads = range(A_HEADS)
    sls = [slice(h * HP, (h + 1) * HP) for h in heads]
    each = lambda f: [f(h) for h in heads]
    kk_n = each(lambda h: kk[:, sls[h]] * lax.rsqrt(jnp.maximum(
        jnp.sum(kk[:, sls[h]] * kk[:, sls[h]], axis=-1, keepdims=True), 1e-24)))
    b_h = each(lambda h: kk_n[h] * a[:, sls[h]])
    at = each(lambda h: -kk_n[h] * a_scale[:, sls[h]])
    ar = each(lambda h: jnp.concatenate([at[h], r_t[:, sls[h]]], axis=0))
    ab = each(lambda h: _dot_t(ar[h], b_h[h] * e_neg[:, sls[h]]))
    ak = each(lambda h: _dot_t(ar[h], k_mod[:, sls[h]] * e_neg[:, sls[h]]))
    a_ab = each(lambda h: jnp.where(strict, ab[h][:C], 0.0))
    a_rb = each(lambda h: jnp.where(incl, ab[h][C:], 0.0))
    a_ak = each(lambda h: jnp.where(strict, ak[h][:C], 0.0))
    a_rk = each(lambda h: jnp.where(incl, ak[h][C:], 0.0))
    tm = each(lambda h: eye_c + a_ab[h])
    pw = each(lambda h: _dot(a_ab[h], a_ab[h]))
    akv = each(lambda h: _dot(a_ak[h], v[:, sls[h]]))
    n = 2
    while 2 * n < C:
        x = each(lambda h: _dot(jnp.concatenate([tm[h], pw[h]], axis=0), pw[h]))
        tm = each(lambda h: tm[h] + x[h][:C])
        pw = each(lambda h: x[h][C:])
        n *= 2
    tm = each(lambda h: tm[h] + _dot(tm[h], pw[h]))
    x2 = each(lambda h: _dot(tm[h], jnp.concatenate([at[h], akv[h]], axis=1)))
    x3 = each(lambda h: _dot(a_rb[h], x2[h]))
    x4 = each(lambda h: _dot((b_h[h] * e_end[:, sls[h]]).T, x2[h]))
    rkv = each(lambda h: _dot(a_rk[h], v[:, sls[h]]))
    kv = each(lambda h: _dot((k_mod[:, sls[h]] * e_end[:, sls[h]]).T, v[:, sls[h]]))
    rm = each(lambda h: jnp.concatenate([r_t[:, sls[h]] + x3[h][:, :HP],
                                         eye * p_end[:, sls[h]] + x4[h][:, :HP]], axis=0))
    x5 = each(lambda h: _dot(rm[h], h_ref[h]))
    outs = []
    for h in heads:
        sl = sls[h]
        h_ref[h] = x5[h][C:] + x4[h][:, HP:] + kv[h]
        y = x5[h][:C] + x3[h][:, HP:] + rkv[h]
        mean = jnp.sum(y, axis=-1, keepdims=True) * (1.0 / A_HEAD_DIM)
        dlt = jnp.where(valid, y - mean, 0.0)
        var = jnp.sum(dlt * dlt, axis=-1, keepdims=True) * (1.0 / A_HEAD_DIM)
        yn = dlt * lax.rsqrt(var + A_GN_EPS) * gng_ref[:, sl] + gnb_ref[:, sl]
        bonus = jnp.sum(r[:, sl] * k_mod[:, sl] * rk_ref[:, sl], axis=-1, keepdims=True) * v[:, sl]
        outs.append((yn + bonus) * g[:, sl])
    o_ref[0] = jnp.concatenate(
        [outs[2 * m] + pltpu.roll(outs[2 * m + 1], A_HEAD_DIM, 1) for m in range(A_HEADS // 2)],
        axis=1).astype(o_ref.dtype)


def _pad_heads(t, heads):
    hd = t.shape[-1] // heads
    t = t.reshape(t.shape[:-1] + (heads, hd))
    t = jnp.pad(t, [(0, 0)] * (t.ndim - 1) + [(0, HP - hd)])
    return t.reshape(t.shape[:-2] + (heads * HP,))


def rwkv7(p, mu, w0, w_up, a0, a_up, g_up, k_k, k_a, r_k, gn_g, gn_b):
    bsz, seq, _ = p.shape
    C = A_CHUNK
    assert seq % C == 0
    z = jnp.zeros((A_RANK_W, A_PADW), F32)
    wa = jnp.concatenate([jnp.concatenate([_pad_heads(w_up, A_HEADS), z], axis=1),
                          jnp.concatenate([z, _pad_heads(a_up, A_HEADS)], axis=1)], axis=0).astype(BF16)
    vecs = [_pad_heads(t.reshape(-1), A_HEADS).reshape(1, -1) for t in (k_k, k_a, r_k, gn_g, gn_b)]
    full = lambda shape: pl.BlockSpec(shape, lambda b, c: (0,) * len(shape))
    assert bsz % A_NB == 0
    return pl.pallas_call(
        _rwkv_kernel,
        grid=(bsz // A_NB, seq // C),
        in_specs=[pl.BlockSpec((A_NB, C, A_COLS), lambda b, c: (b, c, 0)),
                  full((1, A_COLS)), full((1, A_PADW)), full((1, A_PADW)),
                  full((LANE, 2 * A_PADW)), full((A_RANK_G, A_PADW))] + [full((1, A_PADW))] * 5,
        out_specs=pl.BlockSpec((A_NB, C, A_WIDTH), lambda b, c: (b, c, 0)),
        out_shape=jax.ShapeDtypeStruct((bsz, seq, A_WIDTH), BF16),
        scratch_shapes=[pltpu.VMEM((A_NB, SUBLANE, A_COLS), F32), pltpu.VMEM((A_NB * A_HEADS, HP, HP), F32)],
        compiler_params=pltpu.CompilerParams(
            dimension_semantics=("arbitrary", "arbitrary"), vmem_limit_bytes=VMEM_LIMIT),
        name="rwkv7",
    )(p, mu.reshape(1, -1), _pad_heads(w0, A_HEADS).reshape(1, -1), _pad_heads(a0, A_HEADS).reshape(1, -1), wa,
      _pad_heads(g_up, A_HEADS).astype(BF16), *vecs)


B_CHUNK = 64
B_SUB = 16


def _dot_tl(x, y):
    return lax.dot_general(x.astype(BF16), y.astype(BF16), (((0,), (0,)), ((), ())), preferred_element_type=F32)


B_NB = 2


def _hgrn_kernel(p_ref, lb_ref, gn_ref, o_ref, st_ref):
    C, SB, W = B_CHUNK, B_SUB, B_WIDTH
    nsb = C // SB
    D = B_KEY_DIM
    NB = B_NB
    R = NB * C

    @pl.when(pl.program_id(1) == 0)
    def _():
        st_ref[...] = jnp.zeros_like(st_ref)

    p = jnp.concatenate([p_ref[s].astype(F32) for s in range(NB)], axis=0)
    q, f, val, gate = p[:, :W], p[:, W:2 * W], p[:, 2 * W:3 * W], p[:, 3 * W:]
    lb = lb_ref[...]
    log_f = jnp.logaddexp(jnp.log(lb), jnp.log1p(-lb) + jax.nn.log_sigmoid(f))
    k = (1.0 - lb) * jax.nn.sigmoid(-f)
    rowr = lax.broadcasted_iota(jnp.int32, (R, R), 0)
    colr = lax.broadcasted_iota(jnp.int32, (R, R), 1)
    b = jnp.dot((((rowr // C) == (colr // C)) & (rowr >= colr)).astype(F32), log_f,
                precision=lax.Precision.HIGHEST, preferred_element_type=F32)
    b0 = jnp.concatenate(
        [jnp.zeros((SB, W), F32) if i % nsb == 0 else jnp.broadcast_to(b[i * SB - 1:i * SB, :], (SB, W))
         for i in range(NB * nsb)], axis=0)
    qt = q * jnp.exp(b - b0)
    qe = q * jnp.exp(b)
    b_end = [b[(s + 1) * C - 1:(s + 1) * C, :] for s in range(NB)]
    ke = jnp.concatenate([k[s * C:(s + 1) * C] * jnp.exp(b_end[s] - b[s * C:(s + 1) * C]) for s in range(NB)],
                         axis=0)
    e_end = [jnp.exp(b_end[s]) for s in range(NB)]
    rowd = lax.broadcasted_iota(jnp.int32, (C, D), 0)
    lane_c = lax.broadcasted_iota(jnp.int32, (SB, C), 1)
    row_sb = lax.broadcasted_iota(jnp.int32, (SB, C), 0)

    chains = [(s, h) for s in range(NB) for h in range(B_HEADS)]
    each = lambda fn: [fn(n) for n in range(len(chains))]

    def blk(x, n):
        s, h = chains[n]
        return x[s * C:(s + 1) * C, h * D:(h + 1) * D]

    def off_operands(n):
        k_n, b_n, qt_n = blk(k, n), blk(b, n), blk(qt, n)
        kts, qts = [], []
        for i in range(1, nsb):
            b0_i = b_n[i * SB - 1:i * SB]
            kts.append(jnp.where(rowd < i * SB, k_n * jnp.exp(jnp.minimum(b0_i - b_n, 0.0)), 0.0))
            qts.append(jnp.where((rowd >= i * SB) & (rowd < (i + 1) * SB), qt_n, 0.0))
        return jnp.concatenate(kts, axis=1), jnp.concatenate(qts, axis=1)
    offs = each(off_operands)
    st_off = each(lambda n: _dot_t(offs[n][0], offs[n][1]))

    def diag_t(n):
        k_n, b_n, q_n = blk(k, n), blk(b, n), blk(q, n)
        blocks = []
        for i in range(nsb):
            rs = slice(i * SB, (i + 1) * SB)
            k_i, b_i = k_n[rs], b_n[rs]
            acc = jnp.zeros((SB, C), F32)
            for j in range(SB):
                t = i * SB + j
                z = k_i * q_n[t:t + 1] * jnp.exp(jnp.minimum(b_n[t:t + 1] - b_i, 0.0))
                acc = jnp.where(lane_c == t, jnp.sum(z, axis=-1, keepdims=True), acc)
            blocks.append(jnp.where(row_sb + i * SB <= lane_c, acc, 0.0))
        return jnp.concatenate(blocks, axis=0)
    st_diag = each(diag_t)
    o_intra = each(lambda n: _dot_tl(st_off[n] + st_diag[n], blk(val, n)))
    o_state = each(lambda n: _dot_t(blk(qe, n), st_ref[n]))
    upd = each(lambda n: _dot_tl(blk(val, n), blk(ke, n)))
    for s in range(NB):
        outs = []
        for h in range(B_HEADS):
            n = s * B_HEADS + h
            st_ref[n] = st_ref[n] * e_end[s][:, h * D:(h + 1) * D] + upd[n]
            o = o_intra[n] + o_state[n]
            o = o * lax.rsqrt(jnp.mean(o * o, axis=-1, keepdims=True) + NORM_EPS)
            gt = blk(gate, n)
            outs.append(o * gn_ref[:, h * D:(h + 1) * D] * (gt * jax.nn.sigmoid(gt)))
        o_ref[s] = jnp.concatenate(outs, axis=1).astype(o_ref.dtype)


def _hgrn_kernel_one(p_ref, lb_ref, gn_ref, o_ref, st_ref):
    C, SB, W = B_CHUNK, B_SUB, B_WIDTH
    nsb = C // SB
    D = B_KEY_DIM

    @pl.when(pl.program_id(1) == 0)
    def _():
        st_ref[...] = jnp.zeros_like(st_ref)

    p = p_ref[0].astype(F32)
    q, f, val, gate = p[:, :W], p[:, W:2 * W], p[:, 2 * W:3 * W], p[:, 3 * W:]
    lb = lb_ref[...]
    log_f = jnp.logaddexp(jnp.log(lb), jnp.log1p(-lb) + jax.nn.log_sigmoid(f))
    k = (1.0 - lb) * jax.nn.sigmoid(-f)
    row = lax.broadcasted_iota(jnp.int32, (C, C), 0)
    col = lax.broadcasted_iota(jnp.int32, (C, C), 1)
    b = jnp.dot((row >= col).astype(F32), log_f, precision=lax.Precision.HIGHEST,
                preferred_element_type=F32)
    b_end = b[C - 1:C, :]
    b0 = jnp.concatenate(
        [jnp.zeros((SB, W), F32)] + [jnp.broadcast_to(b[i * SB - 1:i * SB, :], (SB, W)) for i in range(1, nsb)],
        axis=0)
    qt = q * jnp.exp(b - b0)
    qe = q * jnp.exp(b)
    ke = k * jnp.exp(b_end - b)
    e_end = jnp.exp(b_end)
    rowd = lax.broadcasted_iota(jnp.int32, (C, D), 0)
    lane_c = lax.broadcasted_iota(jnp.int32, (SB, C), 1)
    row_sb = lax.broadcasted_iota(jnp.int32, (SB, C), 0)

    heads = range(B_HEADS)
    sls = [slice(h * D, (h + 1) * D) for h in heads]
    each = lambda fn: [fn(h) for h in heads]

    def off_operands(h):
        kts, qts = [], []
        for i in range(1, nsb):
            b0_i = b[i * SB - 1:i * SB, sls[h]]
            kts.append(jnp.where(rowd < i * SB, k[:, sls[h]] * jnp.exp(jnp.minimum(b0_i - b[:, sls[h]], 0.0)), 0.0))
            qts.append(jnp.where((rowd >= i * SB) & (rowd < (i + 1) * SB), qt[:, sls[h]], 0.0))
        return jnp.concatenate(kts, axis=1), jnp.concatenate(qts, axis=1)
    offs = each(off_operands)
    st_off = each(lambda h: _dot_t(offs[h][0], offs[h][1]))

    def diag_t(h):
        blocks = []
        for i in range(nsb):
            rs = slice(i * SB, (i + 1) * SB)
            k_i, b_i = k[rs, sls[h]], b[rs, sls[h]]
            acc = jnp.zeros((SB, C), F32)
            for j in range(SB):
                t = i * SB + j
                z = k_i * q[t:t + 1, sls[h]] * jnp.exp(jnp.minimum(b[t:t + 1, sls[h]] - b_i, 0.0))
                acc = jnp.where(lane_c == t, jnp.sum(z, axis=-1, keepdims=True), acc)
            blocks.append(jnp.where(row_sb + i * SB <= lane_c, acc, 0.0))
        return jnp.concatenate(blocks, axis=0)
    st_diag = each(diag_t)
    o_intra = each(lambda h: _dot_tl(st_off[h] + st_diag[h], val[:, sls[h]]))
    o_state = each(lambda h: _dot_t(qe[:, sls[h]], st_ref[h]))
    upd = each(lambda h: _dot_tl(val[:, sls[h]], ke[:, sls[h]]))
    outs = []
    for h in heads:
        st_ref[h] = st_ref[h] * e_end[:, sls[h]] + upd[h]
        o = o_intra[h] + o_state[h]
        o = o * lax.rsqrt(jnp.mean(o * o, axis=-1, keepdims=True) + NORM_EPS)
        gt = gate[:, sls[h]]
        outs.append(o * gn_ref[:, sls[h]] * (gt * jax.nn.sigmoid(gt)))
    o_ref[0] = jnp.concatenate(outs, axis=1).astype(o_ref.dtype)


def hgrn2(p, lb, gn_g):
    bsz, seq, _ = p.shape
    C = B_CHUNK
    assert seq % C == 0 and bsz % B_NB == 0
    vec = pl.BlockSpec((1, B_WIDTH), lambda b, c: (0, 0))
    return pl.pallas_call(
        _hgrn_kernel,
        grid=(bsz // B_NB, seq // C),
        in_specs=[pl.BlockSpec((B_NB, C, B_COLS), lambda b, c: (b, c, 0)), vec, vec],
        out_specs=pl.BlockSpec((B_NB, C, B_WIDTH), lambda b, c: (b, c, 0)),
        out_shape=jax.ShapeDtypeStruct((bsz, seq, B_WIDTH), BF16),
        scratch_shapes=[pltpu.VMEM((B_NB * B_HEADS, B_VAL_DIM, B_KEY_DIM), F32)],
        compiler_params=pltpu.CompilerParams(
            dimension_semantics=("arbitrary", "arbitrary"), vmem_limit_bytes=VMEM_LIMIT),
        name="hgrn2",
    )(p, lb.reshape(1, -1), gn_g.reshape(1, -1))


C_PAD = _round_up(C_COLS, LANE)
QB = 128
C_KV_W = C_KV_HEADS * C_HEAD_DIM
C_K_OFF = C_WIDTH
C_QI_OFF = C_WIDTH + 2 * C_KV_W
C_KI_OFF = C_QI_OFF + IDX_HEADS * IDX_DIM
INT_MIN = -2 ** 31
NEG = -1e30
HALF = C_HEAD_DIM // 2
HALF_RANGE = 1 << 15
VT_ROWS = C_HEAD_DIM + 16


def _rope(t, cos, sin_signed):
    outs = []
    lane = lax.broadcasted_iota(jnp.int32, (t.shape[0], LANE), 1)
    first_half = (lane % C_HEAD_DIM) < HALF
    for c in range(t.shape[1] // LANE):
        x = t[:, c * LANE:(c + 1) * LANE]
        swapped = jnp.where(first_half, pltpu.roll(x, LANE - HALF, 1), pltpu.roll(x, HALF, 1))
        outs.append(x * cos + swapped * sin_signed)
    return outs


GROUP = C_HEADS // C_KV_HEADS


def _split_hi_lo(x):
    hi = x.astype(BF16)
    return hi, (x - hi.astype(F32)).astype(BF16)


def _dsa_kernel(pq_ref, kv_ref, ki_ref, cosq_ref, sinq_ref, cosf_ref, sinf_ref, o_ref,
                k_s, vt_s, ki_s, key_s, khi_s, klo_s, *, k_sel, seq):
    i = pl.program_id(1)
    nk = i + 1
    lane = lax.broadcasted_iota(jnp.int32, (QB, LANE), 1)
    rowi = lax.broadcasted_iota(jnp.int32, (QB, LANE), 0)
    low_half = lane < C_HEAD_DIM

    @pl.when(i == 0)
    def _():
        kv = kv_ref[0].astype(F32)
        k_s[...] = _rope(kv[:, :LANE], cosf_ref[...], sinf_ref[...])[0].astype(BF16)
        vt = kv[:, LANE:].T
        extra = (lax.broadcasted_iota(jnp.int32, (VT_ROWS - C_HEAD_DIM, seq), 0) == 0).astype(F32)
        for c in range(C_KV_HEADS):
            vt_s[c] = jnp.concatenate([vt[c * C_HEAD_DIM:(c + 1) * C_HEAD_DIM], extra], axis=0).astype(BF16)
        lane_f = lax.broadcasted_iota(jnp.int32, (seq, LANE), 1)
        ki = jnp.where(lane_f < IDX_DIM, ki_ref[0].astype(F32), 0.0)
        ki = _rope(ki, cosf_ref[...], sinf_ref[...])[0]
        ki_s[...] = (ki + pltpu.roll(ki, IDX_DIM, 1)).astype(BF16)

    pq = pq_ref[0].astype(F32)
    cos, sin = cosq_ref[...], sinq_ref[...]
    q = _rope(pq[:, :C_WIDTH], cos, sin)
    qi = _rope(pq[:, C_QI_OFF:C_KI_OFF], cos, sin)
    qi3 = jnp.concatenate(
        [jnp.where(low_half if h % 2 == 0 else ~low_half, qi[h // 2], 0.0) for h in range(IDX_HEADS)],
        axis=0).astype(BF16)
    wt = pq[:, C_KI_OFF:C_KI_OFF + LANE].T
    wscale = (IDX_HEADS ** -0.5) * (IDX_DIM ** -0.5)
    w_rows = [wt[IDX_DIM + h:IDX_DIM + h + 1, :] * wscale for h in range(IDX_HEADS)]
    causal_off = i * QB + lane

    npair = (nk + 1) // 2

    def score_tile(jp, _):
        ki_t = ki_s[pl.ds(pl.multiple_of(jp * (2 * QB), 2 * QB), 2 * QB), :]
        d = lax.dot_general(ki_t, qi3, (((1,), (1,)), ((), ())), preferred_element_type=F32)
        for u in range(2):
            j = 2 * jp + u
            sc = jnp.zeros((QB, QB), F32)
            for h in range(IDX_HEADS):
                sc = sc + jnp.maximum(d[u * QB:(u + 1) * QB, h * QB:(h + 1) * QB], 0.0) * w_rows[h]
            bits = pltpu.bitcast(sc + 0.0, jnp.int32)
            key = jnp.where(bits < 0, bits ^ jnp.int32(0x7fffffff), bits)
            key = jnp.where(j * QB + rowi <= causal_off, key, jnp.int32(INT_MIN))
            key_s[j] = key
            khi_s[j] = jnp.right_shift(key, 16).astype(jnp.int16)
            klo_s[j] = ((key & 0xffff) - HALF_RANGE).astype(jnp.int16)
        return 0
    lax.fori_loop(0, npair, score_tile, 0)

    def count16(ref, pred):
        def body(jp, acc):
            for u in range(2):
                acc = acc + jnp.where(pred(ref[2 * jp + u]), jnp.int16(1), jnp.int16(0))
            return acc
        acc = lax.fori_loop(0, npair, body, jnp.zeros((QB, QB), jnp.int16))
        return jnp.sum(acc.astype(jnp.int32), axis=0, keepdims=True)

    def half_search(ref, base):
        def bit_step(b, cand):
            trial = cand | jnp.left_shift(jnp.int32(1), 15 - b)
            t16 = (trial - HALF_RANGE).astype(jnp.int16)
            cnt = base + count16(ref, lambda x: x >= t16)
            return jnp.where(cnt >= k_sel, trial, cand)
        return lax.fori_loop(0, 16, bit_step, jnp.zeros((1, QB), jnp.int32))

    cand_hi = half_search(khi_s, 0)
    hi16 = (cand_hi - HALF_RANGE).astype(jnp.int16)
    above = count16(khi_s, lambda x: x > hi16)

    def mask_low(jp, _):
        for u in range(2):
            j = 2 * jp + u
            klo_s[j] = jnp.where(khi_s[j] == hi16, klo_s[j], jnp.int16(-HALF_RANGE))
        return 0
    lax.fori_loop(0, npair, mask_low, 0)
    cand_lo = half_search(klo_s, above)
    thr = (jnp.left_shift(cand_hi, 16) | cand_lo) ^ jnp.int32(INT_MIN)

    def count(pred):
        def body(jp, acc):
            for u in range(2):
                acc = acc + jnp.where(pred(key_s[2 * jp + u]), 1, 0)
            return acc
        return jnp.sum(lax.fori_loop(0, npair, body, jnp.zeros((QB, QB), jnp.int32)), axis=0, keepdims=True)
    need = k_sel - count(lambda key: key > thr)

    need_f = need.astype(F32)
    tri = (lax.broadcasted_iota(jnp.int32, (2 * QB, 2 * QB), 0)
           >= lax.broadcasted_iota(jnp.int32, (2 * QB, 2 * QB), 1)).astype(BF16)

    scale = C_HEAD_DIM ** -0.5
    kvs = range(C_KV_HEADS)
    in_cs = [(lane >= c * C_HEAD_DIM) & (lane < (c + 1) * C_HEAD_DIM) for c in kvs]

    chains = range(C_HEADS // 2)
    kv_of = [n * 2 // GROUP for n in chains]

    def q_rows(n):
        c = kv_of[n]
        rows = []
        for h in (2 * n, 2 * n + 1):
            x = q[n]
            if h % 2 != c:
                x = pltpu.roll(x, C_HEAD_DIM, 1)
            rows.append(jnp.where(in_cs[c], x * scale, 0.0))
        return jnp.concatenate(rows, axis=0).astype(BF16)
    qcs = [q_rows(n) for n in chains]

    def att_tile(jp, carry):
        off = pl.multiple_of(jp * (2 * QB), 2 * QB)
        keys = jnp.concatenate([key_s[2 * jp], key_s[2 * jp + 1]], axis=0)
        pos = off + lax.broadcasted_iota(jnp.int32, (2 * QB, LANE), 0)
        tie = keys == thr
        rank = carry[-1] + jnp.dot(tri, jnp.where(tie, 1.0, 0.0).astype(BF16), preferred_element_type=F32)
        sel = ((keys > thr) | (tie & (rank <= need_f))) & (pos <= causal_off[:1, :])
        bias1 = jnp.where(sel, 0.0, NEG)
        bias = jnp.concatenate([bias1, bias1], axis=1)
        k_t = k_s[pl.ds(off, 2 * QB), :]
        s = [lax.dot_general(k_t, qcs[n], (((1,), (1,)), ((), ())), preferred_element_type=F32) for n in chains]
        s = [s[n] + bias for n in chains]
        m_new = [jnp.maximum(carry[2 * n], jnp.max(s[n], axis=0, keepdims=True)) for n in chains]
        p = [jnp.exp(s[n] - m_new[n]).astype(BF16) for n in chains]
        v_t = [vt_s[c, :, pl.ds(off, 2 * QB)] for c in kvs]
        pv = [jnp.dot(v_t[kv_of[n]], p[n], preferred_element_type=F32) for n in chains]
        out = []
        for n in chains:
            out += [m_new[n], carry[2 * n + 1] * jnp.exp(carry[2 * n] - m_new[n]) + pv[n]]
        return tuple(out) + (rank[2 * QB - 1:, :],)
    init = ((jnp.full((1, 2 * QB), NEG, F32), jnp.zeros((VT_ROWS, 2 * QB), F32)) * len(chains)
            + (jnp.zeros((1, QB), F32),))
    res = lax.fori_loop(0, npair, att_tile, init)

    tiles = []
    for n in chains:
        acc = res[2 * n + 1]
        out_t = acc[:C_HEAD_DIM] / acc[C_HEAD_DIM:C_HEAD_DIM + 1]
        tiles.append(jnp.concatenate([out_t[:, :QB], out_t[:, QB:]], axis=0).T)
    o_ref[0] = jnp.concatenate(tiles, axis=1).astype(o_ref.dtype)


def _dsa_kernel_rows(pq_ref, kv_ref, ki_ref, cosq_ref, sinq_ref, cosf_ref, sinf_ref, o_ref,
                     k_s, v_s, ki_s, key_s, *, k_sel, seq):
    i = pl.program_id(1)
    nk = i + 1
    lane = lax.broadcasted_iota(jnp.int32, (QB, LANE), 1)
    rowi = lax.broadcasted_iota(jnp.int32, (QB, LANE), 0)
    low_half = lane < C_HEAD_DIM

    @pl.when(i == 0)
    def _():
        kv = kv_ref[0]
        k_s[...] = _rope(kv[:, :LANE], cosf_ref[...], sinf_ref[...])[0].astype(BF16)
        v_s[...] = kv[:, LANE:].astype(BF16)
        lane_f = lax.broadcasted_iota(jnp.int32, (seq, LANE), 1)
        ki = jnp.where(lane_f < IDX_DIM, ki_ref[0], 0.0)
        ki = _rope(ki, cosf_ref[...], sinf_ref[...])[0]
        ki_s[...] = ki + pltpu.roll(ki, IDX_DIM, 1)

    pq = pq_ref[0]
    cos, sin = cosq_ref[...], sinq_ref[...]
    q = _rope(pq[:, :C_WIDTH], cos, sin)
    qi = _rope(pq[:, C_QI_OFF:C_KI_OFF], cos, sin)
    wslab = pq[:, C_KI_OFF:C_KI_OFF + LANE]
    wscale = (IDX_HEADS ** -0.5) * (IDX_DIM ** -0.5)
    wi = [jnp.sum(jnp.where(lane == IDX_DIM + h, wslab, 0.0), axis=-1, keepdims=True) * wscale
          for h in range(IDX_HEADS)]
    qi_h = [jnp.where(low_half if h % 2 == 0 else ~low_half, qi[h // 2], 0.0) for h in range(IDX_HEADS)]

    def score_tile(j, _):
        ki_t = ki_s[pl.ds(pl.multiple_of(j * QB, QB), QB), :]
        sc = jnp.zeros((QB, QB), F32)
        for h in range(IDX_HEADS):
            d = lax.dot_general(qi_h[h], ki_t, (((1,), (1,)), ((), ())), precision=lax.Precision.HIGHEST,
                                preferred_element_type=F32)
            sc = sc + jnp.maximum(d, 0.0) * wi[h]
        bits = pltpu.bitcast(sc + 0.0, jnp.int32)
        key = jnp.where(bits < 0, bits ^ jnp.int32(0x7fffffff), bits)
        key_s[j] = jnp.where(j * QB + lane <= i * QB + rowi, key, jnp.int32(INT_MIN))
        return 0
    lax.fori_loop(0, nk, score_tile, 0)

    def count(pred):
        def body(j, acc):
            return acc + jnp.where(pred(key_s[j], j), 1, 0)
        return jnp.sum(lax.fori_loop(0, nk, body, jnp.zeros((QB, QB), jnp.int32)), axis=-1, keepdims=True)

    def bit_step(b, cand):
        trial = cand | jnp.left_shift(jnp.int32(1), 31 - b)
        cnt = count(lambda key, j: key >= (trial ^ jnp.int32(INT_MIN)))
        return jnp.where(cnt >= k_sel, trial, cand)
    cand = lax.fori_loop(0, 32, bit_step, jnp.zeros((QB, 1), jnp.int32))
    thr = cand ^ jnp.int32(INT_MIN)
    need = k_sel - count(lambda key, j: key > thr)

    nbits = int(np.ceil(np.log2(seq))) + 1

    def idx_step(b, jj):
        trial = jj | jnp.left_shift(jnp.int32(1), nbits - 1 - b)
        cnt = count(lambda key, j: (key == thr) & (j * QB + lane < trial))
        return jnp.where(cnt <= need, trial, jj)
    jlim = lax.fori_loop(0, nbits, idx_step, jnp.zeros((QB, 1), jnp.int32))

    group = C_HEADS // C_KV_HEADS
    scale = C_HEAD_DIM ** -0.5
    outs = []
    for c in range(C_KV_HEADS):
        in_c = (lane >= c * C_HEAD_DIM) & (lane < (c + 1) * C_HEAD_DIM)
        rows = []
        for g in range(group):
            h = c * group + g
            x = q[h // 2]
            if h % 2 != c:
                x = pltpu.roll(x, C_HEAD_DIM, 1)
            rows.append(jnp.where(in_c, x * scale, 0.0))
        qc = jnp.concatenate(rows, axis=0).astype(BF16)
        ones_lane = (1 - c) * C_HEAD_DIM

        def att_tile(j, carry):
            m, acc = carry
            off = pl.multiple_of(j * QB, QB)
            key = key_s[j]
            sel = ((key > thr) | ((key == thr) & (j * QB + lane < jlim))) & (j * QB + lane <= i * QB + rowi)
            bias = jnp.where(sel, 0.0, NEG)
            s = lax.dot_general(qc, k_s[pl.ds(off, QB), :], (((1,), (1,)), ((), ())),
                                preferred_element_type=F32)
            s = jnp.where(jnp.concatenate([bias] * group, axis=0) < 0.0, NEG, s)
            m_new = jnp.maximum(m, jnp.max(s, axis=-1, keepdims=True))
            alpha = jnp.exp(m - m_new)
            p = jnp.exp(s - m_new)
            v_t = v_s[pl.ds(off, QB), :]
            v_t = jnp.where(in_c, v_t, jnp.where(lane == ones_lane, 1.0, 0.0).astype(BF16))
            acc = acc * alpha + jnp.dot(p.astype(BF16), v_t, preferred_element_type=F32)
            return m_new, acc
        m0 = jnp.full((group * QB, 1), NEG, F32)
        acc0 = jnp.zeros((group * QB, LANE), F32)
        _, acc = lax.fori_loop(0, nk, att_tile, (m0, acc0))
        l = jnp.sum(jnp.where(jnp.concatenate([lane] * group, axis=0) == ones_lane, acc, 0.0),
                    axis=-1, keepdims=True)
        out = acc / l
        for g in range(group):
            h = c * group + g
            x = jnp.where(in_c, out[g * QB:(g + 1) * QB], 0.0)
            if h % 2 != c:
                x = pltpu.roll(x, C_HEAD_DIM, 1)
            outs.append(x)
    o_ref[0] = jnp.concatenate([outs[2 * m] + outs[2 * m + 1] for m in range(C_HEADS // 2)],
                               axis=1).astype(o_ref.dtype)


def _rope_lanes(seq):
    inv_freq = ROPE_THETA ** (-jnp.arange(HALF, dtype=F32) * (2.0 / C_HEAD_DIM))
    ang = jnp.arange(seq, dtype=F32)[:, None] * inv_freq[None, :]
    cos, sin = jnp.cos(ang), jnp.sin(ang)
    reps = LANE // C_HEAD_DIM
    return (jnp.tile(jnp.concatenate([cos, cos], axis=1), (1, reps)),
            jnp.tile(jnp.concatenate([-sin, sin], axis=1), (1, reps)))


def dsa(p):
    bsz, seq, width = p.shape
    assert seq % (2 * QB) == 0 and width == C_PAD
    k_sel = min(TOPK_MAX, seq // 4)
    cos, sin = _rope_lanes(seq)
    return pl.pallas_call(
        functools.partial(_dsa_kernel, k_sel=k_sel, seq=seq),
        grid=(bsz, seq // QB),
        in_specs=[pl.BlockSpec((1, QB, C_PAD), lambda b, i: (b, i, 0)),
                  pl.BlockSpec((1, seq, 2 * C_KV_W), lambda b, i: (b, 0, C_K_OFF // (2 * C_KV_W))),
                  pl.BlockSpec((1, seq, LANE), lambda b, i: (b, 0, C_KI_OFF // LANE)),
                  pl.BlockSpec((QB, LANE), lambda b, i: (i, 0)),
                  pl.BlockSpec((QB, LANE), lambda b, i: (i, 0)),
                  pl.BlockSpec((seq, LANE), lambda b, i: (0, 0)),
                  pl.BlockSpec((seq, LANE), lambda b, i: (0, 0))],
        out_specs=pl.BlockSpec((1, QB, C_WIDTH), lambda b, i: (b, i, 0)),
        out_shape=jax.ShapeDtypeStruct((bsz, seq, C_WIDTH), BF16),
        scratch_shapes=[pltpu.VMEM((seq, LANE), BF16), pltpu.VMEM((C_KV_HEADS, VT_ROWS, seq), BF16),
                        pltpu.VMEM((seq, LANE), BF16), pltpu.VMEM((seq // QB, QB, QB), jnp.int32),
                        pltpu.VMEM((seq // QB, QB, QB), jnp.int16), pltpu.VMEM((seq // QB, QB, QB), jnp.int16)],
        compiler_params=pltpu.CompilerParams(
            dimension_semantics=("arbitrary", "arbitrary"), vmem_limit_bytes=VMEM_LIMIT),
        name="dsa",
    )(p, p, p, cos, sin, cos, sin)


def _split(t, sizes):
    out, start = [], 0
    for s in sizes:
        out.append(t[..., start:start + s])
        start += s
    return out


def _mm(x, y):
    return jnp.einsum('...ij,...jk->...ik', x.astype(BF16), y.astype(BF16), preferred_element_type=F32)


def _mm_hi(x, y):
    return jnp.einsum('...ij,...jk->...ik', x, y, precision=lax.Precision.HIGHEST)


def _rwkv7_chunked(r, logw, k, v, a, b, C=64):
    B, S, H, N = r.shape
    nc = S // C

    def ch(t):
        return t.reshape(B, nc, C, H, N).transpose(1, 0, 3, 2, 4)
    r, logw, k, v, a, b = map(ch, (r, logw, k, v, a, b))
    cum = jnp.cumsum(logw, axis=-2)
    cum_prev = cum - logw
    cum_end = cum[..., -1:, :]
    At = a * jnp.exp(cum_prev)
    Rt = r * jnp.exp(cum)
    Bt = b * jnp.exp(-cum)
    Kt = k * jnp.exp(-cum)
    Bh = b * jnp.exp(cum_end - cum)
    Kh = k * jnp.exp(cum_end - cum)
    Pc = jnp.exp(cum_end)[..., 0, :]
    T_ = lambda x: jnp.swapaxes(x, -1, -2)
    ti = jnp.arange(C)
    strict = ti[:, None] > ti[None, :]
    incl = ti[:, None] >= ti[None, :]
    Aab = jnp.where(strict, _mm(At, T_(Bt)), 0.0)
    Aak = jnp.where(strict, _mm(At, T_(Kt)), 0.0)
    Arb = jnp.where(incl, _mm(Rt, T_(Bt)), 0.0)
    Ark = jnp.where(incl, _mm(Rt, T_(Kt)), 0.0)
    Tm = jnp.eye(C, dtype=F32) + Aab
    Pw = Aab
    n = 1
    while 2 * n < C:
        Pw = _mm_hi(Pw, Pw)
        Tm = Tm + _mm_hi(Tm, Pw)
        n *= 2
    Abar = _mm(Tm, At)
    U0 = _mm(Tm, _mm(Aak, v))
    Rbar = Rt + _mm(Arb, Abar)
    Y0 = _mm(Arb, U0) + _mm(Ark, v)
    M = Pc[..., :, None] * jnp.eye(N, dtype=F32) + _mm(T_(Bh), Abar)
    Hd = _mm(T_(Bh), U0) + _mm(T_(Kh), v)

    def step(Hs, inp):
        Rbar_c, Y0_c, M_c, Hd_c = inp
        y = _mm(Rbar_c, Hs) + Y0_c
        Hs = _mm_hi(M_c, Hs) + Hd_c
        return Hs, y
    _, ys = lax.scan(step, jnp.zeros((B, H, N, N), F32), (Rbar, Y0, M, Hd))
    return ys.transpose(1, 0, 3, 2, 4).reshape(B, S, H, N)


def _token_shift(p):
    return jnp.pad(p, ((0, 0), (1, 0), (0, 0)))[:, :-1]


def _rwkv7_branch(p, mu, w0, w_up, a0, a_up, g_up, k_k, k_a, r_k, gn_g, gn_b):
    bsz, seq, _ = p.shape
    p = p + (_token_shift(p) - p) * mu
    r, k, v, wd, ad, gd = _split(p, (A_WIDTH, A_WIDTH, A_WIDTH, A_RANK_W, A_RANK_A, A_RANK_G))
    w_log = -jax.nn.softplus(-(w0 + _mm(jnp.tanh(wd), w_up))) - 0.5
    logw = -jnp.exp(w_log)
    a = jax.nn.sigmoid(a0 + _mm(ad, a_up))
    g = _mm(jax.nn.sigmoid(gd), g_up)

    def hd(t):
        return t.reshape(bsz, seq, A_HEADS, A_HEAD_DIM)
    kk = hd(k * k_k)
    kk = kk * lax.rsqrt(jnp.maximum(jnp.sum(kk * kk, axis=-1, keepdims=True), 1e-24))
    k_mod = hd(k * (1.0 + (a - 1.0) * k_a))
    r_h, v_h, a_h = hd(r), hd(v), hd(a)
    y = _rwkv7_chunked(r_h, hd(logw), k_mod, v_h, -kk, kk * a_h)
    mean = jnp.mean(y, axis=-1, keepdims=True)
    var = jnp.mean(jnp.square(y - mean), axis=-1, keepdims=True)
    y = ((y - mean) * lax.rsqrt(var + A_GN_EPS)).reshape(bsz, seq, A_WIDTH)
    y = y * gn_g + gn_b
    bonus = jnp.sum(r_h * k_mod * r_k, axis=-1, keepdims=True) * v_h
    y = y + bonus.reshape(bsz, seq, A_WIDTH)
    return y * g


def _hgrn2_sub(q, k, g, v, C=64, SB=16):
    B, S, H, dk = q.shape
    dv = v.shape[-1]
    nc = S // C

    def ch(t):
        return t.reshape(B, nc, C, H, t.shape[-1]).transpose(1, 0, 3, 2, 4)
    q, k, g, v = map(ch, (q, k, g, v))
    b = jnp.cumsum(g, axis=-2)
    nsb = C // SB
    T_ = lambda x: jnp.swapaxes(x, -1, -2)

    def step(state, inp):
        qc, kc, bc, vc = inp
        rows = []
        for I in range(nsb):
            sl = slice(I * SB, (I + 1) * SB)
            qI, bI = qc[..., sl, :], bc[..., sl, :]
            b0 = bc[..., I * SB - 1:I * SB, :] if I > 0 else jnp.zeros_like(bc[..., :1, :])
            qt = qI * jnp.exp(bI - b0)
            blocks = []
            if I > 0:
                kt = kc[..., :I * SB, :] * jnp.exp(b0 - bc[..., :I * SB, :])
                blocks.append(_mm(qt, T_(kt)))
            diff = bI[..., :, None, :] - bI[..., None, :, :]
            tri = jnp.tril(jnp.ones((SB, SB), bool))
            dec = jnp.exp(jnp.where(tri[:, :, None], diff, -jnp.inf))
            diag = jnp.sum(qI[..., :, None, :] * kc[..., sl, :][..., None, :, :] * dec, axis=-1)
            blocks.append(diag)
            if I < nsb - 1:
                blocks.append(jnp.zeros(diag.shape[:-1] + (C - (I + 1) * SB,), F32))
            rows.append(jnp.concatenate(blocks, axis=-1))
        scores = jnp.concatenate(rows, axis=-2)
        o = _mm(scores, vc) + _mm(qc * jnp.exp(bc), state)
        b_end = bc[..., -1:, :]
        state = state * T_(jnp.exp(b_end)) + _mm(T_(kc * jnp.exp(b_end - bc)), vc)
        return state, o
    _, o = lax.scan(step, jnp.zeros((B, H, dk, dv), F32), (q, k, b, v))
    return o.transpose(1, 0, 3, 2, 4).reshape(B, S, H, dv)


def _hgrn2_branch(p, lb, gn_g):
    bsz, seq, _ = p.shape
    q, f, i, g = _split(p, (B_WIDTH,) * 4)
    log_f = jnp.logaddexp(jnp.log(lb), jnp.log1p(-lb) + jax.nn.log_sigmoid(f))
    k_in = (1.0 - lb) * jax.nn.sigmoid(-f)
    hd = lambda t: t.reshape(bsz, seq, B_HEADS, -1)
    o = _hgrn2_sub(hd(q), hd(k_in), hd(log_f), hd(i))
    o = o * lax.rsqrt(jnp.mean(o * o, axis=-1, keepdims=True) + NORM_EPS)
    return o.reshape(bsz, seq, B_WIDTH) * gn_g * jax.nn.silu(g)


def _rope_tables(seq, dim):
    half = dim // 2
    inv_freq = ROPE_THETA ** (-jnp.arange(half, dtype=F32) * (2.0 / dim))
    ang = jnp.arange(seq, dtype=F32)[:, None] * inv_freq[None, :]
    return jnp.cos(ang), jnp.sin(ang)


def _apply_rope(t, cos, sin):
    half = t.shape[-1] // 2
    t1, t2 = t[..., :half], t[..., half:]
    c, s = cos[None, :, None, :], sin[None, :, None, :]
    return jnp.concatenate([t1 * c - t2 * s, t2 * c + t1 * s], axis=-1)


def _dsa_branch(p, cos, sin):
    bsz, seq, _ = p.shape
    k_sel = min(TOPK_MAX, seq // 4)
    kv_w = C_KV_HEADS * C_HEAD_DIM
    q, k, v, qi, ki, wi = _split(p[..., :C_COLS], (C_WIDTH, kv_w, kv_w, IDX_HEADS * IDX_DIM, IDX_DIM, IDX_HEADS))
    q = _apply_rope(q.reshape(bsz, seq, C_HEADS, C_HEAD_DIM), cos, sin)
    k = _apply_rope(k.reshape(bsz, seq, C_KV_HEADS, C_HEAD_DIM), cos, sin)
    v = v.reshape(bsz, seq, C_KV_HEADS, C_HEAD_DIM)
    qi = _apply_rope(qi.reshape(bsz, seq, IDX_HEADS, IDX_DIM), cos, sin)
    ki = _apply_rope(ki[:, :, None, :], cos, sin)[:, :, 0, :]
    wi = wi * (IDX_HEADS ** -0.5) * (IDX_DIM ** -0.5)
    dots = jnp.einsum('bqhd,bsd->bqhs', qi, ki, precision=lax.Precision.HIGHEST)
    score = jnp.sum(jax.nn.relu(dots) * wi[..., None], axis=2)
    pos = jnp.arange(seq)
    causal = pos[None, :] <= pos[:, None]
    bits = lax.bitcast_convert_type(score + 0.0, jnp.int32)
    key = jnp.where(bits < 0, bits ^ jnp.int32(0x7fffffff), bits)
    key = jnp.where(causal[None], key, jnp.int32(-2 ** 31))
    ukey = lax.bitcast_convert_type(key ^ jnp.int32(-2 ** 31), jnp.uint32)
    cand = jnp.zeros((bsz, seq, 1), jnp.uint32)
    for bit in range(31, -1, -1):
        trial = cand | jnp.uint32(1 << bit)
        cnt = jnp.sum((ukey >= trial).astype(jnp.int32), axis=-1, keepdims=True)
        cand = jnp.where(cnt >= k_sel, trial, cand)
    gt = ukey > cand
    eq = ukey == cand
    need = k_sel - jnp.sum(gt.astype(jnp.int32), axis=-1, keepdims=True)
    J = jnp.zeros((bsz, seq, 1), jnp.int32)
    for bit in range(int(np.ceil(np.log2(seq))), -1, -1):
        trial = J | (1 << bit)
        cnt = jnp.sum((eq & (pos[None, None, :] < trial)).astype(jnp.int32), axis=-1, keepdims=True)
        J = jnp.where(cnt <= need, trial, J)
    sel = (gt | (eq & (pos[None, None, :] < J))) & causal[None]
    group = C_HEADS // C_KV_HEADS
    qg = q.reshape(bsz, seq, C_KV_HEADS, group, C_HEAD_DIM)
    logits = jnp.einsum('bqcgd,bscd->bcgqs', qg.astype(BF16), k.astype(BF16),
                        preferred_element_type=F32) * (C_HEAD_DIM ** -0.5)
    logits = jnp.where(sel[:, None, None], logits, -jnp.inf)
    prob = jax.nn.softmax(logits, axis=-1)
    out = jnp.einsum('bcgqs,bscd->bqcgd', prob.astype(BF16), v.astype(BF16), preferred_element_type=F32)
    return out.reshape(bsz, seq, C_WIDTH)


def kernel(x, norm_mix_g, w_in, a_mu, a_w0, a_w_up, a_a0, a_a_up, a_g_up, a_k_k, a_k_a, a_r_k, a_gn_g,
           a_gn_b, b_lb_logits, b_gn_g, w_branch, w_o, norm_ffn_g, w_up, conv_w, conv_b, w_down,
           norm_final_g):
    bsz, seq, d = x.shape
    t = bsz * seq
    lb_cum = jnp.cumsum(jax.nn.softmax(b_lb_logits.astype(F32), axis=0), axis=0)
    lower_bounds = lb_cum - lb_cum[0]
    h = x
    for layer in range(DEPTH):
        w = w_in[layer].astype(BF16)
        w_a, w_b, w_c, w_g = _split(w, (A_COLS, B_COLS, C_COLS, GATE_COLS))
        w_c = jnp.pad(w_c, ((0, 0), (0, C_PAD - C_COLS)))
        h2 = h.reshape(t, d)
        g = norm_mix_g[layer]
        p_a = norm_matmul(h2, g, w_a, tm=2048, tn=A_COLS // 2, out_dtype=BF16).reshape(bsz, seq, -1)
        p_b = norm_matmul(h2, g, w_b, tm=2048, tn=1024, out_dtype=BF16).reshape(bsz, seq, -1)
        p_c = norm_matmul(h2, g, w_c, tm=2048, tn=C_PAD, out_dtype=F32).reshape(bsz, seq, -1)
        p_g = norm_matmul(h2, g, w_g, tm=2048, tn=1024, out_dtype=BF16)
        y_a = rwkv7(p_a, a_mu[layer], a_w0[layer], a_w_up[layer], a_a0[layer], a_a_up[layer],
                    a_g_up[layer], a_k_k[layer], a_k_a[layer], a_r_k[layer], a_gn_g[layer], a_gn_b[layer])
        y_b = hgrn2(p_b, lower_bounds[layer], b_gn_g[layer])
        y_c = dsa(p_c)
        wbr = w_branch[layer].astype(BF16)
        h2 = merge((y_a.reshape(t, -1), y_b.reshape(t, -1), y_c.reshape(t, -1)), p_g, h2,
                   (wbr[0], wbr[1], wbr[2]), w_o[layer].astype(BF16), tm=512)
        h = conv_glu(h2.reshape(bsz, seq, d), norm_ffn_g[layer], w_up[layer].astype(BF16), conv_w[layer],
                     conv_b[layer], w_down[layer].astype(BF16),
                     norm_final_g if layer == DEPTH - 1 else None, tm=512, tf=1408)
    return h
```

```python
import functools

import jax
import jax.numpy as jnp
import numpy as np
from jax import lax
from jax.experimental import pallas as pl
from jax.experimental.pallas import tpu as pltpu

F32 = jnp.float32
BF16 = jnp.bfloat16

D_MODEL = 1024
DEPTH = 2
MIX_WIDTH = 512
N_BRANCH = 3
A_HEAD_DIM = 64
A_HEADS = 8
A_WIDTH = 512
A_RANK_W = 64
A_RANK_A = 64
A_RANK_G = 128
A_GN_EPS = 64e-5
B_KEY_DIM = 128
B_VAL_DIM = 128
B_HEADS = 4
B_WIDTH = 512
C_HEAD_DIM = 64
C_HEADS = 8
C_KV_HEADS = 2
C_WIDTH = 512
IDX_HEADS = 4
IDX_DIM = 64
TOPK_MAX = 256
ROPE_THETA = 10000.0
D_FF = 2816
CONV_WIDTH = 3
NORM_EPS = 1e-6

A_COLS = 3 * A_WIDTH + A_RANK_W + A_RANK_A + A_RANK_G
B_COLS = 4 * B_WIDTH
C_COLS = C_WIDTH + 2 * C_KV_HEADS * C_HEAD_DIM + IDX_HEADS * IDX_DIM + IDX_DIM + IDX_HEADS
GATE_COLS = N_BRANCH * D_MODEL

LANE = 128
SUBLANE = 8
VMEM_LIMIT = 48 * 1024 * 1024


def _round_up(n, m):
    return (n + m - 1) // m * m


def _rms(x, g):
    return x * lax.rsqrt(jnp.mean(x * x, axis=-1, keepdims=True) + NORM_EPS) * g


def _norm_matmul_kernel(x_ref, g_ref, w_ref, o_ref, xn_ref):
    @pl.when(pl.program_id(1) == 0)
    def _():
        xn_ref[...] = _rms(x_ref[...], g_ref[...]).astype(BF16)

    o_ref[...] = jnp.dot(xn_ref[...], w_ref[...], preferred_element_type=F32).astype(o_ref.dtype)


def norm_matmul(x, g, w, *, tm, tn, out_dtype):
    t, d = x.shape
    n = w.shape[1]
    assert t % tm == 0 and n % tn == 0
    return pl.pallas_call(
        _norm_matmul_kernel,
        grid=(t // tm, n // tn),
        in_specs=[
            pl.BlockSpec((tm, d), lambda i, j: (i, 0)),
            pl.BlockSpec((1, d), lambda i, j: (0, 0)),
            pl.BlockSpec((d, tn), lambda i, j: (0, j)),
        ],
        out_specs=pl.BlockSpec((tm, tn), lambda i, j: (i, j)),
        out_shape=jax.ShapeDtypeStruct((t, n), out_dtype),
        scratch_shapes=[pltpu.VMEM((tm, d), BF16)],
        compiler_params=pltpu.CompilerParams(
            dimension_semantics=("arbitrary", "arbitrary"), vmem_limit_bytes=VMEM_LIMIT),
        name="norm_matmul",
    )(x, g.reshape(1, d), w)


def _merge_kernel(ya_ref, yb_ref, yc_ref, gate_ref, h_ref, wa_ref, wb_ref, wc_ref, wo_ref, o_ref):
    merged = None
    for b, (y_ref, w_ref) in enumerate(((ya_ref, wa_ref), (yb_ref, wb_ref), (yc_ref, wc_ref))):
        br = jnp.dot(y_ref[...].astype(BF16), w_ref[...], preferred_element_type=F32)
        gt = jax.nn.sigmoid(gate_ref[:, b * D_MODEL:(b + 1) * D_MODEL].astype(F32))
        merged = gt * br if merged is None else merged + gt * br
    o_ref[...] = h_ref[...] + jnp.dot(merged.astype(BF16), wo_ref[...], preferred_element_type=F32)


def merge(ys, gate, h, ws, w_o, *, tm):
    t = h.shape[0]
    assert t % tm == 0
    row = lambda w: pl.BlockSpec((tm, w), lambda i: (i, 0))
    const = lambda a: pl.BlockSpec(a.shape, lambda i: (0, 0))
    return pl.pallas_call(
        _merge_kernel,
        grid=(t // tm,),
        in_specs=[row(y.shape[1]) for y in ys] + [row(GATE_COLS), row(D_MODEL)]
                 + [const(w) for w in ws] + [const(w_o)],
        out_specs=row(D_MODEL),
        out_shape=jax.ShapeDtypeStruct((t, D_MODEL), F32),
        compiler_params=pltpu.CompilerParams(
            dimension_semantics=("arbitrary",), vmem_limit_bytes=VMEM_LIMIT),
        name="merge",
    )(*ys, gate, h, *ws, w_o)


def _shift_rows(cur, prev_tail, shift):
    rolled = pltpu.roll(cur, shift, 0)
    tail = pltpu.roll(prev_tail, shift, 0)
    row = lax.broadcasted_iota(jnp.int32, tail.shape, 0)
    top = jnp.where(row < shift, tail, rolled[:SUBLANE])
    return jnp.concatenate([top, rolled[SUBLANE:]], axis=0)


def _conv_glu_kernel(h_ref, g_ref, wug_ref, wuv_ref, cwg_ref, cwv_ref, cbg_ref, cbv_ref, wd_ref, *rest,
                     final_norm):
    if final_norm:
        gf_ref, o_ref, xn_ref, acc_ref, carry_ref = rest
    else:
        o_ref, xn_ref, acc_ref, carry_ref = rest
    ti = pl.program_id(1)
    fi = pl.program_id(2)
    nf = pl.num_programs(2)

    @pl.when(fi == 0)
    def _():
        xn_ref[...] = _rms(h_ref[0], g_ref[...]).astype(BF16)
        acc_ref[...] = jnp.zeros_like(acc_ref)

    @pl.when(ti == 0)
    def _():
        carry_ref[fi] = jnp.zeros(carry_ref.shape[1:], F32)

    xn = xn_ref[...]

    def conv(w_ref, cw_ref, cb_ref, slot):
        up = jnp.dot(xn, w_ref[...], preferred_element_type=F32)
        tail = carry_ref[fi, slot]
        c = (cb_ref[...] + cw_ref[2:3, :] * up + cw_ref[1:2, :] * _shift_rows(up, tail, 1)
             + cw_ref[0:1, :] * _shift_rows(up, tail, 2))
        carry_ref[fi, slot] = up[up.shape[0] - SUBLANE:]
        return c

    cg = conv(wug_ref, cwg_ref, cbg_ref, 0)
    cv = conv(wuv_ref, cwv_ref, cbv_ref, 1)
    act = (cg * jax.nn.sigmoid(cg) * cv).astype(BF16)
    acc_ref[...] += jnp.dot(act, wd_ref[...], preferred_element_type=F32)

    @pl.when(fi == nf - 1)
    def _():
        out = h_ref[0] + acc_ref[...]
        if final_norm:
            out = _rms(out, gf_ref[...])
        o_ref[0] = out


def conv_glu(h, g, w_up, conv_w, conv_b, w_down, gf, *, tm, tf):
    bsz, seq, d = h.shape
    assert seq % tm == 0 and D_FF % tf == 0
    nf = D_FF // tf
    final_norm = gf is not None
    vec = lambda: pl.BlockSpec((1, d), lambda b, t, f: (0, 0))
    in_specs = [
        pl.BlockSpec((1, tm, d), lambda b, t, f: (b, t, 0)),
        vec(),
        pl.BlockSpec((d, tf), lambda b, t, f: (0, f)),
        pl.BlockSpec((d, tf), lambda b, t, f: (0, f + nf)),
        pl.BlockSpec((CONV_WIDTH, tf), lambda b, t, f: (0, f)),
        pl.BlockSpec((CONV_WIDTH, tf), lambda b, t, f: (0, f + nf)),
        pl.BlockSpec((1, tf), lambda b, t, f: (0, f)),
        pl.BlockSpec((1, tf), lambda b, t, f: (0, f + nf)),
        pl.BlockSpec((tf, d), lambda b, t, f: (f, 0)),
    ]
    args = [h, g.reshape(1, d), w_up, w_up, conv_w, conv_w, conv_b.reshape(1, -1), conv_b.reshape(1, -1), w_down]
    if final_norm:
        in_specs.append(vec())
        args.append(gf.reshape(1, d))
    return pl.pallas_call(
        functools.partial(_conv_glu_kernel, final_norm=final_norm),
        grid=(bsz, seq // tm, nf),
        in_specs=in_specs,
        out_specs=pl.BlockSpec((1, tm, d), lambda b, t, f: (b, t, 0)),
        out_shape=jax.ShapeDtypeStruct((bsz, seq, d), F32),
        scratch_shapes=[pltpu.VMEM((tm, d), BF16), pltpu.VMEM((tm, d), F32),
                        pltpu.VMEM((nf, 2, SUBLANE, tf), F32)],
        compiler_params=pltpu.CompilerParams(
            dimension_semantics=("arbitrary", "arbitrary", "arbitrary"), vmem_limit_bytes=VMEM_LIMIT),
        name="conv_glu",
    )(*args)


HP = LANE
A_PADW = A_HEADS * HP
A_CHUNK = 64
A_LOW = A_RANK_W + A_RANK_A + A_RANK_G
A_PCOLS = 3 * A_PADW + A_LOW


def _dot(x, y):
    return jnp.dot(x.astype(BF16), y.astype(BF16), preferred_element_type=F32)


def _dot_t(x, y):
    return lax.dot_general(x.astype(BF16), y.astype(BF16), (((1,), (1,)), ((), ())), preferred_element_type=F32)


A_NB = 2


def _rwkv_kernel(p_ref, mu_ref, w0_ref, a0_ref, wa_ref, gup_ref, kk_ref, ka_ref, rk_ref, gng_ref, gnb_ref,
                 o_ref, carry_ref, h_ref):
    C = A_CHUNK
    W = A_PADW
    NB = A_NB
    R = NB * C

    @pl.when(pl.program_id(1) == 0)
    def _():
        carry_ref[...] = jnp.zeros_like(carry_ref)
        h_ref[...] = jnp.zeros_like(h_ref)

    blocks = []
    for s in range(NB):
        ps = p_ref[s].astype(F32)
        prev = _shift_rows(ps, carry_ref[s], 1)
        carry_ref[s] = ps[C - SUBLANE:]
        blocks.append(ps + (prev - ps) * mu_ref[...])
    p = jnp.concatenate(blocks, axis=0)
    lane = lax.broadcasted_iota(jnp.int32, (R, LANE), 1)
    valid = lane < A_HEAD_DIM
    valid_c = lax.broadcasted_iota(jnp.int32, (C, LANE), 1) < A_HEAD_DIM

    def pad_heads(x):
        slabs = []
        for h in range(A_HEADS):
            t = x[:, (h // 2) * LANE:(h // 2 + 1) * LANE]
            if h % 2:
                t = pltpu.roll(t, A_HEAD_DIM, 1)
            slabs.append(jnp.where(valid, t, 0.0))
        return jnp.concatenate(slabs, axis=1)
    r, k, v = (pad_heads(p[:, i * A_WIDTH:(i + 1) * A_WIDTH]) for i in range(3))
    low = p[:, 3 * A_WIDTH:3 * A_WIDTH + LANE]
    gd = p[:, 3 * A_WIDTH + LANE:]
    low = jnp.where(lane < A_RANK_W, jnp.tanh(low), low)
    wa = _dot(low, wa_ref[...])
    w_log = -jax.nn.softplus(-(w0_ref[...] + wa[:, :W])) - 0.5
    logw = -jnp.exp(w_log)
    a = jax.nn.sigmoid(a0_ref[...] + wa[:, W:])
    g = _dot(jax.nn.sigmoid(gd), gup_ref[...])
    k_mod = k * (1.0 + (a - 1.0) * ka_ref[...])
    kk = k * kk_ref[...]

    row = lax.broadcasted_iota(jnp.int32, (C, C), 0)
    col = lax.broadcasted_iota(jnp.int32, (C, C), 1)
    incl = row >= col
    strict = row > col
    rowr = lax.broadcasted_iota(jnp.int32, (R, R), 0)
    colr = lax.broadcasted_iota(jnp.int32, (R, R), 1)
    same_seq = (rowr // C) == (colr // C)
    cum = jnp.dot((same_seq & (rowr >= colr)).astype(F32), logw, precision=lax.Precision.HIGHEST,
                  preferred_element_type=F32)
    cum_end = [cum[(s + 1) * C - 1:(s + 1) * C, :] for s in range(NB)]
    e_neg = jnp.exp(-cum)
    e_end = jnp.concatenate([jnp.exp(cum_end[s] - cum[s * C:(s + 1) * C]) for s in range(NB)], axis=0)
    r_t = r * jnp.exp(cum)
    a_scale = jnp.exp(cum - logw)
    p_end = [jnp.exp(cum_end[s]) for s in range(NB)]
    eye = (lax.broadcasted_iota(jnp.int32, (HP, HP), 0) == lax.broadcasted_iota(jnp.int32, (HP, HP), 1)).astype(F32)
    eye_c = (row == col).astype(F32)

    chains = [(s, h) for s in range(NB) for h in range(A_HEADS)]
    nch = range(len(chains))

    def blk(x, n):
        s, h = chains[n]
        return x[s * C:(s + 1) * C, h * HP:(h + 1) * HP]
    each = lambda f: [f(n) for n in nch]
    kk_n = each(lambda n: blk(kk, n) * lax.rsqrt(jnp.maximum(
        jnp.sum(blk(kk, n) * blk(kk, n), axis=-1, keepdims=True), 1e-24)))
    b_h = each(lambda n: kk_n[n] * blk(a, n))
    at = each(lambda n: -kk_n[n] * blk(a_scale, n))
    ar = each(lambda n: jnp.concatenate([at[n], blk(r_t, n)], axis=0))
    ab = each(lambda n: _dot_t(ar[n], b_h[n] * blk(e_neg, n)))
    ak = each(lambda n: _dot_t(ar[n], blk(k_mod, n) * blk(e_neg, n)))
    a_ab = each(lambda n: jnp.where(strict, ab[n][:C], 0.0))
    a_rb = each(lambda n: jnp.where(incl, ab[n][C:], 0.0))
    a_ak = each(lambda n: jnp.where(strict, ak[n][:C], 0.0))
    a_rk = each(lambda n: jnp.where(incl, ak[n][C:], 0.0))
    tm = each(lambda n: eye_c + a_ab[n])
    pw = each(lambda n: _dot(a_ab[n], a_ab[n]))
    akv = each(lambda n: _dot(a_ak[n], blk(v, n)))
    m = 2
    while 2 * m < C:
        x = each(lambda n: _dot(jnp.concatenate([tm[n], pw[n]], axis=0), pw[n]))
        tm = each(lambda n: tm[n] + x[n][:C])
        pw = each(lambda n: x[n][C:])
        m *= 2
    tm = each(lambda n: tm[n] + _dot(tm[n], pw[n]))
    x2 = each(lambda n: _dot(tm[n], jnp.concatenate([at[n], akv[n]], axis=1)))
    x3 = each(lambda n: _dot(a_rb[n], x2[n]))
    x4 = each(lambda n: _dot((b_h[n] * blk(e_end, n)).T, x2[n]))
    rkv = each(lambda n: _dot(a_rk[n], blk(v, n)))
    kv = each(lambda n: _dot((blk(k_mod, n) * blk(e_end, n)).T, blk(v, n)))
    rm = each(lambda n: jnp.concatenate([blk(r_t, n) + x3[n][:, :HP],
                                         eye * p_end[chains[n][0]][:, chains[n][1] * HP:(chains[n][1] + 1) * HP]
                                         + x4[n][:, :HP]], axis=0))
    x5 = each(lambda n: _dot(rm[n], h_ref[n]))
    for s in range(NB):
        outs = []
        for h in range(A_HEADS):
            n = s * A_HEADS + h
            sl = slice(h * HP, (h + 1) * HP)
            h_ref[n] = x5[n][C:] + x4[n][:, HP:] + kv[n]
            y = x5[n][:C] + x3[n][:, HP:] + rkv[n]
            mean = jnp.sum(y, axis=-1, keepdims=True) * (1.0 / A_HEAD_DIM)
            dlt = jnp.where(valid_c, y - mean, 0.0)
            var = jnp.sum(dlt * dlt, axis=-1, keepdims=True) * (1.0 / A_HEAD_DIM)
            yn = dlt * lax.rsqrt(var + A_GN_EPS) * gng_ref[:, sl] + gnb_ref[:, sl]
            bonus = jnp.sum(blk(r, n) * blk(k_mod, n) * rk_ref[:, sl], axis=-1, keepdims=True) * blk(v, n)
            outs.append((yn + bonus) * blk(g, n))
        o_ref[s] = jnp.concatenate(
            [outs[2 * i] + pltpu.roll(outs[2 * i + 1], A_HEAD_DIM, 1) for i in range(A_HEADS // 2)],
            axis=1).astype(o_ref.dtype)


def _rwkv_kernel_one(p_ref, mu_ref, w0_ref, a0_ref, wa_ref, gup_ref, kk_ref, ka_ref, rk_ref, gng_ref, gnb_ref,
                     o_ref, carry_ref, h_ref):
    C = A_CHUNK
    W = A_PADW

    @pl.when(pl.program_id(1) == 0)
    def _():
        carry_ref[...] = jnp.zeros_like(carry_ref)
        h_ref[...] = jnp.zeros_like(h_ref)

    p = p_ref[0].astype(F32)
    prev = _shift_rows(p, carry_ref[...], 1)
    carry_ref[...] = p[C - SUBLANE:]
    p = p + (prev - p) * mu_ref[...]
    lane = lax.broadcasted_iota(jnp.int32, (C, LANE), 1)
    valid = lane < A_HEAD_DIM

    def pad_heads(x):
        slabs = []
        for h in range(A_HEADS):
            s = x[:, (h // 2) * LANE:(h // 2 + 1) * LANE]
            if h % 2:
                s = pltpu.roll(s, A_HEAD_DIM, 1)
            slabs.append(jnp.where(valid, s, 0.0))
        return jnp.concatenate(slabs, axis=1)
    r, k, v = (pad_heads(p[:, i * A_WIDTH:(i + 1) * A_WIDTH]) for i in range(3))
    low = p[:, 3 * A_WIDTH:3 * A_WIDTH + LANE]
    gd = p[:, 3 * A_WIDTH + LANE:]
    low = jnp.where(lane < A_RANK_W, jnp.tanh(low), low)
    wa = _dot(low, wa_ref[...])
    w_log = -jax.nn.softplus(-(w0_ref[...] + wa[:, :W])) - 0.5
    logw = -jnp.exp(w_log)
    a = jax.nn.sigmoid(a0_ref[...] + wa[:, W:])
    g = _dot(jax.nn.sigmoid(gd), gup_ref[...])
    k_mod = k * (1.0 + (a - 1.0) * ka_ref[...])
    kk = k * kk_ref[...]

    row = lax.broadcasted_iota(jnp.int32, (C, C), 0)
    col = lax.broadcasted_iota(jnp.int32, (C, C), 1)
    incl = row >= col
    strict = row > col
    cum = jnp.dot(incl.astype(F32), logw, precision=lax.Precision.HIGHEST, preferred_element_type=F32)
    cum_end = cum[C - 1:C, :]
    e_neg = jnp.exp(-cum)
    e_end = jnp.exp(cum_end - cum)
    r_t = r * jnp.exp(cum)
    a_scale = jnp.exp(cum - logw)
    p_end = jnp.exp(cum_end)
    eye = (lax.broadcasted_iota(jnp.int32, (HP, HP), 0) == lax.broadcasted_iota(jnp.int32, (HP, HP), 1)).astype(F32)
    eye_c = (row == col).astype(F32)

    heads = range(A_HEADS)
    sls = [slice(h * HP, (h + 1) * HP) for h in heads]
    each = lambda f: [f(h) for h in heads]
    kk_n = each(lambda h: kk[:, sls[h]] * lax.rsqrt(jnp.maximum(
        jnp.sum(kk[:, sls[h]] * kk[:, sls[h]], axis=-1, keepdims=True), 1e-24)))
    b_h = each(lambda h: kk_n[h] * a[:, sls[h]])
    at = each(lambda h: -kk_n[h] * a_scale[:, sls[h]])
    ar = each(lambda h: jnp.concatenate([at[h], r_t[:, sls[h]]], axis=0))
    ab = each(lambda h: _dot_t(ar[h], b_h[h] * e_neg[:, sls[h]]))
    ak = each(lambda h: _dot_t(ar[h], k_mod[:, sls[h]] * e_neg[:, sls[h]]))
    a_ab = each(lambda h: jnp.where(strict, ab[h][:C], 0.0))
    a_rb = each(lambda h: jnp.where(incl, ab[h][C:], 0.0))
    a_ak = each(lambda h: jnp.where(strict, ak[h][:C], 0.0))
    a_rk = each(lambda h: jnp.where(incl, ak[h][C:], 0.0))
    tm = each(lambda h: eye_c + a_ab[h])
    pw = each(lambda h: _dot(a_ab[h], a_ab[h]))
    akv = each(lambda h: _dot(a_ak[h], v[:, sls[h]]))
    n = 2
    while 2 * n < C:
        x = each(lambda h: _dot(jnp.concatenate([tm[h], pw[h]], axis=0), pw[h]))
        tm = each(lambda h: tm[h] + x[h][:C])
        pw = each(lambda h: x[h][C:])
        n *= 2
    tm = each(lambda h: tm[h] + _dot(tm[h], pw[h]))
    x2 = each(lambda h: _dot(tm[h], jnp.concatenate([at[h], akv[h]], axis=1)))
    x3 = each(lambda h: _dot(a_rb[h], x2[h]))
    x4 = each(lambda h: _dot((b_h[h] * e_end[:, sls[h]]).T, x2[h]))
    rkv = each(lambda h: _dot(a_rk[h], v[:, sls[h]]))
    kv = each(lambda h: _dot((k_mod[:, sls[h]] * e_end[:, sls[h]]).T, v[:, sls[h]]))
    rm = each(lambda h: jnp.concatenate([r_t[:, sls[h]] + x3[h][:, :HP],
                                         eye * p_end[:, sls[h]] + x4[h][:, :HP]], axis=0))
    x5 = each(lambda h: _dot(rm[h], h_ref[h]))
    outs = []
    for h in heads:
        sl = sls[h]
        h_ref[h] = x5[h][C:] + x4[h][:, HP:] + kv[h]
        y = x5[h][:C] + x3[h][:, HP:] + rkv[h]
        mean = jnp.sum(y, axis=-1, keepdims=True) * (1.0 / A_HEAD_DIM)
        dlt = jnp.where(valid, y - mean, 0.0)
        var = jnp.sum(dlt * dlt, axis=-1, keepdims=True) * (1.0 / A_HEAD_DIM)
        yn = dlt * lax.rsqrt(var + A_GN_EPS) * gng_ref[:, sl] + gnb_ref[:, sl]
        bonus = jnp.sum(r[:, sl] * k_mod[:, sl] * rk_ref[:, sl], axis=-1, keepdims=True) * v[:, sl]
        outs.append((yn + bonus) * g[:, sl])
    o_ref[0] = jnp.concatenate(
        [outs[2 * m] + pltpu.roll(outs[2 * m + 1], A_HEAD_DIM, 1) for m in range(A_HEADS // 2)],
        axis=1).astype(o_ref.dtype)


def _pad_heads(t, heads):
    hd = t.shape[-1] // heads
    t = t.reshape(t.shape[:-1] + (heads, hd))
    t = jnp.pad(t, [(0, 0)] * (t.ndim - 1) + [(0, HP - hd)])
    return t.reshape(t.shape[:-2] + (heads * HP,))


def rwkv7(p, mu, w0, w_up, a0, a_up, g_up, k_k, k_a, r_k, gn_g, gn_b):
    bsz, seq, _ = p.shape
    C = A_CHUNK
    assert seq % C == 0
    z = jnp.zeros((A_RANK_W, A_PADW), F32)
    wa = jnp.concatenate([jnp.concatenate([_pad_heads(w_up, A_HEADS), z], axis=1),
                          jnp.concatenate([z, _pad_heads(a_up, A_HEADS)], axis=1)], axis=0).astype(BF16)
    vecs = [_pad_heads(t.reshape(-1), A_HEADS).reshape(1, -1) for t in (k_k, k_a, r_k, gn_g, gn_b)]
    full = lambda shape: pl.BlockSpec(shape, lambda b, c: (0,) * len(shape))
    assert bsz % A_NB == 0
    return pl.pallas_call(
        _rwkv_kernel,
        grid=(bsz // A_NB, seq // C),
        in_specs=[pl.BlockSpec((A_NB, C, A_COLS), lambda b, c: (b, c, 0)),
                  full((1, A_COLS)), full((1, A_PADW)), full((1, A_PADW)),
                  full((LANE, 2 * A_PADW)), full((A_RANK_G, A_PADW))] + [full((1, A_PADW))] * 5,
        out_specs=pl.BlockSpec((A_NB, C, A_WIDTH), lambda b, c: (b, c, 0)),
        out_shape=jax.ShapeDtypeStruct((bsz, seq, A_WIDTH), BF16),
        scratch_shapes=[pltpu.VMEM((A_NB, SUBLANE, A_COLS), F32), pltpu.VMEM((A_NB * A_HEADS, HP, HP), F32)],
        compiler_params=pltpu.CompilerParams(
            dimension_semantics=("arbitrary", "arbitrary"), vmem_limit_bytes=VMEM_LIMIT),
        name="rwkv7",
    )(p, mu.reshape(1, -1), _pad_heads(w0, A_HEADS).reshape(1, -1), _pad_heads(a0, A_HEADS).reshape(1, -1), wa,
      _pad_heads(g_up, A_HEADS).astype(BF16), *vecs)


B_CHUNK = 64
B_SUB = 16


def _dot_tl(x, y):
    return lax.dot_general(x.astype(BF16), y.astype(BF16), (((0,), (0,)), ((), ())), preferred_element_type=F32)


B_NB = 2


def _hgrn_kernel(p_ref, lb_ref, gn_ref, o_ref, st_ref):
    C, SB, W = B_CHUNK, B_SUB, B_WIDTH
    nsb = C // SB
    D = B_KEY_DIM
    NB = B_NB
    R = NB * C

    @pl.when(pl.program_id(1) == 0)
    def _():
        st_ref[...] = jnp.zeros_like(st_ref)

    p = jnp.concatenate([p_ref[s].astype(F32) for s in range(NB)], axis=0)
    q, f, val, gate = p[:, :W], p[:, W:2 * W], p[:, 2 * W:3 * W], p[:, 3 * W:]
    lb = lb_ref[...]
    log_f = jnp.logaddexp(jnp.log(lb), jnp.log1p(-lb) + jax.nn.log_sigmoid(f))
    k = (1.0 - lb) * jax.nn.sigmoid(-f)
    rowr = lax.broadcasted_iota(jnp.int32, (R, R), 0)
    colr = lax.broadcasted_iota(jnp.int32, (R, R), 1)
    b = jnp.dot((((rowr // C) == (colr // C)) & (rowr >= colr)).astype(F32), log_f,
                precision=lax.Precision.HIGHEST, preferred_element_type=F32)
    b0 = jnp.concatenate(
        [jnp.zeros((SB, W), F32) if i % nsb == 0 else jnp.broadcast_to(b[i * SB - 1:i * SB, :], (SB, W))
         for i in range(NB * nsb)], axis=0)
    qt = q * jnp.exp(b - b0)
    qe = q * jnp.exp(b)
    b_end = [b[(s + 1) * C - 1:(s + 1) * C, :] for s in range(NB)]
    ke = jnp.concatenate([k[s * C:(s + 1) * C] * jnp.exp(b_end[s] - b[s * C:(s + 1) * C]) for s in range(NB)],
                         axis=0)
    e_end = [jnp.exp(b_end[s]) for s in range(NB)]
    rowd = lax.broadcasted_iota(jnp.int32, (C, D), 0)
    lane_c = lax.broadcasted_iota(jnp.int32, (SB, C), 1)
    row_sb = lax.broadcasted_iota(jnp.int32, (SB, C), 0)

    chains = [(s, h) for s in range(NB) for h in range(B_HEADS)]
    each = lambda fn: [fn(n) for n in range(len(chains))]

    def blk(x, n):
        s, h = chains[n]
        return x[s * C:(s + 1) * C, h * D:(h + 1) * D]

    def off_operands(n):
        k_n, b_n, qt_n = blk(k, n), blk(b, n), blk(qt, n)
        kts, qts = [], []
        for i in range(1, nsb):
            b0_i = b_n[i * SB - 1:i * SB]
            kts.append(jnp.where(rowd < i * SB, k_n * jnp.exp(jnp.minimum(b0_i - b_n, 0.0)), 0.0))
            qts.append(jnp.where((rowd >= i * SB) & (rowd < (i + 1) * SB), qt_n, 0.0))
        return jnp.concatenate(kts, axis=1), jnp.concatenate(qts, axis=1)
    offs = each(off_operands)
    st_off = each(lambda n: _dot_t(offs[n][0], offs[n][1]))

    def diag_t(n):
        k_n, b_n, q_n = blk(k, n), blk(b, n), blk(q, n)
        blocks = []
        for i in range(nsb):
            rs = slice(i * SB, (i + 1) * SB)
            k_i, b_i = k_n[rs], b_n[rs]
            acc = jnp.zeros((SB, C), F32)
            for j in range(SB):
                t = i * SB + j
                z = k_i * q_n[t:t + 1] * jnp.exp(jnp.minimum(b_n[t:t + 1] - b_i, 0.0))
                acc = jnp.where(lane_c == t, jnp.sum(z, axis=-1, keepdims=True), acc)
            blocks.append(jnp.where(row_sb + i * SB <= lane_c, acc, 0.0))
        return jnp.concatenate(blocks, axis=0)
    st_diag = each(diag_t)
    o_intra = each(lambda n: _dot_tl(st_off[n] + st_diag[n], blk(val, n)))
    o_state = each(lambda n: _dot_t(blk(qe, n), st_ref[n]))
    upd = each(lambda n: _dot_tl(blk(val, n), blk(ke, n)))
    for s in range(NB):
        outs = []
        for h in range(B_HEADS):
            n = s * B_HEADS + h
            st_ref[n] = st_ref[n] * e_end[s][:, h * D:(h + 1) * D] + upd[n]
            o = o_intra[n] + o_state[n]
            o = o * lax.rsqrt(jnp.mean(o * o, axis=-1, keepdims=True) + NORM_EPS)
            gt = blk(gate, n)
            outs.append(o * gn_ref[:, h * D:(h + 1) * D] * (gt * jax.nn.sigmoid(gt)))
        o_ref[s] = jnp.concatenate(outs, axis=1).astype(o_ref.dtype)


def _hgrn_kernel_one(p_ref, lb_ref, gn_ref, o_ref, st_ref):
    C, SB, W = B_CHUNK, B_SUB, B_WIDTH
    nsb = C // SB
    D = B_KEY_DIM

    @pl.when(pl.program_id(1) == 0)
    def _():
        st_ref[...] = jnp.zeros_like(st_ref)

    p = p_ref[0].astype(F32)
    q, f, val, gate = p[:, :W], p[:, W:2 * W], p[:, 2 * W:3 * W], p[:, 3 * W:]
    lb = lb_ref[...]
    log_f = jnp.logaddexp(jnp.log(lb), jnp.log1p(-lb) + jax.nn.log_sigmoid(f))
    k = (1.0 - lb) * jax.nn.sigmoid(-f)
    row = lax.broadcasted_iota(jnp.int32, (C, C), 0)
    col = lax.broadcasted_iota(jnp.int32, (C, C), 1)
    b = jnp.dot((row >= col).astype(F32), log_f, precision=lax.Precision.HIGHEST,
                preferred_element_type=F32)
    b_end = b[C - 1:C, :]
    b0 = jnp.concatenate(
        [jnp.zeros((SB, W), F32)] + [jnp.broadcast_to(b[i * SB - 1:i * SB, :], (SB, W)) for i in range(1, nsb)],
        axis=0)
    qt = q * jnp.exp(b - b0)
    qe = q * jnp.exp(b)
    ke = k * jnp.exp(b_end - b)
    e_end = jnp.exp(b_end)
    rowd = lax.broadcasted_iota(jnp.int32, (C, D), 0)
    lane_c = lax.broadcasted_iota(jnp.int32, (SB, C), 1)
    row_sb = lax.broadcasted_iota(jnp.int32, (SB, C), 0)

    heads = range(B_HEADS)
    sls = [slice(h * D, (h + 1) * D) for h in heads]
    each = lambda fn: [fn(h) for h in heads]

    def off_operands(h):
        kts, qts = [], []
        for i in range(1, nsb):
            b0_i = b[i * SB - 1:i * SB, sls[h]]
            kts.append(jnp.where(rowd < i * SB, k[:, sls[h]] * jnp.exp(jnp.minimum(b0_i - b[:, sls[h]], 0.0)), 0.0))
            qts.append(jnp.where((rowd >= i * SB) & (rowd < (i + 1) * SB), qt[:, sls[h]], 0.0))
        return jnp.concatenate(kts, axis=1), jnp.concatenate(qts, axis=1)
    offs = each(off_operands)
    st_off = each(lambda h: _dot_t(offs[h][0], offs[h][1]))

    def diag_t(h):
        blocks = []
        for i in range(nsb):
            rs = slice(i * SB, (i + 1) * SB)
            k_i, b_i = k[rs, sls[h]], b[rs, sls[h]]
            acc = jnp.zeros((SB, C), F32)
            for j in range(SB):
                t = i * SB + j
                z = k_i * q[t:t + 1, sls[h]] * jnp.exp(jnp.minimum(b[t:t + 1, sls[h]] - b_i, 0.0))
                acc = jnp.where(lane_c == t, jnp.sum(z, axis=-1, keepdims=True), acc)
            blocks.append(jnp.where(row_sb + i * SB <= lane_c, acc, 0.0))
        return jnp.concatenate(blocks, axis=0)
    st_diag = each(diag_t)
    o_intra = each(lambda h: _dot_tl(st_off[h] + st_diag[h], val[:, sls[h]]))
    o_state = each(lambda h: _dot_t(qe[:, sls[h]], st_ref[h]))
    upd = each(lambda h: _dot_tl(val[:, sls[h]], ke[:, sls[h]]))
    outs = []
    for h in heads:
        st_ref[h] = st_ref[h] * e_end[:, sls[h]] + upd[h]
        o = o_intra[h] + o_state[h]
        o = o * lax.rsqrt(jnp.mean(o * o, axis=-1, keepdims=True) + NORM_EPS)
        gt = gate[:, sls[h]]
        outs.append(o * gn_ref[:, sls[h]] * (gt * jax.nn.sigmoid(gt)))
    o_ref[0] = jnp.concatenate(outs, axis=1).astype(o_ref.dtype)


def hgrn2(p, lb, gn_g):
    bsz, seq, _ = p.shape
    C = B_CHUNK
    assert seq % C == 0 and bsz % B_NB == 0
    vec = pl.BlockSpec((1, B_WIDTH), lambda b, c: (0, 0))
    return pl.pallas_call(
        _hgrn_kernel,
        grid=(bsz // B_NB, seq // C),
        in_specs=[pl.BlockSpec((B_NB, C, B_COLS), lambda b, c: (b, c, 0)), vec, vec],
        out_specs=pl.BlockSpec((B_NB, C, B_WIDTH), lambda b, c: (b, c, 0)),
        out_shape=jax.ShapeDtypeStruct((bsz, seq, B_WIDTH), BF16),
        scratch_shapes=[pltpu.VMEM((B_NB * B_HEADS, B_VAL_DIM, B_KEY_DIM), F32)],
        compiler_params=pltpu.CompilerParams(
            dimension_semantics=("arbitrary", "arbitrary"), vmem_limit_bytes=VMEM_LIMIT),
        name="hgrn2",
    )(p, lb.reshape(1, -1), gn_g.reshape(1, -1))


C_PAD = _round_up(C_COLS, LANE)
QB = 128
C_KV_W = C_KV_HEADS * C_HEAD_DIM
C_K_OFF = C_WIDTH
C_QI_OFF = C_WIDTH + 2 * C_KV_W
C_KI_OFF = C_QI_OFF + IDX_HEADS * IDX_DIM
INT_MIN = -2 ** 31
NEG = -1e30
HALF = C_HEAD_DIM // 2
HALF_RANGE = 1 << 15
VT_ROWS = C_HEAD_DIM + 16


def _rope(t, cos, sin_signed):
    outs = []
    lane = lax.broadcasted_iota(jnp.int32, (t.shape[0], LANE), 1)
    first_half = (lane % C_HEAD_DIM) < HALF
    for c in range(t.shape[1] // LANE):
        x = t[:, c * LANE:(c + 1) * LANE]
        swapped = jnp.where(first_half, pltpu.roll(x, LANE - HALF, 1), pltpu.roll(x, HALF, 1))
        outs.append(x * cos + swapped * sin_signed)
    return outs


GROUP = C_HEADS // C_KV_HEADS


def _split_hi_lo(x):
    hi = x.astype(BF16)
    return hi, (x - hi.astype(F32)).astype(BF16)


QW = 2 * QB


def _dsa_kernel(pq_ref, kv_ref, ki_ref, cosq_ref, sinq_ref, cosf_ref, sinf_ref, o_ref,
                k_s, vt_s, ki_s, key_s, *, k_sel, seq):
    i = pl.program_id(1)
    nblk = i + 1
    lane = lax.broadcasted_iota(jnp.int32, (QW, LANE), 1)
    low_half = lane < C_HEAD_DIM
    row_k = lax.broadcasted_iota(jnp.int32, (QB, QW), 0)
    q_pos = i * QW + lax.broadcasted_iota(jnp.int32, (QB, QW), 1)

    @pl.when(i == 0)
    def _():
        kv = kv_ref[0].astype(F32)
        k_s[...] = _rope(kv[:, :LANE], cosf_ref[...], sinf_ref[...])[0].astype(BF16)
        vt = kv[:, LANE:].T
        extra = (lax.broadcasted_iota(jnp.int32, (VT_ROWS - C_HEAD_DIM, seq), 0) == 0).astype(F32)
        for c in range(C_KV_HEADS):
            vt_s[c] = jnp.concatenate([vt[c * C_HEAD_DIM:(c + 1) * C_HEAD_DIM], extra], axis=0).astype(BF16)
        lane_f = lax.broadcasted_iota(jnp.int32, (seq, LANE), 1)
        ki = jnp.where(lane_f < IDX_DIM, ki_ref[0].astype(F32), 0.0)
        ki = _rope(ki, cosf_ref[...], sinf_ref[...])[0]
        ki_s[...] = (ki + pltpu.roll(ki, IDX_DIM, 1)).astype(BF16)

    pq = pq_ref[0].astype(F32)
    cos, sin = cosq_ref[...], sinq_ref[...]
    q = _rope(pq[:, :C_WIDTH], cos, sin)
    qi = _rope(pq[:, C_QI_OFF:C_KI_OFF], cos, sin)
    qi3 = jnp.concatenate(
        [jnp.where(low_half if h % 2 == 0 else ~low_half, qi[h // 2], 0.0) for h in range(IDX_HEADS)],
        axis=0).astype(BF16)
    wt = pq[:, C_KI_OFF:C_KI_OFF + LANE].T
    wscale = (IDX_HEADS ** -0.5) * (IDX_DIM ** -0.5)
    w_rows = [wt[IDX_DIM + h:IDX_DIM + h + 1, :] * wscale for h in range(IDX_HEADS)]

    def score_blk(jb, _):
        ki_t = ki_s[pl.ds(pl.multiple_of(jb * QW, QW), QW), :]
        d = lax.dot_general(ki_t, qi3, (((1,), (1,)), ((), ())), preferred_element_type=F32)
        for u in range(QW // QB):
            j = (QW // QB) * jb + u
            sc = jnp.zeros((QB, QW), F32)
            for h in range(IDX_HEADS):
                sc = sc + jnp.maximum(d[u * QB:(u + 1) * QB, h * QW:(h + 1) * QW], 0.0) * w_rows[h]
            bits = pltpu.bitcast(sc + 0.0, jnp.int32)
            key = jnp.where(bits < 0, bits ^ jnp.int32(0x7fffffff), bits)
            key_s[j] = jnp.where(j * QB + row_k <= q_pos, key, jnp.int32(INT_MIN))
        return 0
    lax.fori_loop(0, nblk, score_blk, 0)

    def count(pred):
        def body(jb, acc):
            for u in range(QW // QB):
                acc = acc + jnp.where(pred(key_s[(QW // QB) * jb + u]), 1, 0)
            return acc
        return jnp.sum(lax.fori_loop(0, nblk, body, jnp.zeros((QB, QW), jnp.int32)), axis=0, keepdims=True)

    def bit_step(b, cand):
        trial = cand | jnp.left_shift(jnp.int32(1), 31 - b)
        cnt = count(lambda key: key >= (trial ^ jnp.int32(INT_MIN)))
        return jnp.where(cnt >= k_sel, trial, cand)
    cand = lax.fori_loop(0, 32, bit_step, jnp.zeros((1, QW), jnp.int32))
    thr = cand ^ jnp.int32(INT_MIN)
    need_f = (k_sel - count(lambda key: key > thr)).astype(F32)
    tri = (lax.broadcasted_iota(jnp.int32, (QW, QW), 0)
           >= lax.broadcasted_iota(jnp.int32, (QW, QW), 1)).astype(BF16)

    scale = C_HEAD_DIM ** -0.5
    lane_b = lax.broadcasted_iota(jnp.int32, (QB, LANE), 1)
    in_cs = [(lane_b >= c * C_HEAD_DIM) & (lane_b < (c + 1) * C_HEAD_DIM) for c in range(C_KV_HEADS)]
    chains = [(n, e) for n in range(C_HEADS // 2) for e in range(QW // QB)]
    kv_of = [n * 2 // GROUP for n, _ in chains]

    def q_rows(n, e):
        c = n * 2 // GROUP
        rows = []
        for h in (2 * n, 2 * n + 1):
            x = q[n][e * QB:(e + 1) * QB]
            if h % 2 != c:
                x = pltpu.roll(x, C_HEAD_DIM, 1)
            rows.append(jnp.where(in_cs[c], x * scale, 0.0))
        return jnp.concatenate(rows, axis=0).astype(BF16)
    qcs = [q_rows(n, e) for n, e in chains]
    nch = range(len(chains))

    def att_blk(jb, carry):
        off = pl.multiple_of(jb * QW, QW)
        keys = jnp.concatenate([key_s[(QW // QB) * jb + u] for u in range(QW // QB)], axis=0)
        pos = off + lax.broadcasted_iota(jnp.int32, (QW, QW), 0)
        tie = keys == thr
        rank = carry[-1] + jnp.dot(tri, jnp.where(tie, 1.0, 0.0).astype(BF16), preferred_element_type=F32)
        sel = ((keys > thr) | (tie & (rank <= need_f))) & (pos <= q_pos[:1, :])
        bias1 = jnp.where(sel, 0.0, NEG)
        bias = [jnp.concatenate([bias1[:, e * QB:(e + 1) * QB]] * 2, axis=1) for e in range(QW // QB)]
        k_t = k_s[pl.ds(off, QW), :]
        v_t = [vt_s[c, :, pl.ds(off, QW)] for c in range(C_KV_HEADS)]
        s = [lax.dot_general(k_t, qcs[x], (((1,), (1,)), ((), ())), preferred_element_type=F32) for x in nch]
        s = [s[x] + bias[chains[x][1]] for x in nch]
        m_new = [jnp.maximum(carry[2 * x], jnp.max(s[x], axis=0, keepdims=True)) for x in nch]
        p = [jnp.exp(s[x] - m_new[x]).astype(BF16) for x in nch]
        pv = [jnp.dot(v_t[kv_of[x]], p[x], preferred_element_type=F32) for x in nch]
        out = []
        for x in nch:
            out += [m_new[x], carry[2 * x + 1] * jnp.exp(carry[2 * x] - m_new[x]) + pv[x]]
        return tuple(out) + (rank[QW - 1:, :],)
    init = ((jnp.full((1, 2 * QB), NEG, F32), jnp.zeros((VT_ROWS, 2 * QB), F32)) * len(chains)
            + (jnp.zeros((1, QW), F32),))
    res = lax.fori_loop(0, nblk, att_blk, init)

    halves = []
    for e in range(QW // QB):
        tiles = []
        for n in range(C_HEADS // 2):
            acc = res[2 * chains.index((n, e)) + 1]
            out_t = acc[:C_HEAD_DIM] / acc[C_HEAD_DIM:C_HEAD_DIM + 1]
            tiles.append(jnp.concatenate([out_t[:, :QB], out_t[:, QB:]], axis=0).T)
        halves.append(jnp.concatenate(tiles, axis=1))
    o_ref[0] = jnp.concatenate(halves, axis=0).astype(o_ref.dtype)


def _dsa_kernel_qb(pq_ref, kv_ref, ki_ref, cosq_ref, sinq_ref, cosf_ref, sinf_ref, o_ref,
                   k_s, vt_s, ki_s, key_s, khi_s, klo_s, *, k_sel, seq):
    i = pl.program_id(1)
    nk = i + 1
    lane = lax.broadcasted_iota(jnp.int32, (QB, LANE), 1)
    rowi = lax.broadcasted_iota(jnp.int32, (QB, LANE), 0)
    low_half = lane < C_HEAD_DIM

    @pl.when(i == 0)
    def _():
        kv = kv_ref[0].astype(F32)
        k_s[...] = _rope(kv[:, :LANE], cosf_ref[...], sinf_ref[...])[0].astype(BF16)
        vt = kv[:, LANE:].T
        extra = (lax.broadcasted_iota(jnp.int32, (VT_ROWS - C_HEAD_DIM, seq), 0) == 0).astype(F32)
        for c in range(C_KV_HEADS):
            vt_s[c] = jnp.concatenate([vt[c * C_HEAD_DIM:(c + 1) * C_HEAD_DIM], extra], axis=0).astype(BF16)
        lane_f = lax.broadcasted_iota(jnp.int32, (seq, LANE), 1)
        ki = jnp.where(lane_f < IDX_DIM, ki_ref[0].astype(F32), 0.0)
        ki = _rope(ki, cosf_ref[...], sinf_ref[...])[0]
        ki_s[...] = (ki + pltpu.roll(ki, IDX_DIM, 1)).astype(BF16)

    pq = pq_ref[0].astype(F32)
    cos, sin = cosq_ref[...], sinq_ref[...]
    q = _rope(pq[:, :C_WIDTH], cos, sin)
    qi = _rope(pq[:, C_QI_OFF:C_KI_OFF], cos, sin)
    qi3 = jnp.concatenate(
        [jnp.where(low_half if h % 2 == 0 else ~low_half, qi[h // 2], 0.0) for h in range(IDX_HEADS)],
        axis=0).astype(BF16)
    wt = pq[:, C_KI_OFF:C_KI_OFF + LANE].T
    wscale = (IDX_HEADS ** -0.5) * (IDX_DIM ** -0.5)
    w_rows = [wt[IDX_DIM + h:IDX_DIM + h + 1, :] * wscale for h in range(IDX_HEADS)]
    causal_off = i * QB + lane

    npair = (nk + 1) // 2

    def score_tile(jp, _):
        ki_t = ki_s[pl.ds(pl.multiple_of(jp * (2 * QB), 2 * QB), 2 * QB), :]
        d = lax.dot_general(ki_t, qi3, (((1,), (1,)), ((), ())), preferred_element_type=F32)
        for u in range(2):
            j = 2 * jp + u
            sc = jnp.zeros((QB, QB), F32)
            for h in range(IDX_HEADS):
                sc = sc + jnp.maximum(d[u * QB:(u + 1) * QB, h * QB:(h + 1) * QB], 0.0) * w_rows[h]
            bits = pltpu.bitcast(sc + 0.0, jnp.int32)
            key = jnp.where(bits < 0, bits ^ jnp.int32(0x7fffffff), bits)
            key = jnp.where(j * QB + rowi <= causal_off, key, jnp.int32(INT_MIN))
            key_s[j] = key
            khi_s[j] = jnp.right_shift(key, 16).astype(jnp.int16)
            klo_s[j] = ((key & 0xffff) - HALF_RANGE).astype(jnp.int16)
        return 0
    lax.fori_loop(0, npair, score_tile, 0)

    def count16(ref, pred):
        def body(jp, acc):
            for u in range(2):
                acc = acc + jnp.where(pred(ref[2 * jp + u]), jnp.int16(1), jnp.int16(0))
            return acc
        acc = lax.fori_loop(0, npair, body, jnp.zeros((QB, QB), jnp.int16))
        return jnp.sum(acc.astype(jnp.int32), axis=0, keepdims=True)

    def half_search(ref, base):
        def bit_step(b, cand):
            trial = cand | jnp.left_shift(jnp.int32(1), 15 - b)
            t16 = (trial - HALF_RANGE).astype(jnp.int16)
            cnt = base + count16(ref, lambda x: x >= t16)
            return jnp.where(cnt >= k_sel, trial, cand)
        return lax.fori_loop(0, 16, bit_step, jnp.zeros((1, QB), jnp.int32))

    cand_hi = half_search(khi_s, 0)
    hi16 = (cand_hi - HALF_RANGE).astype(jnp.int16)
    above = count16(khi_s, lambda x: x > hi16)

    def mask_low(jp, _):
        for u in range(2):
            j = 2 * jp + u
            klo_s[j] = jnp.where(khi_s[j] == hi16, klo_s[j], jnp.int16(-HALF_RANGE))
        return 0
    lax.fori_loop(0, npair, mask_low, 0)
    cand_lo = half_search(klo_s, above)
    thr = (jnp.left_shift(cand_hi, 16) | cand_lo) ^ jnp.int32(INT_MIN)

    def count(pred):
        def body(jp, acc):
            for u in range(2):
                acc = acc + jnp.where(pred(key_s[2 * jp + u]), 1, 0)
            return acc
        return jnp.sum(lax.fori_loop(0, npair, body, jnp.zeros((QB, QB), jnp.int32)), axis=0, keepdims=True)
    need = k_sel - count(lambda key: key > thr)

    need_f = need.astype(F32)
    tri = (lax.broadcasted_iota(jnp.int32, (2 * QB, 2 * QB), 0)
           >= lax.broadcasted_iota(jnp.int32, (2 * QB, 2 * QB), 1)).astype(BF16)

    scale = C_HEAD_DIM ** -0.5
    kvs = range(C_KV_HEADS)
    in_cs = [(lane >= c * C_HEAD_DIM) & (lane < (c + 1) * C_HEAD_DIM) for c in kvs]

    chains = range(C_HEADS // 2)
    kv_of = [n * 2 // GROUP for n in chains]

    def q_rows(n):
        c = kv_of[n]
        rows = []
        for h in (2 * n, 2 * n + 1):
            x = q[n]
            if h % 2 != c:
                x = pltpu.roll(x, C_HEAD_DIM, 1)
            rows.append(jnp.where(in_cs[c], x * scale, 0.0))
        return jnp.concatenate(rows, axis=0).astype(BF16)
    qcs = [q_rows(n) for n in chains]

    def att_tile(jp, carry):
        off = pl.multiple_of(jp * (2 * QB), 2 * QB)
        keys = jnp.concatenate([key_s[2 * jp], key_s[2 * jp + 1]], axis=0)
        pos = off + lax.broadcasted_iota(jnp.int32, (2 * QB, LANE), 0)
        tie = keys == thr
        rank = carry[-1] + jnp.dot(tri, jnp.where(tie, 1.0, 0.0).astype(BF16), preferred_element_type=F32)
        sel = ((keys > thr) | (tie & (rank <= need_f))) & (pos <= causal_off[:1, :])
        bias1 = jnp.where(sel, 0.0, NEG)
        bias = jnp.concatenate([bias1, bias1], axis=1)
        k_t = k_s[pl.ds(off, 2 * QB), :]
        s = [lax.dot_general(k_t, qcs[n], (((1,), (1,)), ((), ())), preferred_element_type=F32) for n in chains]
        s = [s[n] + bias for n in chains]
        m_new = [jnp.maximum(carry[2 * n], jnp.max(s[n], axis=0, keepdims=True)) for n in chains]
        p = [jnp.exp(s[n] - m_new[n]).astype(BF16) for n in chains]
        v_t = [vt_s[c, :, pl.ds(off, 2 * QB)] for c in kvs]
        pv = [jnp.dot(v_t[kv_of[n]], p[n], preferred_element_type=F32) for n in chains]
        out = []
        for n in chains:
            out += [m_new[n], carry[2 * n + 1] * jnp.exp(carry[2 * n] - m_new[n]) + pv[n]]
        return tuple(out) + (rank[2 * QB - 1:, :],)
    init = ((jnp.full((1, 2 * QB), NEG, F32), jnp.zeros((VT_ROWS, 2 * QB), F32)) * len(chains)
            + (jnp.zeros((1, QB), F32),))
    res = lax.fori_loop(0, npair, att_tile, init)

    tiles = []
    for n in chains:
        acc = res[2 * n + 1]
        out_t = acc[:C_HEAD_DIM] / acc[C_HEAD_DIM:C_HEAD_DIM + 1]
        tiles.append(jnp.concatenate([out_t[:, :QB], out_t[:, QB:]], axis=0).T)
    o_ref[0] = jnp.concatenate(tiles, axis=1).astype(o_ref.dtype)


def _dsa_kernel_rows(pq_ref, kv_ref, ki_ref, cosq_ref, sinq_ref, cosf_ref, sinf_ref, o_ref,
                     k_s, v_s, ki_s, key_s, *, k_sel, seq):
    i = pl.program_id(1)
    nk = i + 1
    lane = lax.broadcasted_iota(jnp.int32, (QB, LANE), 1)
    rowi = lax.broadcasted_iota(jnp.int32, (QB, LANE), 0)
    low_half = lane < C_HEAD_DIM

    @pl.when(i == 0)
    def _():
        kv = kv_ref[0]
        k_s[...] = _rope(kv[:, :LANE], cosf_ref[...], sinf_ref[...])[0].astype(BF16)
        v_s[...] = kv[:, LANE:].astype(BF16)
        lane_f = lax.broadcasted_iota(jnp.int32, (seq, LANE), 1)
        ki = jnp.where(lane_f < IDX_DIM, ki_ref[0], 0.0)
        ki = _rope(ki, cosf_ref[...], sinf_ref[...])[0]
        ki_s[...] = ki + pltpu.roll(ki, IDX_DIM, 1)

    pq = pq_ref[0]
    cos, sin = cosq_ref[...], sinq_ref[...]
    q = _rope(pq[:, :C_WIDTH], cos, sin)
    qi = _rope(pq[:, C_QI_OFF:C_KI_OFF], cos, sin)
    wslab = pq[:, C_KI_OFF:C_KI_OFF + LANE]
    wscale = (IDX_HEADS ** -0.5) * (IDX_DIM ** -0.5)
    wi = [jnp.sum(jnp.where(lane == IDX_DIM + h, wslab, 0.0), axis=-1, keepdims=True) * wscale
          for h in range(IDX_HEADS)]
    qi_h = [jnp.where(low_half if h % 2 == 0 else ~low_half, qi[h // 2], 0.0) for h in range(IDX_HEADS)]

    def score_tile(j, _):
        ki_t = ki_s[pl.ds(pl.multiple_of(j * QB, QB), QB), :]
        sc = jnp.zeros((QB, QB), F32)
        for h in range(IDX_HEADS):
            d = lax.dot_general(qi_h[h], ki_t, (((1,), (1,)), ((), ())), precision=lax.Precision.HIGHEST,
                                preferred_element_type=F32)
            sc = sc + jnp.maximum(d, 0.0) * wi[h]
        bits = pltpu.bitcast(sc + 0.0, jnp.int32)
        key = jnp.where(bits < 0, bits ^ jnp.int32(0x7fffffff), bits)
        key_s[j] = jnp.where(j * QB + lane <= i * QB + rowi, key, jnp.int32(INT_MIN))
        return 0
    lax.fori_loop(0, nk, score_tile, 0)

    def count(pred):
        def body(j, acc):
            return acc + jnp.where(pred(key_s[j], j), 1, 0)
        return jnp.sum(lax.fori_loop(0, nk, body, jnp.zeros((QB, QB), jnp.int32)), axis=-1, keepdims=True)

    def bit_step(b, cand):
        trial = cand | jnp.left_shift(jnp.int32(1), 31 - b)
        cnt = count(lambda key, j: key >= (trial ^ jnp.int32(INT_MIN)))
        return jnp.where(cnt >= k_sel, trial, cand)
    cand = lax.fori_loop(0, 32, bit_step, jnp.zeros((QB, 1), jnp.int32))
    thr = cand ^ jnp.int32(INT_MIN)
    need = k_sel - count(lambda key, j: key > thr)

    nbits = int(np.ceil(np.log2(seq))) + 1

    def idx_step(b, jj):
        trial = jj | jnp.left_shift(jnp.int32(1), nbits - 1 - b)
        cnt = count(lambda key, j: (key == thr) & (j * QB + lane < trial))
        return jnp.where(cnt <= need, trial, jj)
    jlim = lax.fori_loop(0, nbits, idx_step, jnp.zeros((QB, 1), jnp.int32))

    group = C_HEADS // C_KV_HEADS
    scale = C_HEAD_DIM ** -0.5
    outs = []
    for c in range(C_KV_HEADS):
        in_c = (lane >= c * C_HEAD_DIM) & (lane < (c + 1) * C_HEAD_DIM)
        rows = []
        for g in range(group):
            h = c * group + g
            x = q[h // 2]
            if h % 2 != c:
                x = pltpu.roll(x, C_HEAD_DIM, 1)
            rows.append(jnp.where(in_c, x * scale, 0.0))
        qc = jnp.concatenate(rows, axis=0).astype(BF16)
        ones_lane = (1 - c) * C_HEAD_DIM

        def att_tile(j, carry):
            m, acc = carry
            off = pl.multiple_of(j * QB, QB)
            key = key_s[j]
            sel = ((key > thr) | ((key == thr) & (j * QB + lane < jlim))) & (j * QB + lane <= i * QB + rowi)
            bias = jnp.where(sel, 0.0, NEG)
            s = lax.dot_general(qc, k_s[pl.ds(off, QB), :], (((1,), (1,)), ((), ())),
                                preferred_element_type=F32)
            s = jnp.where(jnp.concatenate([bias] * group, axis=0) < 0.0, NEG, s)
            m_new = jnp.maximum(m, jnp.max(s, axis=-1, keepdims=True))
            alpha = jnp.exp(m - m_new)
            p = jnp.exp(s - m_new)
            v_t = v_s[pl.ds(off, QB), :]
            v_t = jnp.where(in_c, v_t, jnp.where(lane == ones_lane, 1.0, 0.0).astype(BF16))
            acc = acc * alpha + jnp.dot(p.astype(BF16), v_t, preferred_element_type=F32)
            return m_new, acc
        m0 = jnp.full((group * QB, 1), NEG, F32)
        acc0 = jnp.zeros((group * QB, LANE), F32)
        _, acc = lax.fori_loop(0, nk, att_tile, (m0, acc0))
        l = jnp.sum(jnp.where(jnp.concatenate([lane] * group, axis=0) == ones_lane, acc, 0.0),
                    axis=-1, keepdims=True)
        out = acc / l
        for g in range(group):
            h = c * group + g
            x = jnp.where(in_c, out[g * QB:(g + 1) * QB], 0.0)
            if h % 2 != c:
                x = pltpu.roll(x, C_HEAD_DIM, 1)
            outs.append(x)
    o_ref[0] = jnp.concatenate([outs[2 * m] + outs[2 * m + 1] for m in range(C_HEADS // 2)],
                               axis=1).astype(o_ref.dtype)


def _rope_lanes(seq):
    inv_freq = ROPE_THETA ** (-jnp.arange(HALF, dtype=F32) * (2.0 / C_HEAD_DIM))
    ang = jnp.arange(seq, dtype=F32)[:, None] * inv_freq[None, :]
    cos, sin = jnp.cos(ang), jnp.sin(ang)
    reps = LANE // C_HEAD_DIM
    return (jnp.tile(jnp.concatenate([cos, cos], axis=1), (1, reps)),
            jnp.tile(jnp.concatenate([-sin, sin], axis=1), (1, reps)))


def dsa(p):
    bsz, seq, width = p.shape
    assert seq % QW == 0 and width == C_PAD
    k_sel = min(TOPK_MAX, seq // 4)
    cos, sin = _rope_lanes(seq)
    return pl.pallas_call(
        functools.partial(_dsa_kernel, k_sel=k_sel, seq=seq),
        grid=(bsz, seq // QW),
        in_specs=[pl.BlockSpec((1, QW, C_PAD), lambda b, i: (b, i, 0)),
                  pl.BlockSpec((1, seq, 2 * C_KV_W), lambda b, i: (b, 0, C_K_OFF // (2 * C_KV_W))),
                  pl.BlockSpec((1, seq, LANE), lambda b, i: (b, 0, C_KI_OFF // LANE)),
                  pl.BlockSpec((QW, LANE), lambda b, i: (i, 0)),
                  pl.BlockSpec((QW, LANE), lambda b, i: (i, 0)),
                  pl.BlockSpec((seq, LANE), lambda b, i: (0, 0)),
                  pl.BlockSpec((seq, LANE), lambda b, i: (0, 0))],
        out_specs=pl.BlockSpec((1, QW, C_WIDTH), lambda b, i: (b, i, 0)),
        out_shape=jax.ShapeDtypeStruct((bsz, seq, C_WIDTH), BF16),
        scratch_shapes=[pltpu.VMEM((seq, LANE), BF16), pltpu.VMEM((C_KV_HEADS, VT_ROWS, seq), BF16),
                        pltpu.VMEM((seq, LANE), BF16), pltpu.VMEM((seq // QB, QB, QW), jnp.int32)],
        compiler_params=pltpu.CompilerParams(
            dimension_semantics=("arbitrary", "arbitrary"), vmem_limit_bytes=VMEM_LIMIT),
        name="dsa",
    )(p, p, p, cos, sin, cos, sin)


def _split(t, sizes):
    out, start = [], 0
    for s in sizes:
        out.append(t[..., start:start + s])
        start += s
    return out


def _mm(x, y):
    return jnp.einsum('...ij,...jk->...ik', x.astype(BF16), y.astype(BF16), preferred_element_type=F32)


def _mm_hi(x, y):
    return jnp.einsum('...ij,...jk->...ik', x, y, precision=lax.Precision.HIGHEST)


def _rwkv7_chunked(r, logw, k, v, a, b, C=64):
    B, S, H, N = r.shape
    nc = S // C

    def ch(t):
        return t.reshape(B, nc, C, H, N).transpose(1, 0, 3, 2, 4)
    r, logw, k, v, a, b = map(ch, (r, logw, k, v, a, b))
    cum = jnp.cumsum(logw, axis=-2)
    cum_prev = cum - logw
    cum_end = cum[..., -1:, :]
    At = a * jnp.exp(cum_prev)
    Rt = r * jnp.exp(cum)
    Bt = b * jnp.exp(-cum)
    Kt = k * jnp.exp(-cum)
    Bh = b * jnp.exp(cum_end - cum)
    Kh = k * jnp.exp(cum_end - cum)
    Pc = jnp.exp(cum_end)[..., 0, :]
    T_ = lambda x: jnp.swapaxes(x, -1, -2)
    ti = jnp.arange(C)
    strict = ti[:, None] > ti[None, :]
    incl = ti[:, None] >= ti[None, :]
    Aab = jnp.where(strict, _mm(At, T_(Bt)), 0.0)
    Aak = jnp.where(strict, _mm(At, T_(Kt)), 0.0)
    Arb = jnp.where(incl, _mm(Rt, T_(Bt)), 0.0)
    Ark = jnp.where(incl, _mm(Rt, T_(Kt)), 0.0)
    Tm = jnp.eye(C, dtype=F32) + Aab
    Pw = Aab
    n = 1
    while 2 * n < C:
        Pw = _mm_hi(Pw, Pw)
        Tm = Tm + _mm_hi(Tm, Pw)
        n *= 2
    Abar = _mm(Tm, At)
    U0 = _mm(Tm, _mm(Aak, v))
    Rbar = Rt + _mm(Arb, Abar)
    Y0 = _mm(Arb, U0) + _mm(Ark, v)
    M = Pc[..., :, None] * jnp.eye(N, dtype=F32) + _mm(T_(Bh), Abar)
    Hd = _mm(T_(Bh), U0) + _mm(T_(Kh), v)

    def step(Hs, inp):
        Rbar_c, Y0_c, M_c, Hd_c = inp
        y = _mm(Rbar_c, Hs) + Y0_c
        Hs = _mm_hi(M_c, Hs) + Hd_c
        return Hs, y
    _, ys = lax.scan(step, jnp.zeros((B, H, N, N), F32), (Rbar, Y0, M, Hd))
    return ys.transpose(1, 0, 3, 2, 4).reshape(B, S, H, N)


def _token_shift(p):
    return jnp.pad(p, ((0, 0), (1, 0), (0, 0)))[:, :-1]


def _rwkv7_branch(p, mu, w0, w_up, a0, a_up, g_up, k_k, k_a, r_k, gn_g, gn_b):
    bsz, seq, _ = p.shape
    p = p + (_token_shift(p) - p) * mu
    r, k, v, wd, ad, gd = _split(p, (A_WIDTH, A_WIDTH, A_WIDTH, A_RANK_W, A_RANK_A, A_RANK_G))
    w_log = -jax.nn.softplus(-(w0 + _mm(jnp.tanh(wd), w_up))) - 0.5
    logw = -jnp.exp(w_log)
    a = jax.nn.sigmoid(a0 + _mm(ad, a_up))
    g = _mm(jax.nn.sigmoid(gd), g_up)

    def hd(t):
        return t.reshape(bsz, seq, A_HEADS, A_HEAD_DIM)
    kk = hd(k * k_k)
    kk = kk * lax.rsqrt(jnp.maximum(jnp.sum(kk * kk, axis=-1, keepdims=True), 1e-24))
    k_mod = hd(k * (1.0 + (a - 1.0) * k_a))
    r_h, v_h, a_h = hd(r), hd(v), hd(a)
    y = _rwkv7_chunked(r_h, hd(logw), k_mod, v_h, -kk, kk * a_h)
    mean = jnp.mean(y, axis=-1, keepdims=True)
    var = jnp.mean(jnp.square(y - mean), axis=-1, keepdims=True)
    y = ((y - mean) * lax.rsqrt(var + A_GN_EPS)).reshape(bsz, seq, A_WIDTH)
    y = y * gn_g + gn_b
    bonus = jnp.sum(r_h * k_mod * r_k, axis=-1, keepdims=True) * v_h
    y = y + bonus.reshape(bsz, seq, A_WIDTH)
    return y * g


def _hgrn2_sub(q, k, g, v, C=64, SB=16):
    B, S, H, dk = q.shape
    dv = v.shape[-1]
    nc = S // C

    def ch(t):
        return t.reshape(B, nc, C, H, t.shape[-1]).transpose(1, 0, 3, 2, 4)
    q, k, g, v = map(ch, (q, k, g, v))
    b = jnp.cumsum(g, axis=-2)
    nsb = C // SB
    T_ = lambda x: jnp.swapaxes(x, -1, -2)

    def step(state, inp):
        qc, kc, bc, vc = inp
        rows = []
        for I in range(nsb):
            sl = slice(I * SB, (I + 1) * SB)
            qI, bI = qc[..., sl, :], bc[..., sl, :]
            b0 = bc[..., I * SB - 1:I * SB, :] if I > 0 else jnp.zeros_like(bc[..., :1, :])
            qt = qI * jnp.exp(bI - b0)
            blocks = []
            if I > 0:
                kt = kc[..., :I * SB, :] * jnp.exp(b0 - bc[..., :I * SB, :])
                blocks.append(_mm(qt, T_(kt)))
            diff = bI[..., :, None, :] - bI[..., None, :, :]
            tri = jnp.tril(jnp.ones((SB, SB), bool))
            dec = jnp.exp(jnp.where(tri[:, :, None], diff, -jnp.inf))
            diag = jnp.sum(qI[..., :, None, :] * kc[..., sl, :][..., None, :, :] * dec, axis=-1)
            blocks.append(diag)
            if I < nsb - 1:
                blocks.append(jnp.zeros(diag.shape[:-1] + (C - (I + 1) * SB,), F32))
            rows.append(jnp.concatenate(blocks, axis=-1))
        scores = jnp.concatenate(rows, axis=-2)
        o = _mm(scores, vc) + _mm(qc * jnp.exp(bc), state)
        b_end = bc[..., -1:, :]
        state = state * T_(jnp.exp(b_end)) + _mm(T_(kc * jnp.exp(b_end - bc)), vc)
        return state, o
    _, o = lax.scan(step, jnp.zeros((B, H, dk, dv), F32), (q, k, b, v))
    return o.transpose(1, 0, 3, 2, 4).reshape(B, S, H, dv)


def _hgrn2_branch(p, lb, gn_g):
    bsz, seq, _ = p.shape
    q, f, i, g = _split(p, (B_WIDTH,) * 4)
    log_f = jnp.logaddexp(jnp.log(lb), jnp.log1p(-lb) + jax.nn.log_sigmoid(f))
    k_in = (1.0 - lb) * jax.nn.sigmoid(-f)
    hd = lambda t: t.reshape(bsz, seq, B_HEADS, -1)
    o = _hgrn2_sub(hd(q), hd(k_in), hd(log_f), hd(i))
    o = o * lax.rsqrt(jnp.mean(o * o, axis=-1, keepdims=True) + NORM_EPS)
    return o.reshape(bsz, seq, B_WIDTH) * gn_g * jax.nn.silu(g)


def _rope_tables(seq, dim):
    half = dim // 2
    inv_freq = ROPE_THETA ** (-jnp.arange(half, dtype=F32) * (2.0 / dim))
    ang = jnp.arange(seq, dtype=F32)[:, None] * inv_freq[None, :]
    return jnp.cos(ang), jnp.sin(ang)


def _apply_rope(t, cos, sin):
    half = t.shape[-1] // 2
    t1, t2 = t[..., :half], t[..., half:]
    c, s = cos[None, :, None, :], sin[None, :, None, :]
    return jnp.concatenate([t1 * c - t2 * s, t2 * c + t1 * s], axis=-1)


def _dsa_branch(p, cos, sin):
    bsz, seq, _ = p.shape
    k_sel = min(TOPK_MAX, seq // 4)
    kv_w = C_KV_HEADS * C_HEAD_DIM
    q, k, v, qi, ki, wi = _split(p[..., :C_COLS], (C_WIDTH, kv_w, kv_w, IDX_HEADS * IDX_DIM, IDX_DIM, IDX_HEADS))
    q = _apply_rope(q.reshape(bsz, seq, C_HEADS, C_HEAD_DIM), cos, sin)
    k = _apply_rope(k.reshape(bsz, seq, C_KV_HEADS, C_HEAD_DIM), cos, sin)
    v = v.reshape(bsz, seq, C_KV_HEADS, C_HEAD_DIM)
    qi = _apply_rope(qi.reshape(bsz, seq, IDX_HEADS, IDX_DIM), cos, sin)
    ki = _apply_rope(ki[:, :, None, :], cos, sin)[:, :, 0, :]
    wi = wi * (IDX_HEADS ** -0.5) * (IDX_DIM ** -0.5)
    dots = jnp.einsum('bqhd,bsd->bqhs', qi, ki, precision=lax.Precision.HIGHEST)
    score = jnp.sum(jax.nn.relu(dots) * wi[..., None], axis=2)
    pos = jnp.arange(seq)
    causal = pos[None, :] <= pos[:, None]
    bits = lax.bitcast_convert_type(score + 0.0, jnp.int32)
    key = jnp.where(bits < 0, bits ^ jnp.int32(0x7fffffff), bits)
    key = jnp.where(causal[None], key, jnp.int32(-2 ** 31))
    ukey = lax.bitcast_convert_type(key ^ jnp.int32(-2 ** 31), jnp.uint32)
    cand = jnp.zeros((bsz, seq, 1), jnp.uint32)
    for bit in range(31, -1, -1):
        trial = cand | jnp.uint32(1 << bit)
        cnt = jnp.sum((ukey >= trial).astype(jnp.int32), axis=-1, keepdims=True)
        cand = jnp.where(cnt >= k_sel, trial, cand)
    gt = ukey > cand
    eq = ukey == cand
    need = k_sel - jnp.sum(gt.astype(jnp.int32), axis=-1, keepdims=True)
    J = jnp.zeros((bsz, seq, 1), jnp.int32)
    for bit in range(int(np.ceil(np.log2(seq))), -1, -1):
        trial = J | (1 << bit)
        cnt = jnp.sum((eq & (pos[None, None, :] < trial)).astype(jnp.int32), axis=-1, keepdims=True)
        J = jnp.where(cnt <= need, trial, J)
    sel = (gt | (eq & (pos[None, None, :] < J))) & causal[None]
    group = C_HEADS // C_KV_HEADS
    qg = q.reshape(bsz, seq, C_KV_HEADS, group, C_HEAD_DIM)
    logits = jnp.einsum('bqcgd,bscd->bcgqs', qg.astype(BF16), k.astype(BF16),
                        preferred_element_type=F32) * (C_HEAD_DIM ** -0.5)
    logits = jnp.where(sel[:, None, None], logits, -jnp.inf)
    prob = jax.nn.softmax(logits, axis=-1)
    out = jnp.einsum('bcgqs,bscd->bqcgd', prob.astype(BF16), v.astype(BF16), preferred_element_type=F32)
    return out.reshape(bsz, seq, C_WIDTH)


def kernel(x, norm_mix_g, w_in, a_mu, a_w0, a_w_up, a_a0, a_a_up, a_g_up, a_k_k, a_k_a, a_r_k, a_gn_g,
           a_gn_b, b_lb_logits, b_gn_g, w_branch, w_o, norm_ffn_g, w_up, conv_w, conv_b, w_down,
           norm_final_g):
    bsz, seq, d = x.shape
    t = bsz * seq
    lb_cum = jnp.cumsum(jax.nn.softmax(b_lb_logits.astype(F32), axis=0), axis=0)
    lower_bounds = lb_cum - lb_cum[0]
    h = x
    for layer in range(DEPTH):
        w = w_in[layer].astype(BF16)
        w_a, w_b, w_c, w_g = _split(w, (A_COLS, B_COLS, C_COLS, GATE_COLS))
        w_c = jnp.pad(w_c, ((0, 0), (0, C_PAD - C_COLS)))
        h2 = h.reshape(t, d)
        g = norm_mix_g[layer]
        p_a = norm_matmul(h2, g, w_a, tm=2048, tn=A_COLS // 2, out_dtype=BF16).reshape(bsz, seq, -1)
        p_b = norm_matmul(h2, g, w_b, tm=2048, tn=1024, out_dtype=BF16).reshape(bsz, seq, -1)
        p_c = norm_matmul(h2, g, w_c, tm=2048, tn=C_PAD, out_dtype=F32).reshape(bsz, seq, -1)
        p_g = norm_matmul(h2, g, w_g, tm=2048, tn=1024, out_dtype=BF16)
        y_a = rwkv7(p_a, a_mu[layer], a_w0[layer], a_w_up[layer], a_a0[layer], a_a_up[layer],
                    a_g_up[layer], a_k_k[layer], a_k_a[layer], a_r_k[layer], a_gn_g[layer], a_gn_b[layer])
        y_b = hgrn2(p_b, lower_bounds[layer], b_gn_g[layer])
        y_c = dsa(p_c)
        wbr = w_branch[layer].astype(BF16)
        h2 = merge((y_a.reshape(t, -1), y_b.reshape(t, -1), y_c.reshape(t, -1)), p_g, h2,
                   (wbr[0], wbr[1], wbr[2]), w_o[layer].astype(BF16), tm=512)
        h = conv_glu(h2.reshape(bsz, seq, d), norm_ffn_g[layer], w_up[layer].astype(BF16), conv_w[layer],
                     conv_b[layer], w_down[layer].astype(BF16),
                     norm_final_g if layer == DEPTH - 1 else None, tm=512, tf=1408)
    return h
```

```python
import functools

import jax
import jax.numpy as jnp
import numpy as np
from jax import lax
from jax.experimental import pallas as pl
from jax.experimental.pallas import tpu as pltpu

F32 = jnp.float32
BF16 = jnp.bfloat16

D_MODEL = 1024
DEPTH = 2
MIX_WIDTH = 512
N_BRANCH = 3
A_HEAD_DIM = 64
A_HEADS = 8
A_WIDTH = 512
A_RANK_W = 64
A_RANK_A = 64
A_RANK_G = 128
A_GN_EPS = 64e-5
B_KEY_DIM = 128
B_VAL_DIM = 128
B_HEADS = 4
B_WIDTH = 512
C_HEAD_DIM = 64
C_HEADS = 8
C_KV_HEADS = 2
C_WIDTH = 512
IDX_HEADS = 4
IDX_DIM = 64
TOPK_MAX = 256
ROPE_THETA = 10000.0
D_FF = 2816
CONV_WIDTH = 3
NORM_EPS = 1e-6

A_COLS = 3 * A_WIDTH + A_RANK_W + A_RANK_A + A_RANK_G
B_COLS = 4 * B_WIDTH
C_COLS = C_WIDTH + 2 * C_KV_HEADS * C_HEAD_DIM + IDX_HEADS * IDX_DIM + IDX_DIM + IDX_HEADS
GATE_COLS = N_BRANCH * D_MODEL

LANE = 128
SUBLANE = 8
VMEM_LIMIT = 48 * 1024 * 1024


def _round_up(n, m):
    return (n + m - 1) // m * m


def _rms(x, g):
    return x * lax.rsqrt(jnp.mean(x * x, axis=-1, keepdims=True) + NORM_EPS) * g


def _norm_matmul_kernel(x_ref, g_ref, w_ref, o_ref, xn_ref):
    @pl.when(pl.program_id(1) == 0)
    def _():
        xn_ref[...] = _rms(x_ref[...], g_ref[...]).astype(BF16)

    o_ref[...] = jnp.dot(xn_ref[...], w_ref[...], preferred_element_type=F32).astype(o_ref.dtype)


def norm_matmul(x, g, w, *, tm, tn, out_dtype):
    t, d = x.shape
    n = w.shape[1]
    assert t % tm == 0 and n % tn == 0
    return pl.pallas_call(
        _norm_matmul_kernel,
        grid=(t // tm, n // tn),
        in_specs=[
            pl.BlockSpec((tm, d), lambda i, j: (i, 0)),
            pl.BlockSpec((1, d), lambda i, j: (0, 0)),
            pl.BlockSpec((d, tn), lambda i, j: (0, j)),
        ],
        out_specs=pl.BlockSpec((tm, tn), lambda i, j: (i, j)),
        out_shape=jax.ShapeDtypeStruct((t, n), out_dtype),
        scratch_shapes=[pltpu.VMEM((tm, d), BF16)],
        compiler_params=pltpu.CompilerParams(
            dimension_semantics=("arbitrary", "arbitrary"), vmem_limit_bytes=VMEM_LIMIT),
        name="norm_matmul",
    )(x, g.reshape(1, d), w)


def _merge_kernel(ya_ref, yb_ref, yc_ref, gate_ref, h_ref, wa_ref, wb_ref, wc_ref, wo_ref, o_ref):
    merged = None
    for b, (y_ref, w_ref) in enumerate(((ya_ref, wa_ref), (yb_ref, wb_ref), (yc_ref, wc_ref))):
        br = jnp.dot(y_ref[...].astype(BF16), w_ref[...], preferred_element_type=F32)
        gt = jax.nn.sigmoid(gate_ref[:, b * D_MODEL:(b + 1) * D_MODEL].astype(F32))
        merged = gt * br if merged is None else merged + gt * br
    o_ref[...] = h_ref[...] + jnp.dot(merged.astype(BF16), wo_ref[...], preferred_element_type=F32)


def merge(ys, gate, h, ws, w_o, *, tm):
    t = h.shape[0]
    assert t % tm == 0
    row = lambda w: pl.BlockSpec((tm, w), lambda i: (i, 0))
    const = lambda a: pl.BlockSpec(a.shape, lambda i: (0, 0))
    return pl.pallas_call(
        _merge_kernel,
        grid=(t // tm,),
        in_specs=[row(y.shape[1]) for y in ys] + [row(GATE_COLS), row(D_MODEL)]
                 + [const(w) for w in ws] + [const(w_o)],
        out_specs=row(D_MODEL),
        out_shape=jax.ShapeDtypeStruct((t, D_MODEL), F32),
        compiler_params=pltpu.CompilerParams(
            dimension_semantics=("arbitrary",), vmem_limit_bytes=VMEM_LIMIT),
        name="merge",
    )(*ys, gate, h, *ws, w_o)


def _shift_rows(cur, prev_tail, shift):
    rolled = pltpu.roll(cur, shift, 0)
    tail = pltpu.roll(prev_tail, shift, 0)
    row = lax.broadcasted_iota(jnp.int32, tail.shape, 0)
    top = jnp.where(row < shift, tail, rolled[:SUBLANE])
    return jnp.concatenate([top, rolled[SUBLANE:]], axis=0)


def _conv_glu_kernel(h_ref, g_ref, wug_ref, wuv_ref, cwg_ref, cwv_ref, cbg_ref, cbv_ref, wd_ref, *rest,
                     final_norm):
    if final_norm:
        gf_ref, o_ref, xn_ref, acc_ref, carry_ref = rest
    else:
        o_ref, xn_ref, acc_ref, carry_ref = rest
    ti = pl.program_id(1)
    fi = pl.program_id(2)
    nf = pl.num_programs(2)

    @pl.when(fi == 0)
    def _():
        xn_ref[...] = _rms(h_ref[0], g_ref[...]).astype(BF16)
        acc_ref[...] = jnp.zeros_like(acc_ref)

    @pl.when(ti == 0)
    def _():
        carry_ref[fi] = jnp.zeros(carry_ref.shape[1:], F32)

    xn = xn_ref[...]

    def conv(w_ref, cw_ref, cb_ref, slot):
        up = jnp.dot(xn, w_ref[...], preferred_element_type=F32)
        tail = carry_ref[fi, slot]
        c = (cb_ref[...] + cw_ref[2:3, :] * up + cw_ref[1:2, :] * _shift_rows(up, tail, 1)
             + cw_ref[0:1, :] * _shift_rows(up, tail, 2))
        carry_ref[fi, slot] = up[up.shape[0] - SUBLANE:]
        return c

    cg = conv(wug_ref, cwg_ref, cbg_ref, 0)
    cv = conv(wuv_ref, cwv_ref, cbv_ref, 1)
    act = (cg * jax.nn.sigmoid(cg) * cv).astype(BF16)
    acc_ref[...] += jnp.dot(act, wd_ref[...], preferred_element_type=F32)

    @pl.when(fi == nf - 1)
    def _():
        out = h_ref[0] + acc_ref[...]
        if final_norm:
            out = _rms(out, gf_ref[...])
        o_ref[0] = out


def conv_glu(h, g, w_up, conv_w, conv_b, w_down, gf, *, tm, tf):
    bsz, seq, d = h.shape
    assert seq % tm == 0 and D_FF % tf == 0
    nf = D_FF // tf
    final_norm = gf is not None
    vec = lambda: pl.BlockSpec((1, d), lambda b, t, f: (0, 0))
    in_specs = [
        pl.BlockSpec((1, tm, d), lambda b, t, f: (b, t, 0)),
        vec(),
        pl.BlockSpec((d, tf), lambda b, t, f: (0, f)),
        pl.BlockSpec((d, tf), lambda b, t, f: (0, f + nf)),
        pl.BlockSpec((CONV_WIDTH, tf), lambda b, t, f: (0, f)),
        pl.BlockSpec((CONV_WIDTH, tf), lambda b, t, f: (0, f + nf)),
        pl.BlockSpec((1, tf), lambda b, t, f: (0, f)),
        pl.BlockSpec((1, tf), lambda b, t, f: (0, f + nf)),
        pl.BlockSpec((tf, d), lambda b, t, f: (f, 0)),
    ]
    args = [h, g.reshape(1, d), w_up, w_up, conv_w, conv_w, conv_b.reshape(1, -1), conv_b.reshape(1, -1), w_down]
    if final_norm:
        in_specs.append(vec())
        args.append(gf.reshape(1, d))
    return pl.pallas_call(
        functools.partial(_conv_glu_kernel, final_norm=final_norm),
        grid=(bsz, seq // tm, nf),
        in_specs=in_specs,
        out_specs=pl.BlockSpec((1, tm, d), lambda b, t, f: (b, t, 0)),
        out_shape=jax.ShapeDtypeStruct((bsz, seq, d), F32),
        scratch_shapes=[pltpu.VMEM((tm, d), BF16), pltpu.VMEM((tm, d), F32),
                        pltpu.VMEM((nf, 2, SUBLANE, tf), F32)],
        compiler_params=pltpu.CompilerParams(
            dimension_semantics=("arbitrary", "arbitrary", "arbitrary"), vmem_limit_bytes=VMEM_LIMIT),
        name="conv_glu",
    )(*args)


HP = LANE
A_PADW = A_HEADS * HP
A_CHUNK = 64
A_LOW = A_RANK_W + A_RANK_A + A_RANK_G
A_PCOLS = 3 * A_PADW + A_LOW


def _dot(x, y):
    return jnp.dot(x.astype(BF16), y.astype(BF16), preferred_element_type=F32)


def _dot_t(x, y):
    return lax.dot_general(x.astype(BF16), y.astype(BF16), (((1,), (1,)), ((), ())), preferred_element_type=F32)


A_NB = 2


def _rwkv_body(p_ref, mu_ref, w0_ref, a0_ref, wa_ref, gup_ref, kk_ref, ka_ref, rk_ref, gng_ref, gnb_ref,
               o_ref, carry_ref, h_ref, tick):
    C = A_CHUNK
    W = A_PADW
    NB = A_NB
    R = NB * C

    @pl.when(pl.program_id(1) == 0)
    def _():
        carry_ref[...] = jnp.zeros_like(carry_ref)
        h_ref[...] = jnp.zeros_like(h_ref)

    blocks = []
    for s in range(NB):
        ps = p_ref[s].astype(F32)
        prev = _shift_rows(ps, carry_ref[s], 1)
        carry_ref[s] = ps[C - SUBLANE:]
        blocks.append(ps + (prev - ps) * mu_ref[...])
    p = jnp.concatenate(blocks, axis=0)
    lane = lax.broadcasted_iota(jnp.int32, (R, LANE), 1)
    valid = lane < A_HEAD_DIM
    valid_c = lax.broadcasted_iota(jnp.int32, (C, LANE), 1) < A_HEAD_DIM

    def pad_heads(x):
        slabs = []
        for h in range(A_HEADS):
            t = x[:, (h // 2) * LANE:(h // 2 + 1) * LANE]
            if h % 2:
                t = pltpu.roll(t, A_HEAD_DIM, 1)
            slabs.append(jnp.where(valid, t, 0.0))
        return jnp.concatenate(slabs, axis=1)
    r, k, v = (pad_heads(p[:, i * A_WIDTH:(i + 1) * A_WIDTH]) for i in range(3))
    low = p[:, 3 * A_WIDTH:3 * A_WIDTH + LANE]
    gd = p[:, 3 * A_WIDTH + LANE:]
    low = jnp.where(lane < A_RANK_W, jnp.tanh(low), low)
    wa = _dot(low, wa_ref[...])
    w_log = -jax.nn.softplus(-(w0_ref[...] + wa[:, :W])) - 0.5
    logw = -jnp.exp(w_log)
    a = jax.nn.sigmoid(a0_ref[...] + wa[:, W:])
    g = _dot(jax.nn.sigmoid(gd), gup_ref[...])
    k_mod = k * (1.0 + (a - 1.0) * ka_ref[...])
    kk = k * kk_ref[...]
    tick()

    row = lax.broadcasted_iota(jnp.int32, (C, C), 0)
    col = lax.broadcasted_iota(jnp.int32, (C, C), 1)
    incl = row >= col
    strict = row > col
    rowr = lax.broadcasted_iota(jnp.int32, (R, R), 0)
    colr = lax.broadcasted_iota(jnp.int32, (R, R), 1)
    same_seq = (rowr // C) == (colr // C)
    cum = jnp.dot((same_seq & (rowr >= colr)).astype(F32), logw, precision=lax.Precision.HIGHEST,
                  preferred_element_type=F32)
    cum_end = [cum[(s + 1) * C - 1:(s + 1) * C, :] for s in range(NB)]
    e_neg = jnp.exp(-cum)
    e_end = jnp.concatenate([jnp.exp(cum_end[s] - cum[s * C:(s + 1) * C]) for s in range(NB)], axis=0)
    r_t = r * jnp.exp(cum)
    a_scale = jnp.exp(cum - logw)
    p_end = [jnp.exp(cum_end[s]) for s in range(NB)]
    eye = (lax.broadcasted_iota(jnp.int32, (HP, HP), 0) == lax.broadcasted_iota(jnp.int32, (HP, HP), 1)).astype(F32)
    eye_c = (row == col).astype(F32)

    chains = [(s, h) for s in range(NB) for h in range(A_HEADS)]
    nch = range(len(chains))

    def blk(x, n):
        s, h = chains[n]
        return x[s * C:(s + 1) * C, h * HP:(h + 1) * HP]
    def each(f):
        out = [f(n) for n in nch]
        tick()
        return out
    kk_n = each(lambda n: blk(kk, n) * lax.rsqrt(jnp.maximum(
        jnp.sum(blk(kk, n) * blk(kk, n), axis=-1, keepdims=True), 1e-24)))
    b_h = each(lambda n: kk_n[n] * blk(a, n))
    at = each(lambda n: -kk_n[n] * blk(a_scale, n))
    ar = each(lambda n: jnp.concatenate([at[n], blk(r_t, n)], axis=0))
    ab = each(lambda n: _dot_t(ar[n], b_h[n] * blk(e_neg, n)))
    ak = each(lambda n: _dot_t(ar[n], blk(k_mod, n) * blk(e_neg, n)))
    a_ab = each(lambda n: jnp.where(strict, ab[n][:C], 0.0))
    a_rb = each(lambda n: jnp.where(incl, ab[n][C:], 0.0))
    a_ak = each(lambda n: jnp.where(strict, ak[n][:C], 0.0))
    a_rk = each(lambda n: jnp.where(incl, ak[n][C:], 0.0))
    tm = each(lambda n: eye_c + a_ab[n])
    pw = each(lambda n: _dot(a_ab[n], a_ab[n]))
    akv = each(lambda n: _dot(a_ak[n], blk(v, n)))
    m = 2
    while 2 * m < C:
        x = each(lambda n: _dot(jnp.concatenate([tm[n], pw[n]], axis=0), pw[n]))
        tm = each(lambda n: tm[n] + x[n][:C])
        pw = each(lambda n: x[n][C:])
        m *= 2
    tm = each(lambda n: tm[n] + _dot(tm[n], pw[n]))
    x2 = each(lambda n: _dot(tm[n], jnp.concatenate([at[n], akv[n]], axis=1)))
    x3 = each(lambda n: _dot(a_rb[n], x2[n]))
    x4 = each(lambda n: _dot((b_h[n] * blk(e_end, n)).T, x2[n]))
    rkv = each(lambda n: _dot(a_rk[n], blk(v, n)))
    kv = each(lambda n: _dot((blk(k_mod, n) * blk(e_end, n)).T, blk(v, n)))
    rm = each(lambda n: jnp.concatenate([blk(r_t, n) + x3[n][:, :HP],
                                         eye * p_end[chains[n][0]][:, chains[n][1] * HP:(chains[n][1] + 1) * HP]
                                         + x4[n][:, :HP]], axis=0))
    x5 = each(lambda n: _dot(rm[n], h_ref[n]))
    for s in range(NB):
        outs = []
        for h in range(A_HEADS):
            n = s * A_HEADS + h
            sl = slice(h * HP, (h + 1) * HP)
            h_ref[n] = x5[n][C:] + x4[n][:, HP:] + kv[n]
            y = x5[n][:C] + x3[n][:, HP:] + rkv[n]
            mean = jnp.sum(y, axis=-1, keepdims=True) * (1.0 / A_HEAD_DIM)
            dlt = jnp.where(valid_c, y - mean, 0.0)
            var = jnp.sum(dlt * dlt, axis=-1, keepdims=True) * (1.0 / A_HEAD_DIM)
            yn = dlt * lax.rsqrt(var + A_GN_EPS) * gng_ref[:, sl] + gnb_ref[:, sl]
            bonus = jnp.sum(blk(r, n) * blk(k_mod, n) * rk_ref[:, sl], axis=-1, keepdims=True) * blk(v, n)
            outs.append((yn + bonus) * blk(g, n))
        o_ref[s] = jnp.concatenate(
            [outs[2 * i] + pltpu.roll(outs[2 * i + 1], A_HEAD_DIM, 1) for i in range(A_HEADS // 2)],
            axis=1).astype(o_ref.dtype)


def _rwkv_kernel_one(p_ref, mu_ref, w0_ref, a0_ref, wa_ref, gup_ref, kk_ref, ka_ref, rk_ref, gng_ref, gnb_ref,
                     o_ref, carry_ref, h_ref):
    C = A_CHUNK
    W = A_PADW

    @pl.when(pl.program_id(1) == 0)
    def _():
        carry_ref[...] = jnp.zeros_like(carry_ref)
        h_ref[...] = jnp.zeros_like(h_ref)

    p = p_ref[0].astype(F32)
    prev = _shift_rows(p, carry_ref[...], 1)
    carry_ref[...] = p[C - SUBLANE:]
    p = p + (prev - p) * mu_ref[...]
    lane = lax.broadcasted_iota(jnp.int32, (C, LANE), 1)
    valid = lane < A_HEAD_DIM

    def pad_heads(x):
        slabs = []
        for h in range(A_HEADS):
            s = x[:, (h // 2) * LANE:(h // 2 + 1) * LANE]
            if h % 2:
                s = pltpu.roll(s, A_HEAD_DIM, 1)
            slabs.append(jnp.where(valid, s, 0.0))
        return jnp.concatenate(slabs, axis=1)
    r, k, v = (pad_heads(p[:, i * A_WIDTH:(i + 1) * A_WIDTH]) for i in range(3))
    low = p[:, 3 * A_WIDTH:3 * A_WIDTH + LANE]
    gd = p[:, 3 * A_WIDTH + LANE:]
    low = jnp.where(lane < A_RANK_W, jnp.tanh(low), low)
    wa = _dot(low, wa_ref[...])
    w_log = -jax.nn.softplus(-(w0_ref[...] + wa[:, :W])) - 0.5
    logw = -jnp.exp(w_log)
    a = jax.nn.sigmoid(a0_ref[...] + wa[:, W:])
    g = _dot(jax.nn.sigmoid(gd), gup_ref[...])
    k_mod = k * (1.0 + (a - 1.0) * ka_ref[...])
    kk = k * kk_ref[...]

    row = lax.broadcasted_iota(jnp.int32, (C, C), 0)
    col = lax.broadcasted_iota(jnp.int32, (C, C), 1)
    incl = row >= col
    strict = row > col
    cum = jnp.dot(incl.astype(F32), logw, precision=lax.Precision.HIGHEST, preferred_element_type=F32)
    cum_end = cum[C - 1:C, :]
    e_neg = jnp.exp(-cum)
    e_end = jnp.exp(cum_end - cum)
    r_t = r * jnp.exp(cum)
    a_scale = jnp.exp(cum - logw)
    p_end = jnp.exp(cum_end)
    eye = (lax.broadcasted_iota(jnp.int32, (HP, HP), 0) == lax.broadcasted_iota(jnp.int32, (HP, HP), 1)).astype(F32)
    eye_c = (row == col).astype(F32)

    heads = range(A_HEADS)
    sls = [slice(h * HP, (h + 1) * HP) for h in heads]
    each = lambda f: [f(h) for h in heads]
    kk_n = each(lambda h: kk[:, sls[h]] * lax.rsqrt(jnp.maximum(
        jnp.sum(kk[:, sls[h]] * kk[:, sls[h]], axis=-1, keepdims=True), 1e-24)))
    b_h = each(lambda h: kk_n[h] * a[:, sls[h]])
    at = each(lambda h: -kk_n[h] * a_scale[:, sls[h]])
    ar = each(lambda h: jnp.concatenate([at[h], r_t[:, sls[h]]], axis=0))
    ab = each(lambda h: _dot_t(ar[h], b_h[h] * e_neg[:, sls[h]]))
    ak = each(lambda h: _dot_t(ar[h], k_mod[:, sls[h]] * e_neg[:, sls[h]]))
    a_ab = each(lambda h: jnp.where(strict, ab[h][:C], 0.0))
    a_rb = each(lambda h: jnp.where(incl, ab[h][C:], 0.0))
    a_ak = each(lambda h: jnp.where(strict, ak[h][:C], 0.0))
    a_rk = each(lambda h: jnp.where(incl, ak[h][C:], 0.0))
    tm = each(lambda h: eye_c + a_ab[h])
    pw = each(lambda h: _dot(a_ab[h], a_ab[h]))
    akv = each(lambda h: _dot(a_ak[h], v[:, sls[h]]))
    n = 2
    while 2 * n < C:
        x = each(lambda h: _dot(jnp.concatenate([tm[h], pw[h]], axis=0), pw[h]))
        tm = each(lambda h: tm[h] + x[h][:C])
        pw = each(lambda h: x[h][C:])
        n *= 2
    tm = each(lambda h: tm[h] + _dot(tm[h], pw[h]))
    x2 = each(lambda h: _dot(tm[h], jnp.concatenate([at[h], akv[h]], axis=1)))
    x3 = each(lambda h: _dot(a_rb[h], x2[h]))
    x4 = each(lambda h: _dot((b_h[h] * e_end[:, sls[h]]).T, x2[h]))
    rkv = each(lambda h: _dot(a_rk[h], v[:, sls[h]]))
    kv = each(lambda h: _dot((k_mod[:, sls[h]] * e_end[:, sls[h]]).T, v[:, sls[h]]))
    rm = each(lambda h: jnp.concatenate([r_t[:, sls[h]] + x3[h][:, :HP],
                                         eye * p_end[:, sls[h]] + x4[h][:, :HP]], axis=0))
    x5 = each(lambda h: _dot(rm[h], h_ref[h]))
    outs = []
    for h in heads:
        sl = sls[h]
        h_ref[h] = x5[h][C:] + x4[h][:, HP:] + kv[h]
        y = x5[h][:C] + x3[h][:, HP:] + rkv[h]
        mean = jnp.sum(y, axis=-1, keepdims=True) * (1.0 / A_HEAD_DIM)
        dlt = jnp.where(valid, y - mean, 0.0)
        var = jnp.sum(dlt * dlt, axis=-1, keepdims=True) * (1.0 / A_HEAD_DIM)
        yn = dlt * lax.rsqrt(var + A_GN_EPS) * gng_ref[:, sl] + gnb_ref[:, sl]
        bonus = jnp.sum(r[:, sl] * k_mod[:, sl] * rk_ref[:, sl], axis=-1, keepdims=True) * v[:, sl]
        outs.append((yn + bonus) * g[:, sl])
    o_ref[0] = jnp.concatenate(
        [outs[2 * m] + pltpu.roll(outs[2 * m + 1], A_HEAD_DIM, 1) for m in range(A_HEADS // 2)],
        axis=1).astype(o_ref.dtype)


def _pad_heads(t, heads):
    hd = t.shape[-1] // heads
    t = t.reshape(t.shape[:-1] + (heads, hd))
    t = jnp.pad(t, [(0, 0)] * (t.ndim - 1) + [(0, HP - hd)])
    return t.reshape(t.shape[:-2] + (heads * HP,))


N_RWKV_IN = 11
N_HGRN_IN = 3


def _rwkv_hgrn_kernel(*refs):
    a_in, b_in = refs[:N_RWKV_IN], refs[N_RWKV_IN:N_RWKV_IN + N_HGRN_IN]
    a_out, b_out, a_carry, a_state, b_state = refs[N_RWKV_IN + N_HGRN_IN:]
    other = _hgrn_stages(*b_in, b_out, b_state)
    _rwkv_body(*a_in, a_out, a_carry, a_state, tick=lambda: next(other, None))
    for _ in other:
        pass


def rwkv7_hgrn2(p_a, mu, w0, w_up, a0, a_up, g_up, k_k, k_a, r_k, gn_g, gn_b, p_b, lb, b_gn_g):
    bsz, seq, _ = p_a.shape
    C, NB = A_CHUNK, A_NB
    assert (B_CHUNK, B_NB) == (C, NB) and seq % C == 0 and bsz % NB == 0
    z = jnp.zeros((A_RANK_W, A_PADW), F32)
    wa = jnp.concatenate([jnp.concatenate([_pad_heads(w_up, A_HEADS), z], axis=1),
                          jnp.concatenate([z, _pad_heads(a_up, A_HEADS)], axis=1)], axis=0).astype(BF16)
    vecs = [_pad_heads(t.reshape(-1), A_HEADS).reshape(1, -1) for t in (k_k, k_a, r_k, gn_g, gn_b)]
    full = lambda shape: pl.BlockSpec(shape, lambda b, c: (0,) * len(shape))
    rows = lambda w: pl.BlockSpec((NB, C, w), lambda b, c: (b, c, 0))
    return pl.pallas_call(
        _rwkv_hgrn_kernel,
        grid=(bsz // NB, seq // C),
        in_specs=[rows(A_COLS), full((1, A_COLS)), full((1, A_PADW)), full((1, A_PADW)),
                  full((LANE, 2 * A_PADW)), full((A_RANK_G, A_PADW))] + [full((1, A_PADW))] * 5
                 + [rows(B_COLS), full((1, B_WIDTH)), full((1, B_WIDTH))],
        out_specs=[rows(A_WIDTH), rows(B_WIDTH)],
        out_shape=[jax.ShapeDtypeStruct((bsz, seq, A_WIDTH), BF16), jax.ShapeDtypeStruct((bsz, seq, B_WIDTH), BF16)],
        scratch_shapes=[pltpu.VMEM((NB, SUBLANE, A_COLS), F32), pltpu.VMEM((NB * A_HEADS, HP, HP), F32),
                        pltpu.VMEM((NB * B_HEADS, B_VAL_DIM, B_KEY_DIM), F32)],
        compiler_params=pltpu.CompilerParams(
            dimension_semantics=("arbitrary", "arbitrary"), vmem_limit_bytes=VMEM_LIMIT),
        name="rwkv7_hgrn2",
    )(p_a, mu.reshape(1, -1), _pad_heads(w0, A_HEADS).reshape(1, -1), _pad_heads(a0, A_HEADS).reshape(1, -1), wa,
      _pad_heads(g_up, A_HEADS).astype(BF16), *vecs, p_b, lb.reshape(1, -1), b_gn_g.reshape(1, -1))


B_CHUNK = 64
B_SUB = 16


def _dot_tl(x, y):
    return lax.dot_general(x.astype(BF16), y.astype(BF16), (((0,), (0,)), ((), ())), preferred_element_type=F32)


B_NB = 2


def _hgrn_stages(p_ref, lb_ref, gn_ref, o_ref, st_ref):
    C, SB, W = B_CHUNK, B_SUB, B_WIDTH
    nsb = C // SB
    D = B_KEY_DIM
    NB = B_NB
    R = NB * C

    @pl.when(pl.program_id(1) == 0)
    def _():
        st_ref[...] = jnp.zeros_like(st_ref)

    p = jnp.concatenate([p_ref[s].astype(F32) for s in range(NB)], axis=0)
    q, f, val, gate = p[:, :W], p[:, W:2 * W], p[:, 2 * W:3 * W], p[:, 3 * W:]
    lb = lb_ref[...]
    log_f = jnp.logaddexp(jnp.log(lb), jnp.log1p(-lb) + jax.nn.log_sigmoid(f))
    k = (1.0 - lb) * jax.nn.sigmoid(-f)
    yield
    rowr = lax.broadcasted_iota(jnp.int32, (R, R), 0)
    colr = lax.broadcasted_iota(jnp.int32, (R, R), 1)
    b = jnp.dot((((rowr // C) == (colr // C)) & (rowr >= colr)).astype(F32), log_f,
                precision=lax.Precision.HIGHEST, preferred_element_type=F32)
    yield
    b0 = jnp.concatenate(
        [jnp.zeros((SB, W), F32) if i % nsb == 0 else jnp.broadcast_to(b[i * SB - 1:i * SB, :], (SB, W))
         for i in range(NB * nsb)], axis=0)
    qt = q * jnp.exp(b - b0)
    qe = q * jnp.exp(b)
    b_end = [b[(s + 1) * C - 1:(s + 1) * C, :] for s in range(NB)]
    ke = jnp.concatenate([k[s * C:(s + 1) * C] * jnp.exp(b_end[s] - b[s * C:(s + 1) * C]) for s in range(NB)],
                         axis=0)
    e_end = [jnp.exp(b_end[s]) for s in range(NB)]
    rowd = lax.broadcasted_iota(jnp.int32, (C, D), 0)
    lane_c = lax.broadcasted_iota(jnp.int32, (SB, C), 1)
    row_sb = lax.broadcasted_iota(jnp.int32, (SB, C), 0)

    chains = [(s, h) for s in range(NB) for h in range(B_HEADS)]
    each = lambda fn: [fn(n) for n in range(len(chains))]

    def blk(x, n):
        s, h = chains[n]
        return x[s * C:(s + 1) * C, h * D:(h + 1) * D]

    def off_operands(n):
        k_n, b_n, qt_n = blk(k, n), blk(b, n), blk(qt, n)
        kts, qts = [], []
        for i in range(1, nsb):
            b0_i = b_n[i * SB - 1:i * SB]
            kts.append(jnp.where(rowd < i * SB, k_n * jnp.exp(jnp.minimum(b0_i - b_n, 0.0)), 0.0))
            qts.append(jnp.where((rowd >= i * SB) & (rowd < (i + 1) * SB), qt_n, 0.0))
        return jnp.concatenate(kts, axis=1), jnp.concatenate(qts, axis=1)
    yield
    offs = []
    for n in range(len(chains)):
        offs.append(off_operands(n))
        yield
    st_off = each(lambda n: _dot_t(offs[n][0], offs[n][1]))

    st_diag = []
    for n in range(len(chains)):
        k_n, b_n, q_n = blk(k, n), blk(b, n), blk(q, n)
        blocks = []
        for i in range(nsb):
            rs = slice(i * SB, (i + 1) * SB)
            k_i, b_i = k_n[rs], b_n[rs]
            acc = jnp.zeros((SB, C), F32)
            for j in range(SB):
                t = i * SB + j
                z = k_i * q_n[t:t + 1] * jnp.exp(jnp.minimum(b_n[t:t + 1] - b_i, 0.0))
                acc = jnp.where(lane_c == t, jnp.sum(z, axis=-1, keepdims=True), acc)
            blocks.append(jnp.where(row_sb + i * SB <= lane_c, acc, 0.0))
            yield
        st_diag.append(jnp.concatenate(blocks, axis=0))
    o_intra = each(lambda n: _dot_tl(st_off[n] + st_diag[n], blk(val, n)))
    yield
    o_state = each(lambda n: _dot_t(blk(qe, n), st_ref[n]))
    upd = each(lambda n: _dot_tl(blk(val, n), blk(ke, n)))
    yield
    for s in range(NB):
        outs = []
        for h in range(B_HEADS):
            n = s * B_HEADS + h
            st_ref[n] = st_ref[n] * e_end[s][:, h * D:(h + 1) * D] + upd[n]
            o = o_intra[n] + o_state[n]
            o = o * lax.rsqrt(jnp.mean(o * o, axis=-1, keepdims=True) + NORM_EPS)
            gt = blk(gate, n)
            outs.append(o * gn_ref[:, h * D:(h + 1) * D] * (gt * jax.nn.sigmoid(gt)))
        o_ref[s] = jnp.concatenate(outs, axis=1).astype(o_ref.dtype)


def _hgrn_kernel_one(p_ref, lb_ref, gn_ref, o_ref, st_ref):
    C, SB, W = B_CHUNK, B_SUB, B_WIDTH
    nsb = C // SB
    D = B_KEY_DIM

    @pl.when(pl.program_id(1) == 0)
    def _():
        st_ref[...] = jnp.zeros_like(st_ref)

    p = p_ref[0].astype(F32)
    q, f, val, gate = p[:, :W], p[:, W:2 * W], p[:, 2 * W:3 * W], p[:, 3 * W:]
    lb = lb_ref[...]
    log_f = jnp.logaddexp(jnp.log(lb), jnp.log1p(-lb) + jax.nn.log_sigmoid(f))
    k = (1.0 - lb) * jax.nn.sigmoid(-f)
    row = lax.broadcasted_iota(jnp.int32, (C, C), 0)
    col = lax.broadcasted_iota(jnp.int32, (C, C), 1)
    b = jnp.dot((row >= col).astype(F32), log_f, precision=lax.Precision.HIGHEST,
                preferred_element_type=F32)
    b_end = b[C - 1:C, :]
    b0 = jnp.concatenate(
        [jnp.zeros((SB, W), F32)] + [jnp.broadcast_to(b[i * SB - 1:i * SB, :], (SB, W)) for i in range(1, nsb)],
        axis=0)
    qt = q * jnp.exp(b - b0)
    qe = q * jnp.exp(b)
    ke = k * jnp.exp(b_end - b)
    e_end = jnp.exp(b_end)
    rowd = lax.broadcasted_iota(jnp.int32, (C, D), 0)
    lane_c = lax.broadcasted_iota(jnp.int32, (SB, C), 1)
    row_sb = lax.broadcasted_iota(jnp.int32, (SB, C), 0)

    heads = range(B_HEADS)
    sls = [slice(h * D, (h + 1) * D) for h in heads]
    each = lambda fn: [fn(h) for h in heads]

    def off_operands(h):
        kts, qts = [], []
        for i in range(1, nsb):
            b0_i = b[i * SB - 1:i * SB, sls[h]]
            kts.append(jnp.where(rowd < i * SB, k[:, sls[h]] * jnp.exp(jnp.minimum(b0_i - b[:, sls[h]], 0.0)), 0.0))
            qts.append(jnp.where((rowd >= i * SB) & (rowd < (i + 1) * SB), qt[:, sls[h]], 0.0))
        return jnp.concatenate(kts, axis=1), jnp.concatenate(qts, axis=1)
    offs = each(off_operands)
    st_off = each(lambda h: _dot_t(offs[h][0], offs[h][1]))

    def diag_t(h):
        blocks = []
        for i in range(nsb):
            rs = slice(i * SB, (i + 1) * SB)
            k_i, b_i = k[rs, sls[h]], b[rs, sls[h]]
            acc = jnp.zeros((SB, C), F32)
            for j in range(SB):
                t = i * SB + j
                z = k_i * q[t:t + 1, sls[h]] * jnp.exp(jnp.minimum(b[t:t + 1, sls[h]] - b_i, 0.0))
                acc = jnp.where(lane_c == t, jnp.sum(z, axis=-1, keepdims=True), acc)
            blocks.append(jnp.where(row_sb + i * SB <= lane_c, acc, 0.0))
        return jnp.concatenate(blocks, axis=0)
    st_diag = each(diag_t)
    o_intra = each(lambda h: _dot_tl(st_off[h] + st_diag[h], val[:, sls[h]]))
    o_state = each(lambda h: _dot_t(qe[:, sls[h]], st_ref[h]))
    upd = each(lambda h: _dot_tl(val[:, sls[h]], ke[:, sls[h]]))
    outs = []
    for h in heads:
        st_ref[h] = st_ref[h] * e_end[:, sls[h]] + upd[h]
        o = o_intra[h] + o_state[h]
        o = o * lax.rsqrt(jnp.mean(o * o, axis=-1, keepdims=True) + NORM_EPS)
        gt = gate[:, sls[h]]
        outs.append(o * gn_ref[:, sls[h]] * (gt * jax.nn.sigmoid(gt)))
    o_ref[0] = jnp.concatenate(outs, axis=1).astype(o_ref.dtype)


def hgrn2(p, lb, gn_g):
    bsz, seq, _ = p.shape
    C = B_CHUNK
    assert seq % C == 0 and bsz % B_NB == 0
    vec = pl.BlockSpec((1, B_WIDTH), lambda b, c: (0, 0))
    return pl.pallas_call(
        _hgrn_kernel,
        grid=(bsz // B_NB, seq // C),
        in_specs=[pl.BlockSpec((B_NB, C, B_COLS), lambda b, c: (b, c, 0)), vec, vec],
        out_specs=pl.BlockSpec((B_NB, C, B_WIDTH), lambda b, c: (b, c, 0)),
        out_shape=jax.ShapeDtypeStruct((bsz, seq, B_WIDTH), BF16),
        scratch_shapes=[pltpu.VMEM((B_NB * B_HEADS, B_VAL_DIM, B_KEY_DIM), F32)],
        compiler_params=pltpu.CompilerParams(
            dimension_semantics=("arbitrary", "arbitrary"), vmem_limit_bytes=VMEM_LIMIT),
        name="hgrn2",
    )(p, lb.reshape(1, -1), gn_g.reshape(1, -1))


C_PAD = _round_up(C_COLS, LANE)
QB = 128
C_KV_W = C_KV_HEADS * C_HEAD_DIM
C_K_OFF = C_WIDTH
C_QI_OFF = C_WIDTH + 2 * C_KV_W
C_KI_OFF = C_QI_OFF + IDX_HEADS * IDX_DIM
INT_MIN = -2 ** 31
NEG = -1e30
HALF = C_HEAD_DIM // 2
HALF_RANGE = 1 << 15
VT_ROWS = C_HEAD_DIM + 16


def _rope(t, cos, sin_signed):
    outs = []
    lane = lax.broadcasted_iota(jnp.int32, (t.shape[0], LANE), 1)
    first_half = (lane % C_HEAD_DIM) < HALF
    for c in range(t.shape[1] // LANE):
        x = t[:, c * LANE:(c + 1) * LANE]
        swapped = jnp.where(first_half, pltpu.roll(x, LANE - HALF, 1), pltpu.roll(x, HALF, 1))
        outs.append(x * cos + swapped * sin_signed)
    return outs


GROUP = C_HEADS // C_KV_HEADS


def _split_hi_lo(x):
    hi = x.astype(BF16)
    return hi, (x - hi.astype(F32)).astype(BF16)


QW = 2 * QB
ATT_GROUP = 8


def _dsa_kernel(pq_ref, kv_ref, ki_ref, cosq_ref, sinq_ref, cosf_ref, sinf_ref, o_ref,
                k_s, vt_s, ki_s, key_s, *, k_sel, seq):
    i = pl.program_id(1)
    nblk = i + 1
    lane = lax.broadcasted_iota(jnp.int32, (QW, LANE), 1)
    low_half = lane < C_HEAD_DIM
    row_k = lax.broadcasted_iota(jnp.int32, (QB, QW), 0)
    q_pos = i * QW + lax.broadcasted_iota(jnp.int32, (QB, QW), 1)

    @pl.when(i == 0)
    def _():
        kv = kv_ref[0].astype(F32)
        k_s[...] = _rope(kv[:, :LANE], cosf_ref[...], sinf_ref[...])[0].astype(BF16)
        vt = kv[:, LANE:].T
        extra = (lax.broadcasted_iota(jnp.int32, (VT_ROWS - C_HEAD_DIM, seq), 0) == 0).astype(F32)
        for c in range(C_KV_HEADS):
            vt_s[c] = jnp.concatenate([vt[c * C_HEAD_DIM:(c + 1) * C_HEAD_DIM], extra], axis=0).astype(BF16)
        lane_f = lax.broadcasted_iota(jnp.int32, (seq, LANE), 1)
        ki = jnp.where(lane_f < IDX_DIM, ki_ref[0].astype(F32), 0.0)
        ki = _rope(ki, cosf_ref[...], sinf_ref[...])[0]
        ki_s[...] = (ki + pltpu.roll(ki, IDX_DIM, 1)).astype(BF16)

    pq = pq_ref[0].astype(F32)
    cos, sin = cosq_ref[...], sinq_ref[...]
    q = _rope(pq[:, :C_WIDTH], cos, sin)
    qi = _rope(pq[:, C_QI_OFF:C_KI_OFF], cos, sin)
    qi3 = jnp.concatenate(
        [jnp.where(low_half if h % 2 == 0 else ~low_half, qi[h // 2], 0.0) for h in range(IDX_HEADS)],
        axis=0).astype(BF16)
    wt = pq[:, C_KI_OFF:C_KI_OFF + LANE].T
    wscale = (IDX_HEADS ** -0.5) * (IDX_DIM ** -0.5)
    w_rows = [wt[IDX_DIM + h:IDX_DIM + h + 1, :] * wscale for h in range(IDX_HEADS)]

    def score_blk(jb, _):
        ki_t = ki_s[pl.ds(pl.multiple_of(jb * QW, QW), QW), :]
        d = lax.dot_general(ki_t, qi3, (((1,), (1,)), ((), ())), preferred_element_type=F32)
        for u in range(QW // QB):
            j = (QW // QB) * jb + u
            sc = jnp.zeros((QB, QW), F32)
            for h in range(IDX_HEADS):
                sc = sc + jnp.maximum(d[u * QB:(u + 1) * QB, h * QW:(h + 1) * QW], 0.0) * w_rows[h]
            bits = pltpu.bitcast(sc + 0.0, jnp.int32)
            key = jnp.where(bits < 0, bits ^ jnp.int32(0x7fffffff), bits)
            key_s[j] = jnp.where(j * QB + row_k <= q_pos, key, jnp.int32(INT_MIN))
        return 0
    lax.fori_loop(0, nblk, score_blk, 0)

    def count(pred):
        def body(jb, acc):
            for u in range(QW // QB):
                acc = acc + jnp.where(pred(key_s[(QW // QB) * jb + u]), 1, 0)
            return acc
        return jnp.sum(lax.fori_loop(0, nblk, body, jnp.zeros((QB, QW), jnp.int32)), axis=0, keepdims=True)

    def bit_step(b, cand):
        trial = cand | jnp.left_shift(jnp.int32(1), 31 - b)
        cnt = count(lambda key: key >= (trial ^ jnp.int32(INT_MIN)))
        return jnp.where(cnt >= k_sel, trial, cand)
    cand = lax.fori_loop(0, 32, bit_step, jnp.zeros((1, QW), jnp.int32))
    thr = cand ^ jnp.int32(INT_MIN)
    need_f = (k_sel - count(lambda key: key > thr)).astype(F32)
    tri = (lax.broadcasted_iota(jnp.int32, (QW, QW), 0)
           >= lax.broadcasted_iota(jnp.int32, (QW, QW), 1)).astype(BF16)

    scale = C_HEAD_DIM ** -0.5
    lane_b = lax.broadcasted_iota(jnp.int32, (QB, LANE), 1)
    in_cs = [(lane_b >= c * C_HEAD_DIM) & (lane_b < (c + 1) * C_HEAD_DIM) for c in range(C_KV_HEADS)]
    chains = [(n, e) for e in range(QW // QB) for n in range(C_HEADS // 2)]
    kv_of = [n * 2 // GROUP for n, _ in chains]

    def q_rows(n, e):
        c = n * 2 // GROUP
        rows = []
        for h in (2 * n, 2 * n + 1):
            x = q[n][e * QB:(e + 1) * QB]
            if h % 2 != c:
                x = pltpu.roll(x, C_HEAD_DIM, 1)
            rows.append(jnp.where(in_cs[c], x * scale, 0.0))
        return jnp.concatenate(rows, axis=0).astype(BF16)
    qcs = [q_rows(n, e) for n, e in chains]
    nch = range(len(chains))

    def att_blk(jb, carry):
        off = pl.multiple_of(jb * QW, QW)
        keys = jnp.concatenate([key_s[(QW // QB) * jb + u] for u in range(QW // QB)], axis=0)
        pos = off + lax.broadcasted_iota(jnp.int32, (QW, QW), 0)
        tie = keys == thr
        rank = carry[-1] + jnp.dot(tri, jnp.where(tie, 1.0, 0.0).astype(BF16), preferred_element_type=F32)
        sel = ((keys > thr) | (tie & (rank <= need_f))) & (pos <= q_pos[:1, :])
        bias1 = jnp.where(sel, 0.0, NEG)
        bias = [jnp.concatenate([bias1[:, e * QB:(e + 1) * QB]] * 2, axis=1) for e in range(QW // QB)]
        k_t = k_s[pl.ds(off, QW), :]
        v_t = [vt_s[c, :, pl.ds(off, QW)] for c in range(C_KV_HEADS)]
        out = [None] * (2 * len(chains))
        for g0 in range(0, len(chains), ATT_GROUP):
            grp = range(g0, g0 + ATT_GROUP)
            s = {x: lax.dot_general(k_t, qcs[x], (((1,), (1,)), ((), ())), preferred_element_type=F32)
                 for x in grp}
            s = {x: s[x] + bias[chains[x][1]] for x in grp}
            m_new = {x: jnp.maximum(carry[2 * x], jnp.max(s[x], axis=0, keepdims=True)) for x in grp}
            p = {x: jnp.exp(s[x] - m_new[x]).astype(BF16) for x in grp}
            pv = {x: jnp.dot(v_t[kv_of[x]], p[x], preferred_element_type=F32) for x in grp}
            for x in grp:
                out[2 * x] = m_new[x]
                out[2 * x + 1] = carry[2 * x + 1] * jnp.exp(carry[2 * x] - m_new[x]) + pv[x]
        return tuple(out) + (rank[QW - 1:, :],)
    init = ((jnp.full((1, 2 * QB), NEG, F32), jnp.zeros((VT_ROWS, 2 * QB), F32)) * len(chains)
            + (jnp.zeros((1, QW), F32),))
    res = lax.fori_loop(0, nblk, att_blk, init)

    halves = []
    for e in range(QW // QB):
        tiles = []
        for n in range(C_HEADS // 2):
            acc = res[2 * chains.index((n, e)) + 1]
            out_t = acc[:C_HEAD_DIM] / acc[C_HEAD_DIM:C_HEAD_DIM + 1]
            tiles.append(jnp.concatenate([out_t[:, :QB], out_t[:, QB:]], axis=0).T)
        halves.append(jnp.concatenate(tiles, axis=1))
    o_ref[0] = jnp.concatenate(halves, axis=0).astype(o_ref.dtype)


def _dsa_kernel_qb(pq_ref, kv_ref, ki_ref, cosq_ref, sinq_ref, cosf_ref, sinf_ref, o_ref,
                   k_s, vt_s, ki_s, key_s, khi_s, klo_s, *, k_sel, seq):
    i = pl.program_id(1)
    nk = i + 1
    lane = lax.broadcasted_iota(jnp.int32, (QB, LANE), 1)
    rowi = lax.broadcasted_iota(jnp.int32, (QB, LANE), 0)
    low_half = lane < C_HEAD_DIM

    @pl.when(i == 0)
    def _():
        kv = kv_ref[0].astype(F32)
        k_s[...] = _rope(kv[:, :LANE], cosf_ref[...], sinf_ref[...])[0].astype(BF16)
        vt = kv[:, LANE:].T
        extra = (lax.broadcasted_iota(jnp.int32, (VT_ROWS - C_HEAD_DIM, seq), 0) == 0).astype(F32)
        for c in range(C_KV_HEADS):
            vt_s[c] = jnp.concatenate([vt[c * C_HEAD_DIM:(c + 1) * C_HEAD_DIM], extra], axis=0).astype(BF16)
        lane_f = lax.broadcasted_iota(jnp.int32, (seq, LANE), 1)
        ki = jnp.where(lane_f < IDX_DIM, ki_ref[0].astype(F32), 0.0)
        ki = _rope(ki, cosf_ref[...], sinf_ref[...])[0]
        ki_s[...] = (ki + pltpu.roll(ki, IDX_DIM, 1)).astype(BF16)

    pq = pq_ref[0].astype(F32)
    cos, sin = cosq_ref[...], sinq_ref[...]
    q = _rope(pq[:, :C_WIDTH], cos, sin)
    qi = _rope(pq[:, C_QI_OFF:C_KI_OFF], cos, sin)
    qi3 = jnp.concatenate(
        [jnp.where(low_half if h % 2 == 0 else ~low_half, qi[h // 2], 0.0) for h in range(IDX_HEADS)],
        axis=0).astype(BF16)
    wt = pq[:, C_KI_OFF:C_KI_OFF + LANE].T
    wscale = (IDX_HEADS ** -0.5) * (IDX_DIM ** -0.5)
    w_rows = [wt[IDX_DIM + h:IDX_DIM + h + 1, :] * wscale for h in range(IDX_HEADS)]
    causal_off = i * QB + lane

    npair = (nk + 1) // 2

    def score_tile(jp, _):
        ki_t = ki_s[pl.ds(pl.multiple_of(jp * (2 * QB), 2 * QB), 2 * QB), :]
        d = lax.dot_general(ki_t, qi3, (((1,), (1,)), ((), ())), preferred_element_type=F32)
        for u in range(2):
            j = 2 * jp + u
            sc = jnp.zeros((QB, QB), F32)
            for h in range(IDX_HEADS):
                sc = sc + jnp.maximum(d[u * QB:(u + 1) * QB, h * QB:(h + 1) * QB], 0.0) * w_rows[h]
            bits = pltpu.bitcast(sc + 0.0, jnp.int32)
            key = jnp.where(bits < 0, bits ^ jnp.int32(0x7fffffff), bits)
            key = jnp.where(j * QB + rowi <= causal_off, key, jnp.int32(INT_MIN))
            key_s[j] = key
            khi_s[j] = jnp.right_shift(key, 16).astype(jnp.int16)
            klo_s[j] = ((key & 0xffff) - HALF_RANGE).astype(jnp.int16)
        return 0
    lax.fori_loop(0, npair, score_tile, 0)

    def count16(ref, pred):
        def body(jp, acc):
            for u in range(2):
                acc = acc + jnp.where(pred(ref[2 * jp + u]), jnp.int16(1), jnp.int16(0))
            return acc
        acc = lax.fori_loop(0, npair, body, jnp.zeros((QB, QB), jnp.int16))
        return jnp.sum(acc.astype(jnp.int32), axis=0, keepdims=True)

    def half_search(ref, base):
        def bit_step(b, cand):
            trial = cand | jnp.left_shift(jnp.int32(1), 15 - b)
            t16 = (trial - HALF_RANGE).astype(jnp.int16)
            cnt = base + count16(ref, lambda x: x >= t16)
            return jnp.where(cnt >= k_sel, trial, cand)
        return lax.fori_loop(0, 16, bit_step, jnp.zeros((1, QB), jnp.int32))

    cand_hi = half_search(khi_s, 0)
    hi16 = (cand_hi - HALF_RANGE).astype(jnp.int16)
    above = count16(khi_s, lambda x: x > hi16)

    def mask_low(jp, _):
        for u in range(2):
            j = 2 * jp + u
            klo_s[j] = jnp.where(khi_s[j] == hi16, klo_s[j], jnp.int16(-HALF_RANGE))
        return 0
    lax.fori_loop(0, npair, mask_low, 0)
    cand_lo = half_search(klo_s, above)
    thr = (jnp.left_shift(cand_hi, 16) | cand_lo) ^ jnp.int32(INT_MIN)

    def count(pred):
        def body(jp, acc):
            for u in range(2):
                acc = acc + jnp.where(pred(key_s[2 * jp + u]), 1, 0)
            return acc
        return jnp.sum(lax.fori_loop(0, npair, body, jnp.zeros((QB, QB), jnp.int32)), axis=0, keepdims=True)
    need = k_sel - count(lambda key: key > thr)

    need_f = need.astype(F32)
    tri = (lax.broadcasted_iota(jnp.int32, (2 * QB, 2 * QB), 0)
           >= lax.broadcasted_iota(jnp.int32, (2 * QB, 2 * QB), 1)).astype(BF16)

    scale = C_HEAD_DIM ** -0.5
    kvs = range(C_KV_HEADS)
    in_cs = [(lane >= c * C_HEAD_DIM) & (lane < (c + 1) * C_HEAD_DIM) for c in kvs]

    chains = range(C_HEADS // 2)
    kv_of = [n * 2 // GROUP for n in chains]

    def q_rows(n):
        c = kv_of[n]
        rows = []
        for h in (2 * n, 2 * n + 1):
            x = q[n]
            if h % 2 != c:
                x = pltpu.roll(x, C_HEAD_DIM, 1)
            rows.append(jnp.where(in_cs[c], x * scale, 0.0))
        return jnp.concatenate(rows, axis=0).astype(BF16)
    qcs = [q_rows(n) for n in chains]

    def att_tile(jp, carry):
        off = pl.multiple_of(jp * (2 * QB), 2 * QB)
        keys = jnp.concatenate([key_s[2 * jp], key_s[2 * jp + 1]], axis=0)
        pos = off + lax.broadcasted_iota(jnp.int32, (2 * QB, LANE), 0)
        tie = keys == thr
        rank = carry[-1] + jnp.dot(tri, jnp.where(tie, 1.0, 0.0).astype(BF16), preferred_element_type=F32)
        sel = ((keys > thr) | (tie & (rank <= need_f))) & (pos <= causal_off[:1, :])
        bias1 = jnp.where(sel, 0.0, NEG)
        bias = jnp.concatenate([bias1, bias1], axis=1)
        k_t = k_s[pl.ds(off, 2 * QB), :]
        s = [lax.dot_general(k_t, qcs[n], (((1,), (1,)), ((), ())), preferred_element_type=F32) for n in chains]
        s = [s[n] + bias for n in chains]
        m_new = [jnp.maximum(carry[2 * n], jnp.max(s[n], axis=0, keepdims=True)) for n in chains]
        p = [jnp.exp(s[n] - m_new[n]).astype(BF16) for n in chains]
        v_t = [vt_s[c, :, pl.ds(off, 2 * QB)] for c in kvs]
        pv = [jnp.dot(v_t[kv_of[n]], p[n], preferred_element_type=F32) for n in chains]
        out = []
        for n in chains:
            out += [m_new[n], carry[2 * n + 1] * jnp.exp(carry[2 * n] - m_new[n]) + pv[n]]
        return tuple(out) + (rank[2 * QB - 1:, :],)
    init = ((jnp.full((1, 2 * QB), NEG, F32), jnp.zeros((VT_ROWS, 2 * QB), F32)) * len(chains)
            + (jnp.zeros((1, QB), F32),))
    res = lax.fori_loop(0, npair, att_tile, init)

    tiles = []
    for n in chains:
        acc = res[2 * n + 1]
        out_t = acc[:C_HEAD_DIM] / acc[C_HEAD_DIM:C_HEAD_DIM + 1]
        tiles.append(jnp.concatenate([out_t[:, :QB], out_t[:, QB:]], axis=0).T)
    o_ref[0] = jnp.concatenate(tiles, axis=1).astype(o_ref.dtype)


def _dsa_kernel_rows(pq_ref, kv_ref, ki_ref, cosq_ref, sinq_ref, cosf_ref, sinf_ref, o_ref,
                     k_s, v_s, ki_s, key_s, *, k_sel, seq):
    i = pl.program_id(1)
    nk = i + 1
    lane = lax.broadcasted_iota(jnp.int32, (QB, LANE), 1)
    rowi = lax.broadcasted_iota(jnp.int32, (QB, LANE), 0)
    low_half = lane < C_HEAD_DIM

    @pl.when(i == 0)
    def _():
        kv = kv_ref[0]
        k_s[...] = _rope(kv[:, :LANE], cosf_ref[...], sinf_ref[...])[0].astype(BF16)
        v_s[...] = kv[:, LANE:].astype(BF16)
        lane_f = lax.broadcasted_iota(jnp.int32, (seq, LANE), 1)
        ki = jnp.where(lane_f < IDX_DIM, ki_ref[0], 0.0)
        ki = _rope(ki, cosf_ref[...], sinf_ref[...])[0]
        ki_s[...] = ki + pltpu.roll(ki, IDX_DIM, 1)

    pq = pq_ref[0]
    cos, sin = cosq_ref[...], sinq_ref[...]
    q = _rope(pq[:, :C_WIDTH], cos, sin)
    qi = _rope(pq[:, C_QI_OFF:C_KI_OFF], cos, sin)
    wslab = pq[:, C_KI_OFF:C_KI_OFF + LANE]
    wscale = (IDX_HEADS ** -0.5) * (IDX_DIM ** -0.5)
    wi = [jnp.sum(jnp.where(lane == IDX_DIM + h, wslab, 0.0), axis=-1, keepdims=True) * wscale
          for h in range(IDX_HEADS)]
    qi_h = [jnp.where(low_half if h % 2 == 0 else ~low_half, qi[h // 2], 0.0) for h in range(IDX_HEADS)]

    def score_tile(j, _):
        ki_t = ki_s[pl.ds(pl.multiple_of(j * QB, QB), QB), :]
        sc = jnp.zeros((QB, QB), F32)
        for h in range(IDX_HEADS):
            d = lax.dot_general(qi_h[h], ki_t, (((1,), (1,)), ((), ())), precision=lax.Precision.HIGHEST,
                                preferred_element_type=F32)
            sc = sc + jnp.maximum(d, 0.0) * wi[h]
        bits = pltpu.bitcast(sc + 0.0, jnp.int32)
        key = jnp.where(bits < 0, bits ^ jnp.int32(0x7fffffff), bits)
        key_s[j] = jnp.where(j * QB + lane <= i * QB + rowi, key, jnp.int32(INT_MIN))
        return 0
    lax.fori_loop(0, nk, score_tile, 0)

    def count(pred):
        def body(j, acc):
            return acc + jnp.where(pred(key_s[j], j), 1, 0)
        return jnp.sum(lax.fori_loop(0, nk, body, jnp.zeros((QB, QB), jnp.int32)), axis=-1, keepdims=True)

    def bit_step(b, cand):
        trial = cand | jnp.left_shift(jnp.int32(1), 31 - b)
        cnt = count(lambda key, j: key >= (trial ^ jnp.int32(INT_MIN)))
        return jnp.where(cnt >= k_sel, trial, cand)
    cand = lax.fori_loop(0, 32, bit_step, jnp.zeros((QB, 1), jnp.int32))
    thr = cand ^ jnp.int32(INT_MIN)
    need = k_sel - count(lambda key, j: key > thr)

    nbits = int(np.ceil(np.log2(seq))) + 1

    def idx_step(b, jj):
        trial = jj | jnp.left_shift(jnp.int32(1), nbits - 1 - b)
        cnt = count(lambda key, j: (key == thr) & (j * QB + lane < trial))
        return jnp.where(cnt <= need, trial, jj)
    jlim = lax.fori_loop(0, nbits, idx_step, jnp.zeros((QB, 1), jnp.int32))

    group = C_HEADS // C_KV_HEADS
    scale = C_HEAD_DIM ** -0.5
    outs = []
    for c in range(C_KV_HEADS):
        in_c = (lane >= c * C_HEAD_DIM) & (lane < (c + 1) * C_HEAD_DIM)
        rows = []
        for g in range(group):
            h = c * group + g
            x = q[h // 2]
            if h % 2 != c:
                x = pltpu.roll(x, C_HEAD_DIM, 1)
            rows.append(jnp.where(in_c, x * scale, 0.0))
        qc = jnp.concatenate(rows, axis=0).astype(BF16)
        ones_lane = (1 - c) * C_HEAD_DIM

        def att_tile(j, carry):
            m, acc = carry
            off = pl.multiple_of(j * QB, QB)
            key = key_s[j]
            sel = ((key > thr) | ((key == thr) & (j * QB + lane < jlim))) & (j * QB + lane <= i * QB + rowi)
            bias = jnp.where(sel, 0.0, NEG)
            s = lax.dot_general(qc, k_s[pl.ds(off, QB), :], (((1,), (1,)), ((), ())),
                                preferred_element_type=F32)
            s = jnp.where(jnp.concatenate([bias] * group, axis=0) < 0.0, NEG, s)
            m_new = jnp.maximum(m, jnp.max(s, axis=-1, keepdims=True))
            alpha = jnp.exp(m - m_new)
            p = jnp.exp(s - m_new)
            v_t = v_s[pl.ds(off, QB), :]
            v_t = jnp.where(in_c, v_t, jnp.where(lane == ones_lane, 1.0, 0.0).astype(BF16))
            acc = acc * alpha + jnp.dot(p.astype(BF16), v_t, preferred_element_type=F32)
            return m_new, acc
        m0 = jnp.full((group * QB, 1), NEG, F32)
        acc0 = jnp.zeros((group * QB, LANE), F32)
        _, acc = lax.fori_loop(0, nk, att_tile, (m0, acc0))
        l = jnp.sum(jnp.where(jnp.concatenate([lane] * group, axis=0) == ones_lane, acc, 0.0),
                    axis=-1, keepdims=True)
        out = acc / l
        for g in range(group):
            h = c * group + g
            x = jnp.where(in_c, out[g * QB:(g + 1) * QB], 0.0)
            if h % 2 != c:
                x = pltpu.roll(x, C_HEAD_DIM, 1)
            outs.append(x)
    o_ref[0] = jnp.concatenate([outs[2 * m] + outs[2 * m + 1] for m in range(C_HEADS // 2)],
                               axis=1).astype(o_ref.dtype)


def _rope_lanes(seq):
    inv_freq = ROPE_THETA ** (-jnp.arange(HALF, dtype=F32) * (2.0 / C_HEAD_DIM))
    ang = jnp.arange(seq, dtype=F32)[:, None] * inv_freq[None, :]
    cos, sin = jnp.cos(ang), jnp.sin(ang)
    reps = LANE // C_HEAD_DIM
    return (jnp.tile(jnp.concatenate([cos, cos], axis=1), (1, reps)),
            jnp.tile(jnp.concatenate([-sin, sin], axis=1), (1, reps)))


def dsa(p):
    bsz, seq, width = p.shape
    assert seq % QW == 0 and width == C_PAD
    k_sel = min(TOPK_MAX, seq // 4)
    cos, sin = _rope_lanes(seq)
    return pl.pallas_call(
        functools.partial(_dsa_kernel, k_sel=k_sel, seq=seq),
        grid=(bsz, seq // QW),
        in_specs=[pl.BlockSpec((1, QW, C_PAD), lambda b, i: (b, i, 0)),
                  pl.BlockSpec((1, seq, 2 * C_KV_W), lambda b, i: (b, 0, C_K_OFF // (2 * C_KV_W))),
                  pl.BlockSpec((1, seq, LANE), lambda b, i: (b, 0, C_KI_OFF // LANE)),
                  pl.BlockSpec((QW, LANE), lambda b, i: (i, 0)),
                  pl.BlockSpec((QW, LANE), lambda b, i: (i, 0)),
                  pl.BlockSpec((seq, LANE), lambda b, i: (0, 0)),
                  pl.BlockSpec((seq, LANE), lambda b, i: (0, 0))],
        out_specs=pl.BlockSpec((1, QW, C_WIDTH), lambda b, i: (b, i, 0)),
        out_shape=jax.ShapeDtypeStruct((bsz, seq, C_WIDTH), BF16),
        scratch_shapes=[pltpu.VMEM((seq, LANE), BF16), pltpu.VMEM((C_KV_HEADS, VT_ROWS, seq), BF16),
                        pltpu.VMEM((seq, LANE), BF16), pltpu.VMEM((seq // QB, QB, QW), jnp.int32)],
        compiler_params=pltpu.CompilerParams(
            dimension_semantics=("arbitrary", "arbitrary"), vmem_limit_bytes=VMEM_LIMIT),
        name="dsa",
    )(p, p, p, cos, sin, cos, sin)


def _split(t, sizes):
    out, start = [], 0
    for s in sizes:
        out.append(t[..., start:start + s])
        start += s
    return out


def _mm(x, y):
    return jnp.einsum('...ij,...jk->...ik', x.astype(BF16), y.astype(BF16), preferred_element_type=F32)


def _mm_hi(x, y):
    return jnp.einsum('...ij,...jk->...ik', x, y, precision=lax.Precision.HIGHEST)


def _rwkv7_chunked(r, logw, k, v, a, b, C=64):
    B, S, H, N = r.shape
    nc = S // C

    def ch(t):
        return t.reshape(B, nc, C, H, N).transpose(1, 0, 3, 2, 4)
    r, logw, k, v, a, b = map(ch, (r, logw, k, v, a, b))
    cum = jnp.cumsum(logw, axis=-2)
    cum_prev = cum - logw
    cum_end = cum[..., -1:, :]
    At = a * jnp.exp(cum_prev)
    Rt = r * jnp.exp(cum)
    Bt = b * jnp.exp(-cum)
    Kt = k * jnp.exp(-cum)
    Bh = b * jnp.exp(cum_end - cum)
    Kh = k * jnp.exp(cum_end - cum)
    Pc = jnp.exp(cum_end)[..., 0, :]
    T_ = lambda x: jnp.swapaxes(x, -1, -2)
    ti = jnp.arange(C)
    strict = ti[:, None] > ti[None, :]
    incl = ti[:, None] >= ti[None, :]
    Aab = jnp.where(strict, _mm(At, T_(Bt)), 0.0)
    Aak = jnp.where(strict, _mm(At, T_(Kt)), 0.0)
    Arb = jnp.where(incl, _mm(Rt, T_(Bt)), 0.0)
    Ark = jnp.where(incl, _mm(Rt, T_(Kt)), 0.0)
    Tm = jnp.eye(C, dtype=F32) + Aab
    Pw = Aab
    n = 1
    while 2 * n < C:
        Pw = _mm_hi(Pw, Pw)
        Tm = Tm + _mm_hi(Tm, Pw)
        n *= 2
    Abar = _mm(Tm, At)
    U0 = _mm(Tm, _mm(Aak, v))
    Rbar = Rt + _mm(Arb, Abar)
    Y0 = _mm(Arb, U0) + _mm(Ark, v)
    M = Pc[..., :, None] * jnp.eye(N, dtype=F32) + _mm(T_(Bh), Abar)
    Hd = _mm(T_(Bh), U0) + _mm(T_(Kh), v)

    def step(Hs, inp):
        Rbar_c, Y0_c, M_c, Hd_c = inp
        y = _mm(Rbar_c, Hs) + Y0_c
        Hs = _mm_hi(M_c, Hs) + Hd_c
        return Hs, y
    _, ys = lax.scan(step, jnp.zeros((B, H, N, N), F32), (Rbar, Y0, M, Hd))
    return ys.transpose(1, 0, 3, 2, 4).reshape(B, S, H, N)


def _token_shift(p):
    return jnp.pad(p, ((0, 0), (1, 0), (0, 0)))[:, :-1]


def _rwkv7_branch(p, mu, w0, w_up, a0, a_up, g_up, k_k, k_a, r_k, gn_g, gn_b):
    bsz, seq, _ = p.shape
    p = p + (_token_shift(p) - p) * mu
    r, k, v, wd, ad, gd = _split(p, (A_WIDTH, A_WIDTH, A_WIDTH, A_RANK_W, A_RANK_A, A_RANK_G))
    w_log = -jax.nn.softplus(-(w0 + _mm(jnp.tanh(wd), w_up))) - 0.5
    logw = -jnp.exp(w_log)
    a = jax.nn.sigmoid(a0 + _mm(ad, a_up))
    g = _mm(jax.nn.sigmoid(gd), g_up)

    def hd(t):
        return t.reshape(bsz, seq, A_HEADS, A_HEAD_DIM)
    kk = hd(k * k_k)
    kk = kk * lax.rsqrt(jnp.maximum(jnp.sum(kk * kk, axis=-1, keepdims=True), 1e-24))
    k_mod = hd(k * (1.0 + (a - 1.0) * k_a))
    r_h, v_h, a_h = hd(r), hd(v), hd(a)
    y = _rwkv7_chunked(r_h, hd(logw), k_mod, v_h, -kk, kk * a_h)
    mean = jnp.mean(y, axis=-1, keepdims=True)
    var = jnp.mean(jnp.square(y - mean), axis=-1, keepdims=True)
    y = ((y - mean) * lax.rsqrt(var + A_GN_EPS)).reshape(bsz, seq, A_WIDTH)
    y = y * gn_g + gn_b
    bonus = jnp.sum(r_h * k_mod * r_k, axis=-1, keepdims=True) * v_h
    y = y + bonus.reshape(bsz, seq, A_WIDTH)
    return y * g


def _hgrn2_sub(q, k, g, v, C=64, SB=16):
    B, S, H, dk = q.shape
    dv = v.shape[-1]
    nc = S // C

    def ch(t):
        return t.reshape(B, nc, C, H, t.shape[-1]).transpose(1, 0, 3, 2, 4)
    q, k, g, v = map(ch, (q, k, g, v))
    b = jnp.cumsum(g, axis=-2)
    nsb = C // SB
    T_ = lambda x: jnp.swapaxes(x, -1, -2)

    def step(state, inp):
        qc, kc, bc, vc = inp
        rows = []
        for I in range(nsb):
            sl = slice(I * SB, (I + 1) * SB)
            qI, bI = qc[..., sl, :], bc[..., sl, :]
            b0 = bc[..., I * SB - 1:I * SB, :] if I > 0 else jnp.zeros_like(bc[..., :1, :])
            qt = qI * jnp.exp(bI - b0)
            blocks = []
            if I > 0:
                kt = kc[..., :I * SB, :] * jnp.exp(b0 - bc[..., :I * SB, :])
                blocks.append(_mm(qt, T_(kt)))
            diff = bI[..., :, None, :] - bI[..., None, :, :]
            tri = jnp.tril(jnp.ones((SB, SB), bool))
            dec = jnp.exp(jnp.where(tri[:, :, None], diff, -jnp.inf))
            diag = jnp.sum(qI[..., :, None, :] * kc[..., sl, :][..., None, :, :] * dec, axis=-1)
            blocks.append(diag)
            if I < nsb - 1:
                blocks.append(jnp.zeros(diag.shape[:-1] + (C - (I + 1) * SB,), F32))
            rows.append(jnp.concatenate(blocks, axis=-1))
        scores = jnp.concatenate(rows, axis=-2)
        o = _mm(scores, vc) + _mm(qc * jnp.exp(bc), state)
        b_end = bc[..., -1:, :]
        state = state * T_(jnp.exp(b_end)) + _mm(T_(kc * jnp.exp(b_end - bc)), vc)
        return state, o
    _, o = lax.scan(step, jnp.zeros((B, H, dk, dv), F32), (q, k, b, v))
    return o.transpose(1, 0, 3, 2, 4).reshape(B, S, H, dv)


def _hgrn2_branch(p, lb, gn_g):
    bsz, seq, _ = p.shape
    q, f, i, g = _split(p, (B_WIDTH,) * 4)
    log_f = jnp.logaddexp(jnp.log(lb), jnp.log1p(-lb) + jax.nn.log_sigmoid(f))
    k_in = (1.0 - lb) * jax.nn.sigmoid(-f)
    hd = lambda t: t.reshape(bsz, seq, B_HEADS, -1)
    o = _hgrn2_sub(hd(q), hd(k_in), hd(log_f), hd(i))
    o = o * lax.rsqrt(jnp.mean(o * o, axis=-1, keepdims=True) + NORM_EPS)
    return o.reshape(bsz, seq, B_WIDTH) * gn_g * jax.nn.silu(g)


def _rope_tables(seq, dim):
    half = dim // 2
    inv_freq = ROPE_THETA ** (-jnp.arange(half, dtype=F32) * (2.0 / dim))
    ang = jnp.arange(seq, dtype=F32)[:, None] * inv_freq[None, :]
    return jnp.cos(ang), jnp.sin(ang)


def _apply_rope(t, cos, sin):
    half = t.shape[-1] // 2
    t1, t2 = t[..., :half], t[..., half:]
    c, s = cos[None, :, None, :], sin[None, :, None, :]
    return jnp.concatenate([t1 * c - t2 * s, t2 * c + t1 * s], axis=-1)


def _dsa_branch(p, cos, sin):
    bsz, seq, _ = p.shape
    k_sel = min(TOPK_MAX, seq // 4)
    kv_w = C_KV_HEADS * C_HEAD_DIM
    q, k, v, qi, ki, wi = _split(p[..., :C_COLS], (C_WIDTH, kv_w, kv_w, IDX_HEADS * IDX_DIM, IDX_DIM, IDX_HEADS))
    q = _apply_rope(q.reshape(bsz, seq, C_HEADS, C_HEAD_DIM), cos, sin)
    k = _apply_rope(k.reshape(bsz, seq, C_KV_HEADS, C_HEAD_DIM), cos, sin)
    v = v.reshape(bsz, seq, C_KV_HEADS, C_HEAD_DIM)
    qi = _apply_rope(qi.reshape(bsz, seq, IDX_HEADS, IDX_DIM), cos, sin)
    ki = _apply_rope(ki[:, :, None, :], cos, sin)[:, :, 0, :]
    wi = wi * (IDX_HEADS ** -0.5) * (IDX_DIM ** -0.5)
    dots = jnp.einsum('bqhd,bsd->bqhs', qi, ki, precision=lax.Precision.HIGHEST)
    score = jnp.sum(jax.nn.relu(dots) * wi[..., None], axis=2)
    pos = jnp.arange(seq)
    causal = pos[None, :] <= pos[:, None]
    bits = lax.bitcast_convert_type(score + 0.0, jnp.int32)
    key = jnp.where(bits < 0, bits ^ jnp.int32(0x7fffffff), bits)
    key = jnp.where(causal[None], key, jnp.int32(-2 ** 31))
    ukey = lax.bitcast_convert_type(key ^ jnp.int32(-2 ** 31), jnp.uint32)
    cand = jnp.zeros((bsz, seq, 1), jnp.uint32)
    for bit in range(31, -1, -1):
        trial = cand | jnp.uint32(1 << bit)
        cnt = jnp.sum((ukey >= trial).astype(jnp.int32), axis=-1, keepdims=True)
        cand = jnp.where(cnt >= k_sel, trial, cand)
    gt = ukey > cand
    eq = ukey == cand
    need = k_sel - jnp.sum(gt.astype(jnp.int32), axis=-1, keepdims=True)
    J = jnp.zeros((bsz, seq, 1), jnp.int32)
    for bit in range(int(np.ceil(np.log2(seq))), -1, -1):
        trial = J | (1 << bit)
        cnt = jnp.sum((eq & (pos[None, None, :] < trial)).astype(jnp.int32), axis=-1, keepdims=True)
        J = jnp.where(cnt <= need, trial, J)
    sel = (gt | (eq & (pos[None, None, :] < J))) & causal[None]
    group = C_HEADS // C_KV_HEADS
    qg = q.reshape(bsz, seq, C_KV_HEADS, group, C_HEAD_DIM)
    logits = jnp.einsum('bqcgd,bscd->bcgqs', qg.astype(BF16), k.astype(BF16),
                        preferred_element_type=F32) * (C_HEAD_DIM ** -0.5)
    logits = jnp.where(sel[:, None, None], logits, -jnp.inf)
    prob = jax.nn.softmax(logits, axis=-1)
    out = jnp.einsum('bcgqs,bscd->bqcgd', prob.astype(BF16), v.astype(BF16), preferred_element_type=F32)
    return out.reshape(bsz, seq, C_WIDTH)


def kernel(x, norm_mix_g, w_in, a_mu, a_w0, a_w_up, a_a0, a_a_up, a_g_up, a_k_k, a_k_a, a_r_k, a_gn_g,
           a_gn_b, b_lb_logits, b_gn_g, w_branch, w_o, norm_ffn_g, w_up, conv_w, conv_b, w_down,
           norm_final_g):
    bsz, seq, d = x.shape
    t = bsz * seq
    lb_cum = jnp.cumsum(jax.nn.softmax(b_lb_logits.astype(F32), axis=0), axis=0)
    lower_bounds = lb_cum - lb_cum[0]
    h = x
    for layer in range(DEPTH):
        w = w_in[layer].astype(BF16)
        w_a, w_b, w_c, w_g = _split(w, (A_COLS, B_COLS, C_COLS, GATE_COLS))
        w_c = jnp.pad(w_c, ((0, 0), (0, C_PAD - C_COLS)))
        h2 = h.reshape(t, d)
        g = norm_mix_g[layer]
        p_a = norm_matmul(h2, g, w_a, tm=2048, tn=A_COLS // 2, out_dtype=BF16).reshape(bsz, seq, -1)
        p_b = norm_matmul(h2, g, w_b, tm=2048, tn=1024, out_dtype=BF16).reshape(bsz, seq, -1)
        p_c = norm_matmul(h2, g, w_c, tm=2048, tn=C_PAD, out_dtype=F32).reshape(bsz, seq, -1)
        p_g = norm_matmul(h2, g, w_g, tm=2048, tn=1024, out_dtype=BF16)
        y_a, y_b = rwkv7_hgrn2(p_a, a_mu[layer], a_w0[layer], a_w_up[layer], a_a0[layer], a_a_up[layer],
                               a_g_up[layer], a_k_k[layer], a_k_a[layer], a_r_k[layer], a_gn_g[layer],
                               a_gn_b[layer], p_b, lower_bounds[layer], b_gn_g[layer])
        y_c = dsa(p_c)
        wbr = w_branch[layer].astype(BF16)
        h2 = merge((y_a.reshape(t, -1), y_b.reshape(t, -1), y_c.reshape(t, -1)), p_g, h2,
                   (wbr[0], wbr[1], wbr[2]), w_o[layer].astype(BF16), tm=512)
        h = conv_glu(h2.reshape(bsz, seq, d), norm_ffn_g[layer], w_up[layer].astype(BF16), conv_w[layer],
                     conv_b[layer], w_down[layer].astype(BF16),
                     norm_final_g if layer == DEPTH - 1 else None, tm=512, tf=1408)
    return h
```

```python
import functools

import jax
import jax.numpy as jnp
from jax import lax
from jax.experimental import pallas as pl
from jax.experimental.pallas import tpu as pltpu

F32 = jnp.float32
BF16 = jnp.bfloat16

D_MODEL = 1024
DEPTH = 2
MIX_WIDTH = 512
N_BRANCH = 3
A_HEAD_DIM = 64
A_HEADS = 8
A_WIDTH = 512
A_RANK_W = 64
A_RANK_A = 64
A_RANK_G = 128
A_GN_EPS = 64e-5
B_KEY_DIM = 128
B_VAL_DIM = 128
B_HEADS = 4
B_WIDTH = 512
C_HEAD_DIM = 64
C_HEADS = 8
C_KV_HEADS = 2
C_WIDTH = 512
IDX_HEADS = 4
IDX_DIM = 64
TOPK_MAX = 256
ROPE_THETA = 10000.0
D_FF = 2816
CONV_WIDTH = 3
NORM_EPS = 1e-6

A_COLS = 3 * A_WIDTH + A_RANK_W + A_RANK_A + A_RANK_G
B_COLS = 4 * B_WIDTH
C_COLS = C_WIDTH + 2 * C_KV_HEADS * C_HEAD_DIM + IDX_HEADS * IDX_DIM + IDX_DIM + IDX_HEADS
GATE_COLS = N_BRANCH * D_MODEL

LANE = 128
SUBLANE = 8
VMEM_LIMIT = 48 * 1024 * 1024
BF16_SUBLANE = 16
PROJ_TM = 2048
PROJ_TN = 1024
MERGE_TM = 512
FFN_TM = 512
FFN_TF = D_FF // 2


def _round_up(n, m):
    return (n + m - 1) // m * m


def _rms(x, g):
    return x * lax.rsqrt(jnp.mean(x * x, axis=-1, keepdims=True) + NORM_EPS) * g


def _norm_matmul_kernel(x_ref, g_ref, w_ref, o_ref, xn_ref):
    @pl.when(pl.program_id(1) == 0)
    def _():
        xn_ref[...] = _rms(x_ref[...], g_ref[...]).astype(BF16)

    o_ref[...] = jnp.dot(xn_ref[...], w_ref[...], preferred_element_type=F32).astype(o_ref.dtype)


def norm_matmul(x, g, w, *, tm, tn, out_dtype):
    t, d = x.shape
    n = w.shape[1]
    assert t % tm == 0 and n % tn == 0
    return pl.pallas_call(
        _norm_matmul_kernel,
        grid=(t // tm, n // tn),
        in_specs=[
            pl.BlockSpec((tm, d), lambda i, j: (i, 0)),
            pl.BlockSpec((1, d), lambda i, j: (0, 0)),
            pl.BlockSpec((d, tn), lambda i, j: (0, j)),
        ],
        out_specs=pl.BlockSpec((tm, tn), lambda i, j: (i, j)),
        out_shape=jax.ShapeDtypeStruct((t, n), out_dtype),
        scratch_shapes=[pltpu.VMEM((tm, d), BF16)],
        compiler_params=pltpu.CompilerParams(
            dimension_semantics=("arbitrary", "arbitrary"), vmem_limit_bytes=VMEM_LIMIT),
        name="norm_matmul",
    )(x, g.reshape(1, d), w)


def _merge_kernel(ya_ref, yb_ref, yc_ref, gate_ref, h_ref, wa_ref, wb_ref, wc_ref, wo_ref, o_ref):
    merged = None
    for b, (y_ref, w_ref) in enumerate(((ya_ref, wa_ref), (yb_ref, wb_ref), (yc_ref, wc_ref))):
        br = jnp.dot(y_ref[...].astype(BF16), w_ref[...], preferred_element_type=F32)
        gt = jax.nn.sigmoid(gate_ref[:, b * D_MODEL:(b + 1) * D_MODEL].astype(F32))
        merged = gt * br if merged is None else merged + gt * br
    o_ref[...] = h_ref[...] + jnp.dot(merged.astype(BF16), wo_ref[...], preferred_element_type=F32)


def merge(ys, gate, h, ws, w_o, *, tm):
    t = h.shape[0]
    assert t % tm == 0
    row = lambda w: pl.BlockSpec((tm, w), lambda i: (i, 0))
    const = lambda a: pl.BlockSpec(a.shape, lambda i: (0, 0))
    return pl.pallas_call(
        _merge_kernel,
        grid=(t // tm,),
        in_specs=[row(y.shape[1]) for y in ys] + [row(GATE_COLS), row(D_MODEL)]
                 + [const(w) for w in ws] + [const(w_o)],
        out_specs=row(D_MODEL),
        out_shape=jax.ShapeDtypeStruct((t, D_MODEL), F32),
        compiler_params=pltpu.CompilerParams(
            dimension_semantics=("arbitrary",), vmem_limit_bytes=VMEM_LIMIT),
        name="merge",
    )(*ys, gate, h, *ws, w_o)


def _shift_rows(cur, prev_tail, shift):
    rolled = pltpu.roll(cur, shift, 0)
    tail = pltpu.roll(prev_tail, shift, 0)
    row = lax.broadcasted_iota(jnp.int32, tail.shape, 0)
    top = jnp.where(row < shift, tail, rolled[:SUBLANE])
    return jnp.concatenate([top, rolled[SUBLANE:]], axis=0)


def _conv_glu_kernel(h_ref, g_ref, wug_ref, wuv_ref, cwg_ref, cwv_ref, cbg_ref, cbv_ref, wd_ref, *rest,
                     final_norm):
    if final_norm:
        gf_ref, o_ref, xn_ref, acc_ref, carry_ref = rest
    else:
        o_ref, xn_ref, acc_ref, carry_ref = rest
    ti = pl.program_id(1)
    fi = pl.program_id(2)
    nf = pl.num_programs(2)

    @pl.when(fi == 0)
    def _():
        xn_ref[...] = _rms(h_ref[0], g_ref[...]).astype(BF16)
        acc_ref[...] = jnp.zeros_like(acc_ref)

    @pl.when(ti == 0)
    def _():
        carry_ref[fi] = jnp.zeros(carry_ref.shape[1:], F32)

    xn = xn_ref[...]

    def conv(w_ref, cw_ref, cb_ref, slot):
        up = jnp.dot(xn, w_ref[...], preferred_element_type=F32)
        tail = carry_ref[fi, slot]
        c = (cb_ref[...] + cw_ref[2:3, :] * up + cw_ref[1:2, :] * _shift_rows(up, tail, 1)
             + cw_ref[0:1, :] * _shift_rows(up, tail, 2))
        carry_ref[fi, slot] = up[up.shape[0] - SUBLANE:]
        return c

    cg = conv(wug_ref, cwg_ref, cbg_ref, 0)
    cv = conv(wuv_ref, cwv_ref, cbv_ref, 1)
    act = (cg * jax.nn.sigmoid(cg) * cv).astype(BF16)
    acc_ref[...] += jnp.dot(act, wd_ref[...], preferred_element_type=F32)

    @pl.when(fi == nf - 1)
    def _():
        out = h_ref[0] + acc_ref[...]
        if final_norm:
            out = _rms(out, gf_ref[...])
        o_ref[0] = out


def conv_glu(h, g, w_up, conv_w, conv_b, w_down, gf, *, tm, tf):
    bsz, seq, d = h.shape
    assert seq % tm == 0 and D_FF % tf == 0
    nf = D_FF // tf
    final_norm = gf is not None
    vec = lambda: pl.BlockSpec((1, d), lambda b, t, f: (0, 0))
    in_specs = [
        pl.BlockSpec((1, tm, d), lambda b, t, f: (b, t, 0)),
        vec(),
        pl.BlockSpec((d, tf), lambda b, t, f: (0, f)),
        pl.BlockSpec((d, tf), lambda b, t, f: (0, f + nf)),
        pl.BlockSpec((CONV_WIDTH, tf), lambda b, t, f: (0, f)),
        pl.BlockSpec((CONV_WIDTH, tf), lambda b, t, f: (0, f + nf)),
        pl.BlockSpec((1, tf), lambda b, t, f: (0, f)),
        pl.BlockSpec((1, tf), lambda b, t, f: (0, f + nf)),
        pl.BlockSpec((tf, d), lambda b, t, f: (f, 0)),
    ]
    args = [h, g.reshape(1, d), w_up, w_up, conv_w, conv_w, conv_b.reshape(1, -1), conv_b.reshape(1, -1), w_down]
    if final_norm:
        in_specs.append(vec())
        args.append(gf.reshape(1, d))
    return pl.pallas_call(
        functools.partial(_conv_glu_kernel, final_norm=final_norm),
        grid=(bsz, seq // tm, nf),
        in_specs=in_specs,
        out_specs=pl.BlockSpec((1, tm, d), lambda b, t, f: (b, t, 0)),
        out_shape=jax.ShapeDtypeStruct((bsz, seq, d), F32),
        scratch_shapes=[pltpu.VMEM((tm, d), BF16), pltpu.VMEM((tm, d), F32),
                        pltpu.VMEM((nf, 2, SUBLANE, tf), F32)],
        compiler_params=pltpu.CompilerParams(
            dimension_semantics=("arbitrary", "arbitrary", "arbitrary"), vmem_limit_bytes=VMEM_LIMIT),
        name="conv_glu",
    )(*args)


HP = LANE
A_PADW = A_HEADS * HP
A_CHUNK = 64
A_NB = 2


def _dot(x, y):
    return jnp.dot(x.astype(BF16), y.astype(BF16), preferred_element_type=F32)


def _dot_t(x, y):
    return lax.dot_general(x.astype(BF16), y.astype(BF16), (((1,), (1,)), ((), ())), preferred_element_type=F32)


def _rwkv_body(p_ref, mu_ref, w0_ref, a0_ref, wa_ref, gup_ref, kk_ref, ka_ref, rk_ref, gng_ref, gnb_ref,
               o_ref, carry_ref, h_ref, tick):
    C = A_CHUNK
    W = A_PADW
    NB = A_NB
    R = NB * C

    @pl.when(pl.program_id(1) == 0)
    def _():
        carry_ref[...] = jnp.zeros_like(carry_ref)
        h_ref[...] = jnp.zeros_like(h_ref)

    blocks = []
    for s in range(NB):
        ps = p_ref[s].astype(F32)
        prev = _shift_rows(ps, carry_ref[s], 1)
        carry_ref[s] = ps[C - SUBLANE:]
        blocks.append(ps + (prev - ps) * mu_ref[...])
    p = jnp.concatenate(blocks, axis=0)
    lane = lax.broadcasted_iota(jnp.int32, (R, LANE), 1)
    valid = lane < A_HEAD_DIM
    valid_c = lax.broadcasted_iota(jnp.int32, (C, LANE), 1) < A_HEAD_DIM

    def pad_heads(x):
        slabs = []
        for h in range(A_HEADS):
            t = x[:, (h // 2) * LANE:(h // 2 + 1) * LANE]
            if h % 2:
                t = pltpu.roll(t, A_HEAD_DIM, 1)
            slabs.append(jnp.where(valid, t, 0.0))
        return jnp.concatenate(slabs, axis=1)
    r, k, v = (pad_heads(p[:, i * A_WIDTH:(i + 1) * A_WIDTH]) for i in range(3))
    low = p[:, 3 * A_WIDTH:3 * A_WIDTH + LANE]
    gd = p[:, 3 * A_WIDTH + LANE:]
    low = jnp.where(lane < A_RANK_W, jnp.tanh(low), low)
    wa = _dot(low, wa_ref[...])
    w_log = -jax.nn.softplus(-(w0_ref[...] + wa[:, :W])) - 0.5
    logw = -jnp.exp(w_log)
    a = jax.nn.sigmoid(a0_ref[...] + wa[:, W:])
    g = _dot(jax.nn.sigmoid(gd), gup_ref[...])
    k_mod = k * (1.0 + (a - 1.0) * ka_ref[...])
    kk = k * kk_ref[...]
    tick()

    row = lax.broadcasted_iota(jnp.int32, (C, C), 0)
    col = lax.broadcasted_iota(jnp.int32, (C, C), 1)
    incl = row >= col
    strict = row > col
    rowr = lax.broadcasted_iota(jnp.int32, (R, R), 0)
    colr = lax.broadcasted_iota(jnp.int32, (R, R), 1)
    same_seq = (rowr // C) == (colr // C)
    cum = jnp.dot((same_seq & (rowr >= colr)).astype(F32), logw, precision=lax.Precision.HIGHEST,
                  preferred_element_type=F32)
    cum_end = [cum[(s + 1) * C - 1:(s + 1) * C, :] for s in range(NB)]
    e_neg = jnp.exp(-cum)
    e_end = jnp.concatenate([jnp.exp(cum_end[s] - cum[s * C:(s + 1) * C]) for s in range(NB)], axis=0)
    r_t = r * jnp.exp(cum)
    a_scale = jnp.exp(cum - logw)
    p_end = [jnp.exp(cum_end[s]) for s in range(NB)]
    eye = (lax.broadcasted_iota(jnp.int32, (HP, HP), 0) == lax.broadcasted_iota(jnp.int32, (HP, HP), 1)).astype(F32)
    eye_c = (row == col).astype(F32)

    chains = [(s, h) for s in range(NB) for h in range(A_HEADS)]
    nch = range(len(chains))

    def blk(x, n):
        s, h = chains[n]
        return x[s * C:(s + 1) * C, h * HP:(h + 1) * HP]

    def each(f):
        out = [f(n) for n in nch]
        tick()
        return out
    kk_n = each(lambda n: blk(kk, n) * lax.rsqrt(jnp.maximum(
        jnp.sum(blk(kk, n) * blk(kk, n), axis=-1, keepdims=True), 1e-24)))
    b_h = each(lambda n: kk_n[n] * blk(a, n))
    at = each(lambda n: -kk_n[n] * blk(a_scale, n))
    ar = each(lambda n: jnp.concatenate([at[n], blk(r_t, n)], axis=0))
    ab = each(lambda n: _dot_t(ar[n], b_h[n] * blk(e_neg, n)))
    ak = each(lambda n: _dot_t(ar[n], blk(k_mod, n) * blk(e_neg, n)))
    a_ab = each(lambda n: jnp.where(strict, ab[n][:C], 0.0))
    a_rb = each(lambda n: jnp.where(incl, ab[n][C:], 0.0))
    a_ak = each(lambda n: jnp.where(strict, ak[n][:C], 0.0))
    a_rk = each(lambda n: jnp.where(incl, ak[n][C:], 0.0))
    tm = each(lambda n: eye_c + a_ab[n])
    pw = each(lambda n: _dot(a_ab[n], a_ab[n]))
    akv = each(lambda n: _dot(a_ak[n], blk(v, n)))
    m = 2
    while 2 * m < C:
        x = each(lambda n: _dot(jnp.concatenate([tm[n], pw[n]], axis=0), pw[n]))
        tm = each(lambda n: tm[n] + x[n][:C])
        pw = each(lambda n: x[n][C:])
        m *= 2
    tm = each(lambda n: tm[n] + _dot(tm[n], pw[n]))
    x2 = each(lambda n: _dot(tm[n], jnp.concatenate([at[n], akv[n]], axis=1)))
    x3 = each(lambda n: _dot(a_rb[n], x2[n]))
    x4 = each(lambda n: _dot((b_h[n] * blk(e_end, n)).T, x2[n]))
    rkv = each(lambda n: _dot(a_rk[n], blk(v, n)))
    kv = each(lambda n: _dot((blk(k_mod, n) * blk(e_end, n)).T, blk(v, n)))
    rm = each(lambda n: jnp.concatenate([blk(r_t, n) + x3[n][:, :HP],
                                         eye * p_end[chains[n][0]][:, chains[n][1] * HP:(chains[n][1] + 1) * HP]
                                         + x4[n][:, :HP]], axis=0))
    x5 = each(lambda n: _dot(rm[n], h_ref[n]))
    for s in range(NB):
        outs = []
        for h in range(A_HEADS):
            n = s * A_HEADS + h
            sl = slice(h * HP, (h + 1) * HP)
            h_ref[n] = x5[n][C:] + x4[n][:, HP:] + kv[n]
            y = x5[n][:C] + x3[n][:, HP:] + rkv[n]
            mean = jnp.sum(y, axis=-1, keepdims=True) * (1.0 / A_HEAD_DIM)
            dlt = jnp.where(valid_c, y - mean, 0.0)
            var = jnp.sum(dlt * dlt, axis=-1, keepdims=True) * (1.0 / A_HEAD_DIM)
            yn = dlt * lax.rsqrt(var + A_GN_EPS) * gng_ref[:, sl] + gnb_ref[:, sl]
            bonus = jnp.sum(blk(r, n) * blk(k_mod, n) * rk_ref[:, sl], axis=-1, keepdims=True) * blk(v, n)
            outs.append((yn + bonus) * blk(g, n))
        o_ref[s] = jnp.concatenate(
            [outs[2 * i] + pltpu.roll(outs[2 * i + 1], A_HEAD_DIM, 1) for i in range(A_HEADS // 2)],
            axis=1).astype(o_ref.dtype)


def _pad_heads(t, heads):
    hd = t.shape[-1] // heads
    t = t.reshape(t.shape[:-1] + (heads, hd))
    t = jnp.pad(t, [(0, 0)] * (t.ndim - 1) + [(0, HP - hd)])
    return t.reshape(t.shape[:-2] + (heads * HP,))


B_CHUNK = 64
B_SUB = 16
B_NB = 2


def _dot_tl(x, y):
    return lax.dot_general(x.astype(BF16), y.astype(BF16), (((0,), (0,)), ((), ())), preferred_element_type=F32)


def _hgrn_stages(p_ref, lb_ref, gn_ref, o_ref, st_ref):
    C, SB, W = B_CHUNK, B_SUB, B_WIDTH
    nsb = C // SB
    D = B_KEY_DIM
    NB = B_NB
    R = NB * C

    @pl.when(pl.program_id(1) == 0)
    def _():
        st_ref[...] = jnp.zeros_like(st_ref)

    p = jnp.concatenate([p_ref[s].astype(F32) for s in range(NB)], axis=0)
    q, f, val, gate = p[:, :W], p[:, W:2 * W], p[:, 2 * W:3 * W], p[:, 3 * W:]
    lb = lb_ref[...]
    log_f = jnp.logaddexp(jnp.log(lb), jnp.log1p(-lb) + jax.nn.log_sigmoid(f))
    k = (1.0 - lb) * jax.nn.sigmoid(-f)
    yield
    rowr = lax.broadcasted_iota(jnp.int32, (R, R), 0)
    colr = lax.broadcasted_iota(jnp.int32, (R, R), 1)
    b = jnp.dot((((rowr // C) == (colr // C)) & (rowr >= colr)).astype(F32), log_f,
                precision=lax.Precision.HIGHEST, preferred_element_type=F32)
    yield
    b0 = jnp.concatenate(
        [jnp.zeros((SB, W), F32) if i % nsb == 0 else jnp.broadcast_to(b[i * SB - 1:i * SB, :], (SB, W))
         for i in range(NB * nsb)], axis=0)
    qt = q * jnp.exp(b - b0)
    qe = q * jnp.exp(b)
    b_end = [b[(s + 1) * C - 1:(s + 1) * C, :] for s in range(NB)]
    ke = jnp.concatenate([k[s * C:(s + 1) * C] * jnp.exp(b_end[s] - b[s * C:(s + 1) * C]) for s in range(NB)],
                         axis=0)
    e_end = [jnp.exp(b_end[s]) for s in range(NB)]
    rowd = lax.broadcasted_iota(jnp.int32, (C, D), 0)
    lane_c = lax.broadcasted_iota(jnp.int32, (SB, C), 1)
    row_sb = lax.broadcasted_iota(jnp.int32, (SB, C), 0)

    chains = [(s, h) for s in range(NB) for h in range(B_HEADS)]
    each = lambda fn: [fn(n) for n in range(len(chains))]

    def blk(x, n):
        s, h = chains[n]
        return x[s * C:(s + 1) * C, h * D:(h + 1) * D]

    def off_operands(n):
        k_n, b_n, qt_n = blk(k, n), blk(b, n), blk(qt, n)
        kts, qts = [], []
        for i in range(1, nsb):
            b0_i = b_n[i * SB - 1:i * SB]
            kts.append(jnp.where(rowd < i * SB, k_n * jnp.exp(jnp.minimum(b0_i - b_n, 0.0)), 0.0))
            qts.append(jnp.where((rowd >= i * SB) & (rowd < (i + 1) * SB), qt_n, 0.0))
        return jnp.concatenate(kts, axis=1), jnp.concatenate(qts, axis=1)
    yield
    offs = []
    for n in range(len(chains)):
        offs.append(off_operands(n))
        yield
    st_off = each(lambda n: _dot_t(offs[n][0], offs[n][1]))

    st_diag = []
    for n in range(len(chains)):
        k_n, b_n, q_n = blk(k, n), blk(b, n), blk(q, n)
        blocks = []
        for i in range(nsb):
            rs = slice(i * SB, (i + 1) * SB)
            k_i, b_i = k_n[rs], b_n[rs]
            acc = jnp.zeros((SB, C), F32)
            for j in range(SB):
                t = i * SB + j
                z = k_i * q_n[t:t + 1] * jnp.exp(b_n[t:t + 1] - b_i)
                acc = jnp.where(lane_c == t, jnp.sum(z, axis=-1, keepdims=True), acc)
            blocks.append(jnp.where(row_sb + i * SB <= lane_c, acc, 0.0))
            yield
        st_diag.append(jnp.concatenate(blocks, axis=0))
    o_intra = each(lambda n: _dot_tl(st_off[n] + st_diag[n], blk(val, n)))
    yield
    o_state = each(lambda n: _dot_t(blk(qe, n), st_ref[n]))
    upd = each(lambda n: _dot_tl(blk(val, n), blk(ke, n)))
    yield
    for s in range(NB):
        outs = []
        for h in range(B_HEADS):
            n = s * B_HEADS + h
            st_ref[n] = st_ref[n] * e_end[s][:, h * D:(h + 1) * D] + upd[n]
            o = o_intra[n] + o_state[n]
            o = o * lax.rsqrt(jnp.mean(o * o, axis=-1, keepdims=True) + NORM_EPS)
            gt = blk(gate, n)
            outs.append(o * gn_ref[:, h * D:(h + 1) * D] * (gt * jax.nn.sigmoid(gt)))
        o_ref[s] = jnp.concatenate(outs, axis=1).astype(o_ref.dtype)


N_RWKV_IN = 11
N_HGRN_IN = 3


def _rwkv_hgrn_kernel(*refs):
    a_in, b_in = refs[:N_RWKV_IN], refs[N_RWKV_IN:N_RWKV_IN + N_HGRN_IN]
    a_out, b_out, a_carry, a_state, b_state = refs[N_RWKV_IN + N_HGRN_IN:]
    other = _hgrn_stages(*b_in, b_out, b_state)
    _rwkv_body(*a_in, a_out, a_carry, a_state, tick=lambda: next(other, None))
    for _ in other:
        pass


def rwkv7_hgrn2(p_a, mu, w0, w_up, a0, a_up, g_up, k_k, k_a, r_k, gn_g, gn_b, p_b, lb, b_gn_g):
    bsz, seq, _ = p_a.shape
    C, NB = A_CHUNK, A_NB
    assert (B_CHUNK, B_NB) == (C, NB) and seq % C == 0 and bsz % NB == 0
    z = jnp.zeros((A_RANK_W, A_PADW), F32)
    wa = jnp.concatenate([jnp.concatenate([_pad_heads(w_up, A_HEADS), z], axis=1),
                          jnp.concatenate([z, _pad_heads(a_up, A_HEADS)], axis=1)], axis=0).astype(BF16)
    vecs = [_pad_heads(t.reshape(-1), A_HEADS).reshape(1, -1) for t in (k_k, k_a, r_k, gn_g, gn_b)]
    full = lambda shape: pl.BlockSpec(shape, lambda b, c: (0,) * len(shape))
    rows = lambda w: pl.BlockSpec((NB, C, w), lambda b, c: (b, c, 0))
    return pl.pallas_call(
        _rwkv_hgrn_kernel,
        grid=(bsz // NB, seq // C),
        in_specs=[rows(A_COLS), full((1, A_COLS)), full((1, A_PADW)), full((1, A_PADW)),
                  full((LANE, 2 * A_PADW)), full((A_RANK_G, A_PADW))] + [full((1, A_PADW))] * 5
                 + [rows(B_COLS), full((1, B_WIDTH)), full((1, B_WIDTH))],
        out_specs=[rows(A_WIDTH), rows(B_WIDTH)],
        out_shape=[jax.ShapeDtypeStruct((bsz, seq, A_WIDTH), BF16), jax.ShapeDtypeStruct((bsz, seq, B_WIDTH), BF16)],
        scratch_shapes=[pltpu.VMEM((NB, SUBLANE, A_COLS), F32), pltpu.VMEM((NB * A_HEADS, HP, HP), F32),
                        pltpu.VMEM((NB * B_HEADS, B_VAL_DIM, B_KEY_DIM), F32)],
        compiler_params=pltpu.CompilerParams(
            dimension_semantics=("arbitrary", "arbitrary"), vmem_limit_bytes=VMEM_LIMIT),
        name="rwkv7_hgrn2",
    )(p_a, mu.reshape(1, -1), _pad_heads(w0, A_HEADS).reshape(1, -1), _pad_heads(a0, A_HEADS).reshape(1, -1), wa,
      _pad_heads(g_up, A_HEADS).astype(BF16), *vecs, p_b, lb.reshape(1, -1), b_gn_g.reshape(1, -1))


C_PAD = _round_up(C_COLS, LANE)
QB = 128
QW = 2 * QB
C_KV_W = C_KV_HEADS * C_HEAD_DIM
C_K_OFF = C_WIDTH
C_QI_OFF = C_WIDTH + 2 * C_KV_W
C_KI_OFF = C_QI_OFF + IDX_HEADS * IDX_DIM
GROUP = C_HEADS // C_KV_HEADS
INT_MIN = -2 ** 31
NEG = -1e30
HALF = C_HEAD_DIM // 2
VT_ROWS = C_HEAD_DIM + BF16_SUBLANE


def _rope(t, cos, sin_signed):
    outs = []
    lane = lax.broadcasted_iota(jnp.int32, (t.shape[0], LANE), 1)
    first_half = (lane % C_HEAD_DIM) < HALF
    for c in range(t.shape[1] // LANE):
        x = t[:, c * LANE:(c + 1) * LANE]
        swapped = jnp.where(first_half, pltpu.roll(x, LANE - HALF, 1), pltpu.roll(x, HALF, 1))
        outs.append(x * cos + swapped * sin_signed)
    return outs


def _dsa_kernel(pq_ref, kv_ref, ki_ref, cosq_ref, sinq_ref, cosf_ref, sinf_ref, o_ref,
                k_s, vt_s, ki_s, key_s, *, k_sel, seq):
    i = pl.program_id(1)
    nblk = i + 1
    lane = lax.broadcasted_iota(jnp.int32, (QW, LANE), 1)
    low_half = lane < C_HEAD_DIM
    row_k = lax.broadcasted_iota(jnp.int32, (QB, QW), 0)
    q_pos = i * QW + lax.broadcasted_iota(jnp.int32, (QB, QW), 1)

    @pl.when(i == 0)
    def _():
        kv = kv_ref[0].astype(F32)
        k_s[...] = _rope(kv[:, :LANE], cosf_ref[...], sinf_ref[...])[0].astype(BF16)
        vt = kv[:, LANE:].T
        extra = (lax.broadcasted_iota(jnp.int32, (VT_ROWS - C_HEAD_DIM, seq), 0) == 0).astype(F32)
        for c in range(C_KV_HEADS):
            vt_s[c] = jnp.concatenate([vt[c * C_HEAD_DIM:(c + 1) * C_HEAD_DIM], extra], axis=0).astype(BF16)
        lane_f = lax.broadcasted_iota(jnp.int32, (seq, LANE), 1)
        ki = jnp.where(lane_f < IDX_DIM, ki_ref[0].astype(F32), 0.0)
        ki = _rope(ki, cosf_ref[...], sinf_ref[...])[0]
        ki_s[...] = (ki + pltpu.roll(ki, IDX_DIM, 1)).astype(BF16)

    pq = pq_ref[0].astype(F32)
    cos, sin = cosq_ref[...], sinq_ref[...]
    q = _rope(pq[:, :C_WIDTH], cos, sin)
    qi = _rope(pq[:, C_QI_OFF:C_KI_OFF], cos, sin)
    qi3 = jnp.concatenate(
        [jnp.where(low_half if h % 2 == 0 else ~low_half, qi[h // 2], 0.0) for h in range(IDX_HEADS)],
        axis=0).astype(BF16)
    wt = pq[:, C_KI_OFF:C_KI_OFF + LANE].T
    wscale = (IDX_HEADS ** -0.5) * (IDX_DIM ** -0.5)
    w_rows = [wt[IDX_DIM + h:IDX_DIM + h + 1, :] * wscale for h in range(IDX_HEADS)]

    def score_blk(jb, _):
        ki_t = ki_s[pl.ds(pl.multiple_of(jb * QW, QW), QW), :]
        d = lax.dot_general(ki_t, qi3, (((1,), (1,)), ((), ())), preferred_element_type=F32)
        for u in range(QW // QB):
            j = (QW // QB) * jb + u
            sc = jnp.zeros((QB, QW), F32)
            for h in range(IDX_HEADS):
                sc = sc + jnp.maximum(d[u * QB:(u + 1) * QB, h * QW:(h + 1) * QW], 0.0) * w_rows[h]
            bits = pltpu.bitcast(sc + 0.0, jnp.int32)
            key = jnp.where(bits < 0, bits ^ jnp.int32(0x7fffffff), bits)
            key_s[j] = jnp.where(j * QB + row_k <= q_pos, key, jnp.int32(INT_MIN))
        return 0
    lax.fori_loop(0, nblk, score_blk, 0)

    def count(pred):
        def body(jb, acc):
            for u in range(QW // QB):
                hit = jnp.where(pred(key_s[(QW // QB) * jb + u]), 1, 0)
                acc = acc + jnp.sum(hit.reshape(QB // SUBLANE, SUBLANE, QW), axis=0)
            return acc
        return jnp.sum(lax.fori_loop(0, nblk, body, jnp.zeros((SUBLANE, QW), jnp.int32)), axis=0, keepdims=True)

    def bit_step(b, cand):
        trial = cand | jnp.left_shift(jnp.int32(1), 31 - b)
        cnt = count(lambda key: key >= (trial ^ jnp.int32(INT_MIN)))
        return jnp.where(cnt >= k_sel, trial, cand)
    cand = lax.fori_loop(0, 32, bit_step, jnp.zeros((1, QW), jnp.int32))
    thr = cand ^ jnp.int32(INT_MIN)
    need_f = (k_sel - count(lambda key: key > thr)).astype(F32)
    tri = (lax.broadcasted_iota(jnp.int32, (QW, QW), 0)
           >= lax.broadcasted_iota(jnp.int32, (QW, QW), 1)).astype(BF16)

    scale = C_HEAD_DIM ** -0.5
    lane_b = lax.broadcasted_iota(jnp.int32, (QB, LANE), 1)
    in_cs = [(lane_b >= c * C_HEAD_DIM) & (lane_b < (c + 1) * C_HEAD_DIM) for c in range(C_KV_HEADS)]
    chains = [(n, e) for e in range(QW // QB) for n in range(C_HEADS // 2)]
    kv_of = [n * 2 // GROUP for n, _ in chains]

    def q_rows(n, e):
        c = n * 2 // GROUP
        rows = []
        for h in (2 * n, 2 * n + 1):
            x = q[n][e * QB:(e + 1) * QB]
            if h % 2 != c:
                x = pltpu.roll(x, C_HEAD_DIM, 1)
            rows.append(jnp.where(in_cs[c], x * scale, 0.0))
        return jnp.concatenate(rows, axis=0).astype(BF16)
    qcs = [q_rows(n, e) for n, e in chains]
    nch = range(len(chains))

    def att_blk(jb, carry):
        off = pl.multiple_of(jb * QW, QW)
        keys = jnp.concatenate([key_s[(QW // QB) * jb + u] for u in range(QW // QB)], axis=0)
        pos = off + lax.broadcasted_iota(jnp.int32, (QW, QW), 0)
        tie = keys == thr
        rank = carry[-1] + jnp.dot(tri, jnp.where(tie, 1.0, 0.0).astype(BF16), preferred_element_type=F32)
        sel = ((keys > thr) | (tie & (rank <= need_f))) & (pos <= q_pos[:1, :])
        bias1 = jnp.where(sel, 0.0, NEG)
        bias = [jnp.concatenate([bias1[:, e * QB:(e + 1) * QB]] * 2, axis=1) for e in range(QW // QB)]
        k_t = k_s[pl.ds(off, QW), :]
        v_t = [vt_s[c, :, pl.ds(off, QW)] for c in range(C_KV_HEADS)]
        s = [lax.dot_general(k_t, qcs[x], (((1,), (1,)), ((), ())), preferred_element_type=F32) for x in nch]
        s = [s[x] + bias[chains[x][1]] for x in nch]
        m_new = [jnp.maximum(carry[2 * x], jnp.max(s[x], axis=0, keepdims=True)) for x in nch]
        p = [jnp.exp(s[x] - m_new[x]).astype(BF16) for x in nch]
        pv = [jnp.dot(v_t[kv_of[x]], p[x], preferred_element_type=F32) for x in nch]
        out = []
        for x in nch:
            out += [m_new[x], carry[2 * x + 1] * jnp.exp(carry[2 * x] - m_new[x]) + pv[x]]
        return tuple(out) + (rank[QW - 1:, :],)
    init = ((jnp.full((1, 2 * QB), NEG, F32), jnp.zeros((VT_ROWS, 2 * QB), F32)) * len(chains)
            + (jnp.zeros((1, QW), F32),))
    res = lax.fori_loop(0, nblk, att_blk, init)

    halves = []
    for e in range(QW // QB):
        tiles = []
        for n in range(C_HEADS // 2):
            acc = res[2 * chains.index((n, e)) + 1]
            out_t = acc[:C_HEAD_DIM] / acc[C_HEAD_DIM:C_HEAD_DIM + 1]
            tiles.append(jnp.concatenate([out_t[:, :QB], out_t[:, QB:]], axis=0).T)
        halves.append(jnp.concatenate(tiles, axis=1))
    o_ref[0] = jnp.concatenate(halves, axis=0).astype(o_ref.dtype)


def _rope_lanes(seq):
    inv_freq = ROPE_THETA ** (-jnp.arange(HALF, dtype=F32) * (2.0 / C_HEAD_DIM))
    ang = jnp.arange(seq, dtype=F32)[:, None] * inv_freq[None, :]
    cos, sin = jnp.cos(ang), jnp.sin(ang)
    reps = LANE // C_HEAD_DIM
    return (jnp.tile(jnp.concatenate([cos, cos], axis=1), (1, reps)),
            jnp.tile(jnp.concatenate([-sin, sin], axis=1), (1, reps)))


def dsa(p):
    bsz, seq, width = p.shape
    assert seq % QW == 0 and width == C_PAD
    k_sel = min(TOPK_MAX, seq // 4)
    cos, sin = _rope_lanes(seq)
    return pl.pallas_call(
        functools.partial(_dsa_kernel, k_sel=k_sel, seq=seq),
        grid=(bsz, seq // QW),
        in_specs=[pl.BlockSpec((1, QW, C_PAD), lambda b, i: (b, i, 0)),
                  pl.BlockSpec((1, seq, 2 * C_KV_W), lambda b, i: (b, 0, C_K_OFF // (2 * C_KV_W))),
                  pl.BlockSpec((1, seq, LANE), lambda b, i: (b, 0, C_KI_OFF // LANE)),
                  pl.BlockSpec((QW, LANE), lambda b, i: (i, 0)),
                  pl.BlockSpec((QW, LANE), lambda b, i: (i, 0)),
                  pl.BlockSpec((seq, LANE), lambda b, i: (0, 0)),
                  pl.BlockSpec((seq, LANE), lambda b, i: (0, 0))],
        out_specs=pl.BlockSpec((1, QW, C_WIDTH), lambda b, i: (b, i, 0)),
        out_shape=jax.ShapeDtypeStruct((bsz, seq, C_WIDTH), BF16),
        scratch_shapes=[pltpu.VMEM((seq, LANE), BF16), pltpu.VMEM((C_KV_HEADS, VT_ROWS, seq), BF16),
                        pltpu.VMEM((seq, LANE), BF16), pltpu.VMEM((seq // QB, QB, QW), jnp.int32)],
        compiler_params=pltpu.CompilerParams(
            dimension_semantics=("arbitrary", "arbitrary"), vmem_limit_bytes=VMEM_LIMIT),
        name="dsa",
    )(p, p, p, cos, sin, cos, sin)


def _split(t, sizes):
    out, start = [], 0
    for s in sizes:
        out.append(t[..., start:start + s])
        start += s
    return out


def kernel(x, norm_mix_g, w_in, a_mu, a_w0, a_w_up, a_a0, a_a_up, a_g_up, a_k_k, a_k_a, a_r_k, a_gn_g,
           a_gn_b, b_lb_logits, b_gn_g, w_branch, w_o, norm_ffn_g, w_up, conv_w, conv_b, w_down,
           norm_final_g):
    bsz, seq, d = x.shape
    t = bsz * seq
    lb_cum = jnp.cumsum(jax.nn.softmax(b_lb_logits.astype(F32), axis=0), axis=0)
    lower_bounds = lb_cum - lb_cum[0]
    h = x
    for layer in range(DEPTH):
        w = w_in[layer].astype(BF16)
        w_a, w_b, w_c, w_g = _split(w, (A_COLS, B_COLS, C_COLS, GATE_COLS))
        w_c = jnp.pad(w_c, ((0, 0), (0, C_PAD - C_COLS)))
        h2 = h.reshape(t, d)
        g = norm_mix_g[layer]
        p_a = norm_matmul(h2, g, w_a, tm=PROJ_TM, tn=A_COLS // 2, out_dtype=BF16).reshape(bsz, seq, -1)
        p_b = norm_matmul(h2, g, w_b, tm=PROJ_TM, tn=PROJ_TN, out_dtype=BF16).reshape(bsz, seq, -1)
        p_c = norm_matmul(h2, g, w_c, tm=PROJ_TM, tn=C_PAD, out_dtype=F32).reshape(bsz, seq, -1)
        p_g = norm_matmul(h2, g, w_g, tm=PROJ_TM, tn=PROJ_TN, out_dtype=BF16)
        y_a, y_b = rwkv7_hgrn2(p_a, a_mu[layer], a_w0[layer], a_w_up[layer], a_a0[layer], a_a_up[layer],
                               a_g_up[layer], a_k_k[layer], a_k_a[layer], a_r_k[layer], a_gn_g[layer],
                               a_gn_b[layer], p_b, lower_bounds[layer], b_gn_g[layer])
        y_c = dsa(p_c)
        wbr = w_branch[layer].astype(BF16)
        h2 = merge((y_a.reshape(t, -1), y_b.reshape(t, -1), y_c.reshape(t, -1)), p_g, h2,
                   (wbr[0], wbr[1], wbr[2]), w_o[layer].astype(BF16), tm=MERGE_TM)
        h = conv_glu(h2.reshape(bsz, seq, d), norm_ffn_g[layer], w_up[layer].astype(BF16), conv_w[layer],
                     conv_b[layer], w_down[layer].astype(BF16),
                     norm_final_g if layer == DEPTH - 1 else None, tm=FFN_TM, tf=FFN_TF)
    return h
```

```python
import functools

import jax
import jax.numpy as jnp
from jax import lax
from jax.experimental import pallas as pl
from jax.experimental.pallas import tpu as pltpu

F32 = jnp.float32
BF16 = jnp.bfloat16

D_MODEL = 1024
DEPTH = 2
MIX_WIDTH = 512
N_BRANCH = 3
A_HEAD_DIM = 64
A_HEADS = 8
A_WIDTH = 512
A_RANK_W = 64
A_RANK_A = 64
A_RANK_G = 128
A_GN_EPS = 64e-5
B_KEY_DIM = 128
B_VAL_DIM = 128
B_HEADS = 4
B_WIDTH = 512
C_HEAD_DIM = 64
C_HEADS = 8
C_KV_HEADS = 2
C_WIDTH = 512
IDX_HEADS = 4
IDX_DIM = 64
TOPK_MAX = 256
ROPE_THETA = 10000.0
D_FF = 2816
CONV_WIDTH = 3
NORM_EPS = 1e-6

A_COLS = 3 * A_WIDTH + A_RANK_W + A_RANK_A + A_RANK_G
B_COLS = 4 * B_WIDTH
C_COLS = C_WIDTH + 2 * C_KV_HEADS * C_HEAD_DIM + IDX_HEADS * IDX_DIM + IDX_DIM + IDX_HEADS
GATE_COLS = N_BRANCH * D_MODEL

LANE = 128
SUBLANE = 8
VMEM_LIMIT = 48 * 1024 * 1024
BF16_SUBLANE = 16
PROJ_TM = 2048
PROJ_TN = 1024
MERGE_TM = 512
FFN_TM = 512
FFN_TF = D_FF // 2


def _round_up(n, m):
    return (n + m - 1) // m * m


def _rms(x, g):
    return x * lax.rsqrt(jnp.mean(x * x, axis=-1, keepdims=True) + NORM_EPS) * g


def _norm_matmul_kernel(x_ref, g_ref, w_ref, o_ref, xn_ref):
    @pl.when(pl.program_id(1) == 0)
    def _():
        xn_ref[...] = _rms(x_ref[...], g_ref[...]).astype(BF16)

    o_ref[...] = jnp.dot(xn_ref[...], w_ref[...], preferred_element_type=F32).astype(o_ref.dtype)


def norm_matmul(x, g, w, *, tm, tn, out_dtype):
    t, d = x.shape
    n = w.shape[1]
    assert t % tm == 0 and n % tn == 0
    return pl.pallas_call(
        _norm_matmul_kernel,
        grid=(t // tm, n // tn),
        in_specs=[
            pl.BlockSpec((tm, d), lambda i, j: (i, 0)),
            pl.BlockSpec((1, d), lambda i, j: (0, 0)),
            pl.BlockSpec((d, tn), lambda i, j: (0, j)),
        ],
        out_specs=pl.BlockSpec((tm, tn), lambda i, j: (i, j)),
        out_shape=jax.ShapeDtypeStruct((t, n), out_dtype),
        scratch_shapes=[pltpu.VMEM((tm, d), BF16)],
        compiler_params=pltpu.CompilerParams(
            dimension_semantics=("arbitrary", "arbitrary"), vmem_limit_bytes=VMEM_LIMIT),
        name="norm_matmul",
    )(x, g.reshape(1, d), w)


def _merge_kernel(ya_ref, yb_ref, yc_ref, gate_ref, h_ref, wa_ref, wb_ref, wc_ref, wo_ref, o_ref):
    merged = None
    for b, (y_ref, w_ref) in enumerate(((ya_ref, wa_ref), (yb_ref, wb_ref), (yc_ref, wc_ref))):
        br = jnp.dot(y_ref[...].astype(BF16), w_ref[...], preferred_element_type=F32)
        gt = jax.nn.sigmoid(gate_ref[:, b * D_MODEL:(b + 1) * D_MODEL].astype(F32))
        merged = gt * br if merged is None else merged + gt * br
    o_ref[...] = h_ref[...] + jnp.dot(merged.astype(BF16), wo_ref[...], preferred_element_type=F32)


def merge(ys, gate, h, w_branch, w_o, *, layer, tm):
    t = h.shape[0]
    assert t % tm == 0
    row = lambda w: pl.BlockSpec((tm, w), lambda i: (i, 0))
    branch_w = [pl.BlockSpec((None, None, MIX_WIDTH, D_MODEL), lambda i, b=b: (layer, b, 0, 0))
                for b in range(N_BRANCH)]
    return pl.pallas_call(
        _merge_kernel,
        grid=(t // tm,),
        in_specs=[row(MIX_WIDTH)] * N_BRANCH + [row(GATE_COLS), row(D_MODEL)] + branch_w
                 + [pl.BlockSpec((None, D_MODEL, D_MODEL), lambda i: (layer, 0, 0))],
        out_specs=row(D_MODEL),
        out_shape=jax.ShapeDtypeStruct((t, D_MODEL), F32),
        compiler_params=pltpu.CompilerParams(
            dimension_semantics=("arbitrary",), vmem_limit_bytes=VMEM_LIMIT),
        name="merge",
    )(*ys, gate, h, *([w_branch] * N_BRANCH), w_o)


def _shift_rows(cur, prev_tail, shift):
    rolled = pltpu.roll(cur, shift, 0)
    tail = pltpu.roll(prev_tail, shift, 0)
    row = lax.broadcasted_iota(jnp.int32, tail.shape, 0)
    top = jnp.where(row < shift, tail, rolled[:SUBLANE])
    return jnp.concatenate([top, rolled[SUBLANE:]], axis=0)


def _conv_glu_kernel(h_ref, g_ref, wug_ref, wuv_ref, cwg_ref, cwv_ref, cbg_ref, cbv_ref, wd_ref, *rest,
                     final_norm):
    if final_norm:
        gf_ref, o_ref, xn_ref, acc_ref, carry_ref = rest
    else:
        o_ref, xn_ref, acc_ref, carry_ref = rest
    ti = pl.program_id(1)
    fi = pl.program_id(2)
    nf = pl.num_programs(2)

    @pl.when(fi == 0)
    def _():
        xn_ref[...] = _rms(h_ref[0], g_ref[...]).astype(BF16)
        acc_ref[...] = jnp.zeros_like(acc_ref)

    @pl.when(ti == 0)
    def _():
        carry_ref[fi] = jnp.zeros(carry_ref.shape[1:], F32)

    xn = xn_ref[...]

    def conv(w_ref, cw_ref, cb_ref, slot):
        up = jnp.dot(xn, w_ref[...], preferred_element_type=F32)
        tail = carry_ref[fi, slot]
        c = (cb_ref[...] + cw_ref[2:3, :] * up + cw_ref[1:2, :] * _shift_rows(up, tail, 1)
             + cw_ref[0:1, :] * _shift_rows(up, tail, 2))
        carry_ref[fi, slot] = up[up.shape[0] - SUBLANE:]
        return c

    cg = conv(wug_ref, cwg_ref, cbg_ref, 0)
    cv = conv(wuv_ref, cwv_ref, cbv_ref, 1)
    act = (cg * jax.nn.sigmoid(cg) * cv).astype(BF16)
    acc_ref[...] += jnp.dot(act, wd_ref[...], preferred_element_type=F32)

    @pl.when(fi == nf - 1)
    def _():
        out = h_ref[0] + acc_ref[...]
        if final_norm:
            out = _rms(out, gf_ref[...])
        o_ref[0] = out


def conv_glu(h, g, w_up, conv_w, conv_b, w_down, gf, *, layer, tm, tf):
    bsz, seq, d = h.shape
    assert seq % tm == 0 and D_FF % tf == 0
    nf = D_FF // tf
    final_norm = gf is not None
    vec = lambda: pl.BlockSpec((1, d), lambda b, t, f: (0, 0))
    in_specs = [
        pl.BlockSpec((1, tm, d), lambda b, t, f: (b, t, 0)),
        vec(),
        pl.BlockSpec((None, d, tf), lambda b, t, f: (layer, 0, f)),
        pl.BlockSpec((None, d, tf), lambda b, t, f: (layer, 0, f + nf)),
        pl.BlockSpec((CONV_WIDTH, tf), lambda b, t, f: (0, f)),
        pl.BlockSpec((CONV_WIDTH, tf), lambda b, t, f: (0, f + nf)),
        pl.BlockSpec((1, tf), lambda b, t, f: (0, f)),
        pl.BlockSpec((1, tf), lambda b, t, f: (0, f + nf)),
        pl.BlockSpec((None, tf, d), lambda b, t, f: (layer, f, 0)),
    ]
    args = [h, g.reshape(1, d), w_up, w_up, conv_w, conv_w, conv_b.reshape(1, -1), conv_b.reshape(1, -1), w_down]
    if final_norm:
        in_specs.append(vec())
        args.append(gf.reshape(1, d))
    return pl.pallas_call(
        functools.partial(_conv_glu_kernel, final_norm=final_norm),
        grid=(bsz, seq // tm, nf),
        in_specs=in_specs,
        out_specs=pl.BlockSpec((1, tm, d), lambda b, t, f: (b, t, 0)),
        out_shape=jax.ShapeDtypeStruct((bsz, seq, d), F32),
        scratch_shapes=[pltpu.VMEM((tm, d), BF16), pltpu.VMEM((tm, d), F32),
                        pltpu.VMEM((nf, 2, SUBLANE, tf), F32)],
        compiler_params=pltpu.CompilerParams(
            dimension_semantics=("arbitrary", "arbitrary", "arbitrary"), vmem_limit_bytes=VMEM_LIMIT),
        name="conv_glu",
    )(*args)


HP = LANE
A_PADW = A_HEADS * HP
A_CHUNK = 64
A_NB = 2


def _dot(x, y):
    return jnp.dot(x.astype(BF16), y.astype(BF16), preferred_element_type=F32)


def _dot_t(x, y):
    return lax.dot_general(x.astype(BF16), y.astype(BF16), (((1,), (1,)), ((), ())), preferred_element_type=F32)


def _rwkv_body(p_ref, mu_ref, w0_ref, a0_ref, wa_ref, gup_ref, kk_ref, ka_ref, rk_ref, gng_ref, gnb_ref,
               o_ref, carry_ref, h_ref, tick):
    C = A_CHUNK
    W = A_PADW
    NB = A_NB
    R = NB * C

    @pl.when(pl.program_id(1) == 0)
    def _():
        carry_ref[...] = jnp.zeros_like(carry_ref)
        h_ref[...] = jnp.zeros_like(h_ref)

    blocks = []
    for s in range(NB):
        ps = p_ref[s].astype(F32)
        prev = _shift_rows(ps, carry_ref[s], 1)
        carry_ref[s] = ps[C - SUBLANE:]
        blocks.append(ps + (prev - ps) * mu_ref[...])
    p = jnp.concatenate(blocks, axis=0)
    lane = lax.broadcasted_iota(jnp.int32, (R, LANE), 1)
    valid = lane < A_HEAD_DIM
    valid_c = lax.broadcasted_iota(jnp.int32, (C, LANE), 1) < A_HEAD_DIM

    def pad_heads(x):
        slabs = []
        for h in range(A_HEADS):
            t = x[:, (h // 2) * LANE:(h // 2 + 1) * LANE]
            if h % 2:
                t = pltpu.roll(t, A_HEAD_DIM, 1)
            slabs.append(jnp.where(valid, t, 0.0))
        return jnp.concatenate(slabs, axis=1)
    r, k, v = (pad_heads(p[:, i * A_WIDTH:(i + 1) * A_WIDTH]) for i in range(3))
    low = p[:, 3 * A_WIDTH:3 * A_WIDTH + LANE]
    gd = p[:, 3 * A_WIDTH + LANE:]
    low = jnp.where(lane < A_RANK_W, jnp.tanh(low), low)
    wa = _dot(low, wa_ref[...])
    w_log = -jax.nn.softplus(-(w0_ref[...] + wa[:, :W])) - 0.5
    logw = -jnp.exp(w_log)
    a = jax.nn.sigmoid(a0_ref[...] + wa[:, W:])
    g = _dot(jax.nn.sigmoid(gd), gup_ref[...])
    k_mod = k * (1.0 + (a - 1.0) * ka_ref[...])
    kk = k * kk_ref[...]
    tick()

    row = lax.broadcasted_iota(jnp.int32, (C, C), 0)
    col = lax.broadcasted_iota(jnp.int32, (C, C), 1)
    incl = row >= col
    strict = row > col
    rowr = lax.broadcasted_iota(jnp.int32, (R, R), 0)
    colr = lax.broadcasted_iota(jnp.int32, (R, R), 1)
    same_seq = (rowr // C) == (colr // C)
    cum = jnp.dot((same_seq & (rowr >= colr)).astype(F32), logw, precision=lax.Precision.HIGHEST,
                  preferred_element_type=F32)
    cum_end = [cum[(s + 1) * C - 1:(s + 1) * C, :] for s in range(NB)]
    e_neg = jnp.exp(-cum)
    e_end = jnp.concatenate([jnp.exp(cum_end[s] - cum[s * C:(s + 1) * C]) for s in range(NB)], axis=0)
    r_t = r * jnp.exp(cum)
    a_scale = jnp.exp(cum - logw)
    p_end = [jnp.exp(cum_end[s]) for s in range(NB)]
    eye = (lax.broadcasted_iota(jnp.int32, (HP, HP), 0) == lax.broadcasted_iota(jnp.int32, (HP, HP), 1)).astype(F32)
    eye_c = (row == col).astype(F32)

    chains = [(s, h) for s in range(NB) for h in range(A_HEADS)]
    nch = range(len(chains))

    def blk(x, n):
        s, h = chains[n]
        return x[s * C:(s + 1) * C, h * HP:(h + 1) * HP]

    def each(f):
        out = [f(n) for n in nch]
        tick()
        return out
    kk_n = each(lambda n: blk(kk, n) * lax.rsqrt(jnp.maximum(
        jnp.sum(blk(kk, n) * blk(kk, n), axis=-1, keepdims=True), 1e-24)))
    b_h = each(lambda n: kk_n[n] * blk(a, n))
    at = each(lambda n: -kk_n[n] * blk(a_scale, n))
    ar = each(lambda n: jnp.concatenate([at[n], blk(r_t, n)], axis=0))
    ab = each(lambda n: _dot_t(ar[n], b_h[n] * blk(e_neg, n)))
    ak = each(lambda n: _dot_t(ar[n], blk(k_mod, n) * blk(e_neg, n)))
    a_ab = each(lambda n: jnp.where(strict, ab[n][:C], 0.0))
    a_rb = each(lambda n: jnp.where(incl, ab[n][C:], 0.0))
    a_ak = each(lambda n: jnp.where(strict, ak[n][:C], 0.0))
    a_rk = each(lambda n: jnp.where(incl, ak[n][C:], 0.0))
    tm = each(lambda n: eye_c + a_ab[n])
    pw = each(lambda n: _dot(a_ab[n], a_ab[n]))
    akv = each(lambda n: _dot(a_ak[n], blk(v, n)))
    m = 2
    while 2 * m < C:
        x = each(lambda n: _dot(jnp.concatenate([tm[n], pw[n]], axis=0), pw[n]))
        tm = each(lambda n: tm[n] + x[n][:C])
        pw = each(lambda n: x[n][C:])
        m *= 2
    tm = each(lambda n: tm[n] + _dot(tm[n], pw[n]))
    x2 = each(lambda n: _dot(tm[n], jnp.concatenate([at[n], akv[n]], axis=1)))
    x3 = each(lambda n: _dot(a_rb[n], x2[n]))
    x4 = each(lambda n: _dot((b_h[n] * blk(e_end, n)).T, x2[n]))
    rkv = each(lambda n: _dot(a_rk[n], blk(v, n)))
    kv = each(lambda n: _dot((blk(k_mod, n) * blk(e_end, n)).T, blk(v, n)))
    rm = each(lambda n: jnp.concatenate([blk(r_t, n) + x3[n][:, :HP],
                                         eye * p_end[chains[n][0]][:, chains[n][1] * HP:(chains[n][1] + 1) * HP]
                                         + x4[n][:, :HP]], axis=0))
    x5 = each(lambda n: _dot(rm[n], h_ref[n]))
    for s in range(NB):
        outs = []
        for h in range(A_HEADS):
            n = s * A_HEADS + h
            sl = slice(h * HP, (h + 1) * HP)
            h_ref[n] = x5[n][C:] + x4[n][:, HP:] + kv[n]
            y = x5[n][:C] + x3[n][:, HP:] + rkv[n]
            mean = jnp.sum(y, axis=-1, keepdims=True) * (1.0 / A_HEAD_DIM)
            dlt = jnp.where(valid_c, y - mean, 0.0)
            var = jnp.sum(dlt * dlt, axis=-1, keepdims=True) * (1.0 / A_HEAD_DIM)
            yn = dlt * lax.rsqrt(var + A_GN_EPS) * gng_ref[:, sl] + gnb_ref[:, sl]
            bonus = jnp.sum(blk(r, n) * blk(k_mod, n) * rk_ref[:, sl], axis=-1, keepdims=True) * blk(v, n)
            outs.append((yn + bonus) * blk(g, n))
        o_ref[s] = jnp.concatenate(
            [outs[2 * i] + pltpu.roll(outs[2 * i + 1], A_HEAD_DIM, 1) for i in range(A_HEADS // 2)],
            axis=1).astype(o_ref.dtype)


def _pad_heads(t, heads):
    hd = t.shape[-1] // heads
    t = t.reshape(t.shape[:-1] + (heads, hd))
    t = jnp.pad(t, [(0, 0)] * (t.ndim - 1) + [(0, HP - hd)])
    return t.reshape(t.shape[:-2] + (heads * HP,))


B_CHUNK = 64
B_SUB = 16
B_NB = 2


def _dot_tl(x, y):
    return lax.dot_general(x.astype(BF16), y.astype(BF16), (((0,), (0,)), ((), ())), preferred_element_type=F32)


def _hgrn_stages(p_ref, lb_ref, gn_ref, o_ref, st_ref):
    C, SB, W = B_CHUNK, B_SUB, B_WIDTH
    nsb = C // SB
    D = B_KEY_DIM
    NB = B_NB
    R = NB * C

    @pl.when(pl.program_id(1) == 0)
    def _():
        st_ref[...] = jnp.zeros_like(st_ref)

    p = jnp.concatenate([p_ref[s].astype(F32) for s in range(NB)], axis=0)
    q, f, val, gate = p[:, :W], p[:, W:2 * W], p[:, 2 * W:3 * W], p[:, 3 * W:]
    lb = lb_ref[...]
    log_f = jnp.logaddexp(jnp.log(lb), jnp.log1p(-lb) + jax.nn.log_sigmoid(f))
    k = (1.0 - lb) * jax.nn.sigmoid(-f)
    yield
    rowr = lax.broadcasted_iota(jnp.int32, (R, R), 0)
    colr = lax.broadcasted_iota(jnp.int32, (R, R), 1)
    b = jnp.dot((((rowr // C) == (colr // C)) & (rowr >= colr)).astype(F32), log_f,
                precision=lax.Precision.HIGHEST, preferred_element_type=F32)
    yield
    b0 = jnp.concatenate(
        [jnp.zeros((SB, W), F32) if i % nsb == 0 else jnp.broadcast_to(b[i * SB - 1:i * SB, :], (SB, W))
         for i in range(NB * nsb)], axis=0)
    qt = q * jnp.exp(b - b0)
    qe = q * jnp.exp(b)
    b_end = [b[(s + 1) * C - 1:(s + 1) * C, :] for s in range(NB)]
    ke = jnp.concatenate([k[s * C:(s + 1) * C] * jnp.exp(b_end[s] - b[s * C:(s + 1) * C]) for s in range(NB)],
                         axis=0)
    e_end = [jnp.exp(b_end[s]) for s in range(NB)]
    rowd = lax.broadcasted_iota(jnp.int32, (C, D), 0)
    lane_c = lax.broadcasted_iota(jnp.int32, (SB, C), 1)
    row_sb = lax.broadcasted_iota(jnp.int32, (SB, C), 0)

    chains = [(s, h) for s in range(NB) for h in range(B_HEADS)]
    each = lambda fn: [fn(n) for n in range(len(chains))]

    def blk(x, n):
        s, h = chains[n]
        return x[s * C:(s + 1) * C, h * D:(h + 1) * D]

    def off_operands(n):
        k_n, b_n, qt_n = blk(k, n), blk(b, n), blk(qt, n)
        kts, qts = [], []
        for i in range(1, nsb):
            b0_i = b_n[i * SB - 1:i * SB]
            kts.append(jnp.where(rowd < i * SB, k_n * jnp.exp(jnp.minimum(b0_i - b_n, 0.0)), 0.0))
            qts.append(jnp.where((rowd >= i * SB) & (rowd < (i + 1) * SB), qt_n, 0.0))
        return jnp.concatenate(kts, axis=1), jnp.concatenate(qts, axis=1)
    yield
    offs = []
    for n in range(len(chains)):
        offs.append(off_operands(n))
        yield
    st_off = each(lambda n: _dot_t(offs[n][0], offs[n][1]))

    st_diag = []
    for n in range(len(chains)):
        k_n, b_n, q_n = blk(k, n), blk(b, n), blk(q, n)
        blocks = []
        for i in range(nsb):
            rs = slice(i * SB, (i + 1) * SB)
            k_i, b_i = k_n[rs], b_n[rs]
            acc = jnp.zeros((SB, C), F32)
            for j in range(SB):
                t = i * SB + j
                z = k_i * q_n[t:t + 1] * jnp.exp(b_n[t:t + 1] - b_i)
                acc = jnp.where(lane_c == t, jnp.sum(z, axis=-1, keepdims=True), acc)
            blocks.append(jnp.where(row_sb + i * SB <= lane_c, acc, 0.0))
            yield
        st_diag.append(jnp.concatenate(blocks, axis=0))
    o_intra = each(lambda n: _dot_tl(st_off[n] + st_diag[n], blk(val, n)))
    yield
    o_state = each(lambda n: _dot_t(blk(qe, n), st_ref[n]))
    upd = each(lambda n: _dot_tl(blk(val, n), blk(ke, n)))
    yield
    for s in range(NB):
        outs = []
        for h in range(B_HEADS):
            n = s * B_HEADS + h
            st_ref[n] = st_ref[n] * e_end[s][:, h * D:(h + 1) * D] + upd[n]
            o = o_intra[n] + o_state[n]
            o = o * lax.rsqrt(jnp.mean(o * o, axis=-1, keepdims=True) + NORM_EPS)
            gt = blk(gate, n)
            outs.append(o * gn_ref[:, h * D:(h + 1) * D] * (gt * jax.nn.sigmoid(gt)))
        o_ref[s] = jnp.concatenate(outs, axis=1).astype(o_ref.dtype)


N_RWKV_IN = 11
N_HGRN_IN = 3


def _rwkv_hgrn_kernel(*refs):
    a_in, b_in = refs[:N_RWKV_IN], refs[N_RWKV_IN:N_RWKV_IN + N_HGRN_IN]
    a_out, b_out, a_carry, a_state, b_state = refs[N_RWKV_IN + N_HGRN_IN:]
    other = _hgrn_stages(*b_in, b_out, b_state)
    _rwkv_body(*a_in, a_out, a_carry, a_state, tick=lambda: next(other, None))
    for _ in other:
        pass


def rwkv7_hgrn2(p_a, mu, w0, w_up, a0, a_up, g_up, k_k, k_a, r_k, gn_g, gn_b, p_b, lb, b_gn_g):
    bsz, seq, _ = p_a.shape
    C, NB = A_CHUNK, A_NB
    assert (B_CHUNK, B_NB) == (C, NB) and seq % C == 0 and bsz % NB == 0
    z = jnp.zeros((A_RANK_W, A_PADW), F32)
    wa = jnp.concatenate([jnp.concatenate([_pad_heads(w_up, A_HEADS), z], axis=1),
                          jnp.concatenate([z, _pad_heads(a_up, A_HEADS)], axis=1)], axis=0).astype(BF16)
    vecs = [_pad_heads(t.reshape(-1), A_HEADS).reshape(1, -1) for t in (k_k, k_a, r_k, gn_g, gn_b)]
    full = lambda shape: pl.BlockSpec(shape, lambda b, c: (0,) * len(shape))
    rows = lambda w: pl.BlockSpec((NB, C, w), lambda b, c: (b, c, 0))
    return pl.pallas_call(
        _rwkv_hgrn_kernel,
        grid=(bsz // NB, seq // C),
        in_specs=[rows(A_COLS), full((1, A_COLS)), full((1, A_PADW)), full((1, A_PADW)),
                  full((LANE, 2 * A_PADW)), full((A_RANK_G, A_PADW))] + [full((1, A_PADW))] * 5
                 + [rows(B_COLS), full((1, B_WIDTH)), full((1, B_WIDTH))],
        out_specs=[rows(A_WIDTH), rows(B_WIDTH)],
        out_shape=[jax.ShapeDtypeStruct((bsz, seq, A_WIDTH), BF16), jax.ShapeDtypeStruct((bsz, seq, B_WIDTH), BF16)],
        scratch_shapes=[pltpu.VMEM((NB, SUBLANE, A_COLS), F32), pltpu.VMEM((NB * A_HEADS, HP, HP), F32),
                        pltpu.VMEM((NB * B_HEADS, B_VAL_DIM, B_KEY_DIM), F32)],
        compiler_params=pltpu.CompilerParams(
            dimension_semantics=("arbitrary", "arbitrary"), vmem_limit_bytes=VMEM_LIMIT),
        name="rwkv7_hgrn2",
    )(p_a, mu.reshape(1, -1), _pad_heads(w0, A_HEADS).reshape(1, -1), _pad_heads(a0, A_HEADS).reshape(1, -1), wa,
      _pad_heads(g_up, A_HEADS).astype(BF16), *vecs, p_b, lb.reshape(1, -1), b_gn_g.reshape(1, -1))


C_PAD = _round_up(C_COLS, LANE)
QB = 128
QW = 2 * QB
C_KV_W = C_KV_HEADS * C_HEAD_DIM
C_K_OFF = C_WIDTH
C_QI_OFF = C_WIDTH + 2 * C_KV_W
C_KI_OFF = C_QI_OFF + IDX_HEADS * IDX_DIM
GROUP = C_HEADS // C_KV_HEADS
INT_MIN = -2 ** 31
NEG = -1e30
HALF = C_HEAD_DIM // 2
VT_ROWS = C_HEAD_DIM + BF16_SUBLANE


def _rope(t, cos, sin_signed):
    outs = []
    lane = lax.broadcasted_iota(jnp.int32, (t.shape[0], LANE), 1)
    first_half = (lane % C_HEAD_DIM) < HALF
    for c in range(t.shape[1] // LANE):
        x = t[:, c * LANE:(c + 1) * LANE]
        swapped = jnp.where(first_half, pltpu.roll(x, LANE - HALF, 1), pltpu.roll(x, HALF, 1))
        outs.append(x * cos + swapped * sin_signed)
    return outs


def _dsa_kernel(pq_ref, kv_ref, ki_ref, cosq_ref, sinq_ref, cosf_ref, sinf_ref, o_ref,
                k_s, vt_s, ki_s, key_s, *, k_sel, seq):
    i = pl.program_id(1)
    nblk = i + 1
    lane = lax.broadcasted_iota(jnp.int32, (QW, LANE), 1)
    low_half = lane < C_HEAD_DIM
    row_k = lax.broadcasted_iota(jnp.int32, (QB, QW), 0)
    q_pos = i * QW + lax.broadcasted_iota(jnp.int32, (QB, QW), 1)

    @pl.when(i == 0)
    def _():
        kv = kv_ref[0].astype(F32)
        k_s[...] = _rope(kv[:, :LANE], cosf_ref[...], sinf_ref[...])[0].astype(BF16)
        vt = kv[:, LANE:].T
        extra = (lax.broadcasted_iota(jnp.int32, (VT_ROWS - C_HEAD_DIM, seq), 0) == 0).astype(F32)
        for c in range(C_KV_HEADS):
            vt_s[c] = jnp.concatenate([vt[c * C_HEAD_DIM:(c + 1) * C_HEAD_DIM], extra], axis=0).astype(BF16)
        lane_f = lax.broadcasted_iota(jnp.int32, (seq, LANE), 1)
        ki = jnp.where(lane_f < IDX_DIM, ki_ref[0].astype(F32), 0.0)
        ki = _rope(ki, cosf_ref[...], sinf_ref[...])[0]
        ki_s[...] = (ki + pltpu.roll(ki, IDX_DIM, 1)).astype(BF16)

    pq = pq_ref[0].astype(F32)
    cos, sin = cosq_ref[...], sinq_ref[...]
    q = _rope(pq[:, :C_WIDTH], cos, sin)
    qi = _rope(pq[:, C_QI_OFF:C_KI_OFF], cos, sin)
    qi3 = jnp.concatenate(
        [jnp.where(low_half if h % 2 == 0 else ~low_half, qi[h // 2], 0.0) for h in range(IDX_HEADS)],
        axis=0).astype(BF16)
    wt = pq[:, C_KI_OFF:C_KI_OFF + LANE].T
    wscale = (IDX_HEADS ** -0.5) * (IDX_DIM ** -0.5)
    w_rows = [wt[IDX_DIM + h:IDX_DIM + h + 1, :] * wscale for h in range(IDX_HEADS)]

    def score_blk(jb, _):
        off = pl.multiple_of(jb * QW, QW)
        d = [lax.dot_general(ki_s[pl.ds(off + u * QB, QB), :], qi3, (((1,), (1,)), ((), ())),
                             preferred_element_type=F32) for u in range(QW // QB)]
        for u in range(QW // QB):
            j = (QW // QB) * jb + u
            sc = jnp.zeros((QB, QW), F32)
            for h in range(IDX_HEADS):
                sc = sc + jnp.maximum(d[u][:, h * QW:(h + 1) * QW], 0.0) * w_rows[h]
            bits = pltpu.bitcast(sc + 0.0, jnp.int32)
            key = jnp.where(bits < 0, bits ^ jnp.int32(0x7fffffff), bits)
            key_s[j] = jnp.where(j * QB + row_k <= q_pos, key, jnp.int32(INT_MIN))
        return 0
    lax.fori_loop(0, nblk, score_blk, 0)

    def count(pred):
        def body(jb, acc):
            for u in range(QW // QB):
                hit = jnp.where(pred(key_s[(QW // QB) * jb + u]), 1, 0)
                acc = acc + jnp.sum(hit.reshape(QB // SUBLANE, SUBLANE, QW), axis=0)
            return acc
        return jnp.sum(lax.fori_loop(0, nblk, body, jnp.zeros((SUBLANE, QW), jnp.int32)), axis=0, keepdims=True)

    def bit_step(b, cand):
        trial = cand | jnp.left_shift(jnp.int32(1), 31 - b)
        cnt = count(lambda key: key >= (trial ^ jnp.int32(INT_MIN)))
        return jnp.where(cnt >= k_sel, trial, cand)
    cand = lax.fori_loop(0, 32, bit_step, jnp.zeros((1, QW), jnp.int32))
    thr = cand ^ jnp.int32(INT_MIN)
    need_f = (k_sel - count(lambda key: key > thr)).astype(F32)
    tri = (lax.broadcasted_iota(jnp.int32, (QW, QW), 0)
           >= lax.broadcasted_iota(jnp.int32, (QW, QW), 1)).astype(BF16)

    scale = C_HEAD_DIM ** -0.5
    lane_b = lax.broadcasted_iota(jnp.int32, (QB, LANE), 1)
    in_cs = [(lane_b >= c * C_HEAD_DIM) & (lane_b < (c + 1) * C_HEAD_DIM) for c in range(C_KV_HEADS)]
    chains = [(n, e) for e in range(QW // QB) for n in range(C_HEADS // 2)]
    kv_of = [n * 2 // GROUP for n, _ in chains]

    def q_rows(n, e):
        c = n * 2 // GROUP
        rows = []
        for h in (2 * n, 2 * n + 1):
            x = q[n][e * QB:(e + 1) * QB]
            if h % 2 != c:
                x = pltpu.roll(x, C_HEAD_DIM, 1)
            rows.append(jnp.where(in_cs[c], x * scale, 0.0))
        return jnp.concatenate(rows, axis=0).astype(BF16)
    qcs = [q_rows(n, e) for n, e in chains]
    nch = range(len(chains))

    def att_blk(jb, carry):
        off = pl.multiple_of(jb * QW, QW)
        k_t = k_s[pl.ds(off, QW), :]
        v_t = [vt_s[c, :, pl.ds(off, QW)] for c in range(C_KV_HEADS)]
        s = [lax.dot_general(k_t, qcs[x], (((1,), (1,)), ((), ())), preferred_element_type=F32) for x in nch]
        keys = jnp.concatenate([key_s[(QW // QB) * jb + u] for u in range(QW // QB)], axis=0)
        pos = off + lax.broadcasted_iota(jnp.int32, (QW, QW), 0)
        tie = keys == thr
        rank = carry[-1] + jnp.dot(tri, jnp.where(tie, 1.0, 0.0).astype(BF16), preferred_element_type=F32)
        sel = ((keys > thr) | (tie & (rank <= need_f))) & (pos <= q_pos[:1, :])
        bias1 = jnp.where(sel, 0.0, NEG)
        bias = [jnp.concatenate([bias1[:, e * QB:(e + 1) * QB]] * 2, axis=1) for e in range(QW // QB)]
        s = [s[x] + bias[chains[x][1]] for x in nch]
        m_new = [jnp.maximum(carry[2 * x], jnp.max(s[x], axis=0, keepdims=True)) for x in nch]
        p = [jnp.exp(s[x] - m_new[x]).astype(BF16) for x in nch]
        pv = [jnp.dot(v_t[kv_of[x]], p[x], preferred_element_type=F32) for x in nch]
        out = []
        for x in nch:
            out += [m_new[x], carry[2 * x + 1] * jnp.exp(carry[2 * x] - m_new[x]) + pv[x]]
        return tuple(out) + (rank[QW - 1:, :],)
    init = ((jnp.full((1, 2 * QB), NEG, F32), jnp.zeros((VT_ROWS, 2 * QB), F32)) * len(chains)
            + (jnp.zeros((1, QW), F32),))
    res = lax.fori_loop(0, nblk, att_blk, init)

    halves = []
    for e in range(QW // QB):
        tiles = []
        for n in range(C_HEADS // 2):
            acc = res[2 * chains.index((n, e)) + 1]
            out_t = acc[:C_HEAD_DIM] / acc[C_HEAD_DIM:C_HEAD_DIM + 1]
            tiles.append(jnp.concatenate([out_t[:, :QB], out_t[:, QB:]], axis=0).T)
        halves.append(jnp.concatenate(tiles, axis=1))
    o_ref[0] = jnp.concatenate(halves, axis=0).astype(o_ref.dtype)


def _rope_lanes(seq):
    inv_freq = ROPE_THETA ** (-jnp.arange(HALF, dtype=F32) * (2.0 / C_HEAD_DIM))
    ang = jnp.arange(seq, dtype=F32)[:, None] * inv_freq[None, :]
    cos, sin = jnp.cos(ang), jnp.sin(ang)
    reps = LANE // C_HEAD_DIM
    return (jnp.tile(jnp.concatenate([cos, cos], axis=1), (1, reps)),
            jnp.tile(jnp.concatenate([-sin, sin], axis=1), (1, reps)))


def dsa(p):
    bsz, seq, width = p.shape
    assert seq % QW == 0 and width == C_PAD
    k_sel = min(TOPK_MAX, seq // 4)
    cos, sin = _rope_lanes(seq)
    return pl.pallas_call(
        functools.partial(_dsa_kernel, k_sel=k_sel, seq=seq),
        grid=(bsz, seq // QW),
        in_specs=[pl.BlockSpec((1, QW, C_PAD), lambda b, i: (b, i, 0)),
                  pl.BlockSpec((1, seq, 2 * C_KV_W), lambda b, i: (b, 0, C_K_OFF // (2 * C_KV_W))),
                  pl.BlockSpec((1, seq, LANE), lambda b, i: (b, 0, C_KI_OFF // LANE)),
                  pl.BlockSpec((QW, LANE), lambda b, i: (i, 0)),
                  pl.BlockSpec((QW, LANE), lambda b, i: (i, 0)),
                  pl.BlockSpec((seq, LANE), lambda b, i: (0, 0)),
                  pl.BlockSpec((seq, LANE), lambda b, i: (0, 0))],
        out_specs=pl.BlockSpec((1, QW, C_WIDTH), lambda b, i: (b, i, 0)),
        out_shape=jax.ShapeDtypeStruct((bsz, seq, C_WIDTH), BF16),
        scratch_shapes=[pltpu.VMEM((seq, LANE), BF16), pltpu.VMEM((C_KV_HEADS, VT_ROWS, seq), BF16),
                        pltpu.VMEM((seq, LANE), BF16), pltpu.VMEM((seq // QB, QB, QW), jnp.int32)],
        compiler_params=pltpu.CompilerParams(
            dimension_semantics=("arbitrary", "arbitrary"), vmem_limit_bytes=VMEM_LIMIT),
        name="dsa",
    )(p, p, p, cos, sin, cos, sin)


def _split(t, sizes):
    out, start = [], 0
    for s in sizes:
        out.append(t[..., start:start + s])
        start += s
    return out


def kernel(x, norm_mix_g, w_in, a_mu, a_w0, a_w_up, a_a0, a_a_up, a_g_up, a_k_k, a_k_a, a_r_k, a_gn_g,
           a_gn_b, b_lb_logits, b_gn_g, w_branch, w_o, norm_ffn_g, w_up, conv_w, conv_b, w_down,
           norm_final_g):
    bsz, seq, d = x.shape
    t = bsz * seq
    lb_cum = jnp.cumsum(jax.nn.softmax(b_lb_logits.astype(F32), axis=0), axis=0)
    lower_bounds = lb_cum - lb_cum[0]
    h = x
    w_branch, w_o, w_up, w_down = (w.astype(BF16) for w in (w_branch, w_o, w_up, w_down))
    for layer in range(DEPTH):
        w_a, w_b, w_c, w_g = (w.astype(BF16)
                              for w in _split(w_in[layer], (A_COLS, B_COLS, C_COLS, GATE_COLS)))
        w_c = jnp.pad(w_c, ((0, 0), (0, C_PAD - C_COLS)))
        h2 = h.reshape(t, d)
        g = norm_mix_g[layer]
        p_a = norm_matmul(h2, g, w_a, tm=PROJ_TM, tn=A_COLS // 2, out_dtype=BF16).reshape(bsz, seq, -1)
        p_b = norm_matmul(h2, g, w_b, tm=PROJ_TM, tn=PROJ_TN, out_dtype=BF16).reshape(bsz, seq, -1)
        p_c = norm_matmul(h2, g, w_c, tm=PROJ_TM, tn=C_PAD, out_dtype=F32).reshape(bsz, seq, -1)
        p_g = norm_matmul(h2, g, w_g, tm=PROJ_TM, tn=PROJ_TN, out_dtype=BF16)
        y_a, y_b = rwkv7_hgrn2(p_a, a_mu[layer], a_w0[layer], a_w_up[layer], a_a0[layer], a_a_up[layer],
                               a_g_up[layer], a_k_k[layer], a_k_a[layer], a_r_k[layer], a_gn_g[layer],
                               a_gn_b[layer], p_b, lower_bounds[layer], b_gn_g[layer])
        y_c = dsa(p_c)
        h2 = merge((y_a.reshape(t, -1), y_b.reshape(t, -1), y_c.reshape(t, -1)), p_g, h2,
                   w_branch, w_o, layer=layer, tm=MERGE_TM)
        h = conv_glu(h2.reshape(bsz, seq, d), norm_ffn_g[layer], w_up, conv_w[layer], conv_b[layer], w_down,
                     norm_final_g if layer == DEPTH - 1 else None, layer=layer, tm=FFN_TM, tf=FFN_TF)
    return h
```
